```python
import jax, jax.numpy as jnp
from jax import lax
import numpy as np

D_MODEL = 2048
BATCH = 8
SEQ = 4096
DEPTH = 4

GRID_W = 64
CTX_LEN = 256

N_HEADS = 4
D_KEY = D_MODEL // 2
D_VAL = D_MODEL
HEAD_K = D_KEY // N_HEADS
HEAD_V = D_VAL // N_HEADS
GATE_RANK = 16
GATE_NORM = 16.0
CHUNK = 64

POOL_WINDOWS = (2, 4, 8, 16)
D_POOL = D_MODEL // 2
POOL_GROUP = D_POOL // len(POOL_WINDOWS)

D_FF = ((8 * D_MODEL + 3 * 256 - 1) // (3 * 256)) * 256

N_BRANCH = 2
N_MOD = 6
DEEPNORM_ALPHA = (2.0 * DEPTH) ** 0.25
DEEPNORM_BETA = (8.0 * DEPTH) ** -0.25
LN_EPS = 1e-5
RMS_EPS = 1e-6

PROJ_WIDTHS = (D_KEY, D_KEY, D_VAL, D_VAL, 2 * GATE_RANK, D_POOL, N_BRANCH * D_MODEL)
D_PROJ = sum(PROJ_WIDTHS)
SPLIT_POINTS = tuple(int(s) for s in np.cumsum(PROJ_WIDTHS)[:-1])

kernel_name = "hybrid_gla_pool_diffusion_trunk"


def _layer_norm(x, gain, bias):
    xf = x.astype(jnp.float32)
    mu = jnp.mean(xf, axis=-1, keepdims=True)
    var = jnp.mean(jnp.square(xf - mu), axis=-1, keepdims=True)
    y = (xf - mu) * lax.rsqrt(var + LN_EPS)
    return (y * gain + bias).astype(x.dtype)


def _modulation(cond, w_ada, b_ada):
    return jnp.split(jax.nn.silu(cond) @ w_ada + b_ada, N_MOD, axis=-1)


def _heads(t, head_dim):
    b, l, _ = t.shape
    return t.reshape(b, l, -1, head_dim).transpose(0, 2, 1, 3)


def _split_proj(proj, w_decay_up, b_decay_up):
    q, k, v, g, a_lr, p, bg = jnp.split(proj, SPLIT_POINTS, axis=-1)
    q = _heads(q * HEAD_K ** -0.5, HEAD_K)
    k = _heads(k, HEAD_K)
    v = _heads(v, HEAD_V)
    log_decay = []
    for d in range(2):
        z = a_lr[..., d * GATE_RANK:(d + 1) * GATE_RANK] @ w_decay_up[d] + b_decay_up[d]
        log_decay.append(_heads(jax.nn.log_sigmoid(z.astype(jnp.float32)) / GATE_NORM, HEAD_K))
    return q, k, v, log_decay[0], log_decay[1], g, p, bg


def _gla_chunked(q, k, v, log_a, s0):
    b_, h_, l_, _ = q.shape
    n = l_ // CHUNK

    def chunks(t):
        return t.reshape(b_, h_, n, CHUNK, t.shape[-1]).astype(jnp.float32)

    q, k, v, log_a = chunks(q), chunks(k), chunks(v), chunks(log_a)
    cum = jnp.cumsum(log_a, axis=3)
    ref = cum[:, :, :, CHUNK // 2 - 1:CHUNK // 2]
    q_in = q * jnp.exp(cum - ref)
    k_in = k * jnp.exp(ref - cum)
    scores = jnp.einsum('bhnid,bhnjd->bhnij', q_in, k_in)
    mask = jnp.tril(jnp.ones((CHUNK, CHUNK), dtype=bool))
    scores = jnp.where(mask, scores, 0.0)
    o_intra = jnp.einsum('bhnij,bhnjv->bhniv', scores, v)
    last = cum[:, :, :, -1:]
    q_inter = q * jnp.exp(cum)
    k_state = k * jnp.exp(last - cum)
    decay_chunk = jnp.exp(last[:, :, :, 0, :])
    xs = (jnp.moveaxis(q_inter, 2, 0), jnp.moveaxis(k_state, 2, 0),
          jnp.moveaxis(v, 2, 0), jnp.moveaxis(decay_chunk, 2, 0))

    def step(state, inp):
        qc, kc, vc, dc = inp
        o = jnp.einsum('bhid,bhdv->bhiv', qc, state)
        state = dc[..., None] * state + jnp.einsum('bhjd,bhjv->bhdv', kc, vc)
        return state, o

    s_final, o_inter = lax.scan(step, s0.astype(jnp.float32), xs)
    o = o_intra + jnp.moveaxis(o_inter, 0, 2)
    return o.reshape(b_, h_, l_, HEAD_V), s_final


def _gla_two_way(q, k, v, la_fwd, la_bwd, s0_fwd, s0_bwd):
    flip = lambda t: jnp.flip(t, axis=2)
    o_f, s_f = _gla_chunked(q, k, v, la_fwd, s0_fwd)
    o_b, s_b = _gla_chunked(flip(q), flip(k), flip(v), flip(la_bwd), s0_bwd)
    return o_f + flip(o_b), s_f, s_b


def _box_mean(t, axis, window):
    n = t.shape[axis]
    lo = window // 2
    hi = window - lo - 1
    cs = jnp.cumsum(t.astype(jnp.float32), axis=axis)
    zero = jnp.zeros_like(lax.slice_in_dim(cs, 0, 1, axis=axis))
    cs = jnp.concatenate([zero, cs], axis=axis)
    idx = jnp.arange(n)
    upper = jnp.minimum(idx + hi + 1, n)
    lower = jnp.maximum(idx - lo, 0)
    total = jnp.take(cs, upper, axis=axis) - jnp.take(cs, lower, axis=axis)
    shape = [1] * t.ndim
    shape[axis] = n
    count = (upper - lower).astype(jnp.float32).reshape(shape)
    return (total / count).astype(t.dtype)


def _pool_branch(p, rows, w_pool_group, pool_scale, w_pool_out):
    outs = []
    for i, w in enumerate(POOL_WINDOWS):
        pg = p[..., i * POOL_GROUP:(i + 1) * POOL_GROUP]
        if rows is None:
            mean = _box_mean(pg, 1, w)
        else:
            b, l, ch = pg.shape
            img = pg.reshape(b, rows, GRID_W, ch)
            mean = _box_mean(_box_mean(img, 2, w), 1, w).reshape(b, l, ch)
        outs.append((mean - pg) @ w_pool_group[i])
    return (jnp.concatenate(outs, axis=-1) * pool_scale) @ w_pool_out


def _gla_output(o, g, gain, w_gla_out):
    of = o.transpose(0, 2, 1, 3)
    of = of * lax.rsqrt(jnp.mean(of * of, axis=-1, keepdims=True) + RMS_EPS)
    b, l = of.shape[:2]
    of = of.reshape(b, l, D_VAL).astype(g.dtype) * gain
    return (of * jax.nn.silu(g)) @ w_gla_out


def _merge(o, g, p, bg, rows, gla_norm_gain, w_gla_out, w_pool_group, pool_scale, w_pool_out, w_out):
    y_gla = _gla_output(o, g, gla_norm_gain, w_gla_out)
    y_pool = _pool_branch(p, rows, w_pool_group, pool_scale, w_pool_out)
    gate_gla, gate_pool = jnp.split(jax.nn.sigmoid(bg), N_BRANCH, axis=-1)
    return (gate_gla * y_gla + gate_pool * y_pool) @ w_out


def _token_mixer(h_lat, h_ctx, rows, ctx_out, w_in, w_decay_up, b_decay_up, gla_norm_gain,
                 w_pool_group, pool_scale, w_gla_out, w_pool_out, w_out):
    q_c, k_c, v_c, laf_c, lab_c, g_c, p_c, bg_c = _split_proj(h_ctx @ w_in, w_decay_up, b_decay_up)
    q_l, k_l, v_l, laf_l, lab_l, g_l, p_l, bg_l = _split_proj(h_lat @ w_in, w_decay_up, b_decay_up)
    s0 = jnp.zeros((h_lat.shape[0], N_HEADS, HEAD_K, HEAD_V), jnp.float32)
    o_c, s_fwd, s_bwd = _gla_two_way(q_c, k_c, v_c, laf_c, lab_c, s0, s0)
    o_l, _, _ = _gla_two_way(q_l, k_l, v_l, laf_l, lab_l, s_fwd, s_bwd)
    y_lat = _merge(o_l, g_l, p_l, bg_l, rows, gla_norm_gain, w_gla_out, w_pool_group,
                   pool_scale, w_pool_out, w_out)
    y_ctx = None
    if ctx_out:
        y_ctx = _merge(o_c, g_c, p_c, bg_c, None, gla_norm_gain, w_gla_out, w_pool_group,
                       pool_scale, w_pool_out, w_out)
    return y_lat, y_ctx


def _swiglu(h, w_ffn_in, w_ffn_out):
    gate, up = jnp.split(h @ w_ffn_in, 2, axis=-1)
    return (jax.nn.silu(gate) * up) @ w_ffn_out


def _fwd_setup_inputs(seed: int = 0) -> dict:
    key = jax.random.key(seed)
    ks = jax.random.split(key, 24)
    f32 = jnp.float32
    nrm = lambda k, shape, s: jax.random.normal(k, shape, f32) * s
    L = DEPTH
    return {
        "x": nrm(ks[0], (BATCH, SEQ, D_MODEL), 1.0),
        "c": nrm(ks[1], (BATCH, D_MODEL), 1.0),
        "ctx": nrm(ks[2], (BATCH, CTX_LEN, D_MODEL), 1.0),
        "c_ctx": nrm(ks[3], (D_MODEL,), 1.0),
        "w_ada": nrm(ks[4], (L, D_MODEL, N_MOD * D_MODEL), 0.5 * D_MODEL ** -0.5),
        "b_ada": nrm(ks[5], (L, N_MOD * D_MODEL), 0.02),
        "w_in": nrm(ks[6], (L, D_MODEL, D_PROJ), D_MODEL ** -0.5),
        "w_decay_up": nrm(ks[7], (L, 2, GATE_RANK, D_KEY), GATE_RANK ** -0.5),
        "b_decay_up": nrm(ks[8], (L, 2, D_KEY), 0.1),
        "gla_norm_gain": 1.0 + nrm(ks[9], (L, D_VAL), 0.1),
        "w_pool_group": nrm(ks[10], (L, len(POOL_WINDOWS), POOL_GROUP, POOL_GROUP), POOL_GROUP ** -0.5),
        "pool_scale": 1.0 + nrm(ks[11], (L, D_POOL), 0.1),
        "w_gla_out": nrm(ks[12], (L, D_VAL, D_MODEL), D_VAL ** -0.5),
        "w_pool_out": nrm(ks[13], (L, D_POOL, D_MODEL), D_POOL ** -0.5),
        "w_out": nrm(ks[14], (L, D_MODEL, D_MODEL), DEEPNORM_BETA * D_MODEL ** -0.5),
        "ln_mix_gain": 1.0 + nrm(ks[15], (L, D_MODEL), 0.1),
        "ln_mix_bias": nrm(ks[16], (L, D_MODEL), 0.02),
        "w_ffn_in": nrm(ks[17], (L, D_MODEL, 2 * D_FF), D_MODEL ** -0.5),
        "w_ffn_out": nrm(ks[18], (L, D_FF, D_MODEL), DEEPNORM_BETA * D_FF ** -0.5),
        "ln_ffn_gain": 1.0 + nrm(ks[19], (L, D_MODEL), 0.1),
        "ln_ffn_bias": nrm(ks[20], (L, D_MODEL), 0.02),
    }


def _fwd_reference(x, c, ctx, c_ctx, w_ada, b_ada, w_in, w_decay_up, b_decay_up, gla_norm_gain,
              w_pool_group, pool_scale, w_gla_out, w_pool_out, w_out, ln_mix_gain, ln_mix_bias,
              w_ffn_in, w_ffn_out, ln_ffn_gain, ln_ffn_bias):
    rows = x.shape[1] // GRID_W
    for layer in range(DEPTH):
        ctx_out = layer < DEPTH - 1
        sh_m, sc_m, gt_m, sh_f, sc_f, gt_f = _modulation(c[:, None, :], w_ada[layer], b_ada[layer])
        csh_m, csc_m, cgt_m, csh_f, csc_f, cgt_f = _modulation(c_ctx, w_ada[layer], b_ada[layer])
        h_lat = x * (1.0 + sc_m) + sh_m
        h_ctx = ctx * (1.0 + csc_m) + csh_m
        mix_lat, mix_ctx = _token_mixer(h_lat, h_ctx, rows, ctx_out, w_in[layer], w_decay_up[layer],
                                        b_decay_up[layer], gla_norm_gain[layer], w_pool_group[layer],
                                        pool_scale[layer], w_gla_out[layer], w_pool_out[layer], w_out[layer])
        x = _layer_norm(DEEPNORM_ALPHA * x + gt_m * mix_lat, ln_mix_gain[layer], ln_mix_bias[layer])
        ffn_lat = _swiglu(x * (1.0 + sc_f) + sh_f, w_ffn_in[layer], w_ffn_out[layer])
        x = _layer_norm(DEEPNORM_ALPHA * x + gt_f * ffn_lat, ln_ffn_gain[layer], ln_ffn_bias[layer])
        if ctx_out:
            ctx = _layer_norm(DEEPNORM_ALPHA * ctx + cgt_m * mix_ctx, ln_mix_gain[layer], ln_mix_bias[layer])
            ffn_ctx = _swiglu(ctx * (1.0 + csc_f) + csh_f, w_ffn_in[layer], w_ffn_out[layer])
            ctx = _layer_norm(DEEPNORM_ALPHA * ctx + cgt_f * ffn_ctx, ln_ffn_gain[layer], ln_ffn_bias[layer])
    return x


import jax as _jax
import jax.numpy as _jnp

TWIN_FORMAT = 'train_step'
FWD_PARAMS = ['x', 'c', 'ctx', 'c_ctx', 'w_ada', 'b_ada', 'w_in', 'w_decay_up', 'b_decay_up', 'gla_norm_gain', 'w_pool_group', 'pool_scale', 'w_gla_out', 'w_pool_out', 'w_out', 'ln_mix_gain', 'ln_mix_bias', 'w_ffn_in', 'w_ffn_out', 'ln_ffn_gain', 'ln_ffn_bias']
TWIN_WEIGHTS = ['c_ctx', 'w_ada', 'b_ada', 'w_in', 'w_decay_up', 'b_decay_up', 'gla_norm_gain', 'w_pool_group', 'pool_scale', 'w_gla_out', 'w_pool_out', 'w_out', 'ln_mix_gain', 'ln_mix_bias', 'w_ffn_in', 'w_ffn_out', 'ln_ffn_gain', 'ln_ffn_bias']
TWIN_DIFF_INPUT = 'x'
TWIN_INPUTS = ['x', 'c', 'ctx', 'c_ctx', 'w_ada', 'b_ada', 'w_in', 'w_decay_up', 'b_decay_up', 'gla_norm_gain', 'w_pool_group', 'pool_scale', 'w_gla_out', 'w_pool_out', 'w_out', 'ln_mix_gain', 'ln_mix_bias', 'w_ffn_in', 'w_ffn_out', 'ln_ffn_gain', 'ln_ffn_bias', 'loss_target', 'm_c_ctx', 'm_w_ada', 'm_b_ada', 'm_w_in', 'm_w_decay_up', 'm_b_decay_up', 'm_gla_norm_gain', 'm_w_pool_group', 'm_pool_scale', 'm_w_gla_out', 'm_w_pool_out', 'm_w_out', 'm_ln_mix_gain', 'm_ln_mix_bias', 'm_w_ffn_in', 'm_w_ffn_out', 'm_ln_ffn_gain', 'm_ln_ffn_bias', 'v_c_ctx', 'v_w_ada', 'v_b_ada', 'v_w_in', 'v_w_decay_up', 'v_b_decay_up', 'v_gla_norm_gain', 'v_w_pool_group', 'v_pool_scale', 'v_w_gla_out', 'v_w_pool_out', 'v_w_out', 'v_ln_mix_gain', 'v_ln_mix_bias', 'v_w_ffn_in', 'v_w_ffn_out', 'v_ln_ffn_gain', 'v_ln_ffn_bias']
TWIN_OUTPUTS = ['loss', 'grad_x', 'grad_c_ctx', 'grad_w_ada', 'grad_b_ada', 'grad_w_in', 'grad_w_decay_up', 'grad_b_decay_up', 'grad_gla_norm_gain', 'grad_w_pool_group', 'grad_pool_scale', 'grad_w_gla_out', 'grad_w_pool_out', 'grad_w_out', 'grad_ln_mix_gain', 'grad_ln_mix_bias', 'grad_w_ffn_in', 'grad_w_ffn_out', 'grad_ln_ffn_gain', 'grad_ln_ffn_bias', 'delta_c_ctx', 'delta_w_ada', 'delta_b_ada', 'delta_w_in', 'delta_w_decay_up', 'delta_b_decay_up', 'delta_gla_norm_gain', 'delta_w_pool_group', 'delta_pool_scale', 'delta_w_gla_out', 'delta_w_pool_out', 'delta_w_out', 'delta_ln_mix_gain', 'delta_ln_mix_bias', 'delta_w_ffn_in', 'delta_w_ffn_out', 'delta_ln_ffn_gain', 'delta_ln_ffn_bias', 'new_m_c_ctx', 'new_m_w_ada', 'new_m_b_ada', 'new_m_w_in', 'new_m_w_decay_up', 'new_m_b_decay_up', 'new_m_gla_norm_gain', 'new_m_w_pool_group', 'new_m_pool_scale', 'new_m_w_gla_out', 'new_m_w_pool_out', 'new_m_w_out', 'new_m_ln_mix_gain', 'new_m_ln_mix_bias', 'new_m_w_ffn_in', 'new_m_w_ffn_out', 'new_m_ln_ffn_gain', 'new_m_ln_ffn_bias', 'new_v_c_ctx', 'new_v_w_ada', 'new_v_b_ada', 'new_v_w_in', 'new_v_w_decay_up', 'new_v_b_decay_up', 'new_v_gla_norm_gain', 'new_v_w_pool_group', 'new_v_pool_scale', 'new_v_w_gla_out', 'new_v_w_pool_out', 'new_v_w_out', 'new_v_ln_mix_gain', 'new_v_ln_mix_bias', 'new_v_w_ffn_in', 'new_v_w_ffn_out', 'new_v_ln_ffn_gain', 'new_v_ln_ffn_bias']
TWIN_LEAF_KINDS = {'loss': 'loss', 'grad_x': 'grad_x', 'grad_c_ctx': 'grad_w', 'grad_w_ada': 'grad_w', 'grad_b_ada': 'grad_w', 'grad_w_in': 'grad_w', 'grad_w_decay_up': 'grad_w', 'grad_b_decay_up': 'grad_w', 'grad_gla_norm_gain': 'grad_w', 'grad_w_pool_group': 'grad_w', 'grad_pool_scale': 'grad_w', 'grad_w_gla_out': 'grad_w', 'grad_w_pool_out': 'grad_w', 'grad_w_out': 'grad_w', 'grad_ln_mix_gain': 'grad_w', 'grad_ln_mix_bias': 'grad_w', 'grad_w_ffn_in': 'grad_w', 'grad_w_ffn_out': 'grad_w', 'grad_ln_ffn_gain': 'grad_w', 'grad_ln_ffn_bias': 'grad_w', 'delta_c_ctx': 'delta_w', 'delta_w_ada': 'delta_w', 'delta_b_ada': 'delta_w', 'delta_w_in': 'delta_w', 'delta_w_decay_up': 'delta_w', 'delta_b_decay_up': 'delta_w', 'delta_gla_norm_gain': 'delta_w', 'delta_w_pool_group': 'delta_w', 'delta_pool_scale': 'delta_w', 'delta_w_gla_out': 'delta_w', 'delta_w_pool_out': 'delta_w', 'delta_w_out': 'delta_w', 'delta_ln_mix_gain': 'delta_w', 'delta_ln_mix_bias': 'delta_w', 'delta_w_ffn_in': 'delta_w', 'delta_w_ffn_out': 'delta_w', 'delta_ln_ffn_gain': 'delta_w', 'delta_ln_ffn_bias': 'delta_w', 'new_m_c_ctx': 'new_m', 'new_m_w_ada': 'new_m', 'new_m_b_ada': 'new_m', 'new_m_w_in': 'new_m', 'new_m_w_decay_up': 'new_m', 'new_m_b_decay_up': 'new_m', 'new_m_gla_norm_gain': 'new_m', 'new_m_w_pool_group': 'new_m', 'new_m_pool_scale': 'new_m', 'new_m_w_gla_out': 'new_m', 'new_m_w_pool_out': 'new_m', 'new_m_w_out': 'new_m', 'new_m_ln_mix_gain': 'new_m', 'new_m_ln_mix_bias': 'new_m', 'new_m_w_ffn_in': 'new_m', 'new_m_w_ffn_out': 'new_m', 'new_m_ln_ffn_gain': 'new_m', 'new_m_ln_ffn_bias': 'new_m', 'new_v_c_ctx': 'new_v', 'new_v_w_ada': 'new_v', 'new_v_b_ada': 'new_v', 'new_v_w_in': 'new_v', 'new_v_w_decay_up': 'new_v', 'new_v_b_decay_up': 'new_v', 'new_v_gla_norm_gain': 'new_v', 'new_v_w_pool_group': 'new_v', 'new_v_pool_scale': 'new_v', 'new_v_w_gla_out': 'new_v', 'new_v_w_pool_out': 'new_v', 'new_v_w_out': 'new_v', 'new_v_ln_mix_gain': 'new_v', 'new_v_ln_mix_bias': 'new_v', 'new_v_w_ffn_in': 'new_v', 'new_v_w_ffn_out': 'new_v', 'new_v_ln_ffn_gain': 'new_v', 'new_v_ln_ffn_bias': 'new_v'}


def _forward(args):
    return _fwd_reference(*[args[k] for k in FWD_PARAMS])


def _output_shape():
    def fwd():
        inp = _fwd_setup_inputs(0)
        return _fwd_reference(*[inp[k] for k in FWD_PARAMS])
    out = _jax.eval_shape(fwd)
    return out.shape, out.dtype

N_MICROBATCH = 1
ADAM_LR = 0.001
ADAM_B1 = 0.9
ADAM_B2 = 0.999
ADAM_EPS = 1e-08
ADAM_WD = 0.01
ADAM_STEP = 10
PER_EXAMPLE_BATCH_AXIS = {'x': 0, 'c': 0, 'ctx': 0, 'loss_target': 0}
SHARED_INPUTS = []
_WEIGHT_DTYPES = {'c_ctx': _jnp.float32, 'w_ada': _jnp.float32, 'b_ada': _jnp.float32, 'w_in': _jnp.float32, 'w_decay_up': _jnp.float32, 'b_decay_up': _jnp.float32, 'gla_norm_gain': _jnp.float32, 'w_pool_group': _jnp.float32, 'pool_scale': _jnp.float32, 'w_gla_out': _jnp.float32, 'w_pool_out': _jnp.float32, 'w_out': _jnp.float32, 'ln_mix_gain': _jnp.float32, 'ln_mix_bias': _jnp.float32, 'w_ffn_in': _jnp.float32, 'w_ffn_out': _jnp.float32, 'ln_ffn_gain': _jnp.float32, 'ln_ffn_bias': _jnp.float32}
MOMENT_SCALE = {'c_ctx': 7.471504e-04, 'w_ada': 6.872207e-03, 'b_ada': 1.234822e-02, 'w_in': 2.865597e-03, 'w_decay_up': 4.338448e-04, 'b_decay_up': 1.134846e-03, 'gla_norm_gain': 2.472035e-03, 'w_pool_group': 5.472796e-03, 'pool_scale': 5.449829e-03, 'w_gla_out': 2.471875e-03, 'w_pool_out': 3.881125e-03, 'w_out': 1.093927e-02, 'ln_mix_gain': 3.432791e+00, 'ln_mix_bias': 3.369966e-01, 'w_ffn_in': 3.006374e-03, 'w_ffn_out': 1.172003e-02, 'ln_ffn_gain': 9.641858e+00, 'ln_ffn_bias': 5.260384e-01}


def _to_microbatches(a, axis):
    t = _jnp.moveaxis(a, axis, 0)
    t = t.reshape((N_MICROBATCH, t.shape[0] // N_MICROBATCH) + t.shape[1:])
    return _jnp.moveaxis(t, 1, axis + 1)


def setup_inputs(seed: int = 0) -> dict:
    inp = _fwd_setup_inputs(seed)
    key = _jax.random.fold_in(_jax.random.key(seed), 7919)
    shape, _ = _output_shape()
    out = dict(inp)
    out["loss_target"] = _jax.random.normal(_jax.random.fold_in(key, 0), shape, _jnp.float32)
    for i, name in enumerate(TWIN_WEIGHTS):
        w = inp[name].astype(_jnp.float32)
        if MOMENT_SCALE is None:
            s = _jnp.sqrt(_jnp.mean(_jnp.square(w)) + 1e-30)
        else:
            s = MOMENT_SCALE[name]
        km, kv = _jax.random.split(_jax.random.fold_in(key, i + 1))
        out[name] = w
        out["m_" + name] = s * _jax.random.normal(km, w.shape, _jnp.float32)
        out["v_" + name] = (s * s) * _jax.random.uniform(kv, w.shape, _jnp.float32, 0.5, 1.5)
    if N_MICROBATCH > 1:
        for name, axis in PER_EXAMPLE_BATCH_AXIS.items():
            out[name] = _to_microbatches(out[name], axis)
    return {'x': out['x'], 'c': out['c'], 'ctx': out['ctx'], 'c_ctx': out['c_ctx'], 'w_ada': out['w_ada'], 'b_ada': out['b_ada'], 'w_in': out['w_in'], 'w_decay_up': out['w_decay_up'], 'b_decay_up': out['b_decay_up'], 'gla_norm_gain': out['gla_norm_gain'], 'w_pool_group': out['w_pool_group'], 'pool_scale': out['pool_scale'], 'w_gla_out': out['w_gla_out'], 'w_pool_out': out['w_pool_out'], 'w_out': out['w_out'], 'ln_mix_gain': out['ln_mix_gain'], 'ln_mix_bias': out['ln_mix_bias'], 'w_ffn_in': out['w_ffn_in'], 'w_ffn_out': out['w_ffn_out'], 'ln_ffn_gain': out['ln_ffn_gain'], 'ln_ffn_bias': out['ln_ffn_bias'], 'loss_target': out['loss_target'], 'm_c_ctx': out['m_c_ctx'], 'm_w_ada': out['m_w_ada'], 'm_b_ada': out['m_b_ada'], 'm_w_in': out['m_w_in'], 'm_w_decay_up': out['m_w_decay_up'], 'm_b_decay_up': out['m_b_decay_up'], 'm_gla_norm_gain': out['m_gla_norm_gain'], 'm_w_pool_group': out['m_w_pool_group'], 'm_pool_scale': out['m_pool_scale'], 'm_w_gla_out': out['m_w_gla_out'], 'm_w_pool_out': out['m_w_pool_out'], 'm_w_out': out['m_w_out'], 'm_ln_mix_gain': out['m_ln_mix_gain'], 'm_ln_mix_bias': out['m_ln_mix_bias'], 'm_w_ffn_in': out['m_w_ffn_in'], 'm_w_ffn_out': out['m_w_ffn_out'], 'm_ln_ffn_gain': out['m_ln_ffn_gain'], 'm_ln_ffn_bias': out['m_ln_ffn_bias'], 'v_c_ctx': out['v_c_ctx'], 'v_w_ada': out['v_w_ada'], 'v_b_ada': out['v_b_ada'], 'v_w_in': out['v_w_in'], 'v_w_decay_up': out['v_w_decay_up'], 'v_b_decay_up': out['v_b_decay_up'], 'v_gla_norm_gain': out['v_gla_norm_gain'], 'v_w_pool_group': out['v_w_pool_group'], 'v_pool_scale': out['v_pool_scale'], 'v_w_gla_out': out['v_w_gla_out'], 'v_w_pool_out': out['v_w_pool_out'], 'v_w_out': out['v_w_out'], 'v_ln_mix_gain': out['v_ln_mix_gain'], 'v_ln_mix_bias': out['v_ln_mix_bias'], 'v_w_ffn_in': out['v_w_ffn_in'], 'v_w_ffn_out': out['v_w_ffn_out'], 'v_ln_ffn_gain': out['v_ln_ffn_gain'], 'v_ln_ffn_bias': out['v_ln_ffn_bias']}


def _loss(weights, diff, rest, loss_target):
    with _jax.named_scope("forward"):
        args = {**rest, TWIN_DIFF_INPUT: diff, **{k: w.astype(_WEIGHT_DTYPES[k]) for k, w in weights.items()}}
        y = _forward(args)
    with _jax.named_scope("loss_head"):
        err = _jnp.square(y.astype(_jnp.float32) - loss_target)
        return 0.5 * _jnp.sum(_jnp.mean(err, axis=-1)) if err.ndim else 0.5 * err


def _adamw(w, g, m, v):
    m = ADAM_B1 * m + (1.0 - ADAM_B1) * g
    v = ADAM_B2 * v + (1.0 - ADAM_B2) * _jnp.square(g)
    m_hat = m / (1.0 - ADAM_B1 ** ADAM_STEP)
    v_hat = v / (1.0 - ADAM_B2 ** ADAM_STEP)
    delta = -ADAM_LR * (m_hat / (_jnp.sqrt(v_hat) + ADAM_EPS) + ADAM_WD * w)
    return delta, m, v


def reference(x, c, ctx, c_ctx, w_ada, b_ada, w_in, w_decay_up, b_decay_up, gla_norm_gain, w_pool_group, pool_scale, w_gla_out, w_pool_out, w_out, ln_mix_gain, ln_mix_bias, w_ffn_in, w_ffn_out, ln_ffn_gain, ln_ffn_bias, loss_target, m_c_ctx, m_w_ada, m_b_ada, m_w_in, m_w_decay_up, m_b_decay_up, m_gla_norm_gain, m_w_pool_group, m_pool_scale, m_w_gla_out, m_w_pool_out, m_w_out, m_ln_mix_gain, m_ln_mix_bias, m_w_ffn_in, m_w_ffn_out, m_ln_ffn_gain, m_ln_ffn_bias, v_c_ctx, v_w_ada, v_b_ada, v_w_in, v_w_decay_up, v_b_decay_up, v_gla_norm_gain, v_w_pool_group, v_pool_scale, v_w_gla_out, v_w_pool_out, v_w_out, v_ln_mix_gain, v_ln_mix_bias, v_w_ffn_in, v_w_ffn_out, v_ln_ffn_gain, v_ln_ffn_bias):
    given = dict(x=x, c=c, ctx=ctx, c_ctx=c_ctx, w_ada=w_ada, b_ada=b_ada, w_in=w_in, w_decay_up=w_decay_up, b_decay_up=b_decay_up, gla_norm_gain=gla_norm_gain, w_pool_group=w_pool_group, pool_scale=pool_scale, w_gla_out=w_gla_out, w_pool_out=w_pool_out, w_out=w_out, ln_mix_gain=ln_mix_gain, ln_mix_bias=ln_mix_bias, w_ffn_in=w_ffn_in, w_ffn_out=w_ffn_out, ln_ffn_gain=ln_ffn_gain, ln_ffn_bias=ln_ffn_bias, loss_target=loss_target, m_c_ctx=m_c_ctx, m_w_ada=m_w_ada, m_b_ada=m_b_ada, m_w_in=m_w_in, m_w_decay_up=m_w_decay_up, m_b_decay_up=m_b_decay_up, m_gla_norm_gain=m_gla_norm_gain, m_w_pool_group=m_w_pool_group, m_pool_scale=m_pool_scale, m_w_gla_out=m_w_gla_out, m_w_pool_out=m_w_pool_out, m_w_out=m_w_out, m_ln_mix_gain=m_ln_mix_gain, m_ln_mix_bias=m_ln_mix_bias, m_w_ffn_in=m_w_ffn_in, m_w_ffn_out=m_w_ffn_out, m_ln_ffn_gain=m_ln_ffn_gain, m_ln_ffn_bias=m_ln_ffn_bias, v_c_ctx=v_c_ctx, v_w_ada=v_w_ada, v_b_ada=v_b_ada, v_w_in=v_w_in, v_w_decay_up=v_w_decay_up, v_b_decay_up=v_b_decay_up, v_gla_norm_gain=v_gla_norm_gain, v_w_pool_group=v_w_pool_group, v_pool_scale=v_pool_scale, v_w_gla_out=v_w_gla_out, v_w_pool_out=v_w_pool_out, v_w_out=v_w_out, v_ln_mix_gain=v_ln_mix_gain, v_ln_mix_bias=v_ln_mix_bias, v_w_ffn_in=v_w_ffn_in, v_w_ffn_out=v_w_ffn_out, v_ln_ffn_gain=v_ln_ffn_gain, v_ln_ffn_bias=v_ln_ffn_bias)
    weights = {n: given[n] for n in TWIN_WEIGHTS}
    shared = {n: given[n] for n in SHARED_INPUTS}
    per_example = {n: given[n] for n in ['x', 'c', 'ctx']}
    grad_fn = _jax.value_and_grad(_loss, argnums=(0, 1))

    def one_microbatch(ex, loss_target):
        ex = dict(ex)
        diff = ex.pop(TWIN_DIFF_INPUT)
        return grad_fn(weights, diff, {**shared, **ex}, loss_target)

    if N_MICROBATCH == 1:
        loss, (grad_w, grad_x) = one_microbatch(per_example, given["loss_target"])
    else:
        def body(carry, xs):
            loss_sum, grad_sum = carry
            l_k, (gw_k, gx_k) = one_microbatch(xs[0], xs[1])
            with _jax.named_scope("update"):
                return (loss_sum + l_k, _jax.tree.map(_jnp.add, grad_sum, gw_k)), gx_k

        init = (_jnp.zeros((), _jnp.float32), _jax.tree.map(_jnp.zeros_like, weights))
        (loss, grad_w), grad_x = _jax.lax.scan(body, init, (per_example, given["loss_target"]))
    with _jax.named_scope("update"):
        delta_w, new_m, new_v = {}, {}, {}
        for n in TWIN_WEIGHTS:
            delta_w[n], new_m[n], new_v[n] = _adamw(weights[n], grad_w[n], given["m_" + n], given["v_" + n])
    return (loss, grad_x, *[grad_w[n] for n in TWIN_WEIGHTS], *[delta_w[n] for n in TWIN_WEIGHTS],
            *[new_m[n] for n in TWIN_WEIGHTS], *[new_v[n] for n in TWIN_WEIGHTS])
```

```python
import functools

import jax
import jax.numpy as jnp
from jax import lax
from jax.experimental import pallas as pl
from jax.experimental.pallas import tpu as pltpu

F32 = jnp.float32
MXU_DTYPE = jnp.bfloat16
WIRE_DTYPE = jnp.bfloat16

GRID_W = 64
CHUNK = 64
N_HEADS = 4
GATE_RANK = 16
GATE_NORM = 16.0
N_MOD = 6
N_POOL = 4
LN_EPS = 1e-5
RMS_EPS = 1e-6
ADAM_LR = 0.001
ADAM_B1 = 0.9
ADAM_B2 = 0.999
ADAM_EPS = 1e-08
ADAM_WD = 0.01
ADAM_STEP = 10

LANES = 128
SUBLANES = 8
ALR_W = 256
PACK_W = 2048
VMEM_LIMIT = 56 * 1024 * 1024
N_CHIPS = 4
N_DEV = 8
MESH = pl.DeviceIdType.MESH

NN = ((1,), (0,))
NT = ((1,), (1,))
TN = ((0,), (0,))


def _dot(a, b, dims):
    return lax.dot_general(a.astype(MXU_DTYPE), b.astype(MXU_DTYPE), (dims, ((), ())),
                           preferred_element_type=F32)


def _dot_f32(a, b, dims):
    return lax.dot_general(a.astype(F32), b.astype(F32), (dims, ((), ())),
                           precision=lax.Precision.HIGHEST, preferred_element_type=F32)


def _dot_mask(mask, x, dims):
    m = mask.astype(MXU_DTYPE)
    if MXU_DTYPE == F32:
        return lax.dot_general(m, x, (dims, ((), ())), preferred_element_type=F32)
    acc = None
    rest = x
    for _ in range(3):
        piece = rest.astype(MXU_DTYPE)
        rest = rest - piece.astype(F32)
        part = lax.dot_general(m, piece, (dims, ((), ())), preferred_element_type=F32)
        acc = part if acc is None else acc + part
    return acc


def _pick(n, cands):
    for c in cands:
        if n % c == 0:
            return c
    return n


def _params():
    return pltpu.CompilerParams(vmem_limit_bytes=VMEM_LIMIT)


def _sigmoid(x):
    e = jnp.exp(-jnp.abs(x))
    return jnp.where(x >= 0, 1.0 / (1.0 + e), e / (1.0 + e))


def _silu(x):
    return x * _sigmoid(x)


def _dsilu(x):
    s = _sigmoid(x)
    return s * (1.0 + x * (1.0 - s))


def matmul(name, a, b, form, m, n, k, *, tm, tn, out_dtype=F32, a_off=(0, 0), b_off=(0, 0), bias=None, bias_off=0,
           precise=False):
    assert m % tm == 0 and n % tn == 0, (name, m, n, tm, tn)
    if form == 'tn':
        a_spec = pl.BlockSpec((k, tm), lambda j, i: (a_off[0], i + a_off[1]))
    else:
        a_spec = pl.BlockSpec((tm, k), lambda j, i: (i + a_off[0], a_off[1]))
    if form == 'nt':
        b_spec = pl.BlockSpec((tn, k), lambda j, i: (j + b_off[0], b_off[1]))
    else:
        b_spec = pl.BlockSpec((k, tn), lambda j, i: (b_off[0], j + b_off[1]))
    dims = {'nn': NN, 'nt': NT, 'tn': TN}[form]
    in_specs = [a_spec, b_spec]
    args = [a, b]
    if bias is not None:
        in_specs.append(pl.BlockSpec((SUBLANES, tn), lambda j, i: (0, j + bias_off)))
        args.append(bias)

    def body(*refs):
        a_ref, b_ref = refs[0], refs[1]
        o_ref = refs[-1]
        acc = (_dot_f32 if precise else _dot)(a_ref[...], b_ref[...], dims)
        if bias is not None:
            acc = acc + refs[2][0:1, :]
        o_ref[...] = acc.astype(o_ref.dtype)

    return pl.pallas_call(
        body, name=name, grid=(n // tn, m // tm), in_specs=in_specs,
        out_specs=pl.BlockSpec((tm, tn), lambda j, i: (i, j)),
        out_shape=jax.ShapeDtypeStruct((m, n), out_dtype), compiler_params=_params(),
    )(*args)


def ew(name, body, nrow, ncol, tm, tn, row_ins, vec_ins, row_outs, sum_outs=(), ctx_tiles=1):
    def rmap(roff):
        return roff if callable(roff) else (lambda i: i + roff)

    in_specs = []
    for arr, roff, coff in row_ins:
        in_specs.append(pl.BlockSpec((tm, tn), functools.partial(lambda j, i, r, c: (r(i), j + c), r=rmap(roff), c=coff)))
    for arr, coff in vec_ins:
        in_specs.append(pl.BlockSpec((SUBLANES, tn), functools.partial(lambda j, i, c: (0, j + c), c=coff)))
    out_specs = [pl.BlockSpec((tm, tn), lambda j, i: (i, j)) for _ in row_outs]
    out_specs += [pl.BlockSpec((SUBLANES, tn), lambda j, i: (0, j)) for _ in sum_outs]
    out_shape = [jax.ShapeDtypeStruct((r, c), dt) for dt, r, c in row_outs]
    out_shape += [jax.ShapeDtypeStruct((SUBLANES, c), F32) for c, _ in sum_outs]
    n_row, n_vec, n_ro = len(row_ins), len(vec_ins), len(row_outs)

    def kern(*refs):
        j, i = pl.program_id(0), pl.program_id(1)
        rows = [r[...] for r in refs[:n_row]]
        vecs = [r[...] for r in refs[n_row:n_row + n_vec]]
        outs = refs[n_row + n_vec:]
        is_ctx = i < ctx_tiles
        res, sums = body(is_ctx, j, rows, vecs)
        for ref, val in zip(outs[:n_ro], res):
            ref[...] = val.astype(ref.dtype)
        for ref, val, (_, by_class) in zip(outs[n_ro:], sums, sum_outs):
            @pl.when(i == 0)
            def _():
                ref[...] = jnp.zeros_like(ref)
            if by_class:
                ref[0:1, :] += jnp.where(is_ctx, val, 0.0)
                ref[1:2, :] += jnp.where(is_ctx, 0.0, val)
            else:
                ref[0:1, :] += val

    outs = pl.pallas_call(
        kern, name=name, grid=(ncol, nrow), in_specs=in_specs, out_specs=out_specs, out_shape=out_shape,
        compiler_params=_params(),
    )(*[a for a, _, _ in row_ins], *[a for a, _ in vec_ins])
    return list(outs)


def _cls(vec, is_ctx):
    return jnp.where(is_ctx, vec[0:1, :], vec[1:2, :])


def _colsum(x):
    return jnp.sum(x, axis=0, keepdims=True)


def _chunk_map(cfg, rev):
    nctx, nc = cfg.CL // CHUNK, cfg.R // CHUNK
    if not rev:
        return lambda s: s
    return lambda s: jnp.where(s < nctx, nctx - 1 - s, nctx + nc - 1 - s)


def _gla_chunk(q_ref, k_ref, a_ref, wd_ref, bd_ref, rev, scale):
    q = q_ref[...] * scale
    k = k_ref[...]
    z = _dot_f32(a_ref[...], wd_ref[...], NN) + bd_ref[0:1, :]
    la = (jnp.minimum(z, 0.0) - jnp.log(1.0 + jnp.exp(-jnp.abs(z)))) * (1.0 / GATE_NORM)
    r = lax.broadcasted_iota(jnp.int32, (CHUNK, CHUNK), 0)
    c = lax.broadcasted_iota(jnp.int32, (CHUNK, CHUNK), 1)
    keep = (r <= c) if rev else (r >= c)
    tri = keep.astype(F32)
    cum = _dot_f32(tri, la, NN)
    mid = CHUNK // 2 if rev else CHUNK // 2 - 1
    end = 0 if rev else CHUNK - 1
    ref = cum[mid:mid + 1, :]
    last = cum[end:end + 1, :]
    return dict(q=q, k=k, z=z, keep=keep, tri=tri, q_in=q * jnp.exp(cum - ref), k_in=k * jnp.exp(ref - cum),
                e_q=jnp.exp(cum), e_k=jnp.exp(last - cum), e_inq=jnp.exp(cum - ref), e_ink=jnp.exp(ref - cum),
                e_last=jnp.exp(last))


def _gla_in_specs(cfg, rows_of):
    hk, hv, dk = cfg.HK, cfg.HV, cfg.DK
    return [
        pl.BlockSpec((CHUNK, hk), lambda s, h: (rows_of(s), h)),
        pl.BlockSpec((CHUNK, hk), lambda s, h: (rows_of(s), dk // hk + h)),
        pl.BlockSpec((CHUNK, hv), lambda s, h: (rows_of(s), 2 * dk // hv + h)),
        pl.BlockSpec((CHUNK, ALR_W), lambda s, h: (rows_of(s), (cfg.NP - ALR_W) // ALR_W)),
        pl.BlockSpec((ALR_W, hk), lambda s, h: (0, h)),
        pl.BlockSpec((SUBLANES, hk), lambda s, h: (0, h)),
    ]


def gla_fwd(name, proj, wd, bd, cfg, rev):
    hk, hv = cfg.HK, cfg.HV
    nc = cfg.R // CHUNK
    cmap = _chunk_map(cfg, rev)
    scale = hk ** -0.5

    def body(q_ref, k_ref, v_ref, a_ref, wd_ref, bd_ref, o_ref, ss_ref, st_scr):
        s, h = pl.program_id(0), pl.program_id(1)

        @pl.when(s == 0)
        def _():
            st_scr[h] = jnp.zeros((hv, hk), F32)

        t = _gla_chunk(q_ref, k_ref, a_ref, wd_ref, bd_ref, rev, scale)
        v = v_ref[...]
        st = st_scr[h]
        ss_ref[0, 0] = st
        a = jnp.where(t['keep'], _dot(t['q_in'], t['k_in'], NT), 0.0)
        o_ref[...] = _dot(a, v, NN) + _dot(t['q'] * t['e_q'], st, NT)
        st_scr[h] = st * t['e_last'] + _dot(v, t['k'] * t['e_k'], TN)

    return pl.pallas_call(
        body, name=name, grid=(nc, N_HEADS), in_specs=_gla_in_specs(cfg, cmap),
        out_specs=[pl.BlockSpec((CHUNK, hv), lambda s, h: (cmap(s), h)),
                   pl.BlockSpec((1, 1, hv, hk), lambda s, h: (s, h, 0, 0))],
        out_shape=[jax.ShapeDtypeStruct((cfg.R, cfg.DV), F32),
                   jax.ShapeDtypeStruct((nc, N_HEADS, hv, hk), F32)],
        scratch_shapes=[pltpu.VMEM((N_HEADS, hv, hk), F32)], compiler_params=_params(),
    )(proj, proj, proj, proj, wd, bd)


def gla_bwd(name, proj, wd, bd, states, d_o, cfg, rev, addends=None):
    hk, hv = cfg.HK, cfg.HV
    nc = cfg.R // CHUNK
    cmap = _chunk_map(cfg, rev)
    rows_of = lambda g: cmap(nc - 1 - g)
    scale = hk ** -0.5
    n_add = 0 if addends is None else 4

    def body(*refs):
        q_ref, k_ref, v_ref, a_ref, wd_ref, bd_ref, ss_ref, do_ref = refs[:8]
        adds = refs[8:8 + n_add]
        dq_ref, dk_ref, dv_ref, da_ref, dwd_ref, dbd_ref, dst_scr = refs[8 + n_add:]
        g, h = pl.program_id(0), pl.program_id(1)

        @pl.when(g == 0)
        def _():
            dst_scr[h] = jnp.zeros((hv, hk), F32)
            dwd_ref[h] = jnp.zeros((ALR_W, hk), F32)
            dbd_ref[h] = jnp.zeros((SUBLANES, hk), F32)

        t = _gla_chunk(q_ref, k_ref, a_ref, wd_ref, bd_ref, rev, scale)
        v = v_ref[...]
        d_out = do_ref[...]
        st = ss_ref[0, 0]
        dst = dst_scr[h]
        q_int = t['q'] * t['e_q']
        k_st = t['k'] * t['e_k']
        a = jnp.where(t['keep'], _dot(t['q_in'], t['k_in'], NT), 0.0)
        da = jnp.where(t['keep'], _dot(d_out, v, NT), 0.0)
        dv = _dot(a, d_out, TN) + _dot(k_st, dst, NT)
        dq = _dot(d_out, st, NN) * t['e_q'] + _dot(da, t['k_in'], NN) * t['e_inq']
        dk = _dot(v, dst, NN) * t['e_k'] + _dot(da, t['q_in'], TN) * t['e_ink']
        dst_scr[h] = dst * t['e_last'] + _dot(d_out, q_int, TN)
        dg = t['q'] * dq - t['k'] * dk
        st_end = st * t['e_last'] + _dot(v, k_st, TN)
        dla = _dot_f32(t['tri'], dg, TN) + _colsum(dst * st_end)
        dz = dla * (1.0 / GATE_NORM) * _sigmoid(-t['z'])
        dalr = _dot_f32(dz, wd_ref[...], NT)
        dwd_ref[h] += _dot_f32(a_ref[...], dz, TN)
        dbd_ref[h] += jnp.broadcast_to(_colsum(dz), (SUBLANES, hk))
        dq = dq * scale
        if n_add:
            dq, dk, dv = dq + adds[0][...], dk + adds[1][...], dv + adds[2][...]
        dq_ref[...] = dq
        dk_ref[...] = dk
        dv_ref[...] = dv

        @pl.when(h == 0)
        def _():
            da_ref[...] = adds[3][...] if n_add else jnp.zeros_like(da_ref)

        da_ref[...] += dalr

    qk_spec = pl.BlockSpec((CHUNK, hk), lambda g, h: (rows_of(g), h))
    v_spec = pl.BlockSpec((CHUNK, hv), lambda g, h: (rows_of(g), h))
    a_spec = pl.BlockSpec((CHUNK, ALR_W), lambda g, h: (rows_of(g), 0))
    in_specs = _gla_in_specs(cfg, rows_of) + [
        pl.BlockSpec((1, 1, hv, hk), lambda g, h: (nc - 1 - g, h, 0, 0)), v_spec]
    args = [proj, proj, proj, proj, wd, bd, states, d_o]
    if n_add:
        in_specs += [qk_spec, qk_spec, v_spec, a_spec]
        args += list(addends)
    return pl.pallas_call(
        body, name=name, grid=(nc, N_HEADS), in_specs=in_specs,
        out_specs=[qk_spec, qk_spec, v_spec, a_spec,
                   pl.BlockSpec((N_HEADS, ALR_W, hk), lambda g, h: (0, 0, 0)),
                   pl.BlockSpec((N_HEADS, SUBLANES, hk), lambda g, h: (0, 0, 0))],
        out_shape=[jax.ShapeDtypeStruct((cfg.R, cfg.DK), F32), jax.ShapeDtypeStruct((cfg.R, cfg.DK), F32),
                   jax.ShapeDtypeStruct((cfg.R, cfg.DV), F32), jax.ShapeDtypeStruct((cfg.R, ALR_W), F32),
                   jax.ShapeDtypeStruct((N_HEADS, ALR_W, hk), F32),
                   jax.ShapeDtypeStruct((N_HEADS, SUBLANES, hk), F32)],
        scratch_shapes=[pltpu.VMEM((N_HEADS, hv, hk), F32)],
        compiler_params=_params(),
    )(*args)


def pool_mix(name, src, coff, cfg, transpose):
    tm, pg = cfg.TM, cfg.PG
    mt = cfg.R // tm
    reach = -(-(max(2 ** N_POOL // 2, 1) * GRID_W) // tm)
    nk = 2 * reach + 1
    img_rows = cfg.S // GRID_W
    shift = GRID_W.bit_length() - 1

    def ktile(m, d):
        return jnp.where(m == 0, 0, jnp.clip(m + d - reach, 1, mt - 1))

    def counts(idx, is_ctx, lo, hi):
        ctx_n = jnp.minimum(idx + hi + 1, cfg.CL) - jnp.maximum(idx - lo, 0)
        r, c = idx >> shift, idx & (GRID_W - 1)
        lat_n = ((jnp.minimum(r + hi + 1, img_rows) - jnp.maximum(r - lo, 0))
                 * (jnp.minimum(c + hi + 1, GRID_W) - jnp.maximum(c - lo, 0)))
        return jnp.where(is_ctx, ctx_n, lat_n).astype(F32)

    def body(src_ref, self_ref, o_ref, acc):
        g, m, d = pl.program_id(0), pl.program_id(1), pl.program_id(2)
        lo = jnp.left_shift(1, g)
        hi = lo - 1
        is_ctx = m == 0
        kt = m + d - reach
        valid = jnp.where(is_ctx, d == reach, (kt >= 1) & (kt <= mt - 1))
        seg = jnp.where(is_ctx, 0, cfg.CL)

        @pl.when(d == 0)
        def _():
            acc[...] = jnp.zeros_like(acc)

        @pl.when(valid)
        def _():
            row = lax.broadcasted_iota(jnp.int32, (tm, tm), 0) + (m * tm - seg)
            col = lax.broadcasted_iota(jnp.int32, (tm, tm), 1) + (kt * tm - seg)
            ctr, mem = (col, row) if transpose else (row, col)
            in_ctx = (mem >= ctr - lo) & (mem <= ctr + hi)
            cr, cc = ctr >> shift, ctr & (GRID_W - 1)
            mr, mc = mem >> shift, mem & (GRID_W - 1)
            in_lat = (mr >= cr - lo) & (mr <= cr + hi) & (mc >= cc - lo) & (mc <= cc + hi)
            mask = jnp.where(is_ctx, in_ctx.astype(F32), in_lat.astype(F32))
            x = src_ref[...]
            if transpose:
                kidx = lax.broadcasted_iota(jnp.int32, (tm, 1), 0) + (kt * tm - seg)
                x = x / counts(kidx, is_ctx, lo, hi)
            acc[...] += _dot_mask(mask, x, NN)

        @pl.when(d == nk - 1)
        def _():
            res = acc[...]
            if not transpose:
                midx = lax.broadcasted_iota(jnp.int32, (tm, 1), 0) + (m * tm - seg)
                res = res / counts(midx, is_ctx, lo, hi)
            o_ref[...] = (res - self_ref[...]).astype(o_ref.dtype)

    return pl.pallas_call(
        body, name=name, grid=(N_POOL, mt, nk),
        in_specs=[pl.BlockSpec((tm, pg), lambda g, m, d: (ktile(m, d), coff + g)),
                  pl.BlockSpec((tm, pg), lambda g, m, d: (m, coff + g))],
        out_specs=pl.BlockSpec((tm, pg), lambda g, m, d: (m, g)),
        out_shape=jax.ShapeDtypeStruct((cfg.R, cfg.DP), F32),
        scratch_shapes=[pltpu.VMEM((tm, pg), F32)], compiler_params=_params(),
    )(src, src)


def _my_place():
    return lax.axis_index("x"), lax.axis_index("y"), lax.axis_index("c")


def _flip(v, bit):
    return 1 - v if bit else v


def all_gather8(name, block):
    rows, w = block.shape

    def body(x_ref, out_ref, send_sems, recv_sems, local_sem):
        x, y, c = _my_place()
        me = 4 * x + 2 * y + c
        mine = pltpu.make_async_copy(x_ref, out_ref.at[me], local_sem)
        mine.start()
        sends = []
        for k in range(1, N_DEV):
            peer = (_flip(x, k & 4), _flip(y, k & 2), _flip(c, k & 1))
            cp = pltpu.make_async_remote_copy(src_ref=x_ref, dst_ref=out_ref.at[me], send_sem=send_sems.at[k - 1],
                                              recv_sem=recv_sems.at[k - 1], device_id=peer, device_id_type=MESH)
            cp.start()
            sends.append(cp)
        for k in range(1, N_DEV):
            peer = (_flip(x, k & 4), _flip(y, k & 2), _flip(c, k & 1))
            slot = 4 * peer[0] + 2 * peer[1] + peer[2]
            pltpu.make_async_remote_copy(src_ref=x_ref, dst_ref=out_ref.at[slot], send_sem=send_sems.at[k - 1],
                                         recv_sem=recv_sems.at[k - 1], device_id=peer, device_id_type=MESH).wait_recv()
        for cp in sends:
            cp.wait_send()
        mine.wait()

    return pl.pallas_call(
        body, name=name, out_shape=jax.ShapeDtypeStruct((N_DEV, rows, w), block.dtype),
        in_specs=[pl.BlockSpec(memory_space=pl.ANY)], out_specs=pl.BlockSpec(memory_space=pl.ANY),
        scratch_shapes=[pltpu.SemaphoreType.DMA((N_DEV - 1,)), pltpu.SemaphoreType.DMA((N_DEV - 1,)),
                        pltpu.SemaphoreType.DMA],
    )(block)


def gather_chips(name, shards):
    n = len(shards)

    def body(*refs):
        srcs, dsts = refs[:n], refs[n:2 * n]
        send_sems, recv_sems, local_sems = refs[2 * n:]
        x, y, c = _my_place()
        me = 2 * x + y
        local = [pltpu.make_async_copy(srcs[t], dsts[t].at[me], local_sems.at[t]) for t in range(n)]
        for cp in local:
            cp.start()
        sends = []
        for k in range(1, N_CHIPS):
            peer = (_flip(x, k & 2), _flip(y, k & 1), c)
            for t in range(n):
                cp = pltpu.make_async_remote_copy(
                    src_ref=srcs[t], dst_ref=dsts[t].at[me], send_sem=send_sems.at[t * 3 + k - 1],
                    recv_sem=recv_sems.at[t * 3 + k - 1], device_id=peer, device_id_type=MESH)
                cp.start()
                sends.append(cp)
        for k in range(1, N_CHIPS):
            peer = (_flip(x, k & 2), _flip(y, k & 1), c)
            slot = 2 * peer[0] + peer[1]
            for t in range(n):
                pltpu.make_async_remote_copy(
                    src_ref=srcs[t], dst_ref=dsts[t].at[slot], send_sem=send_sems.at[t * 3 + k - 1],
                    recv_sem=recv_sems.at[t * 3 + k - 1], device_id=peer, device_id_type=MESH).wait_recv()
        for cp in sends:
            cp.wait_send()
        for cp in local:
            cp.wait()

    return pl.pallas_call(
        body, name=name, out_shape=[jax.ShapeDtypeStruct((N_CHIPS,) + s.shape, s.dtype) for s in shards],
        in_specs=[pl.BlockSpec(memory_space=pl.ANY)] * n, out_specs=[pl.BlockSpec(memory_space=pl.ANY)] * n,
        scratch_shapes=[pltpu.SemaphoreType.DMA((3 * n,)), pltpu.SemaphoreType.DMA((3 * n,)),
                        pltpu.SemaphoreType.DMA((n,))],
    )(*shards)


def scatter_chips(name, bufs):
    n = len(bufs)

    def body(*refs):
        srcs, dsts = refs[:n], refs[n:2 * n]
        send_sems, recv_sems = refs[2 * n:]
        x, y, c = _my_place()
        copies = []
        for k in range(1, N_CHIPS):
            peer = (_flip(x, k & 2), _flip(y, k & 1), c)
            for t in range(n):
                cp = pltpu.make_async_remote_copy(
                    src_ref=srcs[t].at[k - 1], dst_ref=dsts[t].at[k - 1], send_sem=send_sems.at[t * 3 + k - 1],
                    recv_sem=recv_sems.at[t * 3 + k - 1], device_id=peer, device_id_type=MESH)
                cp.start()
                copies.append(cp)
        for cp in copies:
            cp.wait_recv()
        for cp in copies:
            cp.wait_send()

    return pl.pallas_call(
        body, name=name, out_shape=[jax.ShapeDtypeStruct(b.shape, b.dtype) for b in bufs],
        in_specs=[pl.BlockSpec(memory_space=pl.ANY)] * n, out_specs=[pl.BlockSpec(memory_space=pl.ANY)] * n,
        scratch_shapes=[pltpu.SemaphoreType.DMA((3 * n,)), pltpu.SemaphoreType.DMA((3 * n,))],
    )(*bufs)


def swap_cores(name, bufs):
    n = len(bufs)

    def body(*refs):
        srcs, dsts = refs[:n], refs[n:2 * n]
        send_sems, recv_sems = refs[2 * n:]
        x, y, c = _my_place()
        copies = []
        for t in range(n):
            cp = pltpu.make_async_remote_copy(
                src_ref=srcs[t], dst_ref=dsts[t], send_sem=send_sems.at[t], recv_sem=recv_sems.at[t],
                device_id=(x, y, 1 - c), device_id_type=MESH)
            cp.start()
            copies.append(cp)
        for cp in copies:
            cp.wait_recv()
        for cp in copies:
            cp.wait_send()

    return pl.pallas_call(
        body, name=name, out_shape=[jax.ShapeDtypeStruct(b.shape, b.dtype) for b in bufs],
        in_specs=[pl.BlockSpec(memory_space=pl.ANY)] * n, out_specs=[pl.BlockSpec(memory_space=pl.ANY)] * n,
        scratch_shapes=[pltpu.SemaphoreType.DMA((n,)), pltpu.SemaphoreType.DMA((n,))],
    )(*bufs)


def _vec8(*rows):
    w = rows[0].shape[-1]
    out = jnp.zeros((SUBLANES, w), F32)
    for r, v in enumerate(rows):
        out = out.at[r].set(v.reshape(w).astype(F32))
    return out


def _pack(arrays):
    parts = []
    for a in arrays:
        flat = a.reshape(-1).astype(F32)
        pad = (-flat.shape[0]) % PACK_W
        parts.append(jnp.pad(flat, (0, pad)))
    flat = jnp.concatenate(parts)
    pad = (-flat.shape[0]) % (PACK_W * SUBLANES)
    return jnp.pad(flat, (0, pad)).reshape(-1, PACK_W)


def _unpack(packed, shapes):
    flat = packed.reshape(-1)
    out, pos = [], 0
    for shp in shapes:
        size = 1
        for d in shp:
            size *= d
        out.append(flat[pos:pos + size].reshape(shp))
        pos += size + (-size) % PACK_W
    return out


def _rows2d(a):
    return a.reshape(-1, a.shape[-1])


class _Cfg:
    pass


def _adam(name, w, grads, m, v):
    rows, width = w.shape
    tm = _pick(rows, (128, 64, 32, 16, 8))
    c1 = 1.0 - ADAM_B1 ** ADAM_STEP
    c2 = 1.0 - ADAM_B2 ** ADAM_STEP

    def body(i, j, rows_, vecs):
        wv, mv, vv = rows_[0], rows_[1], rows_[2]
        g = rows_[3]
        for extra in rows_[4:]:
            g = g + extra
        m_new = ADAM_B1 * mv + (1.0 - ADAM_B1) * g
        v_new = ADAM_B2 * vv + (1.0 - ADAM_B2) * (g * g)
        delta = -ADAM_LR * ((m_new / c1) / (jnp.sqrt(v_new / c2) + ADAM_EPS) + ADAM_WD * wv)
        return [g, delta, m_new, v_new], []

    return ew(name, body, rows // tm, 1, tm, width, [(w, 0, 0), (m, 0, 0), (v, 0, 0)] + [(g, 0, 0) for g in grads], [],
              [(F32, rows, width)] * 4)


def _sum_rows(name, arr, nparts, rows, width, dtype=F32):
    tm = _pick(rows, (256, 128, 64, 32, 16, 8))
    nblk = rows // tm

    def body(i, j, rows_, vecs):
        acc = rows_[0].astype(F32)
        for r in rows_[1:]:
            acc = acc + r.astype(F32)
        return [acc], []

    return ew(name, body, nblk, 1, tm, width, [(arr, p * nblk, 0) for p in range(nparts)], [], [(dtype, rows, width)])[0]


def kernel(x, c, ctx, c_ctx, w_ada, b_ada, w_in, w_decay_up, b_decay_up, gla_norm_gain, w_pool_group, pool_scale, w_gla_out, w_pool_out, w_out, ln_mix_gain, ln_mix_bias, w_ffn_in, w_ffn_out, ln_ffn_gain, ln_ffn_bias, loss_target, m_c_ctx, m_w_ada, m_b_ada, m_w_in, m_w_decay_up, m_b_decay_up, m_gla_norm_gain, m_w_pool_group, m_pool_scale, m_w_gla_out, m_w_pool_out, m_w_out, m_ln_mix_gain, m_ln_mix_bias, m_w_ffn_in, m_w_ffn_out, m_ln_ffn_gain, m_ln_ffn_bias, v_c_ctx, v_w_ada, v_b_ada, v_w_in, v_w_decay_up, v_b_decay_up, v_gla_norm_gain, v_w_pool_group, v_pool_scale, v_w_gla_out, v_w_pool_out, v_w_out, v_ln_mix_gain, v_ln_mix_bias, v_w_ffn_in, v_w_ffn_out, v_ln_ffn_gain, v_ln_ffn_bias):
    weights = dict(c_ctx=c_ctx, w_ada=w_ada, b_ada=b_ada, w_in=w_in, w_decay_up=w_decay_up, b_decay_up=b_decay_up,
                   gla_norm_gain=gla_norm_gain, w_pool_group=w_pool_group, pool_scale=pool_scale, w_gla_out=w_gla_out,
                   w_pool_out=w_pool_out, w_out=w_out, ln_mix_gain=ln_mix_gain, ln_mix_bias=ln_mix_bias,
                   w_ffn_in=w_ffn_in, w_ffn_out=w_ffn_out, ln_ffn_gain=ln_ffn_gain, ln_ffn_bias=ln_ffn_bias)
    mom1 = dict(c_ctx=m_c_ctx, w_ada=m_w_ada, b_ada=m_b_ada, w_in=m_w_in, w_decay_up=m_w_decay_up, b_decay_up=m_b_decay_up,
                gla_norm_gain=m_gla_norm_gain, w_pool_group=m_w_pool_group, pool_scale=m_pool_scale, w_gla_out=m_w_gla_out,
                w_pool_out=m_w_pool_out, w_out=m_w_out, ln_mix_gain=m_ln_mix_gain, ln_mix_bias=m_ln_mix_bias,
                w_ffn_in=m_w_ffn_in, w_ffn_out=m_w_ffn_out, ln_ffn_gain=m_ln_ffn_gain, ln_ffn_bias=m_ln_ffn_bias)
    mom2 = dict(c_ctx=v_c_ctx, w_ada=v_w_ada, b_ada=v_b_ada, w_in=v_w_in, w_decay_up=v_w_decay_up, b_decay_up=v_b_decay_up,
                gla_norm_gain=v_gla_norm_gain, w_pool_group=v_w_pool_group, pool_scale=v_pool_scale, w_gla_out=v_w_gla_out,
                w_pool_out=v_w_pool_out, w_out=v_w_out, ln_mix_gain=v_ln_mix_gain, ln_mix_bias=v_ln_mix_bias,
                w_ffn_in=v_w_ffn_in, w_ffn_out=v_w_ffn_out, ln_ffn_gain=v_ln_ffn_gain, ln_ffn_bias=v_ln_ffn_bias)
    names = list(weights)

    cfg = _Cfg()
    L, D = w_ada.shape[0], x.shape[-1]
    S, CL = x.shape[1], ctx.shape[1]
    cfg.L, cfg.D, cfg.S, cfg.CL, cfg.R, cfg.TM = L, D, S, CL, S + CL, CL
    DK, DV, DP = D // 2, D, D // 2
    cfg.DK, cfg.DV, cfg.DP = DK, DV, DP
    cfg.HK, cfg.HV, cfg.PG = DK // N_HEADS, DV // N_HEADS, DP // N_POOL
    DFF = w_ffn_out.shape[1] * N_CHIPS
    NP = 2 * DK + 2 * DV + DP + 2 * D + ALR_W
    cfg.NP, cfg.DFF = NP, DFF
    R, TM, HK, HV, PG = cfg.R, cfg.TM, cfg.HK, cfg.HV, cfg.PG
    MT = R // TM
    alpha = (2.0 * L) ** 0.25
    assert S % TM == 0 and TM % CHUNK == 0 and S % GRID_W == 0 and TM % GRID_W == 0
    OFF_G, OFF_P, OFF_BGA, OFF_BGB = 2 * DK + DV, 2 * DK + 2 * DV, 2 * DK + 2 * DV + DP, 2 * DK + 2 * DV + DP + D
    ALR0 = 2 * DK + 2 * DV
    TE = 512
    TL = TM // 2
    assert D % TE == 0 and DP % TE == 0

    xi, yi, ci = _my_place()
    chip = 2 * xi + yi
    dev = 4 * xi + 2 * yi + ci

    n_ada = w_ada.shape[-1]
    c_all = all_gather8("ag_cond", jnp.pad(c.reshape(1, D), ((0, SUBLANES - 1), (0, 0))))[:, 0, :]
    cond = jnp.concatenate([c_all, c_ctx.reshape(1, D), jnp.zeros((16 - N_DEV - 1, D), F32)], axis=0)

    def silu_body(i, j, rows_, vecs):
        return [_silu(rows_[0]), _dsilu(rows_[0])], []

    act, dact = ew("cond_silu", silu_body, 1, 1, 16, D, [(cond, 0, 0)], [], [(F32, 16, D)] * 2)
    w_ada2 = w_ada.reshape(L * D, n_ada)
    b_ada_mine = lax.dynamic_slice_in_dim(b_ada, chip * n_ada, n_ada, axis=1)
    tn_ada = _pick(n_ada, (1024, 512, 256, 128))
    mods = [matmul("ada_fwd", act, w_ada2, 'nn', 16, n_ada, D, tm=16, tn=tn_ada, b_off=(l, 0),
                   bias=_vec8(b_ada_mine[l]), precise=True) for l in range(L)]
    mods_all = all_gather8("ag_mods", jnp.concatenate(mods, axis=0))
    mods_all = mods_all[0::2].reshape(N_CHIPS, L, 16, n_ada).transpose(1, 2, 0, 3).reshape(L, 16, N_MOD * D)
    modv = [_vec8(mods_all[l, N_DEV], lax.dynamic_index_in_dim(mods_all[l], dev, 0, keepdims=False)) for l in range(L)]
    MB = D // TE

    big = ['w_in', 'w_gla_out', 'w_pool_out', 'w_out', 'w_ffn_in', 'w_ffn_out', 'w_pool_group']
    gathered = gather_chips("gather_weights", [weights[nm].astype(WIRE_DTYPE) for nm in big]
                            + [w_decay_up, b_decay_up])
    gw = dict(zip(big + ['w_decay_up', 'b_decay_up'], gathered))

    def cols_together(g):
        return jnp.moveaxis(g, 0, -2).reshape(g.shape[1:-1] + (N_CHIPS * g.shape[-1],))

    def rows_together(g):
        return jnp.moveaxis(g, 0, 1).reshape((g.shape[1], N_CHIPS * g.shape[2], g.shape[3]))

    win_ref = cols_together(gw['w_in'])
    win = jnp.concatenate([win_ref[..., :ALR0], win_ref[..., ALR0 + 2 * GATE_RANK:], win_ref[..., ALR0:ALR0 + 2 * GATE_RANK],
                           jnp.zeros((L, D, ALR_W - 2 * GATE_RANK), WIRE_DTYPE)], axis=-1).reshape(L * D, NP)
    wgo = rows_together(gw['w_gla_out']).reshape(L * DV, D)
    wpo = cols_together(gw['w_pool_out']).reshape(L * DP, D)
    wout = rows_together(gw['w_out']).reshape(L * D, D)
    wfi = cols_together(gw['w_ffn_in']).reshape(L * D, 2 * DFF)
    wfo = rows_together(gw['w_ffn_out']).reshape(L * DFF, D)
    wpg = jnp.moveaxis(gw['w_pool_group'], 0, 2).reshape(L, N_POOL, PG, PG)
    wpg_bd = jnp.zeros((L, N_POOL, PG, N_POOL, PG), WIRE_DTYPE)
    for g in range(N_POOL):
        wpg_bd = wpg_bd.at[:, g, :, g, :].set(wpg[:, g])
    wpg_bd = wpg_bd.reshape(L * DP, DP)
    wdu = cols_together(gw['w_decay_up'])
    bdu = cols_together(gw['b_decay_up'])
    wd_pad = [[jnp.zeros((ALR_W, DK), F32).at[d * GATE_RANK:(d + 1) * GATE_RANK].set(wdu[l, d]) for d in range(2)]
              for l in range(L)]
    bd_pad = [[_vec8(bdu[l, d]) for d in range(2)] for l in range(L)]

    tn_np = _pick(NP, (1280, 1024, 768, 512, 256, 128))
    tn_d = _pick(D, (1024, 512, 256, 128))
    tn_ff2 = _pick(2 * DFF, (1024, 512, 256, 128))
    tn_ff = _pick(DFF, (1024, 512, 256, 128))
    tn_dp = _pick(DP, (1024, 512, 256, 128))
    tw = lambda n_: _pick(n_, (512, 256, 128))

    gain8 = lambda v_: _vec8(v_)

    xs = jnp.concatenate([ctx.reshape(CL, D), x.reshape(S, D)], axis=0)
    saved = []
    for l in range(L):
        sv = {}
        mv = modv[l]

        def mod_body(i, j, rows_, vecs):
            return [rows_[0] * (1.0 + _cls(vecs[1], i)) + _cls(vecs[0], i)], []

        h1 = ew("modulate", mod_body, MT, D // tn_d, TM, tn_d, [(xs, 0, 0)],
                [(mv, 0 * (D // tn_d)), (mv, 1 * (D // tn_d))], [(MXU_DTYPE, R, D)])[0]
        proj = matmul("proj", h1, win, 'nn', R, NP, D, tm=TM, tn=tn_np, b_off=(l, 0))
        o_f, st_f = gla_fwd("gla_fwd", proj, wd_pad[l][0], bd_pad[l][0], cfg, False)
        o_b, st_b = gla_fwd("gla_rev", proj, wd_pad[l][1], bd_pad[l][1], cfg, True)

        def post_body(i, j, rows_, vecs):
            o = rows_[0] + rows_[1]
            on = o * lax.rsqrt(jnp.mean(o * o, axis=-1, keepdims=True) + RMS_EPS)
            return [on * vecs[0][0:1, :] * _silu(rows_[2])], []

        a_gla = ew("gla_post", post_body, MT, N_HEADS, TM, HV, [(o_f, 0, 0), (o_b, 0, 0), (proj, 0, OFF_G // HV)],
                   [(gain8(gla_norm_gain[l]), 0)], [(MXU_DTYPE, R, DV)])[0]
        y_gla = matmul("gla_out", a_gla, wgo, 'nn', R, D, DV, tm=TM, tn=tn_d, b_off=(l, 0))
        u_pool = pool_mix("pool_fwd", proj, OFF_P // PG, cfg, False)
        t_pool = matmul("pool_group", u_pool, wpg_bd, 'nn', R, DP, DP, tm=TM, tn=tn_dp, b_off=(l, 0))

        def scale_body(i, j, rows_, vecs):
            return [rows_[0] * vecs[0][0:1, :]], []

        ts_pool = ew("pool_scale", scale_body, MT, DP // TE, TM, TE, [(t_pool, 0, 0)], [(gain8(pool_scale[l]), 0)],
                     [(MXU_DTYPE, R, DP)])[0]
        y_pool = matmul("pool_out", ts_pool, wpo, 'nn', R, D, DP, tm=TM, tn=tn_d, b_off=(l, 0))

        def merge_body(i, j, rows_, vecs):
            return [_sigmoid(rows_[2]) * rows_[0] + _sigmoid(rows_[3]) * rows_[1]], []

        mg = ew("merge", merge_body, MT, D // TE, TM, TE,
                [(y_gla, 0, 0), (y_pool, 0, 0), (proj, 0, OFF_BGA // TE), (proj, 0, OFF_BGB // TE)], [],
                [(MXU_DTYPE, R, D)])[0]
        y_mix = matmul("mix_out", mg, wout, 'nn', R, D, D, tm=TM, tn=tn_d, b_off=(l, 0))

        def ln_body(i, j, rows_, vecs):
            r = alpha * rows_[0] + _cls(vecs[0], i) * rows_[1]
            mu = jnp.mean(r, axis=-1, keepdims=True)
            rc = r - mu
            var = jnp.mean(rc * rc, axis=-1, keepdims=True)
            return [rc * lax.rsqrt(var + LN_EPS) * vecs[1][0:1, :] + vecs[2][0:1, :]], []

        x1 = ew("resid_ln", ln_body, R // TL, 1, TL, D, [(xs, 0, 0), (y_mix, 0, 0)],
                [(mv, 2), (gain8(ln_mix_gain[l]), 0), (gain8(ln_mix_bias[l]), 0)], [(F32, R, D)], ctx_tiles=CL // TL)[0]
        h2 = ew("modulate", mod_body, MT, D // tn_d, TM, tn_d, [(x1, 0, 0)],
                [(mv, 3 * (D // tn_d)), (mv, 4 * (D // tn_d))], [(MXU_DTYPE, R, D)])[0]
        u_ffn = matmul("ffn_in", h2, wfi, 'nn', R, 2 * DFF, D, tm=TM, tn=tn_ff2, b_off=(l, 0))
        te_ff = _pick(DFF, (512, 256, 128))

        def swiglu_body(i, j, rows_, vecs):
            return [_silu(rows_[0]) * rows_[1]], []

        s_ffn = ew("swiglu", swiglu_body, MT, DFF // te_ff, TM, te_ff, [(u_ffn, 0, 0), (u_ffn, 0, DFF // te_ff)], [],
                   [(MXU_DTYPE, R, DFF)])[0]
        f_ffn = matmul("ffn_out", s_ffn, wfo, 'nn', R, D, DFF, tm=TM, tn=tn_d, b_off=(l, 0))
        x2 = ew("resid_ln", ln_body, R // TL, 1, TL, D, [(x1, 0, 0), (f_ffn, 0, 0)],
                [(mv, 5), (gain8(ln_ffn_gain[l]), 0), (gain8(ln_ffn_bias[l]), 0)], [(F32, R, D)], ctx_tiles=CL // TL)[0]
        sv.update(xs=xs, h1=h1, proj=proj, o_f=o_f, o_b=o_b, st_f=st_f, st_b=st_b, a_gla=a_gla, y_gla=y_gla, u_pool=u_pool,
                  t_pool=t_pool, ts_pool=ts_pool, y_pool=y_pool, mg=mg, y_mix=y_mix, x1=x1, h2=h2, u_ffn=u_ffn, s_ffn=s_ffn,
                  f_ffn=f_ffn)
        saved.append(sv)
        xs = x2

    tgt = loss_target.reshape(S, D)

    def loss_body(i, j, rows_, vecs):
        d = jnp.where(i, 0.0, rows_[0] - rows_[1])
        return [d * (1.0 / D)], [_colsum(d * d)]

    d_x, sq = ew("loss", loss_body, MT, 1, TM, D, [(xs, 0, 0), (tgt, lambda i: jnp.maximum(i - 1, 0), 0)], [],
                 [(F32, R, D)], [(D, False)])

    def total_body(i, j, rows_, vecs):
        return [jnp.broadcast_to(jnp.sum(rows_[0], axis=-1, keepdims=True), (SUBLANES, D)) * (0.5 / D)], []

    loss_local = ew("loss_total", total_body, 1, 1, SUBLANES, D, [(sq, 0, 0)], [], [(F32, SUBLANES, D)])[0][0, 0]
    loss = lax.psum(loss_local, ("x", "y", "c"))

    gsmall = {nm: [None] * L for nm in ['gla_norm_gain', 'pool_scale', 'ln_mix_gain', 'ln_mix_bias', 'ln_ffn_gain',
                                        'ln_ffn_bias', 'w_decay_up', 'b_decay_up']}
    gbig = {nm: [None] * L for nm in big}
    dmod = [None] * L

    def ln_bwd(name, x_in, br, d_out, mv, gt_blk, gain):
        def body(i, j, rows_, vecs):
            gt = _cls(vecs[0], i)
            r = alpha * rows_[0] + gt * rows_[1]
            mu = jnp.mean(r, axis=-1, keepdims=True)
            rc = r - mu
            rstd = lax.rsqrt(jnp.mean(rc * rc, axis=-1, keepdims=True) + LN_EPS)
            xhat = rc * rstd
            dxh = rows_[2] * vecs[1][0:1, :]
            dr = rstd * (dxh - jnp.mean(dxh, axis=-1, keepdims=True) - xhat * jnp.mean(dxh * xhat, axis=-1, keepdims=True))
            return [dr, gt * dr], [_colsum(rows_[2] * xhat), _colsum(rows_[2]), _colsum(dr * rows_[1])]

        return ew(name, body, R // TL, 1, TL, D, [(x_in, 0, 0), (br, 0, 0), (d_out, 0, 0)], [(mv, gt_blk), (gain8(gain), 0)],
                  [(F32, R, D), (MXU_DTYPE, R, D)], [(D, False), (D, False), (D, True)], ctx_tiles=CL // TL)

    def mod_bwd(name, d_h, x_in, d_r, mv, sc_blk):
        def body(i, j, rows_, vecs):
            return ([rows_[0] * (1.0 + _cls(vecs[0], i)) + alpha * rows_[2]],
                    [_colsum(rows_[0] * rows_[1]), _colsum(rows_[0])])

        return ew(name, body, MT, D // tn_d, TM, tn_d, [(d_h, 0, 0), (x_in, 0, 0), (d_r, 0, 0)],
                  [(mv, sc_blk * (D // tn_d))], [(F32, R, D)], [(D, True), (D, True)])

    for l in reversed(range(L)):
        sv = saved[l]
        mv = modv[l]
        proj = sv['proj']
        dr2, d_f, g_gain, g_bias, g_gt_f = ln_bwd("ln_bwd", sv['x1'], sv['f_ffn'], d_x, mv, 5, ln_ffn_gain[l])
        gsmall['ln_ffn_gain'][l], gsmall['ln_ffn_bias'][l] = g_gain[0], g_bias[0]
        d_s = matmul("ffn_out_dx", d_f, wfo, 'nt', R, DFF, D, tm=TM, tn=tn_ff, b_off=(l * (DFF // tn_ff), 0))
        gbig['w_ffn_out'][l] = matmul("ffn_out_dw", sv['s_ffn'], d_f, 'tn', DFF, D, R, tm=tw(DFF), tn=tn_d)
        te_ff = _pick(DFF, (512, 256, 128))
        nh = DFF // te_ff

        def swiglu_bwd_body(i, j, rows_, vecs):
            d_s_, gate, up = rows_
            return [jnp.where(j < nh, d_s_ * up * _dsilu(gate), d_s_ * _silu(gate))], []

        half = lambda j_: j_ % nh
        d_u = pl_ew_cols("swiglu_bwd", swiglu_bwd_body, MT, 2 * nh, TM, te_ff,
                         [(d_s, half), (sv['u_ffn'], half), (sv['u_ffn'], lambda j_: nh + j_ % nh)], (MXU_DTYPE, R, 2 * DFF))
        tn_x = 512
        d_h2 = matmul("ffn_in_dx", d_u, wfi, 'nt', R, D, 2 * DFF, tm=TM, tn=tn_x, b_off=(l * (D // tn_x), 0))
        gbig['w_ffn_in'][l] = matmul("ffn_in_dw", sv['h2'], d_u, 'tn', D, 2 * DFF, R, tm=tw(D), tn=tn_ff2)
        d_x1, g_sc_f, g_sh_f = mod_bwd("mod_bwd", d_h2, sv['x1'], dr2, mv, 4)
        dr1, d_y, g_gain, g_bias, g_gt_m = ln_bwd("ln_bwd", sv['xs'], sv['y_mix'], d_x1, mv, 2, ln_mix_gain[l])
        gsmall['ln_mix_gain'][l], gsmall['ln_mix_bias'][l] = g_gain[0], g_bias[0]
        d_mg = matmul("mix_out_dx", d_y, wout, 'nt', R, D, D, tm=TM, tn=tn_d, b_off=(l * (D // tn_d), 0))
        gbig['w_out'][l] = matmul("mix_out_dw", sv['mg'], d_y, 'tn', D, D, R, tm=tw(D), tn=tn_d)

        def merge_bwd_body(i, j, rows_, vecs):
            d_m, yg, yp, ba, bb = rows_
            sa, sb = _sigmoid(ba), _sigmoid(bb)
            return [d_m * sa, d_m * sb, d_m * yg * sa * (1.0 - sa), d_m * yp * sb * (1.0 - sb)], []

        d_yg, d_yp, d_bga, d_bgb = ew(
            "merge_bwd", merge_bwd_body, MT, D // TE, TM, TE,
            [(d_mg, 0, 0), (sv['y_gla'], 0, 0), (sv['y_pool'], 0, 0), (proj, 0, OFF_BGA // TE), (proj, 0, OFF_BGB // TE)], [],
            [(MXU_DTYPE, R, D), (MXU_DTYPE, R, D), (F32, R, D), (F32, R, D)])
        d_ts = matmul("pool_out_dx", d_yp, wpo, 'nt', R, DP, D, tm=TM, tn=tn_dp, b_off=(l * (DP // tn_dp), 0))
        gbig['w_pool_out'][l] = matmul("pool_out_dw", sv['ts_pool'], d_yp, 'tn', DP, D, R, tm=tw(DP), tn=tn_d)

        def scale_bwd_body(i, j, rows_, vecs):
            return [rows_[0] * vecs[0][0:1, :]], [_colsum(rows_[0] * rows_[1])]

        d_t, g_ps = ew("pool_scale_bwd", scale_bwd_body, MT, DP // TE, TM, TE, [(d_ts, 0, 0), (sv['t_pool'], 0, 0)],
                       [(gain8(pool_scale[l]), 0)], [(MXU_DTYPE, R, DP)], [(DP, False)])
        gsmall['pool_scale'][l] = g_ps[0]
        d_u_pool = matmul("pool_group_dx", d_t, wpg_bd, 'nt', R, DP, DP, tm=TM, tn=tn_dp, b_off=(l * (DP // tn_dp), 0))
        g_bd = matmul("pool_group_dw", sv['u_pool'], d_t, 'tn', DP, DP, R, tm=tw(DP), tn=tn_dp)
        gbig['w_pool_group'][l] = jnp.stack([g_bd[g * PG:(g + 1) * PG, g * PG:(g + 1) * PG] for g in range(N_POOL)])
        d_p = pool_mix("pool_bwd", d_u_pool, 0, cfg, True)
        d_a = matmul("gla_out_dx", d_yg, wgo, 'nt', R, DV, D, tm=TM, tn=tn_d, b_off=(l * (DV // tn_d), 0))
        gbig['w_gla_out'][l] = matmul("gla_out_dw", sv['a_gla'], d_yg, 'tn', DV, D, R, tm=tw(DV), tn=tn_d)

        def post_bwd_body(i, j, rows_, vecs):
            d_a_, o_f_, o_b_, g_ = rows_
            gain = vecs[0][0:1, :]
            o = o_f_ + o_b_
            rstd = lax.rsqrt(jnp.mean(o * o, axis=-1, keepdims=True) + RMS_EPS)
            on = o * rstd
            sg = _silu(g_)
            d_on = d_a_ * gain * sg
            d_o_ = rstd * (d_on - on * jnp.mean(d_on * on, axis=-1, keepdims=True))
            return [d_o_, d_a_ * on * gain * _dsilu(g_)], [_colsum(d_a_ * on * sg)]

        d_o, d_g, g_gng = ew("gla_post_bwd", post_bwd_body, MT, N_HEADS, TM, HV,
                             [(d_a, 0, 0), (sv['o_f'], 0, 0), (sv['o_b'], 0, 0), (proj, 0, OFF_G // HV)],
                             [(gain8(gla_norm_gain[l]), 0)], [(F32, R, DV), (F32, R, DV)], [(DV, False)])
        gsmall['gla_norm_gain'][l] = g_gng[0]
        part = gla_bwd("gla_bwd", proj, wd_pad[l][0], bd_pad[l][0], sv['st_f'], d_o, cfg, False)
        full = gla_bwd("gla_rev_bwd", proj, wd_pad[l][1], bd_pad[l][1], sv['st_b'], d_o, cfg, True, addends=part[:4])
        d_q, d_k, d_v, d_alr = full[:4]
        g_wd = [jnp.moveaxis(res[4], 0, 1).reshape(ALR_W, DK)[d * GATE_RANK:(d + 1) * GATE_RANK]
                for d, res in enumerate((part, full))]
        g_bd_ = [res[5][:, 0, :].reshape(DK) for res in (part, full)]
        gsmall['w_decay_up'][l], gsmall['b_decay_up'][l] = jnp.stack(g_wd), jnp.stack(g_bd_)
        d_proj = jnp.concatenate([t_.astype(MXU_DTYPE) for t_ in (d_q, d_k, d_v, d_g, d_p, d_bga, d_bgb, d_alr)], axis=-1)
        d_h1 = matmul("proj_dx", d_proj, win, 'nt', R, D, NP, tm=TM, tn=tn_x, b_off=(l * (D // tn_x), 0))
        gbig['w_in'][l] = matmul("proj_dw", sv['h1'], d_proj, 'tn', D, NP, R, tm=tw(D), tn=tn_np)
        d_x, g_sc_m, g_sh_m = mod_bwd("mod_bwd", d_h1, sv['xs'], dr1, mv, 1)
        dmod[l] = jnp.concatenate([g_sh_m[:2], g_sc_m[:2], g_gt_m[:2], g_sh_f[:2], g_sc_f[:2], g_gt_f[:2]], axis=-1)

    grad_x = d_x[CL:].reshape(x.shape)

    dmod = jnp.stack(dmod)
    summed = [dmod[:, 0]] + [jnp.stack(gsmall[nm]) for nm in
                             ['ln_mix_gain', 'ln_mix_bias', 'ln_ffn_gain', 'ln_ffn_bias', 'gla_norm_gain', 'pool_scale',
                              'w_decay_up', 'b_decay_up']]
    pack = _pack([dmod[:, 1]] + summed)
    prow = pack.shape[0]
    packs = all_gather8("ag_small", pack)
    tot = _sum_rows("sum_small", packs.reshape(N_DEV * prow, PACK_W), N_DEV, prow, PACK_W)
    shapes = [(L, N_MOD * D)] + [a.shape for a in summed]
    tot = _unpack(tot, shapes)
    dmod_ctx = tot[1]
    g_rep = dict(zip(['ln_mix_gain', 'ln_mix_bias', 'ln_ffn_gain', 'ln_ffn_bias', 'gla_norm_gain', 'pool_scale'], tot[2:8]))
    g_wdu_full, g_bdu_full = tot[8], tot[9]
    dmod_lat = jnp.stack([_unpack(packs[d_], shapes[:1])[0] for d_ in range(N_DEV)], axis=1)
    dm_all = jnp.concatenate([dmod_lat, dmod_ctx[:, None, :], jnp.zeros((L, 16 - N_DEV - 1, N_MOD * D), F32)], axis=1)

    dm_mine = lax.dynamic_slice_in_dim(dm_all, chip * n_ada, n_ada, axis=2)
    g_w_ada = jnp.stack([matmul("ada_dw", act, dm_mine[l], 'tn', D, n_ada, 16, tm=tw(D), tn=tn_ada, precise=True)
                         for l in range(L)])

    def bsum_body(i, j, rows_, vecs):
        return [jnp.broadcast_to(_colsum(rows_[0]), rows_[0].shape)], []

    g_b_ada = jnp.stack([ew("ada_db", bsum_body, 1, 1, 16, N_MOD * D, [(dm_all[l], 0, 0)], [],
                            [(F32, 16, N_MOD * D)])[0][0] for l in range(L)])
    part_c = [matmul("ada_dc", dm_mine[l], w_ada2, 'nt', 16, D, n_ada, tm=16, tn=tn_d, b_off=(l * (D // tn_d), 0),
                     precise=True)
              for l in range(L)]
    parts_c = all_gather8("ag_dcond", jnp.concatenate(part_c, axis=0))
    dc_rows = parts_c[0::2].reshape(N_CHIPS * L * 16, D)

    def dc_body(i, j, rows_, vecs):
        acc = rows_[0]
        for r_ in rows_[1:-1]:
            acc = acc + r_
        return [acc * rows_[-1]], []

    g_c_ctx = ew("dcond", dc_body, 1, 1, 16, D, [(dc_rows, p_, 0) for p_ in range(N_CHIPS * L)] + [(dact, 0, 0)], [],
                 [(F32, 16, D)])[0][N_DEV]

    def to_ref_cols(g):
        return jnp.concatenate([g[..., :ALR0], g[..., NP - ALR_W:NP - ALR_W + 2 * GATE_RANK], g[..., ALR0:NP - ALR_W]], axis=-1)

    full_g = {nm: jnp.stack(gbig[nm]) for nm in big}
    full_g['w_in'] = to_ref_cols(full_g['w_in'])
    shard_axis = {'w_in': 2, 'w_gla_out': 1, 'w_pool_out': 2, 'w_out': 1, 'w_ffn_in': 2, 'w_ffn_out': 1, 'w_pool_group': 2}

    def shard_of(nm, which):
        ax = shard_axis[nm]
        width = weights[nm].shape[ax]
        return lax.dynamic_slice_in_dim(full_g[nm], which * width, width, axis=ax)

    own = {nm: _rows2d(shard_of(nm, chip)) for nm in big}
    sendbufs = [jnp.stack([shard_of(nm, jnp.bitwise_xor(chip, k)).astype(WIRE_DTYPE) for k in range(1, N_CHIPS)])
                for nm in big]
    recvd = scatter_chips("scatter_grads", sendbufs)
    partial = []
    for nm, rb in zip(big, recvd):
        rows_, width = own[nm].shape
        tm_ = _pick(rows_, (256, 128, 64, 32, 16, 8))

        def psum_body(i, j, rows__, vecs):
            return [rows__[0] + rows__[1].astype(F32) + rows__[2].astype(F32) + rows__[3].astype(F32)], []

        rb2 = rb.reshape(3 * rows_, width)
        partial.append(ew("sum_grads", psum_body, rows_ // tm_, 1, tm_, width,
                          [(own[nm], 0, 0)] + [(rb2, k * (rows_ // tm_), 0) for k in range(3)], [], [(F32, rows_, width)])[0])
    other = swap_cores("swap_grads", partial)

    out_g, out_d, out_m, out_v = {}, {}, {}, {}
    for nm, mine, theirs in zip(big, partial, other):
        res = _adam("adam_" + nm, _rows2d(weights[nm]), [mine, theirs], _rows2d(mom1[nm]), _rows2d(mom2[nm]))
        out_g[nm], out_d[nm], out_m[nm], out_v[nm] = [r_.reshape(weights[nm].shape) for r_ in res]
    res = _adam("adam_w_ada", _rows2d(w_ada), [_rows2d(g_w_ada)], _rows2d(m_w_ada), _rows2d(v_w_ada))
    out_g['w_ada'], out_d['w_ada'], out_m['w_ada'], out_v['w_ada'] = [r_.reshape(w_ada.shape) for r_ in res]
    n_wd, n_bd = w_decay_up.shape[-1], b_decay_up.shape[-1]
    small_g = dict(g_rep, c_ctx=g_c_ctx, b_ada=g_b_ada,
                   w_decay_up=lax.dynamic_slice_in_dim(g_wdu_full, chip * n_wd, n_wd, axis=3),
                   b_decay_up=lax.dynamic_slice_in_dim(g_bdu_full, chip * n_bd, n_bd, axis=2))
    small = [nm for nm in names if nm not in big and nm != 'w_ada']
    res = _adam("adam_small", _pack([weights[nm] for nm in small]), [_pack([small_g[nm] for nm in small])],
                _pack([mom1[nm] for nm in small]), _pack([mom2[nm] for nm in small]))
    small_shapes = [weights[nm].shape for nm in small]
    for dst, packed in zip((out_g, out_d, out_m, out_v), res):
        for nm, val in zip(small, _unpack(packed, small_shapes)):
            dst[nm] = val

    return (loss, grad_x, *[out_g[nm] for nm in names], *[out_d[nm] for nm in names],
            *[out_m[nm] for nm in names], *[out_v[nm] for nm in names])


def pl_ew_cols(name, body, nrow, ncol, tm, tn, row_ins, out):
    dt, rows, cols = out

    def kern(*refs):
        j, i = pl.program_id(0), pl.program_id(1)
        res, _ = body(i, j, [r[...] for r in refs[:-1]], [])
        refs[-1][...] = res[0].astype(refs[-1].dtype)

    in_specs = [pl.BlockSpec((tm, tn), functools.partial(lambda j, i, f: (i, f(j)), f=f)) for _, f in row_ins]
    return pl.pallas_call(
        kern, name=name, grid=(ncol, nrow), in_specs=in_specs, out_specs=pl.BlockSpec((tm, tn), lambda j, i: (i, j)),
        out_shape=jax.ShapeDtypeStruct((rows, cols), dt), compiler_params=_params(),
    )(*[a for a, _ in row_ins])
```

```python
import functools

import jax
import jax.numpy as jnp
from jax import lax
from jax.experimental import pallas as pl
from jax.experimental.pallas import tpu as pltpu

F32 = jnp.float32
MXU_DTYPE = jnp.bfloat16
WIRE_DTYPE = jnp.bfloat16

GRID_W = 64
CHUNK = 64
N_HEADS = 4
GATE_RANK = 16
GATE_NORM = 16.0
N_MOD = 6
N_POOL = 4
LN_EPS = 1e-5
RMS_EPS = 1e-6
ADAM_LR = 0.001
ADAM_B1 = 0.9
ADAM_B2 = 0.999
ADAM_EPS = 1e-08
ADAM_WD = 0.01
ADAM_STEP = 10

LANES = 128
SUBLANES = 8
ALR_W = 256
PACK_W = 2048
VMEM_LIMIT = 56 * 1024 * 1024
N_CHIPS = 4
N_DEV = 8
MESH = pl.DeviceIdType.MESH

NN = ((1,), (0,))
NT = ((1,), (1,))
TN = ((0,), (0,))


def _dot(a, b, dims):
    return lax.dot_general(a.astype(MXU_DTYPE), b.astype(MXU_DTYPE), (dims, ((), ())),
                           preferred_element_type=F32)


def _dot_f32(a, b, dims):
    return lax.dot_general(a.astype(F32), b.astype(F32), (dims, ((), ())),
                           precision=lax.Precision.HIGHEST, preferred_element_type=F32)


def _dot_mask(mask, x, dims):
    m = mask.astype(MXU_DTYPE)
    if MXU_DTYPE == F32:
        return lax.dot_general(m, x, (dims, ((), ())), preferred_element_type=F32)
    acc = None
    rest = x
    for _ in range(3):
        piece = rest.astype(MXU_DTYPE)
        rest = rest - piece.astype(F32)
        part = lax.dot_general(m, piece, (dims, ((), ())), preferred_element_type=F32)
        acc = part if acc is None else acc + part
    return acc


def _dot_3x(a, b, dims):
    if MXU_DTYPE == F32:
        return lax.dot_general(a, b, (dims, ((), ())), preferred_element_type=F32)
    a_hi, b_hi = a.astype(MXU_DTYPE), b.astype(MXU_DTYPE)
    a_lo = (a - a_hi.astype(F32)).astype(MXU_DTYPE)
    b_lo = (b - b_hi.astype(F32)).astype(MXU_DTYPE)
    dot = lambda u, w: lax.dot_general(u, w, (dims, ((), ())), preferred_element_type=F32)
    return dot(a_hi, b_hi) + (dot(a_lo, b_hi) + dot(a_hi, b_lo))


def _pick(n, cands):
    for c in cands:
        if n % c == 0:
            return c
    return n


def _params():
    return pltpu.CompilerParams(vmem_limit_bytes=VMEM_LIMIT)


def _sigmoid(x):
    return 0.5 + 0.5 * jnp.tanh(0.5 * x)


def _silu(x):
    return x * _sigmoid(x)


def _dsilu(x):
    s = _sigmoid(x)
    return s * (1.0 + x * (1.0 - s))


def matmul(name, a, b, form, m, n, k, *, tm, tn, out_dtype=F32, a_off=(0, 0), b_off=(0, 0), bias=None, bias_off=0,
           precise=False):
    assert m % tm == 0 and n % tn == 0, (name, m, n, tm, tn)
    if form == 'tn':
        a_spec = pl.BlockSpec((k, tm), lambda j, i: (a_off[0], i + a_off[1]))
    else:
        a_spec = pl.BlockSpec((tm, k), lambda j, i: (i + a_off[0], a_off[1]))
    if form == 'nt':
        b_spec = pl.BlockSpec((tn, k), lambda j, i: (j + b_off[0], b_off[1]))
    else:
        b_spec = pl.BlockSpec((k, tn), lambda j, i: (b_off[0], j + b_off[1]))
    dims = {'nn': NN, 'nt': NT, 'tn': TN}[form]
    in_specs = [a_spec, b_spec]
    args = [a, b]
    if bias is not None:
        in_specs.append(pl.BlockSpec((SUBLANES, tn), lambda j, i: (0, j + bias_off)))
        args.append(bias)

    def body(*refs):
        a_ref, b_ref = refs[0], refs[1]
        o_ref = refs[-1]
        acc = (_dot_f32 if precise else _dot)(a_ref[...], b_ref[...], dims)
        if bias is not None:
            acc = acc + refs[2][0:1, :]
        o_ref[...] = acc.astype(o_ref.dtype)

    return pl.pallas_call(
        body, name=name, grid=(n // tn, m // tm), in_specs=in_specs,
        out_specs=pl.BlockSpec((tm, tn), lambda j, i: (i, j)),
        out_shape=jax.ShapeDtypeStruct((m, n), out_dtype), compiler_params=_params(),
    )(*args)


def ew(name, body, nrow, ncol, tm, tn, row_ins, vec_ins, row_outs, sum_outs=(), ctx_tiles=1, pass_i=False):
    def rmap(roff):
        return roff if callable(roff) else (lambda i: i + roff)

    in_specs = []
    for arr, roff, coff in row_ins:
        in_specs.append(pl.BlockSpec((tm, tn), functools.partial(lambda j, i, r, c: (r(i), j + c), r=rmap(roff), c=coff)))
    for arr, coff in vec_ins:
        in_specs.append(pl.BlockSpec((SUBLANES, tn), functools.partial(lambda j, i, c: (0, j + c), c=coff)))
    out_specs = [pl.BlockSpec((tm, tn), lambda j, i: (i, j)) for _ in row_outs]
    out_specs += [pl.BlockSpec((SUBLANES, tn), lambda j, i: (0, j)) for _ in sum_outs]
    out_shape = [jax.ShapeDtypeStruct((r, c), dt) for dt, r, c in row_outs]
    out_shape += [jax.ShapeDtypeStruct((SUBLANES, c), F32) for c, _ in sum_outs]
    n_row, n_vec, n_ro = len(row_ins), len(vec_ins), len(row_outs)

    def kern(*refs):
        j, i = pl.program_id(0), pl.program_id(1)
        rows = [r[...] for r in refs[:n_row]]
        vecs = [r[...] for r in refs[n_row:n_row + n_vec]]
        outs = refs[n_row + n_vec:]
        is_ctx = i < ctx_tiles
        res, sums = body(i if pass_i else is_ctx, j, rows, vecs)
        for ref, val in zip(outs[:n_ro], res):
            ref[...] = val.astype(ref.dtype)
        for ref, val, (_, by_class) in zip(outs[n_ro:], sums, sum_outs):
            @pl.when(i == 0)
            def _():
                ref[...] = jnp.zeros_like(ref)
            if by_class:
                ref[0:1, :] += jnp.where(is_ctx, val, 0.0)
                ref[1:2, :] += jnp.where(is_ctx, 0.0, val)
            else:
                ref[0:1, :] += val

    outs = pl.pallas_call(
        kern, name=name, grid=(ncol, nrow), in_specs=in_specs, out_specs=out_specs, out_shape=out_shape,
        compiler_params=_params(),
    )(*[a for a, _, _ in row_ins], *[a for a, _ in vec_ins])
    return list(outs)


def _cls(vec, is_ctx):
    return jnp.where(is_ctx, vec[0:1, :], vec[1:2, :])


def _colsum(x):
    return jnp.sum(x, axis=0, keepdims=True)


def _chunk_map(cfg, rev):
    nctx, nc = cfg.CL // CHUNK, cfg.R // CHUNK
    if not rev:
        return lambda s: s
    return lambda s: jnp.where(s < nctx, nctx - 1 - s, nctx + nc - 1 - s)


def _gla_chunk(q_ref, k_ref, a_ref, wd_ref, bd_ref, rev, scale):
    q = q_ref[...] * scale
    k = k_ref[...]
    z = _dot_3x(a_ref[...], wd_ref[...], NN) + bd_ref[0:1, :]
    la = (jnp.minimum(z, 0.0) - jnp.log(1.0 + jnp.exp(-jnp.abs(z)))) * (1.0 / GATE_NORM)
    r = lax.broadcasted_iota(jnp.int32, (CHUNK, CHUNK), 0)
    c = lax.broadcasted_iota(jnp.int32, (CHUNK, CHUNK), 1)
    keep = (r <= c) if rev else (r >= c)
    tri = keep.astype(F32)
    cum = _dot_mask(tri, la, NN)
    mid = CHUNK // 2 if rev else CHUNK // 2 - 1
    end = 0 if rev else CHUNK - 1
    ref = cum[mid:mid + 1, :]
    last = cum[end:end + 1, :]
    return dict(q=q, k=k, z=z, keep=keep, tri=tri, q_in=q * jnp.exp(cum - ref), k_in=k * jnp.exp(ref - cum),
                e_q=jnp.exp(cum), e_k=jnp.exp(last - cum), e_inq=jnp.exp(cum - ref), e_ink=jnp.exp(ref - cum),
                e_last=jnp.exp(last))


def _gla_in_specs(cfg, rows_of):
    dk, dv = cfg.DK, cfg.DV
    return [
        pl.BlockSpec((CHUNK, dk), lambda s: (rows_of(s), 0)),
        pl.BlockSpec((CHUNK, dk), lambda s: (rows_of(s), 1)),
        pl.BlockSpec((CHUNK, dv), lambda s: (rows_of(s), 2 * dk // dv)),
        pl.BlockSpec((CHUNK, ALR_W), lambda s: (rows_of(s), (cfg.NP - ALR_W) // ALR_W)),
        pl.BlockSpec((ALR_W, dk), lambda s: (0, 0)),
        pl.BlockSpec((SUBLANES, dk), lambda s: (0, 0)),
    ]


def gla_fwd(name, proj, wd, bd, cfg, rev):
    hk, hv = cfg.HK, cfg.HV
    nc = cfg.R // CHUNK
    cmap = _chunk_map(cfg, rev)
    scale = hk ** -0.5

    def body(q_ref, k_ref, v_ref, a_ref, wd_ref, bd_ref, o_ref, ss_ref, st_scr):
        s = pl.program_id(0)

        @pl.when(s == 0)
        def _():
            st_scr[...] = jnp.zeros_like(st_scr)

        t = _gla_chunk(q_ref, k_ref, a_ref, wd_ref, bd_ref, rev, scale)
        q_int, k_st = t['q'] * t['e_q'], t['k'] * t['e_k']
        for h in range(N_HEADS):
            ks, vs = slice(h * hk, (h + 1) * hk), slice(h * hv, (h + 1) * hv)
            v = v_ref[:, vs]
            st = st_scr[h]
            ss_ref[0, h] = st
            a = jnp.where(t['keep'], _dot(t['q_in'][:, ks], t['k_in'][:, ks], NT), 0.0)
            o_ref[:, vs] = _dot(a, v, NN) + _dot(q_int[:, ks], st, NT)
            st_scr[h] = st * t['e_last'][:, ks] + _dot(v, k_st[:, ks], TN)

    return pl.pallas_call(
        body, name=name, grid=(nc,), in_specs=_gla_in_specs(cfg, cmap),
        out_specs=[pl.BlockSpec((CHUNK, cfg.DV), lambda s: (cmap(s), 0)),
                   pl.BlockSpec((1, N_HEADS, hv, hk), lambda s: (s, 0, 0, 0))],
        out_shape=[jax.ShapeDtypeStruct((cfg.R, cfg.DV), F32),
                   jax.ShapeDtypeStruct((nc, N_HEADS, hv, hk), F32)],
        scratch_shapes=[pltpu.VMEM((N_HEADS, hv, hk), F32)], compiler_params=_params(),
    )(proj, proj, proj, proj, wd, bd)


def gla_bwd(name, proj, wd, bd, states, d_o, cfg, rev, addends=None):
    hk, hv = cfg.HK, cfg.HV
    nc = cfg.R // CHUNK
    cmap = _chunk_map(cfg, rev)
    rows_of = lambda g: cmap(nc - 1 - g)
    scale = hk ** -0.5
    n_add = 0 if addends is None else 4

    def body(*refs):
        q_ref, k_ref, v_ref, a_ref, wd_ref, bd_ref, ss_ref, do_ref = refs[:8]
        adds = refs[8:8 + n_add]
        dq_ref, dk_ref, dv_ref, da_ref, dwd_ref, dbd_ref, dst_scr = refs[8 + n_add:]
        g = pl.program_id(0)

        @pl.when(g == 0)
        def _():
            dst_scr[...] = jnp.zeros_like(dst_scr)
            dwd_ref[...] = jnp.zeros_like(dwd_ref)
            dbd_ref[...] = jnp.zeros_like(dbd_ref)

        t = _gla_chunk(q_ref, k_ref, a_ref, wd_ref, bd_ref, rev, scale)
        q_int, k_st = t['q'] * t['e_q'], t['k'] * t['e_k']
        dq_h, dk_h, carry_h = [], [], []
        for h in range(N_HEADS):
            ks, vs = slice(h * hk, (h + 1) * hk), slice(h * hv, (h + 1) * hv)
            v = v_ref[:, vs]
            d_out = do_ref[:, vs]
            st = ss_ref[0, h]
            dst = dst_scr[h]
            a = jnp.where(t['keep'], _dot(t['q_in'][:, ks], t['k_in'][:, ks], NT), 0.0)
            da = jnp.where(t['keep'], _dot(d_out, v, NT), 0.0)
            dv = _dot(a, d_out, TN) + _dot(k_st[:, ks], dst, NT)
            dq_h.append(_dot(d_out, st, NN) * t['e_q'][:, ks] + _dot(da, t['k_in'][:, ks], NN) * t['e_inq'][:, ks])
            dk_h.append(_dot(v, dst, NN) * t['e_k'][:, ks] + _dot(da, t['q_in'][:, ks], TN) * t['e_ink'][:, ks])
            dst_scr[h] = dst * t['e_last'][:, ks] + _dot(d_out, q_int[:, ks], TN)
            st_end = st * t['e_last'][:, ks] + _dot(v, k_st[:, ks], TN)
            carry_h.append(_colsum(dst * st_end))
            dv_ref[:, vs] = dv + adds[2][:, vs] if n_add else dv
        dq = jnp.concatenate(dq_h, axis=-1)
        dk = jnp.concatenate(dk_h, axis=-1)
        dg = t['q'] * dq - t['k'] * dk
        dla = _dot_mask(t['tri'], dg, TN) + jnp.concatenate(carry_h, axis=-1)
        dz = dla * (1.0 / GATE_NORM) * _sigmoid(-t['z'])
        dalr = _dot_3x(dz, wd_ref[...], NT)
        dwd_ref[...] += _dot_3x(a_ref[...], dz, TN)
        dbd_ref[...] += jnp.broadcast_to(_colsum(dz), dbd_ref.shape)
        dq = dq * scale
        if n_add:
            dq, dk, dalr = dq + adds[0][...], dk + adds[1][...], dalr + adds[3][...]
        dq_ref[...] = dq
        dk_ref[...] = dk
        da_ref[...] = dalr

    qk_spec = pl.BlockSpec((CHUNK, cfg.DK), lambda g: (rows_of(g), 0))
    v_spec = pl.BlockSpec((CHUNK, cfg.DV), lambda g: (rows_of(g), 0))
    a_spec = pl.BlockSpec((CHUNK, ALR_W), lambda g: (rows_of(g), 0))
    in_specs = _gla_in_specs(cfg, rows_of) + [
        pl.BlockSpec((1, N_HEADS, hv, hk), lambda g: (nc - 1 - g, 0, 0, 0)), v_spec]
    args = [proj, proj, proj, proj, wd, bd, states, d_o]
    if n_add:
        in_specs += [qk_spec, qk_spec, v_spec, a_spec]
        args += list(addends)
    return pl.pallas_call(
        body, name=name, grid=(nc,), in_specs=in_specs,
        out_specs=[qk_spec, qk_spec, v_spec, a_spec,
                   pl.BlockSpec((ALR_W, cfg.DK), lambda g: (0, 0)),
                   pl.BlockSpec((SUBLANES, cfg.DK), lambda g: (0, 0))],
        out_shape=[jax.ShapeDtypeStruct((cfg.R, cfg.DK), F32), jax.ShapeDtypeStruct((cfg.R, cfg.DK), F32),
                   jax.ShapeDtypeStruct((cfg.R, cfg.DV), F32), jax.ShapeDtypeStruct((cfg.R, ALR_W), F32),
                   jax.ShapeDtypeStruct((ALR_W, cfg.DK), F32),
                   jax.ShapeDtypeStruct((SUBLANES, cfg.DK), F32)],
        scratch_shapes=[pltpu.VMEM((N_HEADS, hv, hk), F32)],
        compiler_params=_params(),
    )(*args)


def pool_mix(name, src, coff, cfg, transpose):
    tm, pg = cfg.TM, cfg.PG
    mt = cfg.R // tm
    reach = -(-(max(2 ** N_POOL // 2, 1) * GRID_W) // tm)
    nk = 2 * reach + 1
    img_rows = cfg.S // GRID_W
    shift = GRID_W.bit_length() - 1

    def ktile(m, d):
        return jnp.where(m == 0, 0, jnp.clip(m + d - reach, 1, mt - 1))

    def counts(idx, is_ctx, lo, hi):
        ctx_n = jnp.minimum(idx + hi + 1, cfg.CL) - jnp.maximum(idx - lo, 0)
        r, c = idx >> shift, idx & (GRID_W - 1)
        lat_n = ((jnp.minimum(r + hi + 1, img_rows) - jnp.maximum(r - lo, 0))
                 * (jnp.minimum(c + hi + 1, GRID_W) - jnp.maximum(c - lo, 0)))
        return jnp.where(is_ctx, ctx_n, lat_n).astype(F32)

    def body(src_ref, self_ref, o_ref, acc):
        g, m, d = pl.program_id(0), pl.program_id(1), pl.program_id(2)
        lo = jnp.left_shift(1, g)
        hi = lo - 1
        is_ctx = m == 0
        kt = m + d - reach
        valid = jnp.where(is_ctx, d == reach, (kt >= 1) & (kt <= mt - 1))
        seg = jnp.where(is_ctx, 0, cfg.CL)

        @pl.when(d == 0)
        def _():
            acc[...] = jnp.zeros_like(acc)

        @pl.when(valid)
        def _():
            row = lax.broadcasted_iota(jnp.int32, (tm, tm), 0) + (m * tm - seg)
            col = lax.broadcasted_iota(jnp.int32, (tm, tm), 1) + (kt * tm - seg)
            ctr, mem = (col, row) if transpose else (row, col)
            in_ctx = (mem >= ctr - lo) & (mem <= ctr + hi)
            cr, cc = ctr >> shift, ctr & (GRID_W - 1)
            mr, mc = mem >> shift, mem & (GRID_W - 1)
            in_lat = (mr >= cr - lo) & (mr <= cr + hi) & (mc >= cc - lo) & (mc <= cc + hi)
            mask = jnp.where(is_ctx, in_ctx.astype(F32), in_lat.astype(F32))
            x = src_ref[...]
            if transpose:
                kidx = lax.broadcasted_iota(jnp.int32, (tm, 1), 0) + (kt * tm - seg)
                x = x / counts(kidx, is_ctx, lo, hi)
            acc[...] += _dot_mask(mask, x, NN)

        @pl.when(d == nk - 1)
        def _():
            res = acc[...]
            if not transpose:
                midx = lax.broadcasted_iota(jnp.int32, (tm, 1), 0) + (m * tm - seg)
                res = res / counts(midx, is_ctx, lo, hi)
            o_ref[...] = (res - self_ref[...]).astype(o_ref.dtype)

    return pl.pallas_call(
        body, name=name, grid=(N_POOL, mt, nk),
        in_specs=[pl.BlockSpec((tm, pg), lambda g, m, d: (ktile(m, d), coff + g)),
                  pl.BlockSpec((tm, pg), lambda g, m, d: (m, coff + g))],
        out_specs=pl.BlockSpec((tm, pg), lambda g, m, d: (m, g)),
        out_shape=jax.ShapeDtypeStruct((cfg.R, cfg.DP), F32),
        scratch_shapes=[pltpu.VMEM((tm, pg), F32)], compiler_params=_params(),
    )(src, src)


def _my_place():
    return lax.axis_index("x"), lax.axis_index("y"), lax.axis_index("c")


def _flip(v, bit):
    return 1 - v if bit else v


def all_gather8(name, block):
    rows, w = block.shape

    def body(x_ref, out_ref, send_sems, recv_sems, local_sem):
        x, y, c = _my_place()
        me = 4 * x + 2 * y + c
        mine = pltpu.make_async_copy(x_ref, out_ref.at[me], local_sem)
        mine.start()
        sends = []
        for k in range(1, N_DEV):
            peer = (_flip(x, k & 4), _flip(y, k & 2), _flip(c, k & 1))
            cp = pltpu.make_async_remote_copy(src_ref=x_ref, dst_ref=out_ref.at[me], send_sem=send_sems.at[k - 1],
                                              recv_sem=recv_sems.at[k - 1], device_id=peer, device_id_type=MESH)
            cp.start()
            sends.append(cp)
        for k in range(1, N_DEV):
            peer = (_flip(x, k & 4), _flip(y, k & 2), _flip(c, k & 1))
            slot = 4 * peer[0] + 2 * peer[1] + peer[2]
            pltpu.make_async_remote_copy(src_ref=x_ref, dst_ref=out_ref.at[slot], send_sem=send_sems.at[k - 1],
                                         recv_sem=recv_sems.at[k - 1], device_id=peer, device_id_type=MESH).wait_recv()
        for cp in sends:
            cp.wait_send()
        mine.wait()

    return pl.pallas_call(
        body, name=name, out_shape=jax.ShapeDtypeStruct((N_DEV, rows, w), block.dtype),
        in_specs=[pl.BlockSpec(memory_space=pl.ANY)], out_specs=pl.BlockSpec(memory_space=pl.ANY),
        scratch_shapes=[pltpu.SemaphoreType.DMA((N_DEV - 1,)), pltpu.SemaphoreType.DMA((N_DEV - 1,)),
                        pltpu.SemaphoreType.DMA],
    )(block)


def gather_chips(name, shards, n_split):
    n = len(shards)
    n_ici = 3 * n

    def body(*refs):
        srcs, dsts = refs[:n], refs[n:2 * n]
        send_sems, recv_sems, fwd_send_sems, fwd_recv_sems, local_sems = refs[2 * n:]
        x, y, c = _my_place()
        me = 2 * x + y
        sibling = (x, y, 1 - c)

        def part(ref, t, core):
            if t >= n_split:
                return ref
            h = shards[t].shape[1] // 2
            return ref.at[:, pl.ds(core * h, h)]

        local = [pltpu.make_async_copy(srcs[t], dsts[t].at[me], local_sems.at[t]) for t in range(n)]
        for cp in local:
            cp.start()
        sends = []
        for k in range(1, N_CHIPS):
            peer = (_flip(x, k & 2), _flip(y, k & 1), c)
            for t in range(n):
                cp = pltpu.make_async_remote_copy(
                    src_ref=part(srcs[t], t, c), dst_ref=part(dsts[t].at[me], t, c), send_sem=send_sems.at[t * 3 + k - 1],
                    recv_sem=recv_sems.at[t * 3 + k - 1], device_id=peer, device_id_type=MESH)
                cp.start()
                sends.append(cp)
        for k in range(1, N_CHIPS):
            peer = (_flip(x, k & 2), _flip(y, k & 1), c)
            slot = 2 * peer[0] + peer[1]
            for t in range(n):
                landed = part(dsts[t].at[slot], t, c)
                pltpu.make_async_remote_copy(
                    src_ref=part(srcs[t], t, c), dst_ref=landed, send_sem=send_sems.at[t * 3 + k - 1],
                    recv_sem=recv_sems.at[t * 3 + k - 1], device_id=peer, device_id_type=MESH).wait_recv()
                if t < n_split:
                    cp = pltpu.make_async_remote_copy(
                        src_ref=landed, dst_ref=landed, send_sem=fwd_send_sems.at[t * 3 + k - 1],
                        recv_sem=fwd_recv_sems.at[t * 3 + k - 1], device_id=sibling, device_id_type=MESH)
                    cp.start()
                    sends.append(cp)
        for k in range(1, N_CHIPS):
            slot = 2 * _flip(x, k & 2) + _flip(y, k & 1)
            for t in range(n_split):
                theirs = part(dsts[t].at[slot], t, 1 - c)
                pltpu.make_async_remote_copy(
                    src_ref=theirs, dst_ref=theirs, send_sem=fwd_send_sems.at[t * 3 + k - 1],
                    recv_sem=fwd_recv_sems.at[t * 3 + k - 1], device_id=sibling, device_id_type=MESH).wait_recv()
        for cp in sends:
            cp.wait_send()
        for cp in local:
            cp.wait()

    return pl.pallas_call(
        body, name=name, out_shape=[jax.ShapeDtypeStruct((N_CHIPS,) + s.shape, s.dtype) for s in shards],
        in_specs=[pl.BlockSpec(memory_space=pl.ANY)] * n, out_specs=[pl.BlockSpec(memory_space=pl.ANY)] * n,
        scratch_shapes=[pltpu.SemaphoreType.DMA((n_ici,)), pltpu.SemaphoreType.DMA((n_ici,)),
                        pltpu.SemaphoreType.DMA((n_ici,)), pltpu.SemaphoreType.DMA((n_ici,)),
                        pltpu.SemaphoreType.DMA((n,))],
    )(*shards)


def scatter_chips(name, bufs):
    n = len(bufs)

    def body(*refs):
        srcs, dsts = refs[:n], refs[n:2 * n]
        send_sems, recv_sems = refs[2 * n:]
        x, y, c = _my_place()
        copies = []
        for k in range(1, N_CHIPS):
            peer = (_flip(x, k & 2), _flip(y, k & 1), c)
            for t in range(n):
                cp = pltpu.make_async_remote_copy(
                    src_ref=srcs[t].at[k - 1], dst_ref=dsts[t].at[k - 1], send_sem=send_sems.at[t * 3 + k - 1],
                    recv_sem=recv_sems.at[t * 3 + k - 1], device_id=peer, device_id_type=MESH)
                cp.start()
                copies.append(cp)
        for cp in copies:
            cp.wait_recv()
        for cp in copies:
            cp.wait_send()

    return pl.pallas_call(
        body, name=name, out_shape=[jax.ShapeDtypeStruct(b.shape, b.dtype) for b in bufs],
        in_specs=[pl.BlockSpec(memory_space=pl.ANY)] * n, out_specs=[pl.BlockSpec(memory_space=pl.ANY)] * n,
        scratch_shapes=[pltpu.SemaphoreType.DMA((3 * n,)), pltpu.SemaphoreType.DMA((3 * n,))],
    )(*bufs)


def swap_cores(name, bufs):
    n = len(bufs)

    def body(*refs):
        srcs, dsts = refs[:n], refs[n:2 * n]
        send_sems, recv_sems = refs[2 * n:]
        x, y, c = _my_place()
        copies = []
        for t in range(n):
            cp = pltpu.make_async_remote_copy(
                src_ref=srcs[t], dst_ref=dsts[t], send_sem=send_sems.at[t], recv_sem=recv_sems.at[t],
                device_id=(x, y, 1 - c), device_id_type=MESH)
            cp.start()
            copies.append(cp)
        for cp in copies:
            cp.wait_recv()
        for cp in copies:
            cp.wait_send()

    return pl.pallas_call(
        body, name=name, out_shape=[jax.ShapeDtypeStruct(b.shape, b.dtype) for b in bufs],
        in_specs=[pl.BlockSpec(memory_space=pl.ANY)] * n, out_specs=[pl.BlockSpec(memory_space=pl.ANY)] * n,
        scratch_shapes=[pltpu.SemaphoreType.DMA((n,)), pltpu.SemaphoreType.DMA((n,))],
    )(*bufs)


def _vec8(*rows):
    w = rows[0].shape[-1]
    out = jnp.zeros((SUBLANES, w), F32)
    for r, v in enumerate(rows):
        out = out.at[r].set(v.reshape(w).astype(F32))
    return out


def _pack(arrays):
    parts = []
    for a in arrays:
        flat = a.reshape(-1).astype(F32)
        pad = (-flat.shape[0]) % PACK_W
        parts.append(jnp.pad(flat, (0, pad)))
    flat = jnp.concatenate(parts)
    pad = (-flat.shape[0]) % (PACK_W * SUBLANES)
    return jnp.pad(flat, (0, pad)).reshape(-1, PACK_W)


def _unpack(packed, shapes):
    flat = packed.reshape(-1)
    out, pos = [], 0
    for shp in shapes:
        size = 1
        for d in shp:
            size *= d
        out.append(flat[pos:pos + size].reshape(shp))
        pos += size + (-size) % PACK_W
    return out


def _rows2d(a):
    return a.reshape(-1, a.shape[-1])


class _Cfg:
    pass


def _adam(name, w, grads, m, v, half_rows=None, core=None):
    rows, width = w.shape
    c1 = 1.0 - ADAM_B1 ** ADAM_STEP
    c2 = 1.0 - ADAM_B2 ** ADAM_STEP

    def update(wv, mv, vv, g):
        m_new = ADAM_B1 * mv + (1.0 - ADAM_B1) * g
        v_new = ADAM_B2 * vv + (1.0 - ADAM_B2) * (g * g)
        delta = -ADAM_LR * ((m_new / c1) / (jnp.sqrt(v_new / c2) + ADAM_EPS) + ADAM_WD * wv)
        return [g, delta, m_new, v_new], []

    if half_rows is None:
        tm = _pick(rows, (128, 64, 32, 16, 8))

        def body(i, j, rows_, vecs):
            g = rows_[3]
            for extra in rows_[4:]:
                g = g + extra
            return update(rows_[0], rows_[1], rows_[2], g)

        return ew(name, body, rows // tm, 1, tm, width, [(w, 0, 0), (m, 0, 0), (v, 0, 0)] + [(g, 0, 0) for g in grads], [],
                  [(F32, rows, width)] * 4)

    tm = _pick(half_rows, (128, 64, 32, 16, 8))
    nb = half_rows // tm
    run_tile = lambda i: (i // (2 * nb)) * nb + i % nb
    core_vec = jnp.broadcast_to(core.astype(F32), (SUBLANES, width))

    def body(i, j, rows_, vecs):
        owner = ((i // nb) % 2).astype(F32)
        g = jnp.where(vecs[0][0:1, :] == owner, rows_[3], rows_[4])
        return update(rows_[0], rows_[1], rows_[2], g)

    return ew(name, body, rows // tm, 1, tm, width,
              [(w, 0, 0), (m, 0, 0), (v, 0, 0), (grads[0], run_tile, 0), (grads[1], run_tile, 0)], [(core_vec, 0)],
              [(F32, rows, width)] * 4, pass_i=True)


def _sum_rows(name, arr, nparts, rows, width, dtype=F32):
    tm = _pick(rows, (256, 128, 64, 32, 16, 8))
    nblk = rows // tm

    def body(i, j, rows_, vecs):
        acc = rows_[0].astype(F32)
        for r in rows_[1:]:
            acc = acc + r.astype(F32)
        return [acc], []

    return ew(name, body, nblk, 1, tm, width, [(arr, p * nblk, 0) for p in range(nparts)], [], [(dtype, rows, width)])[0]


def kernel(x, c, ctx, c_ctx, w_ada, b_ada, w_in, w_decay_up, b_decay_up, gla_norm_gain, w_pool_group, pool_scale, w_gla_out, w_pool_out, w_out, ln_mix_gain, ln_mix_bias, w_ffn_in, w_ffn_out, ln_ffn_gain, ln_ffn_bias, loss_target, m_c_ctx, m_w_ada, m_b_ada, m_w_in, m_w_decay_up, m_b_decay_up, m_gla_norm_gain, m_w_pool_group, m_pool_scale, m_w_gla_out, m_w_pool_out, m_w_out, m_ln_mix_gain, m_ln_mix_bias, m_w_ffn_in, m_w_ffn_out, m_ln_ffn_gain, m_ln_ffn_bias, v_c_ctx, v_w_ada, v_b_ada, v_w_in, v_w_decay_up, v_b_decay_up, v_gla_norm_gain, v_w_pool_group, v_pool_scale, v_w_gla_out, v_w_pool_out, v_w_out, v_ln_mix_gain, v_ln_mix_bias, v_w_ffn_in, v_w_ffn_out, v_ln_ffn_gain, v_ln_ffn_bias):
    weights = dict(c_ctx=c_ctx, w_ada=w_ada, b_ada=b_ada, w_in=w_in, w_decay_up=w_decay_up, b_decay_up=b_decay_up,
                   gla_norm_gain=gla_norm_gain, w_pool_group=w_pool_group, pool_scale=pool_scale, w_gla_out=w_gla_out,
                   w_pool_out=w_pool_out, w_out=w_out, ln_mix_gain=ln_mix_gain, ln_mix_bias=ln_mix_bias,
                   w_ffn_in=w_ffn_in, w_ffn_out=w_ffn_out, ln_ffn_gain=ln_ffn_gain, ln_ffn_bias=ln_ffn_bias)
    mom1 = dict(c_ctx=m_c_ctx, w_ada=m_w_ada, b_ada=m_b_ada, w_in=m_w_in, w_decay_up=m_w_decay_up, b_decay_up=m_b_decay_up,
                gla_norm_gain=m_gla_norm_gain, w_pool_group=m_w_pool_group, pool_scale=m_pool_scale, w_gla_out=m_w_gla_out,
                w_pool_out=m_w_pool_out, w_out=m_w_out, ln_mix_gain=m_ln_mix_gain, ln_mix_bias=m_ln_mix_bias,
                w_ffn_in=m_w_ffn_in, w_ffn_out=m_w_ffn_out, ln_ffn_gain=m_ln_ffn_gain, ln_ffn_bias=m_ln_ffn_bias)
    mom2 = dict(c_ctx=v_c_ctx, w_ada=v_w_ada, b_ada=v_b_ada, w_in=v_w_in, w_decay_up=v_w_decay_up, b_decay_up=v_b_decay_up,
                gla_norm_gain=v_gla_norm_gain, w_pool_group=v_w_pool_group, pool_scale=v_pool_scale, w_gla_out=v_w_gla_out,
                w_pool_out=v_w_pool_out, w_out=v_w_out, ln_mix_gain=v_ln_mix_gain, ln_mix_bias=v_ln_mix_bias,
                w_ffn_in=v_w_ffn_in, w_ffn_out=v_w_ffn_out, ln_ffn_gain=v_ln_ffn_gain, ln_ffn_bias=v_ln_ffn_bias)
    names = list(weights)

    cfg = _Cfg()
    L, D = w_ada.shape[0], x.shape[-1]
    S, CL = x.shape[1], ctx.shape[1]
    cfg.L, cfg.D, cfg.S, cfg.CL, cfg.R, cfg.TM = L, D, S, CL, S + CL, CL
    DK, DV, DP = D // 2, D, D // 2
    cfg.DK, cfg.DV, cfg.DP = DK, DV, DP
    cfg.HK, cfg.HV, cfg.PG = DK // N_HEADS, DV // N_HEADS, DP // N_POOL
    DFF = w_ffn_out.shape[1] * N_CHIPS
    NP = 2 * DK + 2 * DV + DP + 2 * D + ALR_W
    cfg.NP, cfg.DFF = NP, DFF
    R, TM, HK, HV, PG = cfg.R, cfg.TM, cfg.HK, cfg.HV, cfg.PG
    MT = R // TM
    alpha = (2.0 * L) ** 0.25
    assert S % TM == 0 and TM % CHUNK == 0 and S % GRID_W == 0 and TM % GRID_W == 0
    OFF_G, OFF_P, OFF_BGA, OFF_BGB = 2 * DK + DV, 2 * DK + 2 * DV, 2 * DK + 2 * DV + DP, 2 * DK + 2 * DV + DP + D
    ALR0 = 2 * DK + 2 * DV
    TE = 512
    TL = TM // 2
    assert D % TE == 0 and DP % TE == 0

    xi, yi, ci = _my_place()
    chip = 2 * xi + yi
    dev = 4 * xi + 2 * yi + ci

    n_ada = w_ada.shape[-1]
    c_all = all_gather8("ag_cond", jnp.pad(c.reshape(1, D), ((0, SUBLANES - 1), (0, 0))))[:, 0, :]
    cond = jnp.concatenate([c_all, c_ctx.reshape(1, D), jnp.zeros((16 - N_DEV - 1, D), F32)], axis=0)

    def silu_body(i, j, rows_, vecs):
        return [_silu(rows_[0]), _dsilu(rows_[0])], []

    act, dact = ew("cond_silu", silu_body, 1, 1, 16, D, [(cond, 0, 0)], [], [(F32, 16, D)] * 2)
    w_ada2 = w_ada.reshape(L * D, n_ada)
    b_ada_mine = lax.dynamic_slice_in_dim(b_ada, chip * n_ada, n_ada, axis=1)
    tn_ada = _pick(n_ada, (1024, 512, 256, 128))
    mods = [matmul("ada_fwd", act, w_ada2, 'nn', 16, n_ada, D, tm=16, tn=tn_ada, b_off=(l, 0),
                   bias=_vec8(b_ada_mine[l]), precise=True) for l in range(L)]
    mods_all = all_gather8("ag_mods", jnp.concatenate(mods, axis=0))
    mods_all = mods_all[0::2].reshape(N_CHIPS, L, 16, n_ada).transpose(1, 2, 0, 3).reshape(L, 16, N_MOD * D)
    modv = [_vec8(mods_all[l, N_DEV], lax.dynamic_index_in_dim(mods_all[l], dev, 0, keepdims=False)) for l in range(L)]
    MB = D // TE

    big = ['w_in', 'w_gla_out', 'w_pool_out', 'w_out', 'w_ffn_in', 'w_ffn_out', 'w_pool_group']
    gathered = gather_chips("gather_weights", [weights[nm].astype(WIRE_DTYPE) for nm in big]
                            + [w_decay_up, b_decay_up], n_split=len(big))
    gw = dict(zip(big + ['w_decay_up', 'b_decay_up'], gathered))

    def cols_together(g):
        return jnp.moveaxis(g, 0, -2).reshape(g.shape[1:-1] + (N_CHIPS * g.shape[-1],))

    def rows_together(g):
        return jnp.moveaxis(g, 0, 1).reshape((g.shape[1], N_CHIPS * g.shape[2], g.shape[3]))

    win_ref = cols_together(gw['w_in'])
    win = jnp.concatenate([win_ref[..., :ALR0], win_ref[..., ALR0 + 2 * GATE_RANK:], win_ref[..., ALR0:ALR0 + 2 * GATE_RANK],
                           jnp.zeros((L, D, ALR_W - 2 * GATE_RANK), WIRE_DTYPE)], axis=-1).reshape(L * D, NP)
    wgo = rows_together(gw['w_gla_out']).reshape(L * DV, D)
    wpo = cols_together(gw['w_pool_out']).reshape(L * DP, D)
    wout = rows_together(gw['w_out']).reshape(L * D, D)
    wfi = cols_together(gw['w_ffn_in']).reshape(L * D, 2 * DFF)
    wfo = rows_together(gw['w_ffn_out']).reshape(L * DFF, D)
    wpg = jnp.moveaxis(gw['w_pool_group'], 0, 2).reshape(L, N_POOL, PG, PG)
    wpg_bd = jnp.zeros((L, N_POOL, PG, N_POOL, PG), WIRE_DTYPE)
    for g in range(N_POOL):
        wpg_bd = wpg_bd.at[:, g, :, g, :].set(wpg[:, g])
    wpg_bd = wpg_bd.reshape(L * DP, DP)
    wdu = cols_together(gw['w_decay_up'])
    bdu = cols_together(gw['b_decay_up'])
    wd_pad = [[jnp.zeros((ALR_W, DK), F32).at[d * GATE_RANK:(d + 1) * GATE_RANK].set(wdu[l, d]) for d in range(2)]
              for l in range(L)]
    bd_pad = [[_vec8(bdu[l, d]) for d in range(2)] for l in range(L)]

    tn_np = _pick(NP, (1280, 1024, 768, 512, 256, 128))
    tn_d = _pick(D, (1024, 512, 256, 128))
    tn_ff2 = _pick(2 * DFF, (1024, 512, 256, 128))
    tn_ff = _pick(DFF, (2816, 1408, 1024, 512, 256, 128))
    tn_dp = _pick(DP, (1024, 512, 256, 128))
    tw = lambda n_: _pick(n_, (512, 256, 128))

    gain8 = lambda v_: _vec8(v_)

    xs = jnp.concatenate([ctx.reshape(CL, D), x.reshape(S, D)], axis=0)
    saved = []
    for l in range(L):
        sv = {}
        mv = modv[l]

        def mod_body(i, j, rows_, vecs):
            return [rows_[0] * (1.0 + _cls(vecs[1], i)) + _cls(vecs[0], i)], []

        h1 = ew("modulate", mod_body, MT, D // tn_d, TM, tn_d, [(xs, 0, 0)],
                [(mv, 0 * (D // tn_d)), (mv, 1 * (D // tn_d))], [(MXU_DTYPE, R, D)])[0]
        proj = matmul("proj", h1, win, 'nn', R, NP, D, tm=TM, tn=tn_np, b_off=(l, 0))
        o_f, st_f = gla_fwd("gla_fwd", proj, wd_pad[l][0], bd_pad[l][0], cfg, False)
        o_b, st_b = gla_fwd("gla_rev", proj, wd_pad[l][1], bd_pad[l][1], cfg, True)

        def post_body(i, j, rows_, vecs):
            o = rows_[0] + rows_[1]
            on = o * lax.rsqrt(jnp.mean(o * o, axis=-1, keepdims=True) + RMS_EPS)
            return [on * vecs[0][0:1, :] * _silu(rows_[2])], []

        a_gla = ew("gla_post", post_body, MT, N_HEADS, TM, HV, [(o_f, 0, 0), (o_b, 0, 0), (proj, 0, OFF_G // HV)],
                   [(gain8(gla_norm_gain[l]), 0)], [(MXU_DTYPE, R, DV)])[0]
        y_gla = matmul("gla_out", a_gla, wgo, 'nn', R, D, DV, tm=TM, tn=tn_d, b_off=(l, 0))
        u_pool = pool_mix("pool_fwd", proj, OFF_P // PG, cfg, False)
        t_pool = matmul("pool_group", u_pool, wpg_bd, 'nn', R, DP, DP, tm=TM, tn=tn_dp, b_off=(l, 0))

        def scale_body(i, j, rows_, vecs):
            return [rows_[0] * vecs[0][0:1, :]], []

        ts_pool = ew("pool_scale", scale_body, MT, DP // TE, TM, TE, [(t_pool, 0, 0)], [(gain8(pool_scale[l]), 0)],
                     [(MXU_DTYPE, R, DP)])[0]
        y_pool = matmul("pool_out", ts_pool, wpo, 'nn', R, D, DP, tm=TM, tn=tn_d, b_off=(l, 0))

        def merge_body(i, j, rows_, vecs):
            return [_sigmoid(rows_[2]) * rows_[0] + _sigmoid(rows_[3]) * rows_[1]], []

        mg = ew("merge", merge_body, MT, D // TE, TM, TE,
                [(y_gla, 0, 0), (y_pool, 0, 0), (proj, 0, OFF_BGA // TE), (proj, 0, OFF_BGB // TE)], [],
                [(MXU_DTYPE, R, D)])[0]
        y_mix = matmul("mix_out", mg, wout, 'nn', R, D, D, tm=TM, tn=tn_d, b_off=(l, 0))

        def ln_body(i, j, rows_, vecs):
            r = alpha * rows_[0] + _cls(vecs[0], i) * rows_[1]
            mu = jnp.mean(r, axis=-1, keepdims=True)
            rc = r - mu
            var = jnp.mean(rc * rc, axis=-1, keepdims=True)
            return [rc * lax.rsqrt(var + LN_EPS) * vecs[1][0:1, :] + vecs[2][0:1, :]], []

        x1 = ew("resid_ln", ln_body, R // TL, 1, TL, D, [(xs, 0, 0), (y_mix, 0, 0)],
                [(mv, 2), (gain8(ln_mix_gain[l]), 0), (gain8(ln_mix_bias[l]), 0)], [(F32, R, D)], ctx_tiles=CL // TL)[0]
        h2 = ew("modulate", mod_body, MT, D // tn_d, TM, tn_d, [(x1, 0, 0)],
                [(mv, 3 * (D // tn_d)), (mv, 4 * (D // tn_d))], [(MXU_DTYPE, R, D)])[0]
        u_ffn = matmul("ffn_in", h2, wfi, 'nn', R, 2 * DFF, D, tm=TM, tn=tn_ff2, b_off=(l, 0))
        te_ff = _pick(DFF, (512, 256, 128))

        def swiglu_body(i, j, rows_, vecs):
            return [_silu(rows_[0]) * rows_[1]], []

        s_ffn = ew("swiglu", swiglu_body, MT, DFF // te_ff, TM, te_ff, [(u_ffn, 0, 0), (u_ffn, 0, DFF // te_ff)], [],
                   [(MXU_DTYPE, R, DFF)])[0]
        f_ffn = matmul("ffn_out", s_ffn, wfo, 'nn', R, D, DFF, tm=TM, tn=tn_d, b_off=(l, 0))
        x2 = ew("resid_ln", ln_body, R // TL, 1, TL, D, [(x1, 0, 0), (f_ffn, 0, 0)],
                [(mv, 5), (gain8(ln_ffn_gain[l]), 0), (gain8(ln_ffn_bias[l]), 0)], [(F32, R, D)], ctx_tiles=CL // TL)[0]
        sv.update(xs=xs, h1=h1, proj=proj, o_f=o_f, o_b=o_b, st_f=st_f, st_b=st_b, a_gla=a_gla, y_gla=y_gla, u_pool=u_pool,
                  t_pool=t_pool, ts_pool=ts_pool, y_pool=y_pool, mg=mg, y_mix=y_mix, x1=x1, h2=h2, u_ffn=u_ffn, s_ffn=s_ffn,
                  f_ffn=f_ffn)
        saved.append(sv)
        xs = x2

    tgt = loss_target.reshape(S, D)

    def loss_body(i, j, rows_, vecs):
        d = jnp.where(i, 0.0, rows_[0] - rows_[1])
        return [d * (1.0 / D)], [_colsum(d * d)]

    d_x, sq = ew("loss", loss_body, MT, 1, TM, D, [(xs, 0, 0), (tgt, lambda i: jnp.maximum(i - 1, 0), 0)], [],
                 [(F32, R, D)], [(D, False)])

    def total_body(i, j, rows_, vecs):
        return [jnp.broadcast_to(jnp.sum(rows_[0], axis=-1, keepdims=True), (SUBLANES, D)) * (0.5 / D)], []

    loss_local = ew("loss_total", total_body, 1, 1, SUBLANES, D, [(sq, 0, 0)], [], [(F32, SUBLANES, D)])[0][0, 0]
    loss = lax.psum(loss_local, ("x", "y", "c"))

    gsmall = {nm: [None] * L for nm in ['gla_norm_gain', 'pool_scale', 'ln_mix_gain', 'ln_mix_bias', 'ln_ffn_gain',
                                        'ln_ffn_bias', 'w_decay_up', 'b_decay_up']}
    gbig = {nm: [None] * L for nm in big}
    dmod = [None] * L

    def ln_bwd(name, x_in, br, d_out, mv, gt_blk, gain):
        def body(i, j, rows_, vecs):
            gt = _cls(vecs[0], i)
            r = alpha * rows_[0] + gt * rows_[1]
            mu = jnp.mean(r, axis=-1, keepdims=True)
            rc = r - mu
            rstd = lax.rsqrt(jnp.mean(rc * rc, axis=-1, keepdims=True) + LN_EPS)
            xhat = rc * rstd
            dxh = rows_[2] * vecs[1][0:1, :]
            dr = rstd * (dxh - jnp.mean(dxh, axis=-1, keepdims=True) - xhat * jnp.mean(dxh * xhat, axis=-1, keepdims=True))
            return [dr, gt * dr], [_colsum(rows_[2] * xhat), _colsum(rows_[2]), _colsum(dr * rows_[1])]

        return ew(name, body, R // TL, 1, TL, D, [(x_in, 0, 0), (br, 0, 0), (d_out, 0, 0)], [(mv, gt_blk), (gain8(gain), 0)],
                  [(F32, R, D), (MXU_DTYPE, R, D)], [(D, False), (D, False), (D, True)], ctx_tiles=CL // TL)

    def mod_bwd(name, d_h, x_in, d_r, mv, sc_blk):
        def body(i, j, rows_, vecs):
            return ([rows_[0] * (1.0 + _cls(vecs[0], i)) + alpha * rows_[2]],
                    [_colsum(rows_[0] * rows_[1]), _colsum(rows_[0])])

        return ew(name, body, MT, D // tn_d, TM, tn_d, [(d_h, 0, 0), (x_in, 0, 0), (d_r, 0, 0)],
                  [(mv, sc_blk * (D // tn_d))], [(F32, R, D)], [(D, True), (D, True)])

    for l in reversed(range(L)):
        sv = saved[l]
        mv = modv[l]
        proj = sv['proj']
        dr2, d_f, g_gain, g_bias, g_gt_f = ln_bwd("ln_bwd", sv['x1'], sv['f_ffn'], d_x, mv, 5, ln_ffn_gain[l])
        gsmall['ln_ffn_gain'][l], gsmall['ln_ffn_bias'][l] = g_gain[0], g_bias[0]
        d_s = matmul("ffn_out_dx", d_f, wfo, 'nt', R, DFF, D, tm=TM, tn=tn_ff, b_off=(l * (DFF // tn_ff), 0))
        gbig['w_ffn_out'][l] = matmul("ffn_out_dw", sv['s_ffn'], d_f, 'tn', DFF, D, R, tm=tw(DFF), tn=tn_d)
        te_ff = _pick(DFF, (512, 256, 128))
        nh = DFF // te_ff

        d_u = swiglu_bwd("swiglu_bwd", d_s, sv['u_ffn'], MT, nh, TM, te_ff)
        tn_x = 512
        d_h2 = matmul("ffn_in_dx", d_u, wfi, 'nt', R, D, 2 * DFF, tm=TM, tn=tn_x, b_off=(l * (D // tn_x), 0))
        gbig['w_ffn_in'][l] = matmul("ffn_in_dw", sv['h2'], d_u, 'tn', D, 2 * DFF, R, tm=tw(D), tn=tn_ff2)
        d_x1, g_sc_f, g_sh_f = mod_bwd("mod_bwd", d_h2, sv['x1'], dr2, mv, 4)
        dr1, d_y, g_gain, g_bias, g_gt_m = ln_bwd("ln_bwd", sv['xs'], sv['y_mix'], d_x1, mv, 2, ln_mix_gain[l])
        gsmall['ln_mix_gain'][l], gsmall['ln_mix_bias'][l] = g_gain[0], g_bias[0]
        d_mg = matmul("mix_out_dx", d_y, wout, 'nt', R, D, D, tm=TM, tn=tn_d, b_off=(l * (D // tn_d), 0))
        gbig['w_out'][l] = matmul("mix_out_dw", sv['mg'], d_y, 'tn', D, D, R, tm=tw(D), tn=tn_d)

        def merge_bwd_body(i, j, rows_, vecs):
            d_m, yg, yp, ba, bb = rows_
            sa, sb = _sigmoid(ba), _sigmoid(bb)
            return [d_m * sa, d_m * sb, d_m * yg * sa * (1.0 - sa), d_m * yp * sb * (1.0 - sb)], []

        d_yg, d_yp, d_bga, d_bgb = ew(
            "merge_bwd", merge_bwd_body, MT, D // TE, TM, TE,
            [(d_mg, 0, 0), (sv['y_gla'], 0, 0), (sv['y_pool'], 0, 0), (proj, 0, OFF_BGA // TE), (proj, 0, OFF_BGB // TE)], [],
            [(MXU_DTYPE, R, D), (MXU_DTYPE, R, D), (F32, R, D), (F32, R, D)])
        d_ts = matmul("pool_out_dx", d_yp, wpo, 'nt', R, DP, D, tm=TM, tn=tn_dp, b_off=(l * (DP // tn_dp), 0))
        gbig['w_pool_out'][l] = matmul("pool_out_dw", sv['ts_pool'], d_yp, 'tn', DP, D, R, tm=tw(DP), tn=tn_d)

        def scale_bwd_body(i, j, rows_, vecs):
            return [rows_[0] * vecs[0][0:1, :]], [_colsum(rows_[0] * rows_[1])]

        d_t, g_ps = ew("pool_scale_bwd", scale_bwd_body, MT, DP // TE, TM, TE, [(d_ts, 0, 0), (sv['t_pool'], 0, 0)],
                       [(gain8(pool_scale[l]), 0)], [(MXU_DTYPE, R, DP)], [(DP, False)])
        gsmall['pool_scale'][l] = g_ps[0]
        d_u_pool = matmul("pool_group_dx", d_t, wpg_bd, 'nt', R, DP, DP, tm=TM, tn=tn_dp, b_off=(l * (DP // tn_dp), 0))
        g_bd = matmul("pool_group_dw", sv['u_pool'], d_t, 'tn', DP, DP, R, tm=tw(DP), tn=tn_dp)
        gbig['w_pool_group'][l] = jnp.stack([g_bd[g * PG:(g + 1) * PG, g * PG:(g + 1) * PG] for g in range(N_POOL)])
        d_p = pool_mix("pool_bwd", d_u_pool, 0, cfg, True)
        d_a = matmul("gla_out_dx", d_yg, wgo, 'nt', R, DV, D, tm=TM, tn=tn_d, b_off=(l * (DV // tn_d), 0))
        gbig['w_gla_out'][l] = matmul("gla_out_dw", sv['a_gla'], d_yg, 'tn', DV, D, R, tm=tw(DV), tn=tn_d)

        def post_bwd_body(i, j, rows_, vecs):
            d_a_, o_f_, o_b_, g_ = rows_
            gain = vecs[0][0:1, :]
            o = o_f_ + o_b_
            rstd = lax.rsqrt(jnp.mean(o * o, axis=-1, keepdims=True) + RMS_EPS)
            on = o * rstd
            sg = _silu(g_)
            d_on = d_a_ * gain * sg
            d_o_ = rstd * (d_on - on * jnp.mean(d_on * on, axis=-1, keepdims=True))
            return [d_o_, d_a_ * on * gain * _dsilu(g_)], [_colsum(d_a_ * on * sg)]

        d_o, d_g, g_gng = ew("gla_post_bwd", post_bwd_body, MT, N_HEADS, TM, HV,
                             [(d_a, 0, 0), (sv['o_f'], 0, 0), (sv['o_b'], 0, 0), (proj, 0, OFF_G // HV)],
                             [(gain8(gla_norm_gain[l]), 0)], [(F32, R, DV), (F32, R, DV)], [(DV, False)])
        gsmall['gla_norm_gain'][l] = g_gng[0]
        part = gla_bwd("gla_bwd", proj, wd_pad[l][0], bd_pad[l][0], sv['st_f'], d_o, cfg, False)
        full = gla_bwd("gla_rev_bwd", proj, wd_pad[l][1], bd_pad[l][1], sv['st_b'], d_o, cfg, True, addends=part[:4])
        d_q, d_k, d_v, d_alr = full[:4]
        g_wd = [res[4][d * GATE_RANK:(d + 1) * GATE_RANK] for d, res in enumerate((part, full))]
        g_bd_ = [res[5][0] for res in (part, full)]
        gsmall['w_decay_up'][l], gsmall['b_decay_up'][l] = jnp.stack(g_wd), jnp.stack(g_bd_)
        d_proj = jnp.concatenate([t_.astype(MXU_DTYPE) for t_ in (d_q, d_k, d_v, d_g, d_p, d_bga, d_bgb, d_alr)], axis=-1)
        d_h1 = matmul("proj_dx", d_proj, win, 'nt', R, D, NP, tm=TM, tn=tn_x, b_off=(l * (D // tn_x), 0))
        gbig['w_in'][l] = matmul("proj_dw", sv['h1'], d_proj, 'tn', D, NP, R, tm=tw(D), tn=tn_np)
        d_x, g_sc_m, g_sh_m = mod_bwd("mod_bwd", d_h1, sv['xs'], dr1, mv, 1)
        dmod[l] = jnp.concatenate([g_sh_m[:2], g_sc_m[:2], g_gt_m[:2], g_sh_f[:2], g_sc_f[:2], g_gt_f[:2]], axis=-1)

    grad_x = d_x[CL:].reshape(x.shape)

    dmod = jnp.stack(dmod)
    summed = [dmod[:, 0]] + [jnp.stack(gsmall[nm]) for nm in
                             ['ln_mix_gain', 'ln_mix_bias', 'ln_ffn_gain', 'ln_ffn_bias', 'gla_norm_gain', 'pool_scale',
                              'w_decay_up', 'b_decay_up']]
    pack = _pack([dmod[:, 1]] + summed)
    prow = pack.shape[0]
    packs = all_gather8("ag_small", pack)
    tot = _sum_rows("sum_small", packs.reshape(N_DEV * prow, PACK_W), N_DEV, prow, PACK_W)
    shapes = [(L, N_MOD * D)] + [a.shape for a in summed]
    tot = _unpack(tot, shapes)
    dmod_ctx = tot[1]
    g_rep = dict(zip(['ln_mix_gain', 'ln_mix_bias', 'ln_ffn_gain', 'ln_ffn_bias', 'gla_norm_gain', 'pool_scale'], tot[2:8]))
    g_wdu_full, g_bdu_full = tot[8], tot[9]
    dmod_lat = jnp.stack([_unpack(packs[d_], shapes[:1])[0] for d_ in range(N_DEV)], axis=1)
    dm_all = jnp.concatenate([dmod_lat, dmod_ctx[:, None, :], jnp.zeros((L, 16 - N_DEV - 1, N_MOD * D), F32)], axis=1)

    dm_mine = lax.dynamic_slice_in_dim(dm_all, chip * n_ada, n_ada, axis=2)
    g_w_ada = jnp.stack([matmul("ada_dw", act, dm_mine[l], 'tn', D, n_ada, 16, tm=tw(D), tn=tn_ada, precise=True)
                         for l in range(L)])

    def bsum_body(i, j, rows_, vecs):
        return [jnp.broadcast_to(_colsum(rows_[0]), rows_[0].shape)], []

    g_b_ada = jnp.stack([ew("ada_db", bsum_body, 1, 1, 16, N_MOD * D, [(dm_all[l], 0, 0)], [],
                            [(F32, 16, N_MOD * D)])[0][0] for l in range(L)])
    part_c = [matmul("ada_dc", dm_mine[l], w_ada2, 'nt', 16, D, n_ada, tm=16, tn=tn_d, b_off=(l * (D // tn_d), 0),
                     precise=True)
              for l in range(L)]
    parts_c = all_gather8("ag_dcond", jnp.concatenate(part_c, axis=0))
    dc_rows = parts_c[0::2].reshape(N_CHIPS * L * 16, D)

    def dc_body(i, j, rows_, vecs):
        acc = rows_[0]
        for r_ in rows_[1:-1]:
            acc = acc + r_
        return [acc * rows_[-1]], []

    g_c_ctx = ew("dcond", dc_body, 1, 1, 16, D, [(dc_rows, p_, 0) for p_ in range(N_CHIPS * L)] + [(dact, 0, 0)], [],
                 [(F32, 16, D)])[0][N_DEV]

    def to_ref_cols(g):
        return jnp.concatenate([g[..., :ALR0], g[..., NP - ALR_W:NP - ALR_W + 2 * GATE_RANK], g[..., ALR0:NP - ALR_W]], axis=-1)

    full_g = {nm: jnp.stack(gbig[nm]) for nm in big}
    full_g['w_in'] = to_ref_cols(full_g['w_in'])
    shard_axis = {'w_in': 2, 'w_gla_out': 1, 'w_pool_out': 2, 'w_out': 1, 'w_ffn_in': 2, 'w_ffn_out': 1, 'w_pool_group': 2}

    def shard_of(nm, which):
        ax = shard_axis[nm]
        width = weights[nm].shape[ax]
        return lax.dynamic_slice_in_dim(full_g[nm], which * width, width, axis=ax)

    def halves(a):
        h = a.shape[1] // 2
        return (lax.dynamic_slice_in_dim(a, ci * h, h, axis=1), lax.dynamic_slice_in_dim(a, (1 - ci) * h, h, axis=1))

    keep_r, give_r, keep_o, give_o = [], [], [], []
    for nm in big:
        parts = [halves(shard_of(nm, jnp.bitwise_xor(chip, k)).astype(WIRE_DTYPE)) for k in range(1, N_CHIPS)]
        keep_r.append(jnp.stack([p_[0] for p_ in parts]))
        give_r.append(jnp.stack([p_[1] for p_ in parts]))
        mine_, theirs_ = halves(shard_of(nm, chip))
        keep_o.append(mine_)
        give_o.append(theirs_)
    got = swap_cores("swap_partials", give_r + give_o)
    got_r, got_o = got[:len(big)], got[len(big):]

    def add2_body(i, j, rows__, vecs):
        return [rows__[0].astype(F32) + rows__[1].astype(F32)], []

    send_r = []
    for a_, b_ in zip(keep_r, got_r):
        a2, b2 = _rows2d(a_), _rows2d(b_)
        tm_ = _pick(a2.shape[0], (256, 128, 64, 32, 16, 8))
        send_r.append(ew("add_partials", add2_body, a2.shape[0] // tm_, 1, tm_, a2.shape[1], [(a2, 0, 0), (b2, 0, 0)], [],
                         [(WIRE_DTYPE, a2.shape[0], a2.shape[1])])[0].reshape(a_.shape))
    recvd = scatter_chips("scatter_grads", send_r)
    finished = []
    for a_, b_, rb in zip(keep_o, got_o, recvd):
        a2, b2 = _rows2d(a_), _rows2d(b_)
        rows_, width = a2.shape
        tm_ = _pick(rows_, (256, 128, 64, 32, 16, 8))

        def psum_body(i, j, rows__, vecs):
            return [(rows__[0] + rows__[1]) + rows__[2].astype(F32) + rows__[3].astype(F32) + rows__[4].astype(F32)], []

        rb2 = rb.reshape(3 * rows_, width)
        finished.append(ew("sum_grads", psum_body, rows_ // tm_, 1, tm_, width,
                           [(a2, 0, 0), (b2, 0, 0)] + [(rb2, k * (rows_ // tm_), 0) for k in range(3)], [],
                           [(F32, rows_, width)])[0])
    other = swap_cores("swap_grads", finished)

    out_g, out_d, out_m, out_v = {}, {}, {}, {}
    for nm, mine, theirs in zip(big, finished, other):
        shp = weights[nm].shape
        half_rows = (shp[1] // 2) * (shp[2] if len(shp) == 4 else 1)
        res = _adam("adam_" + nm, _rows2d(weights[nm]), [mine, theirs], _rows2d(mom1[nm]), _rows2d(mom2[nm]),
                    half_rows=half_rows, core=ci)
        out_g[nm], out_d[nm], out_m[nm], out_v[nm] = [r_.reshape(shp) for r_ in res]
    res = _adam("adam_w_ada", _rows2d(w_ada), [_rows2d(g_w_ada)], _rows2d(m_w_ada), _rows2d(v_w_ada))
    out_g['w_ada'], out_d['w_ada'], out_m['w_ada'], out_v['w_ada'] = [r_.reshape(w_ada.shape) for r_ in res]
    n_wd, n_bd = w_decay_up.shape[-1], b_decay_up.shape[-1]
    small_g = dict(g_rep, c_ctx=g_c_ctx, b_ada=g_b_ada,
                   w_decay_up=lax.dynamic_slice_in_dim(g_wdu_full, chip * n_wd, n_wd, axis=3),
                   b_decay_up=lax.dynamic_slice_in_dim(g_bdu_full, chip * n_bd, n_bd, axis=2))
    small = [nm for nm in names if nm not in big and nm != 'w_ada']
    res = _adam("adam_small", _pack([weights[nm] for nm in small]), [_pack([small_g[nm] for nm in small])],
                _pack([mom1[nm] for nm in small]), _pack([mom2[nm] for nm in small]))
    small_shapes = [weights[nm].shape for nm in small]
    for dst, packed in zip((out_g, out_d, out_m, out_v), res):
        for nm, val in zip(small, _unpack(packed, small_shapes)):
            dst[nm] = val

    return (loss, grad_x, *[out_g[nm] for nm in names], *[out_d[nm] for nm in names],
            *[out_m[nm] for nm in names], *[out_v[nm] for nm in names])


def swiglu_bwd(name, d_s, u, nrow, nh, tm, tn):
    def kern(ds_ref, gate_ref, up_ref, o_ref):
        j = pl.program_id(0)

        @pl.when(j < nh)
        def _():
            o_ref[...] = (ds_ref[...] * up_ref[...] * _dsilu(gate_ref[...])).astype(o_ref.dtype)

        @pl.when(j >= nh)
        def _():
            o_ref[...] = (ds_ref[...] * _silu(gate_ref[...])).astype(o_ref.dtype)

    return pl.pallas_call(
        kern, name=name, grid=(2 * nh, nrow),
        in_specs=[pl.BlockSpec((tm, tn), lambda j, i: (i, j % nh)), pl.BlockSpec((tm, tn), lambda j, i: (i, j % nh)),
                  pl.BlockSpec((tm, tn), lambda j, i: (i, nh + j % nh))],
        out_specs=pl.BlockSpec((tm, tn), lambda j, i: (i, j)),
        out_shape=jax.ShapeDtypeStruct((u.shape[0], 2 * nh * tn), MXU_DTYPE), compiler_params=_params(),
    )(d_s, u, u)
```

```python
import functools

import jax
import jax.numpy as jnp
from jax import lax
from jax.experimental import pallas as pl
from jax.experimental.pallas import tpu as pltpu

F32 = jnp.float32
MXU_DTYPE = jnp.bfloat16
WIRE_DTYPE = jnp.bfloat16

GRID_W = 64
CHUNK = 64
N_HEADS = 4
GATE_RANK = 16
GATE_NORM = 16.0
N_MOD = 6
N_POOL = 4
LN_EPS = 1e-5
RMS_EPS = 1e-6
ADAM_LR = 0.001
ADAM_B1 = 0.9
ADAM_B2 = 0.999
ADAM_EPS = 1e-08
ADAM_WD = 0.01
ADAM_STEP = 10

LANES = 128
SUBLANES = 8
ALR_W = 256
PACK_W = 2048
VMEM_LIMIT = 56 * 1024 * 1024
N_CHIPS = 4
N_DEV = 8
MESH = pl.DeviceIdType.MESH

NN = ((1,), (0,))
NT = ((1,), (1,))
TN = ((0,), (0,))


def _dot(a, b, dims):
    return lax.dot_general(a.astype(MXU_DTYPE), b.astype(MXU_DTYPE), (dims, ((), ())),
                           preferred_element_type=F32)


def _dot_f32(a, b, dims):
    return lax.dot_general(a.astype(F32), b.astype(F32), (dims, ((), ())),
                           precision=lax.Precision.HIGHEST, preferred_element_type=F32)


def _dot_mask(mask, x, dims):
    m = mask.astype(MXU_DTYPE)
    if MXU_DTYPE == F32:
        return lax.dot_general(m, x, (dims, ((), ())), preferred_element_type=F32)
    acc = None
    rest = x
    for _ in range(3):
        piece = rest.astype(MXU_DTYPE)
        rest = rest - piece.astype(F32)
        part = lax.dot_general(m, piece, (dims, ((), ())), preferred_element_type=F32)
        acc = part if acc is None else acc + part
    return acc


def _dot_3x(a, b, dims):
    if MXU_DTYPE == F32:
        return lax.dot_general(a, b, (dims, ((), ())), preferred_element_type=F32)
    a_hi, b_hi = a.astype(MXU_DTYPE), b.astype(MXU_DTYPE)
    a_lo = (a - a_hi.astype(F32)).astype(MXU_DTYPE)
    b_lo = (b - b_hi.astype(F32)).astype(MXU_DTYPE)
    dot = lambda u, w: lax.dot_general(u, w, (dims, ((), ())), preferred_element_type=F32)
    return dot(a_hi, b_hi) + (dot(a_lo, b_hi) + dot(a_hi, b_lo))


def _pick(n, cands):
    for c in cands:
        if n % c == 0:
            return c
    return n


def _params():
    return pltpu.CompilerParams(vmem_limit_bytes=VMEM_LIMIT)


def _sigmoid(x):
    return 0.5 + 0.5 * jnp.tanh(0.5 * x)


def _silu(x):
    return x * _sigmoid(x)


def _dsilu(x):
    s = _sigmoid(x)
    return s * (1.0 + x * (1.0 - s))


def matmul(name, a, b, form, m, n, k, *, tm, tn, out_dtype=F32, a_off=(0, 0), b_off=(0, 0), bias=None, bias_off=0,
           precise=False, comm=None):
    assert m % tm == 0 and n % tn == 0, (name, m, n, tm, tn)
    if form == 'tn':
        a_spec = pl.BlockSpec((k, tm), lambda j, i: (a_off[0], i + a_off[1]))
    else:
        a_spec = pl.BlockSpec((tm, k), lambda j, i: (i + a_off[0], a_off[1]))
    if form == 'nt':
        b_spec = pl.BlockSpec((tn, k), lambda j, i: (j + b_off[0], b_off[1]))
    else:
        b_spec = pl.BlockSpec((k, tn), lambda j, i: (b_off[0], j + b_off[1]))
    dims = {'nn': NN, 'nt': NT, 'tn': TN}[form]
    in_specs = [a_spec, b_spec]
    args = [a, b]
    if bias is not None:
        in_specs.append(pl.BlockSpec((SUBLANES, tn), lambda j, i: (0, j + bias_off)))
        args.append(bias)

    n_own = len(args)
    nj, ni = n // tn, m // tm
    out_spec = pl.BlockSpec((tm, tn), lambda j, i: (i, j))
    out_shape = jax.ShapeDtypeStruct((m, n), out_dtype)
    if comm is None:
        def body(*refs):
            acc = (_dot_f32 if precise else _dot)(refs[0][...], refs[1][...], dims)
            if bias is not None:
                acc = acc + refs[2][0:1, :]
            refs[-1][...] = acc.astype(refs[-1].dtype)

        return pl.pallas_call(body, name=name, grid=(nj, ni), in_specs=in_specs, out_specs=out_spec,
                              out_shape=out_shape, compiler_params=_params())(*args)

    n_ci, n_co = len(comm.ins), len(comm.outs)

    def hosted(*refs):
        c_in = refs[n_own:n_own + n_ci]
        o_ref = refs[n_own + n_ci]
        c_out = refs[n_own + n_ci + 1:n_own + n_ci + 1 + n_co]
        sems = refs[n_own + n_ci + 1 + n_co:]
        j, i = pl.program_id(0), pl.program_id(1)

        @pl.when((j == 0) & (i == 0))
        def _():
            comm.start(c_in, c_out, sems)

        acc = (_dot_f32 if precise else _dot)(refs[0][...], refs[1][...], dims)
        if bias is not None:
            acc = acc + refs[2][0:1, :]
        o_ref[...] = acc.astype(o_ref.dtype)

        @pl.when((j == nj - 1) & (i == ni - 1))
        def _():
            comm.finish(c_in, c_out, sems)

    any_spec = pl.BlockSpec(memory_space=pl.ANY)
    res = pl.pallas_call(
        hosted, name=name, grid=(nj, ni), in_specs=in_specs + [any_spec] * n_ci,
        out_specs=[out_spec] + [any_spec] * n_co, out_shape=[out_shape] + list(comm.outs),
        scratch_shapes=list(comm.sems), compiler_params=_params(),
    )(*args, *comm.ins)
    return res[0], list(res[1:])


def ew(name, body, nrow, ncol, tm, tn, row_ins, vec_ins, row_outs, sum_outs=(), ctx_tiles=1, pass_i=False):
    def rmap(roff):
        return roff if callable(roff) else (lambda i: i + roff)

    in_specs = []
    for arr, roff, coff in row_ins:
        in_specs.append(pl.BlockSpec((tm, tn), functools.partial(lambda j, i, r, c: (r(i), j + c), r=rmap(roff), c=coff)))
    for arr, coff in vec_ins:
        in_specs.append(pl.BlockSpec((SUBLANES, tn), functools.partial(lambda j, i, c: (0, j + c), c=coff)))
    out_specs = [pl.BlockSpec((tm, tn), lambda j, i: (i, j)) for _ in row_outs]
    out_specs += [pl.BlockSpec((SUBLANES, tn), lambda j, i: (0, j)) for _ in sum_outs]
    out_shape = [jax.ShapeDtypeStruct((r, c), dt) for dt, r, c in row_outs]
    out_shape += [jax.ShapeDtypeStruct((SUBLANES, c), F32) for c, _ in sum_outs]
    n_row, n_vec, n_ro = len(row_ins), len(vec_ins), len(row_outs)

    def kern(*refs):
        j, i = pl.program_id(0), pl.program_id(1)
        rows = [r[...] for r in refs[:n_row]]
        vecs = [r[...] for r in refs[n_row:n_row + n_vec]]
        outs = refs[n_row + n_vec:]
        is_ctx = i < ctx_tiles
        res, sums = body(i if pass_i else is_ctx, j, rows, vecs)
        for ref, val in zip(outs[:n_ro], res):
            ref[...] = val.astype(ref.dtype)
        for ref, val, (_, by_class) in zip(outs[n_ro:], sums, sum_outs):
            @pl.when(i == 0)
            def _():
                ref[...] = jnp.zeros_like(ref)
            if by_class:
                ref[0:1, :] += jnp.where(is_ctx, val, 0.0)
                ref[1:2, :] += jnp.where(is_ctx, 0.0, val)
            else:
                ref[0:1, :] += val

    outs = pl.pallas_call(
        kern, name=name, grid=(ncol, nrow), in_specs=in_specs, out_specs=out_specs, out_shape=out_shape,
        compiler_params=_params(),
    )(*[a for a, _, _ in row_ins], *[a for a, _ in vec_ins])
    return list(outs)


def _cls(vec, is_ctx):
    return jnp.where(is_ctx, vec[0:1, :], vec[1:2, :])


def _colsum(x):
    return jnp.sum(x, axis=0, keepdims=True)


def _chunk_map(cfg, rev):
    nctx, nc = cfg.CL // CHUNK, cfg.R // CHUNK
    if not rev:
        return lambda s: s
    return lambda s: jnp.where(s < nctx, nctx - 1 - s, nctx + nc - 1 - s)


def _gla_chunk(q_ref, k_ref, a_ref, wd_ref, bd_ref, rev, scale):
    q = q_ref[...] * scale
    k = k_ref[...]
    z = _dot_3x(a_ref[...], wd_ref[...], NN) + bd_ref[0:1, :]
    la = (jnp.minimum(z, 0.0) - jnp.log(1.0 + jnp.exp(-jnp.abs(z)))) * (1.0 / GATE_NORM)
    r = lax.broadcasted_iota(jnp.int32, (CHUNK, CHUNK), 0)
    c = lax.broadcasted_iota(jnp.int32, (CHUNK, CHUNK), 1)
    keep = (r <= c) if rev else (r >= c)
    tri = keep.astype(F32)
    cum = _dot_mask(tri, la, NN)
    mid = CHUNK // 2 if rev else CHUNK // 2 - 1
    end = 0 if rev else CHUNK - 1
    ref = cum[mid:mid + 1, :]
    last = cum[end:end + 1, :]
    return dict(q=q, k=k, z=z, keep=keep, tri=tri, q_in=q * jnp.exp(cum - ref), k_in=k * jnp.exp(ref - cum),
                e_q=jnp.exp(cum), e_k=jnp.exp(last - cum), e_inq=jnp.exp(cum - ref), e_ink=jnp.exp(ref - cum),
                e_last=jnp.exp(last))


def _gla_in_specs(cfg, rows_of):
    dk, dv = cfg.DK, cfg.DV
    return [
        pl.BlockSpec((CHUNK, dk), lambda s: (rows_of(s), 0)),
        pl.BlockSpec((CHUNK, dk), lambda s: (rows_of(s), 1)),
        pl.BlockSpec((CHUNK, dv), lambda s: (rows_of(s), 2 * dk // dv)),
        pl.BlockSpec((CHUNK, ALR_W), lambda s: (rows_of(s), (cfg.NP - ALR_W) // ALR_W)),
        pl.BlockSpec((ALR_W, dk), lambda s: (0, 0)),
        pl.BlockSpec((SUBLANES, dk), lambda s: (0, 0)),
    ]


def gla_fwd(name, proj, wd, bd, cfg, rev):
    hk, hv = cfg.HK, cfg.HV
    nc = cfg.R // CHUNK
    cmap = _chunk_map(cfg, rev)
    scale = hk ** -0.5

    def body(q_ref, k_ref, v_ref, a_ref, wd_ref, bd_ref, o_ref, ss_ref, st_scr):
        s = pl.program_id(0)

        @pl.when(s == 0)
        def _():
            st_scr[...] = jnp.zeros_like(st_scr)

        t = _gla_chunk(q_ref, k_ref, a_ref, wd_ref, bd_ref, rev, scale)
        q_int, k_st = t['q'] * t['e_q'], t['k'] * t['e_k']
        for h in range(N_HEADS):
            ks, vs = slice(h * hk, (h + 1) * hk), slice(h * hv, (h + 1) * hv)
            v = v_ref[:, vs]
            st = st_scr[h]
            ss_ref[0, h] = st
            a = jnp.where(t['keep'], _dot(t['q_in'][:, ks], t['k_in'][:, ks], NT), 0.0)
            o_ref[:, vs] = _dot(a, v, NN) + _dot(q_int[:, ks], st, NT)
            st_scr[h] = st * t['e_last'][:, ks] + _dot(v, k_st[:, ks], TN)

    return pl.pallas_call(
        body, name=name, grid=(nc,), in_specs=_gla_in_specs(cfg, cmap),
        out_specs=[pl.BlockSpec((CHUNK, cfg.DV), lambda s: (cmap(s), 0)),
                   pl.BlockSpec((1, N_HEADS, hv, hk), lambda s: (s, 0, 0, 0))],
        out_shape=[jax.ShapeDtypeStruct((cfg.R, cfg.DV), F32),
                   jax.ShapeDtypeStruct((nc, N_HEADS, hv, hk), F32)],
        scratch_shapes=[pltpu.VMEM((N_HEADS, hv, hk), F32)], compiler_params=_params(),
    )(proj, proj, proj, proj, wd, bd)


def gla_bwd(name, proj, wd, bd, states, d_o, cfg, rev, addends=None):
    hk, hv = cfg.HK, cfg.HV
    nc = cfg.R // CHUNK
    cmap = _chunk_map(cfg, rev)
    rows_of = lambda g: cmap(nc - 1 - g)
    scale = hk ** -0.5
    n_add = 0 if addends is None else 4

    def body(*refs):
        q_ref, k_ref, v_ref, a_ref, wd_ref, bd_ref, ss_ref, do_ref = refs[:8]
        adds = refs[8:8 + n_add]
        dq_ref, dk_ref, dv_ref, da_ref, dwd_ref, dbd_ref, dst_scr = refs[8 + n_add:]
        g = pl.program_id(0)

        @pl.when(g == 0)
        def _():
            dst_scr[...] = jnp.zeros_like(dst_scr)
            dwd_ref[...] = jnp.zeros_like(dwd_ref)
            dbd_ref[...] = jnp.zeros_like(dbd_ref)

        t = _gla_chunk(q_ref, k_ref, a_ref, wd_ref, bd_ref, rev, scale)
        q_int, k_st = t['q'] * t['e_q'], t['k'] * t['e_k']
        dq_h, dk_h, carry_h = [], [], []
        for h in range(N_HEADS):
            ks, vs = slice(h * hk, (h + 1) * hk), slice(h * hv, (h + 1) * hv)
            v = v_ref[:, vs]
            d_out = do_ref[:, vs]
            st = ss_ref[0, h]
            dst = dst_scr[h]
            a = jnp.where(t['keep'], _dot(t['q_in'][:, ks], t['k_in'][:, ks], NT), 0.0)
            da = jnp.where(t['keep'], _dot(d_out, v, NT), 0.0)
            dv = _dot(a, d_out, TN) + _dot(k_st[:, ks], dst, NT)
            dq_h.append(_dot(d_out, st, NN) * t['e_q'][:, ks] + _dot(da, t['k_in'][:, ks], NN) * t['e_inq'][:, ks])
            dk_h.append(_dot(v, dst, NN) * t['e_k'][:, ks] + _dot(da, t['q_in'][:, ks], TN) * t['e_ink'][:, ks])
            dst_scr[h] = dst * t['e_last'][:, ks] + _dot(d_out, q_int[:, ks], TN)
            st_end = st * t['e_last'][:, ks] + _dot(v, k_st[:, ks], TN)
            carry_h.append(_colsum(dst * st_end))
            dv_ref[:, vs] = dv + adds[2][:, vs] if n_add else dv
        dq = jnp.concatenate(dq_h, axis=-1)
        dk = jnp.concatenate(dk_h, axis=-1)
        dg = t['q'] * dq - t['k'] * dk
        dla = _dot_mask(t['tri'], dg, TN) + jnp.concatenate(carry_h, axis=-1)
        dz = dla * (1.0 / GATE_NORM) * _sigmoid(-t['z'])
        dalr = _dot_3x(dz, wd_ref[...], NT)
        dwd_ref[...] += _dot_3x(a_ref[...], dz, TN)
        dbd_ref[...] += jnp.broadcast_to(_colsum(dz), dbd_ref.shape)
        dq = dq * scale
        if n_add:
            dq, dk, dalr = dq + adds[0][...], dk + adds[1][...], dalr + adds[3][...]
        dq_ref[...] = dq
        dk_ref[...] = dk
        da_ref[...] = dalr

    qk_spec = pl.BlockSpec((CHUNK, cfg.DK), lambda g: (rows_of(g), 0))
    v_spec = pl.BlockSpec((CHUNK, cfg.DV), lambda g: (rows_of(g), 0))
    a_spec = pl.BlockSpec((CHUNK, ALR_W), lambda g: (rows_of(g), 0))
    in_specs = _gla_in_specs(cfg, rows_of) + [
        pl.BlockSpec((1, N_HEADS, hv, hk), lambda g: (nc - 1 - g, 0, 0, 0)), v_spec]
    args = [proj, proj, proj, proj, wd, bd, states, d_o]
    if n_add:
        in_specs += [qk_spec, qk_spec, v_spec, a_spec]
        args += list(addends)
    return pl.pallas_call(
        body, name=name, grid=(nc,), in_specs=in_specs,
        out_specs=[qk_spec, qk_spec, v_spec, a_spec,
                   pl.BlockSpec((ALR_W, cfg.DK), lambda g: (0, 0)),
                   pl.BlockSpec((SUBLANES, cfg.DK), lambda g: (0, 0))],
        out_shape=[jax.ShapeDtypeStruct((cfg.R, cfg.DK), F32), jax.ShapeDtypeStruct((cfg.R, cfg.DK), F32),
                   jax.ShapeDtypeStruct((cfg.R, cfg.DV), F32), jax.ShapeDtypeStruct((cfg.R, ALR_W), F32),
                   jax.ShapeDtypeStruct((ALR_W, cfg.DK), F32),
                   jax.ShapeDtypeStruct((SUBLANES, cfg.DK), F32)],
        scratch_shapes=[pltpu.VMEM((N_HEADS, hv, hk), F32)],
        compiler_params=_params(),
    )(*args)


def pool_mix(name, src, coff, cfg, transpose):
    tm, pg = cfg.TM, cfg.PG
    mt = cfg.R // tm
    reach = -(-(max(2 ** N_POOL // 2, 1) * GRID_W) // tm)
    nk = 2 * reach + 1
    img_rows = cfg.S // GRID_W
    shift = GRID_W.bit_length() - 1

    def ktile(m, d):
        return jnp.where(m == 0, 0, jnp.clip(m + d - reach, 1, mt - 1))

    def counts(idx, is_ctx, lo, hi):
        ctx_n = jnp.minimum(idx + hi + 1, cfg.CL) - jnp.maximum(idx - lo, 0)
        r, c = idx >> shift, idx & (GRID_W - 1)
        lat_n = ((jnp.minimum(r + hi + 1, img_rows) - jnp.maximum(r - lo, 0))
                 * (jnp.minimum(c + hi + 1, GRID_W) - jnp.maximum(c - lo, 0)))
        return jnp.where(is_ctx, ctx_n, lat_n).astype(F32)

    def body(src_ref, self_ref, o_ref, acc):
        g, m, d = pl.program_id(0), pl.program_id(1), pl.program_id(2)
        lo = jnp.left_shift(1, g)
        hi = lo - 1
        is_ctx = m == 0
        kt = m + d - reach
        valid = jnp.where(is_ctx, d == reach, (kt >= 1) & (kt <= mt - 1))
        seg = jnp.where(is_ctx, 0, cfg.CL)

        @pl.when(d == 0)
        def _():
            acc[...] = jnp.zeros_like(acc)

        @pl.when(valid)
        def _():
            row = lax.broadcasted_iota(jnp.int32, (tm, tm), 0) + (m * tm - seg)
            col = lax.broadcasted_iota(jnp.int32, (tm, tm), 1) + (kt * tm - seg)
            ctr, mem = (col, row) if transpose else (row, col)
            in_ctx = (mem >= ctr - lo) & (mem <= ctr + hi)
            cr, cc = ctr >> shift, ctr & (GRID_W - 1)
            mr, mc = mem >> shift, mem & (GRID_W - 1)
            in_lat = (mr >= cr - lo) & (mr <= cr + hi) & (mc >= cc - lo) & (mc <= cc + hi)
            mask = jnp.where(is_ctx, in_ctx.astype(F32), in_lat.astype(F32))
            x = src_ref[...]
            if transpose:
                kidx = lax.broadcasted_iota(jnp.int32, (tm, 1), 0) + (kt * tm - seg)
                x = x / counts(kidx, is_ctx, lo, hi)
            acc[...] += _dot_mask(mask, x, NN)

        @pl.when(d == nk - 1)
        def _():
            res = acc[...]
            if not transpose:
                midx = lax.broadcasted_iota(jnp.int32, (tm, 1), 0) + (m * tm - seg)
                res = res / counts(midx, is_ctx, lo, hi)
            o_ref[...] = (res - self_ref[...]).astype(o_ref.dtype)

    return pl.pallas_call(
        body, name=name, grid=(N_POOL, mt, nk),
        in_specs=[pl.BlockSpec((tm, pg), lambda g, m, d: (ktile(m, d), coff + g)),
                  pl.BlockSpec((tm, pg), lambda g, m, d: (m, coff + g))],
        out_specs=pl.BlockSpec((tm, pg), lambda g, m, d: (m, g)),
        out_shape=jax.ShapeDtypeStruct((cfg.R, cfg.DP), F32),
        scratch_shapes=[pltpu.VMEM((tm, pg), F32)], compiler_params=_params(),
    )(src, src)


def _my_place():
    return lax.axis_index("x"), lax.axis_index("y"), lax.axis_index("c")


def _flip(v, bit):
    return 1 - v if bit else v


def all_gather8(name, block):
    rows, w = block.shape

    def body(x_ref, out_ref, send_sems, recv_sems, local_sem):
        x, y, c = _my_place()
        me = 4 * x + 2 * y + c
        mine = pltpu.make_async_copy(x_ref, out_ref.at[me], local_sem)
        mine.start()
        sends = []
        for k in range(1, N_DEV):
            peer = (_flip(x, k & 4), _flip(y, k & 2), _flip(c, k & 1))
            cp = pltpu.make_async_remote_copy(src_ref=x_ref, dst_ref=out_ref.at[me], send_sem=send_sems.at[k - 1],
                                              recv_sem=recv_sems.at[k - 1], device_id=peer, device_id_type=MESH)
            cp.start()
            sends.append(cp)
        for k in range(1, N_DEV):
            peer = (_flip(x, k & 4), _flip(y, k & 2), _flip(c, k & 1))
            slot = 4 * peer[0] + 2 * peer[1] + peer[2]
            pltpu.make_async_remote_copy(src_ref=x_ref, dst_ref=out_ref.at[slot], send_sem=send_sems.at[k - 1],
                                         recv_sem=recv_sems.at[k - 1], device_id=peer, device_id_type=MESH).wait_recv()
        for cp in sends:
            cp.wait_send()
        mine.wait()

    return pl.pallas_call(
        body, name=name, out_shape=jax.ShapeDtypeStruct((N_DEV, rows, w), block.dtype),
        in_specs=[pl.BlockSpec(memory_space=pl.ANY)], out_specs=pl.BlockSpec(memory_space=pl.ANY),
        scratch_shapes=[pltpu.SemaphoreType.DMA((N_DEV - 1,)), pltpu.SemaphoreType.DMA((N_DEV - 1,)),
                        pltpu.SemaphoreType.DMA],
    )(block)


class _Comm:
    def __init__(self, ins, outs, sems, start, finish):
        self.ins, self.outs, self.sems, self.start, self.finish = list(ins), list(outs), list(sems), start, finish


def run_comm(name, comm):
    def body(*refs):
        n_i, n_o = len(comm.ins), len(comm.outs)
        comm.start(refs[:n_i], refs[n_i:n_i + n_o], refs[n_i + n_o:])
        comm.finish(refs[:n_i], refs[n_i:n_i + n_o], refs[n_i + n_o:])

    any_spec = pl.BlockSpec(memory_space=pl.ANY)
    return list(pl.pallas_call(body, name=name, out_shape=comm.outs, in_specs=[any_spec] * len(comm.ins),
                               out_specs=[any_spec] * len(comm.outs), scratch_shapes=comm.sems)(*comm.ins))


def gather_plan(shards, n_split):
    n = len(shards)

    def part(ref, t, core):
        if t >= n_split:
            return ref
        h = shards[t].shape[0] // 2
        return ref.at[pl.ds(core * h, h)]

    def ici(srcs, dsts, sems, k, t, slot, place):
        x, y, c = place
        return pltpu.make_async_remote_copy(
            src_ref=part(srcs[t], t, c), dst_ref=part(dsts[t].at[slot], t, c), send_sem=sems[0].at[t * 3 + k - 1],
            recv_sem=sems[1].at[t * 3 + k - 1], device_id=(_flip(x, k & 2), _flip(y, k & 1), c), device_id_type=MESH)

    def handed(dsts, sems, k, t, slot, place, core):
        x, y, c = place
        half = part(dsts[t].at[slot], t, core)
        return pltpu.make_async_remote_copy(
            src_ref=half, dst_ref=half, send_sem=sems[2].at[t * 3 + k - 1], recv_sem=sems[3].at[t * 3 + k - 1],
            device_id=(x, y, 1 - c), device_id_type=MESH)

    def start(srcs, dsts, sems):
        place = _my_place()
        me = 2 * place[0] + place[1]
        for t in range(n):
            pltpu.make_async_copy(srcs[t], dsts[t].at[me], sems[4].at[t]).start()
        for k in range(1, N_CHIPS):
            for t in range(n):
                ici(srcs, dsts, sems, k, t, me, place).start()

    def finish(srcs, dsts, sems):
        place = _my_place()
        x, y, c = place
        me = 2 * x + y
        for k in range(1, N_CHIPS):
            slot = 2 * _flip(x, k & 2) + _flip(y, k & 1)
            for t in range(n):
                ici(srcs, dsts, sems, k, t, slot, place).wait_recv()
                if t < n_split:
                    handed(dsts, sems, k, t, slot, place, c).start()
        for k in range(1, N_CHIPS):
            slot = 2 * _flip(x, k & 2) + _flip(y, k & 1)
            for t in range(n_split):
                handed(dsts, sems, k, t, slot, place, 1 - c).wait_recv()
        for k in range(1, N_CHIPS):
            slot = 2 * _flip(x, k & 2) + _flip(y, k & 1)
            for t in range(n):
                ici(srcs, dsts, sems, k, t, me, place).wait_send()
                if t < n_split:
                    handed(dsts, sems, k, t, slot, place, c).wait_send()
        for t in range(n):
            pltpu.make_async_copy(srcs[t], dsts[t].at[me], sems[4].at[t]).wait()

    outs = [jax.ShapeDtypeStruct((N_CHIPS,) + s.shape, s.dtype) for s in shards]
    sems = [pltpu.SemaphoreType.DMA((3 * n,))] * 4 + [pltpu.SemaphoreType.DMA((n,))]
    return _Comm(shards, outs, sems, start, finish)


def scatter_plan(bufs):
    n = len(bufs)

    def copies(srcs, dsts, sems):
        x, y, c = _my_place()
        return [pltpu.make_async_remote_copy(
            src_ref=srcs[t].at[k - 1], dst_ref=dsts[t].at[k - 1], send_sem=sems[0].at[t * 3 + k - 1],
            recv_sem=sems[1].at[t * 3 + k - 1], device_id=(_flip(x, k & 2), _flip(y, k & 1), c), device_id_type=MESH)
            for k in range(1, N_CHIPS) for t in range(n)]

    def start(srcs, dsts, sems):
        for cp in copies(srcs, dsts, sems):
            cp.start()

    def finish(srcs, dsts, sems):
        for cp in copies(srcs, dsts, sems):
            cp.wait_recv()
        for cp in copies(srcs, dsts, sems):
            cp.wait_send()

    return _Comm(bufs, [jax.ShapeDtypeStruct(b.shape, b.dtype) for b in bufs],
                 [pltpu.SemaphoreType.DMA((3 * n,)), pltpu.SemaphoreType.DMA((3 * n,))], start, finish)


def swap_plan(bufs):
    n = len(bufs)

    def copies(srcs, dsts, sems):
        x, y, c = _my_place()
        return [pltpu.make_async_remote_copy(src_ref=srcs[t], dst_ref=dsts[t], send_sem=sems[0].at[t],
                                             recv_sem=sems[1].at[t], device_id=(x, y, 1 - c), device_id_type=MESH)
                for t in range(n)]

    def start(srcs, dsts, sems):
        for cp in copies(srcs, dsts, sems):
            cp.start()

    def finish(srcs, dsts, sems):
        for cp in copies(srcs, dsts, sems):
            cp.wait_recv()
        for cp in copies(srcs, dsts, sems):
            cp.wait_send()

    return _Comm(bufs, [jax.ShapeDtypeStruct(b.shape, b.dtype) for b in bufs],
                 [pltpu.SemaphoreType.DMA((n,)), pltpu.SemaphoreType.DMA((n,))], start, finish)


def _vec8(*rows):
    w = rows[0].shape[-1]
    out = jnp.zeros((SUBLANES, w), F32)
    for r, v in enumerate(rows):
        out = out.at[r].set(v.reshape(w).astype(F32))
    return out


def _pack(arrays):
    parts = []
    for a in arrays:
        flat = a.reshape(-1).astype(F32)
        pad = (-flat.shape[0]) % PACK_W
        parts.append(jnp.pad(flat, (0, pad)))
    flat = jnp.concatenate(parts)
    pad = (-flat.shape[0]) % (PACK_W * SUBLANES)
    return jnp.pad(flat, (0, pad)).reshape(-1, PACK_W)


def _unpack(packed, shapes):
    flat = packed.reshape(-1)
    out, pos = [], 0
    for shp in shapes:
        size = 1
        for d in shp:
            size *= d
        out.append(flat[pos:pos + size].reshape(shp))
        pos += size + (-size) % PACK_W
    return out


def _rows2d(a):
    return a.reshape(-1, a.shape[-1])


class _Cfg:
    pass


def _adam(name, w, grads, m, v, half_rows=None, core=None):
    rows, width = w.shape
    c1 = 1.0 - ADAM_B1 ** ADAM_STEP
    c2 = 1.0 - ADAM_B2 ** ADAM_STEP

    def update(wv, mv, vv, g):
        m_new = ADAM_B1 * mv + (1.0 - ADAM_B1) * g
        v_new = ADAM_B2 * vv + (1.0 - ADAM_B2) * (g * g)
        delta = -ADAM_LR * ((m_new / c1) / (jnp.sqrt(v_new / c2) + ADAM_EPS) + ADAM_WD * wv)
        return [g, delta, m_new, v_new], []

    if half_rows is None:
        tm = _pick(rows, (128, 64, 32, 16, 8))

        def body(i, j, rows_, vecs):
            g = rows_[3]
            for extra in rows_[4:]:
                g = g + extra
            return update(rows_[0], rows_[1], rows_[2], g)

        return ew(name, body, rows // tm, 1, tm, width, [(w, 0, 0), (m, 0, 0), (v, 0, 0)] + [(g, 0, 0) for g in grads], [],
                  [(F32, rows, width)] * 4)

    tm = _pick(half_rows, (128, 64, 32, 16, 8))
    nb = half_rows // tm
    run_tile = lambda i: (i // (2 * nb)) * nb + i % nb
    core_vec = jnp.broadcast_to(core.astype(F32), (SUBLANES, width))

    def body(i, j, rows_, vecs):
        owner = ((i // nb) % 2).astype(F32)
        g = jnp.where(vecs[0][0:1, :] == owner, rows_[3], rows_[4])
        return update(rows_[0], rows_[1], rows_[2], g)

    return ew(name, body, rows // tm, 1, tm, width,
              [(w, 0, 0), (m, 0, 0), (v, 0, 0), (grads[0], run_tile, 0), (grads[1], run_tile, 0)], [(core_vec, 0)],
              [(F32, rows, width)] * 4, pass_i=True)


def _sum_rows(name, arr, nparts, rows, width, dtype=F32):
    tm = _pick(rows, (256, 128, 64, 32, 16, 8))
    nblk = rows // tm

    def body(i, j, rows_, vecs):
        acc = rows_[0].astype(F32)
        for r in rows_[1:]:
            acc = acc + r.astype(F32)
        return [acc], []

    return ew(name, body, nblk, 1, tm, width, [(arr, p * nblk, 0) for p in range(nparts)], [], [(dtype, rows, width)])[0]


def kernel(x, c, ctx, c_ctx, w_ada, b_ada, w_in, w_decay_up, b_decay_up, gla_norm_gain, w_pool_group, pool_scale, w_gla_out, w_pool_out, w_out, ln_mix_gain, ln_mix_bias, w_ffn_in, w_ffn_out, ln_ffn_gain, ln_ffn_bias, loss_target, m_c_ctx, m_w_ada, m_b_ada, m_w_in, m_w_decay_up, m_b_decay_up, m_gla_norm_gain, m_w_pool_group, m_pool_scale, m_w_gla_out, m_w_pool_out, m_w_out, m_ln_mix_gain, m_ln_mix_bias, m_w_ffn_in, m_w_ffn_out, m_ln_ffn_gain, m_ln_ffn_bias, v_c_ctx, v_w_ada, v_b_ada, v_w_in, v_w_decay_up, v_b_decay_up, v_gla_norm_gain, v_w_pool_group, v_pool_scale, v_w_gla_out, v_w_pool_out, v_w_out, v_ln_mix_gain, v_ln_mix_bias, v_w_ffn_in, v_w_ffn_out, v_ln_ffn_gain, v_ln_ffn_bias):
    weights = dict(c_ctx=c_ctx, w_ada=w_ada, b_ada=b_ada, w_in=w_in, w_decay_up=w_decay_up, b_decay_up=b_decay_up,
                   gla_norm_gain=gla_norm_gain, w_pool_group=w_pool_group, pool_scale=pool_scale, w_gla_out=w_gla_out,
                   w_pool_out=w_pool_out, w_out=w_out, ln_mix_gain=ln_mix_gain, ln_mix_bias=ln_mix_bias,
                   w_ffn_in=w_ffn_in, w_ffn_out=w_ffn_out, ln_ffn_gain=ln_ffn_gain, ln_ffn_bias=ln_ffn_bias)
    mom1 = dict(c_ctx=m_c_ctx, w_ada=m_w_ada, b_ada=m_b_ada, w_in=m_w_in, w_decay_up=m_w_decay_up, b_decay_up=m_b_decay_up,
                gla_norm_gain=m_gla_norm_gain, w_pool_group=m_w_pool_group, pool_scale=m_pool_scale, w_gla_out=m_w_gla_out,
                w_pool_out=m_w_pool_out, w_out=m_w_out, ln_mix_gain=m_ln_mix_gain, ln_mix_bias=m_ln_mix_bias,
                w_ffn_in=m_w_ffn_in, w_ffn_out=m_w_ffn_out, ln_ffn_gain=m_ln_ffn_gain, ln_ffn_bias=m_ln_ffn_bias)
    mom2 = dict(c_ctx=v_c_ctx, w_ada=v_w_ada, b_ada=v_b_ada, w_in=v_w_in, w_decay_up=v_w_decay_up, b_decay_up=v_b_decay_up,
                gla_norm_gain=v_gla_norm_gain, w_pool_group=v_w_pool_group, pool_scale=v_pool_scale, w_gla_out=v_w_gla_out,
                w_pool_out=v_w_pool_out, w_out=v_w_out, ln_mix_gain=v_ln_mix_gain, ln_mix_bias=v_ln_mix_bias,
                w_ffn_in=v_w_ffn_in, w_ffn_out=v_w_ffn_out, ln_ffn_gain=v_ln_ffn_gain, ln_ffn_bias=v_ln_ffn_bias)
    names = list(weights)

    cfg = _Cfg()
    L, D = w_ada.shape[0], x.shape[-1]
    S, CL = x.shape[1], ctx.shape[1]
    cfg.L, cfg.D, cfg.S, cfg.CL, cfg.R, cfg.TM = L, D, S, CL, S + CL, CL
    DK, DV, DP = D // 2, D, D // 2
    cfg.DK, cfg.DV, cfg.DP = DK, DV, DP
    cfg.HK, cfg.HV, cfg.PG = DK // N_HEADS, DV // N_HEADS, DP // N_POOL
    DFF = w_ffn_out.shape[1] * N_CHIPS
    NP = 2 * DK + 2 * DV + DP + 2 * D + ALR_W
    cfg.NP, cfg.DFF = NP, DFF
    R, TM, HK, HV, PG = cfg.R, cfg.TM, cfg.HK, cfg.HV, cfg.PG
    MT = R // TM
    alpha = (2.0 * L) ** 0.25
    assert S % TM == 0 and TM % CHUNK == 0 and S % GRID_W == 0 and TM % GRID_W == 0
    OFF_G, OFF_P, OFF_BGA, OFF_BGB = 2 * DK + DV, 2 * DK + 2 * DV, 2 * DK + 2 * DV + DP, 2 * DK + 2 * DV + DP + D
    ALR0 = 2 * DK + 2 * DV
    TE = 512
    TL = TM // 2
    assert D % TE == 0 and DP % TE == 0

    xi, yi, ci = _my_place()
    chip = 2 * xi + yi
    dev = 4 * xi + 2 * yi + ci

    n_ada = w_ada.shape[-1]
    c_all = all_gather8("ag_cond", jnp.pad(c.reshape(1, D), ((0, SUBLANES - 1), (0, 0))))[:, 0, :]
    cond = jnp.concatenate([c_all, c_ctx.reshape(1, D), jnp.zeros((16 - N_DEV - 1, D), F32)], axis=0)

    def silu_body(i, j, rows_, vecs):
        return [_silu(rows_[0]), _dsilu(rows_[0])], []

    act, dact = ew("cond_silu", silu_body, 1, 1, 16, D, [(cond, 0, 0)], [], [(F32, 16, D)] * 2)
    w_ada2 = w_ada.reshape(L * D, n_ada)
    b_ada_mine = lax.dynamic_slice_in_dim(b_ada, chip * n_ada, n_ada, axis=1)
    tn_ada = _pick(n_ada, (1024, 512, 256, 128))
    mods = [matmul("ada_fwd", act, w_ada2, 'nn', 16, n_ada, D, tm=16, tn=tn_ada, b_off=(l, 0),
                   bias=_vec8(b_ada_mine[l]), precise=True) for l in range(L)]
    mods_all = all_gather8("ag_mods", jnp.concatenate(mods, axis=0))
    mods_all = mods_all[0::2].reshape(N_CHIPS, L, 16, n_ada).transpose(1, 2, 0, 3).reshape(L, 16, N_MOD * D)
    modv = [_vec8(mods_all[l, N_DEV], lax.dynamic_index_in_dim(mods_all[l], dev, 0, keepdims=False)) for l in range(L)]
    MB = D // TE

    big = ['w_in', 'w_gla_out', 'w_pool_out', 'w_out', 'w_ffn_in', 'w_ffn_out', 'w_pool_group']
    with_proj = ['w_in', 'w_gla_out', 'w_pool_out', 'w_out', 'w_pool_group']
    with_ffn = ['w_ffn_in', 'w_ffn_out']
    layer_shards = lambda nms, l: [weights[nm][l].astype(WIRE_DTYPE) for nm in nms]

    def cols_together(g):
        return jnp.moveaxis(g, 0, -2).reshape(g.shape[1:-1] + (N_CHIPS * g.shape[-1],))

    def rows_together(g):
        return g.reshape((N_CHIPS * g.shape[1], g.shape[2]))

    def assemble(gw):
        win_ref = cols_together(gw['w_in'])
        win = jnp.concatenate([win_ref[:, :ALR0], win_ref[:, ALR0 + 2 * GATE_RANK:], win_ref[:, ALR0:ALR0 + 2 * GATE_RANK],
                               jnp.zeros((D, ALR_W - 2 * GATE_RANK), WIRE_DTYPE)], axis=-1)
        wpg = jnp.moveaxis(gw['w_pool_group'], 0, 1).reshape(N_POOL, PG, PG)
        wpg_bd = jnp.zeros((N_POOL, PG, N_POOL, PG), WIRE_DTYPE)
        for g in range(N_POOL):
            wpg_bd = wpg_bd.at[g, :, g, :].set(wpg[g])
        return dict(win=win, wgo=rows_together(gw['w_gla_out']), wpo=cols_together(gw['w_pool_out']),
                    wout=rows_together(gw['w_out']), wfi=cols_together(gw['w_ffn_in']), wfo=rows_together(gw['w_ffn_out']),
                    wpg_bd=wpg_bd.reshape(DP, DP))

    first = run_comm("gather_weights", gather_plan(layer_shards(with_proj + with_ffn, 0) + [w_decay_up, b_decay_up],
                                                   n_split=len(big)))
    W = [assemble(dict(zip(with_proj + with_ffn, first[:len(big)])))] + [None] * (L - 1)
    wdu = cols_together(first[len(big)])
    bdu = cols_together(first[len(big) + 1])
    wd_pad = [[jnp.zeros((ALR_W, DK), F32).at[d * GATE_RANK:(d + 1) * GATE_RANK].set(wdu[l, d]) for d in range(2)]
              for l in range(L)]
    bd_pad = [[_vec8(bdu[l, d]) for d in range(2)] for l in range(L)]

    tn_np = _pick(NP, (1280, 1024, 768, 512, 256, 128))
    tn_d = _pick(D, (1024, 512, 256, 128))
    tn_ff2 = _pick(2 * DFF, (1024, 512, 256, 128))
    tn_ff = _pick(DFF, (2816, 1408, 1024, 512, 256, 128))
    tn_dp = _pick(DP, (1024, 512, 256, 128))
    tw = lambda n_: _pick(n_, (512, 256, 128))

    gain8 = lambda v_: _vec8(v_)

    xs = jnp.concatenate([ctx.reshape(CL, D), x.reshape(S, D)], axis=0)
    saved = []
    for l in range(L):
        sv = {}
        mv = modv[l]

        def mod_body(i, j, rows_, vecs):
            return [rows_[0] * (1.0 + _cls(vecs[1], i)) + _cls(vecs[0], i)], []

        h1 = ew("modulate", mod_body, MT, D // tn_d, TM, tn_d, [(xs, 0, 0)],
                [(mv, 0 * (D // tn_d)), (mv, 1 * (D // tn_d))], [(MXU_DTYPE, R, D)])[0]
        Wl = W[l]
        if l + 1 < L:
            proj, got_a = matmul("proj_and_gather", h1, Wl['win'], 'nn', R, NP, D, tm=TM, tn=tn_np,
                                 comm=gather_plan(layer_shards(with_proj, l + 1), n_split=len(with_proj)))
        else:
            proj = matmul("proj", h1, Wl['win'], 'nn', R, NP, D, tm=TM, tn=tn_np)
        o_f, st_f = gla_fwd("gla_fwd", proj, wd_pad[l][0], bd_pad[l][0], cfg, False)
        o_b, st_b = gla_fwd("gla_rev", proj, wd_pad[l][1], bd_pad[l][1], cfg, True)

        def post_body(i, j, rows_, vecs):
            o = rows_[0] + rows_[1]
            on = o * lax.rsqrt(jnp.mean(o * o, axis=-1, keepdims=True) + RMS_EPS)
            return [on * vecs[0][0:1, :] * _silu(rows_[2])], []

        a_gla = ew("gla_post", post_body, MT, N_HEADS, TM, HV, [(o_f, 0, 0), (o_b, 0, 0), (proj, 0, OFF_G // HV)],
                   [(gain8(gla_norm_gain[l]), 0)], [(MXU_DTYPE, R, DV)])[0]
        y_gla = matmul("gla_out", a_gla, Wl['wgo'], 'nn', R, D, DV, tm=TM, tn=tn_d)
        u_pool = pool_mix("pool_fwd", proj, OFF_P // PG, cfg, False)
        t_pool = matmul("pool_group", u_pool, Wl['wpg_bd'], 'nn', R, DP, DP, tm=TM, tn=tn_dp)

        def scale_body(i, j, rows_, vecs):
            return [rows_[0] * vecs[0][0:1, :]], []

        ts_pool = ew("pool_scale", scale_body, MT, DP // TE, TM, TE, [(t_pool, 0, 0)], [(gain8(pool_scale[l]), 0)],
                     [(MXU_DTYPE, R, DP)])[0]
        y_pool = matmul("pool_out", ts_pool, Wl['wpo'], 'nn', R, D, DP, tm=TM, tn=tn_d)

        def merge_body(i, j, rows_, vecs):
            return [_sigmoid(rows_[2]) * rows_[0] + _sigmoid(rows_[3]) * rows_[1]], []

        mg = ew("merge", merge_body, MT, D // TE, TM, TE,
                [(y_gla, 0, 0), (y_pool, 0, 0), (proj, 0, OFF_BGA // TE), (proj, 0, OFF_BGB // TE)], [],
                [(MXU_DTYPE, R, D)])[0]
        y_mix = matmul("mix_out", mg, Wl['wout'], 'nn', R, D, D, tm=TM, tn=tn_d)

        def ln_body(i, j, rows_, vecs):
            r = alpha * rows_[0] + _cls(vecs[0], i) * rows_[1]
            mu = jnp.mean(r, axis=-1, keepdims=True)
            rc = r - mu
            var = jnp.mean(rc * rc, axis=-1, keepdims=True)
            return [rc * lax.rsqrt(var + LN_EPS) * vecs[1][0:1, :] + vecs[2][0:1, :]], []

        x1 = ew("resid_ln", ln_body, R // TL, 1, TL, D, [(xs, 0, 0), (y_mix, 0, 0)],
                [(mv, 2), (gain8(ln_mix_gain[l]), 0), (gain8(ln_mix_bias[l]), 0)], [(F32, R, D)], ctx_tiles=CL // TL)[0]
        h2 = ew("modulate", mod_body, MT, D // tn_d, TM, tn_d, [(x1, 0, 0)],
                [(mv, 3 * (D // tn_d)), (mv, 4 * (D // tn_d))], [(MXU_DTYPE, R, D)])[0]
        if l + 1 < L:
            u_ffn, got_b = matmul("ffn_in_and_gather", h2, Wl['wfi'], 'nn', R, 2 * DFF, D, tm=TM, tn=tn_ff2,
                                  comm=gather_plan(layer_shards(with_ffn, l + 1), n_split=len(with_ffn)))
            W[l + 1] = assemble(dict(zip(with_proj + with_ffn, got_a + got_b)))
        else:
            u_ffn = matmul("ffn_in", h2, Wl['wfi'], 'nn', R, 2 * DFF, D, tm=TM, tn=tn_ff2)
        te_ff = _pick(DFF, (512, 256, 128))

        def swiglu_body(i, j, rows_, vecs):
            return [_silu(rows_[0]) * rows_[1]], []

        s_ffn = ew("swiglu", swiglu_body, MT, DFF // te_ff, TM, te_ff, [(u_ffn, 0, 0), (u_ffn, 0, DFF // te_ff)], [],
                   [(MXU_DTYPE, R, DFF)])[0]
        f_ffn = matmul("ffn_out", s_ffn, Wl['wfo'], 'nn', R, D, DFF, tm=TM, tn=tn_d)
        x2 = ew("resid_ln", ln_body, R // TL, 1, TL, D, [(x1, 0, 0), (f_ffn, 0, 0)],
                [(mv, 5), (gain8(ln_ffn_gain[l]), 0), (gain8(ln_ffn_bias[l]), 0)], [(F32, R, D)], ctx_tiles=CL // TL)[0]
        sv.update(xs=xs, h1=h1, proj=proj, o_f=o_f, o_b=o_b, st_f=st_f, st_b=st_b, a_gla=a_gla, y_gla=y_gla, u_pool=u_pool,
                  t_pool=t_pool, ts_pool=ts_pool, y_pool=y_pool, mg=mg, y_mix=y_mix, x1=x1, h2=h2, u_ffn=u_ffn, s_ffn=s_ffn,
                  f_ffn=f_ffn)
        saved.append(sv)
        xs = x2

    tgt = loss_target.reshape(S, D)

    def loss_body(i, j, rows_, vecs):
        d = jnp.where(i, 0.0, rows_[0] - rows_[1])
        return [d * (1.0 / D)], [_colsum(d * d)]

    d_x, sq = ew("loss", loss_body, MT, 1, TM, D, [(xs, 0, 0), (tgt, lambda i: jnp.maximum(i - 1, 0), 0)], [],
                 [(F32, R, D)], [(D, False)])

    def total_body(i, j, rows_, vecs):
        return [jnp.broadcast_to(jnp.sum(rows_[0], axis=-1, keepdims=True), (SUBLANES, D)) * (0.5 / D)], []

    loss_local = ew("loss_total", total_body, 1, 1, SUBLANES, D, [(sq, 0, 0)], [], [(F32, SUBLANES, D)])[0][0, 0]
    loss = lax.psum(loss_local, ("x", "y", "c"))

    gsmall = {nm: [None] * L for nm in ['gla_norm_gain', 'pool_scale', 'ln_mix_gain', 'ln_mix_bias', 'ln_ffn_gain',
                                        'ln_ffn_bias', 'w_decay_up', 'b_decay_up']}
    gbig = {nm: [None] * L for nm in big}
    dmod = [None] * L

    def ln_bwd(name, x_in, br, d_out, mv, gt_blk, gain):
        def body(i, j, rows_, vecs):
            gt = _cls(vecs[0], i)
            r = alpha * rows_[0] + gt * rows_[1]
            mu = jnp.mean(r, axis=-1, keepdims=True)
            rc = r - mu
            rstd = lax.rsqrt(jnp.mean(rc * rc, axis=-1, keepdims=True) + LN_EPS)
            xhat = rc * rstd
            dxh = rows_[2] * vecs[1][0:1, :]
            dr = rstd * (dxh - jnp.mean(dxh, axis=-1, keepdims=True) - xhat * jnp.mean(dxh * xhat, axis=-1, keepdims=True))
            return [dr, gt * dr], [_colsum(rows_[2] * xhat), _colsum(rows_[2]), _colsum(dr * rows_[1])]

        return ew(name, body, R // TL, 1, TL, D, [(x_in, 0, 0), (br, 0, 0), (d_out, 0, 0)], [(mv, gt_blk), (gain8(gain), 0)],
                  [(F32, R, D), (MXU_DTYPE, R, D)], [(D, False), (D, False), (D, True)], ctx_tiles=CL // TL)

    def mod_bwd(name, d_h, x_in, d_r, mv, sc_blk):
        def body(i, j, rows_, vecs):
            return ([rows_[0] * (1.0 + _cls(vecs[0], i)) + alpha * rows_[2]],
                    [_colsum(rows_[0] * rows_[1]), _colsum(rows_[0])])

        return ew(name, body, MT, D // tn_d, TM, tn_d, [(d_h, 0, 0), (x_in, 0, 0), (d_r, 0, 0)],
                  [(mv, sc_blk * (D // tn_d))], [(F32, R, D)], [(D, True), (D, True)])

    shard_axis = {'w_in': 1, 'w_gla_out': 0, 'w_pool_out': 1, 'w_out': 0, 'w_ffn_in': 1, 'w_ffn_out': 0, 'w_pool_group': 1}
    scatter_groups = [['w_in'], ['w_ffn_in'], ['w_gla_out', 'w_pool_out', 'w_out', 'w_ffn_out', 'w_pool_group']]

    def to_ref_cols(g):
        return jnp.concatenate([g[:, :ALR0], g[:, NP - ALR_W:NP - ALR_W + 2 * GATE_RANK], g[:, ALR0:NP - ALR_W]], axis=-1)

    def halves(a):
        h = a.shape[0] // 2
        return (lax.dynamic_slice_in_dim(a, ci * h, h, axis=0), lax.dynamic_slice_in_dim(a, (1 - ci) * h, h, axis=0))

    def cut_layer(l):
        cut = dict(keep_r={}, give_r={}, keep_o={}, give_o={})
        for nm in big:
            g = to_ref_cols(gbig[nm][l]) if nm == 'w_in' else gbig[nm][l]
            ax = shard_axis[nm]
            width = weights[nm].shape[ax + 1]
            piece = lambda which: lax.dynamic_slice_in_dim(g, which * width, width, axis=ax)
            parts = [halves(piece(jnp.bitwise_xor(chip, k)).astype(WIRE_DTYPE)) for k in range(1, N_CHIPS)]
            cut['keep_r'][nm] = jnp.stack([p_[0] for p_ in parts])
            cut['give_r'][nm] = jnp.stack([p_[1] for p_ in parts])
            cut['keep_o'][nm], cut['give_o'][nm] = halves(piece(chip))
        return cut

    def swap_partials_plan(cut):
        return swap_plan([cut['give_r'][nm] for nm in big] + [cut['give_o'][nm] for nm in big])

    def add_partials(cut, got):
        cut['got_o'] = dict(zip(big, got[len(big):]))
        cut['send_r'] = {}
        for nm, theirs in zip(big, got[:len(big)]):
            a2, b2 = _rows2d(cut['keep_r'][nm]), _rows2d(theirs)
            tm_ = _pick(a2.shape[0], (256, 128, 64, 32, 16, 8))

            def add2_body(i, j, rows__, vecs):
                return [rows__[0].astype(F32) + rows__[1].astype(F32)], []

            cut['send_r'][nm] = ew("add_partials", add2_body, a2.shape[0] // tm_, 1, tm_, a2.shape[1],
                                   [(a2, 0, 0), (b2, 0, 0)], [], [(WIRE_DTYPE, a2.shape[0], a2.shape[1])]
                                   )[0].reshape(cut['keep_r'][nm].shape)
        cut['recvd'] = {}

    def sum_grads(cut):
        done = {}
        for nm in big:
            a2, b2 = _rows2d(cut['keep_o'][nm]), _rows2d(cut['got_o'][nm])
            rows_, width = a2.shape
            tm_ = _pick(rows_, (256, 128, 64, 32, 16, 8))

            def psum_body(i, j, rows__, vecs):
                return [(rows__[0] + rows__[1]) + rows__[2].astype(F32) + rows__[3].astype(F32) + rows__[4].astype(F32)], []

            rb2 = cut['recvd'][nm].reshape(3 * rows_, width)
            done[nm] = ew("sum_grads", psum_body, rows_ // tm_, 1, tm_, width,
                          [(a2, 0, 0), (b2, 0, 0)] + [(rb2, k * (rows_ // tm_), 0) for k in range(3)], [],
                          [(F32, rows_, width)])[0]
        return done

    def scatter_group_plan(cut, group):
        return scatter_plan([cut['send_r'][nm] for nm in group])

    finished = [None] * L
    pending = None
    for l in reversed(range(L)):
        sv = saved[l]
        mv = modv[l]
        Wl = W[l]
        proj = sv['proj']
        dr2, d_f, g_gain, g_bias, g_gt_f = ln_bwd("ln_bwd", sv['x1'], sv['f_ffn'], d_x, mv, 5, ln_ffn_gain[l])
        gsmall['ln_ffn_gain'][l], gsmall['ln_ffn_bias'][l] = g_gain[0], g_bias[0]
        if pending is not None:
            d_s, got = matmul("ffn_out_dx_and_swap", d_f, Wl['wfo'], 'nt', R, DFF, D, tm=TM, tn=tn_ff,
                              comm=swap_partials_plan(pending))
            add_partials(pending, got)
        else:
            d_s = matmul("ffn_out_dx", d_f, Wl['wfo'], 'nt', R, DFF, D, tm=TM, tn=tn_ff)
        gbig['w_ffn_out'][l] = matmul("ffn_out_dw", sv['s_ffn'], d_f, 'tn', DFF, D, R, tm=tw(DFF), tn=tn_d)
        te_ff = _pick(DFF, (512, 256, 128))
        nh = DFF // te_ff

        d_u = swiglu_bwd("swiglu_bwd", d_s, sv['u_ffn'], MT, nh, TM, te_ff)
        tn_x = 512
        if pending is not None:
            d_h2, got = matmul("ffn_in_dx_and_scatter", d_u, Wl['wfi'], 'nt', R, D, 2 * DFF, tm=TM, tn=tn_x,
                               comm=scatter_group_plan(pending, scatter_groups[0]))
            pending['recvd'].update(zip(scatter_groups[0], got))
            gbig['w_ffn_in'][l], got = matmul("ffn_in_dw_and_scatter", sv['h2'], d_u, 'tn', D, 2 * DFF, R, tm=tw(D),
                                              tn=tn_ff2, comm=scatter_group_plan(pending, scatter_groups[1]))
            pending['recvd'].update(zip(scatter_groups[1], got))
        else:
            d_h2 = matmul("ffn_in_dx", d_u, Wl['wfi'], 'nt', R, D, 2 * DFF, tm=TM, tn=tn_x)
            gbig['w_ffn_in'][l] = matmul("ffn_in_dw", sv['h2'], d_u, 'tn', D, 2 * DFF, R, tm=tw(D), tn=tn_ff2)
        d_x1, g_sc_f, g_sh_f = mod_bwd("mod_bwd", d_h2, sv['x1'], dr2, mv, 4)
        dr1, d_y, g_gain, g_bias, g_gt_m = ln_bwd("ln_bwd", sv['xs'], sv['y_mix'], d_x1, mv, 2, ln_mix_gain[l])
        gsmall['ln_mix_gain'][l], gsmall['ln_mix_bias'][l] = g_gain[0], g_bias[0]
        d_mg = matmul("mix_out_dx", d_y, Wl['wout'], 'nt', R, D, D, tm=TM, tn=tn_d)
        gbig['w_out'][l] = matmul("mix_out_dw", sv['mg'], d_y, 'tn', D, D, R, tm=tw(D), tn=tn_d)

        def merge_bwd_body(i, j, rows_, vecs):
            d_m, yg, yp, ba, bb = rows_
            sa, sb = _sigmoid(ba), _sigmoid(bb)
            return [d_m * sa, d_m * sb, d_m * yg * sa * (1.0 - sa), d_m * yp * sb * (1.0 - sb)], []

        d_yg, d_yp, d_bga, d_bgb = ew(
            "merge_bwd", merge_bwd_body, MT, D // TE, TM, TE,
            [(d_mg, 0, 0), (sv['y_gla'], 0, 0), (sv['y_pool'], 0, 0), (proj, 0, OFF_BGA // TE), (proj, 0, OFF_BGB // TE)], [],
            [(MXU_DTYPE, R, D), (MXU_DTYPE, R, D), (F32, R, D), (F32, R, D)])
        d_ts = matmul("pool_out_dx", d_yp, Wl['wpo'], 'nt', R, DP, D, tm=TM, tn=tn_dp)
        gbig['w_pool_out'][l] = matmul("pool_out_dw", sv['ts_pool'], d_yp, 'tn', DP, D, R, tm=tw(DP), tn=tn_d)

        def scale_bwd_body(i, j, rows_, vecs):
            return [rows_[0] * vecs[0][0:1, :]], [_colsum(rows_[0] * rows_[1])]

        d_t, g_ps = ew("pool_scale_bwd", scale_bwd_body, MT, DP // TE, TM, TE, [(d_ts, 0, 0), (sv['t_pool'], 0, 0)],
                       [(gain8(pool_scale[l]), 0)], [(MXU_DTYPE, R, DP)], [(DP, False)])
        gsmall['pool_scale'][l] = g_ps[0]
        d_u_pool = matmul("pool_group_dx", d_t, Wl['wpg_bd'], 'nt', R, DP, DP, tm=TM, tn=tn_dp)
        g_bd = matmul("pool_group_dw", sv['u_pool'], d_t, 'tn', DP, DP, R, tm=tw(DP), tn=tn_dp)
        gbig['w_pool_group'][l] = jnp.stack([g_bd[g * PG:(g + 1) * PG, g * PG:(g + 1) * PG] for g in range(N_POOL)])
        d_p = pool_mix("pool_bwd", d_u_pool, 0, cfg, True)
        d_a = matmul("gla_out_dx", d_yg, Wl['wgo'], 'nt', R, DV, D, tm=TM, tn=tn_d)
        gbig['w_gla_out'][l] = matmul("gla_out_dw", sv['a_gla'], d_yg, 'tn', DV, D, R, tm=tw(DV), tn=tn_d)

        def post_bwd_body(i, j, rows_, vecs):
            d_a_, o_f_, o_b_, g_ = rows_
            gain = vecs[0][0:1, :]
            o = o_f_ + o_b_
            rstd = lax.rsqrt(jnp.mean(o * o, axis=-1, keepdims=True) + RMS_EPS)
            on = o * rstd
            sg = _silu(g_)
            d_on = d_a_ * gain * sg
            d_o_ = rstd * (d_on - on * jnp.mean(d_on * on, axis=-1, keepdims=True))
            return [d_o_, d_a_ * on * gain * _dsilu(g_)], [_colsum(d_a_ * on * sg)]

        d_o, d_g, g_gng = ew("gla_post_bwd", post_bwd_body, MT, N_HEADS, TM, HV,
                             [(d_a, 0, 0), (sv['o_f'], 0, 0), (sv['o_b'], 0, 0), (proj, 0, OFF_G // HV)],
                             [(gain8(gla_norm_gain[l]), 0)], [(F32, R, DV), (F32, R, DV)], [(DV, False)])
        gsmall['gla_norm_gain'][l] = g_gng[0]
        part = gla_bwd("gla_bwd", proj, wd_pad[l][0], bd_pad[l][0], sv['st_f'], d_o, cfg, False)
        full = gla_bwd("gla_rev_bwd", proj, wd_pad[l][1], bd_pad[l][1], sv['st_b'], d_o, cfg, True, addends=part[:4])
        d_q, d_k, d_v, d_alr = full[:4]
        g_wd = [res[4][d * GATE_RANK:(d + 1) * GATE_RANK] for d, res in enumerate((part, full))]
        g_bd_ = [res[5][0] for res in (part, full)]
        gsmall['w_decay_up'][l], gsmall['b_decay_up'][l] = jnp.stack(g_wd), jnp.stack(g_bd_)
        d_proj = jnp.concatenate([t_.astype(MXU_DTYPE) for t_ in (d_q, d_k, d_v, d_g, d_p, d_bga, d_bgb, d_alr)], axis=-1)
        if pending is not None:
            d_h1, got = matmul("proj_dx_and_scatter", d_proj, Wl['win'], 'nt', R, D, NP, tm=TM, tn=tn_x,
                               comm=scatter_group_plan(pending, scatter_groups[2]))
            pending['recvd'].update(zip(scatter_groups[2], got))
            finished[l + 1] = sum_grads(pending)
        else:
            d_h1 = matmul("proj_dx", d_proj, Wl['win'], 'nt', R, D, NP, tm=TM, tn=tn_x)
        gbig['w_in'][l] = matmul("proj_dw", sv['h1'], d_proj, 'tn', D, NP, R, tm=tw(D), tn=tn_np)
        d_x, g_sc_m, g_sh_m = mod_bwd("mod_bwd", d_h1, sv['xs'], dr1, mv, 1)
        dmod[l] = jnp.concatenate([g_sh_m[:2], g_sc_m[:2], g_gt_m[:2], g_sh_f[:2], g_sc_f[:2], g_gt_f[:2]], axis=-1)
        pending = cut_layer(l)

    grad_x = d_x[CL:].reshape(x.shape)
    add_partials(pending, run_comm("swap_partials", swap_partials_plan(pending)))
    pending['recvd'] = dict(zip(big, run_comm("scatter_grads", scatter_plan([pending['send_r'][nm] for nm in big]))))
    finished[0] = sum_grads(pending)

    dmod = jnp.stack(dmod)
    summed = [dmod[:, 0]] + [jnp.stack(gsmall[nm]) for nm in
                             ['ln_mix_gain', 'ln_mix_bias', 'ln_ffn_gain', 'ln_ffn_bias', 'gla_norm_gain', 'pool_scale',
                              'w_decay_up', 'b_decay_up']]
    pack = _pack([dmod[:, 1]] + summed)
    prow = pack.shape[0]
    packs = all_gather8("ag_small", pack)
    tot = _sum_rows("sum_small", packs.reshape(N_DEV * prow, PACK_W), N_DEV, prow, PACK_W)
    shapes = [(L, N_MOD * D)] + [a.shape for a in summed]
    tot = _unpack(tot, shapes)
    dmod_ctx = tot[1]
    g_rep = dict(zip(['ln_mix_gain', 'ln_mix_bias', 'ln_ffn_gain', 'ln_ffn_bias', 'gla_norm_gain', 'pool_scale'], tot[2:8]))
    g_wdu_full, g_bdu_full = tot[8], tot[9]
    dmod_lat = jnp.stack([_unpack(packs[d_], shapes[:1])[0] for d_ in range(N_DEV)], axis=1)
    dm_all = jnp.concatenate([dmod_lat, dmod_ctx[:, None, :], jnp.zeros((L, 16 - N_DEV - 1, N_MOD * D), F32)], axis=1)

    dm_mine = lax.dynamic_slice_in_dim(dm_all, chip * n_ada, n_ada, axis=2)
    g_w_ada = jnp.stack([matmul("ada_dw", act, dm_mine[l], 'tn', D, n_ada, 16, tm=tw(D), tn=tn_ada, precise=True)
                         for l in range(L)])

    def bsum_body(i, j, rows_, vecs):
        return [jnp.broadcast_to(_colsum(rows_[0]), rows_[0].shape)], []

    g_b_ada = jnp.stack([ew("ada_db", bsum_body, 1, 1, 16, N_MOD * D, [(dm_all[l], 0, 0)], [],
                            [(F32, 16, N_MOD * D)])[0][0] for l in range(L)])
    part_c = [matmul("ada_dc", dm_mine[l], w_ada2, 'nt', 16, D, n_ada, tm=16, tn=tn_d, b_off=(l * (D // tn_d), 0),
                     precise=True)
              for l in range(L)]
    parts_c = all_gather8("ag_dcond", jnp.concatenate(part_c, axis=0))
    dc_rows = parts_c[0::2].reshape(N_CHIPS * L * 16, D)

    def dc_body(i, j, rows_, vecs):
        acc = rows_[0]
        for r_ in rows_[1:-1]:
            acc = acc + r_
        return [acc * rows_[-1]], []

    g_c_ctx = ew("dcond", dc_body, 1, 1, 16, D, [(dc_rows, p_, 0) for p_ in range(N_CHIPS * L)] + [(dact, 0, 0)], [],
                 [(F32, 16, D)])[0][N_DEV]

    mine_all = [jnp.concatenate([finished[l][nm] for l in range(L)], axis=0) for nm in big]
    other = run_comm("swap_grads", swap_plan(mine_all))

    out_g, out_d, out_m, out_v = {}, {}, {}, {}
    for nm, mine, theirs in zip(big, mine_all, other):
        shp = weights[nm].shape
        half_rows = (shp[1] // 2) * (shp[2] if len(shp) == 4 else 1)
        res = _adam("adam_" + nm, _rows2d(weights[nm]), [mine, theirs], _rows2d(mom1[nm]), _rows2d(mom2[nm]),
                    half_rows=half_rows, core=ci)
        out_g[nm], out_d[nm], out_m[nm], out_v[nm] = [r_.reshape(shp) for r_ in res]
    res = _adam("adam_w_ada", _rows2d(w_ada), [_rows2d(g_w_ada)], _rows2d(m_w_ada), _rows2d(v_w_ada))
    out_g['w_ada'], out_d['w_ada'], out_m['w_ada'], out_v['w_ada'] = [r_.reshape(w_ada.shape) for r_ in res]
    n_wd, n_bd = w_decay_up.shape[-1], b_decay_up.shape[-1]
    small_g = dict(g_rep, c_ctx=g_c_ctx, b_ada=g_b_ada,
                   w_decay_up=lax.dynamic_slice_in_dim(g_wdu_full, chip * n_wd, n_wd, axis=3),
                   b_decay_up=lax.dynamic_slice_in_dim(g_bdu_full, chip * n_bd, n_bd, axis=2))
    small = [nm for nm in names if nm not in big and nm != 'w_ada']
    res = _adam("adam_small", _pack([weights[nm] for nm in small]), [_pack([small_g[nm] for nm in small])],
                _pack([mom1[nm] for nm in small]), _pack([mom2[nm] for nm in small]))
    small_shapes = [weights[nm].shape for nm in small]
    for dst, packed in zip((out_g, out_d, out_m, out_v), res):
        for nm, val in zip(small, _unpack(packed, small_shapes)):
            dst[nm] = val

    return (loss, grad_x, *[out_g[nm] for nm in names], *[out_d[nm] for nm in names],
            *[out_m[nm] for nm in names], *[out_v[nm] for nm in names])


def swiglu_bwd(name, d_s, u, nrow, nh, tm, tn):
    def kern(ds_ref, gate_ref, up_ref, o_ref):
        j = pl.program_id(0)

        @pl.when(j < nh)
        def _():
            o_ref[...] = (ds_ref[...] * up_ref[...] * _dsilu(gate_ref[...])).astype(o_ref.dtype)

        @pl.when(j >= nh)
        def _():
            o_ref[...] = (ds_ref[...] * _silu(gate_ref[...])).astype(o_ref.dtype)

    return pl.pallas_call(
        kern, name=name, grid=(2 * nh, nrow),
        in_specs=[pl.BlockSpec((tm, tn), lambda j, i: (i, j % nh)), pl.BlockSpec((tm, tn), lambda j, i: (i, j % nh)),
                  pl.BlockSpec((tm, tn), lambda j, i: (i, nh + j % nh))],
        out_specs=pl.BlockSpec((tm, tn), lambda j, i: (i, j)),
        out_shape=jax.ShapeDtypeStruct((u.shape[0], 2 * nh * tn), MXU_DTYPE), compiler_params=_params(),
    )(d_s, u, u)
```

```python
import functools

import jax
import jax.numpy as jnp
import numpy as np
from jax import lax
from jax.experimental import pallas as pl
from jax.experimental.pallas import tpu as pltpu

F32 = jnp.float32
MXU_DTYPE = jnp.bfloat16
WIRE_DTYPE = jnp.bfloat16

GRID_W = 64
CHUNK = 64
N_HEADS = 4
GATE_RANK = 16
GATE_NORM = 16.0
N_MOD = 6
N_POOL = 4
LN_EPS = 1e-5
RMS_EPS = 1e-6
ADAM_LR = 0.001
ADAM_B1 = 0.9
ADAM_B2 = 0.999
ADAM_EPS = 1e-08
ADAM_WD = 0.01
ADAM_STEP = 10

LANES = 128
SUBLANES = 8
ALR_W = 256
PACK_W = 2048
VMEM_LIMIT = 56 * 1024 * 1024
N_CHIPS = 4
N_DEV = 8
MESH = pl.DeviceIdType.MESH

NN = ((1,), (0,))
NT = ((1,), (1,))
TN = ((0,), (0,))


def _dot(a, b, dims):
    return lax.dot_general(a.astype(MXU_DTYPE), b.astype(MXU_DTYPE), (dims, ((), ())),
                           preferred_element_type=F32)


def _dot_f32(a, b, dims):
    return lax.dot_general(a.astype(F32), b.astype(F32), (dims, ((), ())),
                           precision=lax.Precision.HIGHEST, preferred_element_type=F32)


def _dot_mask(mask, x, dims):
    m = mask.astype(MXU_DTYPE)
    if MXU_DTYPE == F32:
        return lax.dot_general(m, x, (dims, ((), ())), preferred_element_type=F32)
    acc = None
    rest = x
    for _ in range(3):
        piece = rest.astype(MXU_DTYPE)
        rest = rest - piece.astype(F32)
        part = lax.dot_general(m, piece, (dims, ((), ())), preferred_element_type=F32)
        acc = part if acc is None else acc + part
    return acc


def _dot_3x(a, b, dims):
    if MXU_DTYPE == F32:
        return lax.dot_general(a, b, (dims, ((), ())), preferred_element_type=F32)
    a_hi, b_hi = a.astype(MXU_DTYPE), b.astype(MXU_DTYPE)
    a_lo = (a - a_hi.astype(F32)).astype(MXU_DTYPE)
    b_lo = (b - b_hi.astype(F32)).astype(MXU_DTYPE)
    dot = lambda u, w: lax.dot_general(u, w, (dims, ((), ())), preferred_element_type=F32)
    return dot(a_hi, b_hi) + (dot(a_lo, b_hi) + dot(a_hi, b_lo))


def _pick(n, cands):
    for c in cands:
        if n % c == 0:
            return c
    return n


def _params():
    return pltpu.CompilerParams(vmem_limit_bytes=VMEM_LIMIT)


def _sigmoid(x):
    return 0.5 + 0.5 * jnp.tanh(0.5 * x)


def _silu(x):
    return x * _sigmoid(x)


def _dsilu(x):
    s = _sigmoid(x)
    return s * (1.0 + x * (1.0 - s))


def matmul(name, a, b, form, m, n, k, *, tm, tn, out_dtype=F32, a_off=(0, 0), b_off=(0, 0), bias=None, bias_off=0,
           precise=False, comm=None):
    assert m % tm == 0 and n % tn == 0, (name, m, n, tm, tn)
    if form == 'tn':
        a_spec = pl.BlockSpec((k, tm), lambda j, i: (a_off[0], i + a_off[1]))
    else:
        a_spec = pl.BlockSpec((tm, k), lambda j, i: (i + a_off[0], a_off[1]))
    if form == 'nt':
        b_spec = pl.BlockSpec((tn, k), lambda j, i: (j + b_off[0], b_off[1]))
    else:
        b_spec = pl.BlockSpec((k, tn), lambda j, i: (b_off[0], j + b_off[1]))
    dims = {'nn': NN, 'nt': NT, 'tn': TN}[form]
    in_specs = [a_spec, b_spec]
    args = [a, b]
    if bias is not None:
        in_specs.append(pl.BlockSpec((SUBLANES, tn), lambda j, i: (0, j + bias_off)))
        args.append(bias)

    n_own = len(args)
    nj, ni = n // tn, m // tm
    out_spec = pl.BlockSpec((tm, tn), lambda j, i: (i, j))
    out_shape = jax.ShapeDtypeStruct((m, n), out_dtype)
    if comm is None:
        def body(*refs):
            acc = (_dot_f32 if precise else _dot)(refs[0][...], refs[1][...], dims)
            if bias is not None:
                acc = acc + refs[2][0:1, :]
            refs[-1][...] = acc.astype(refs[-1].dtype)

        return pl.pallas_call(body, name=name, grid=(nj, ni), in_specs=in_specs, out_specs=out_spec,
                              out_shape=out_shape, compiler_params=_params())(*args)

    n_ci, n_co = len(comm.ins), len(comm.outs)

    def hosted(*refs):
        c_in = refs[n_own:n_own + n_ci]
        o_ref = refs[n_own + n_ci]
        c_out = refs[n_own + n_ci + 1:n_own + n_ci + 1 + n_co]
        sems = refs[n_own + n_ci + 1 + n_co:]
        j, i = pl.program_id(0), pl.program_id(1)

        @pl.when((j == 0) & (i == 0))
        def _():
            comm.start(c_in, c_out, sems)

        acc = (_dot_f32 if precise else _dot)(refs[0][...], refs[1][...], dims)
        if bias is not None:
            acc = acc + refs[2][0:1, :]
        o_ref[...] = acc.astype(o_ref.dtype)

        @pl.when((j == nj - 1) & (i == ni - 1))
        def _():
            comm.finish(c_in, c_out, sems)

    any_spec = pl.BlockSpec(memory_space=pl.ANY)
    res = pl.pallas_call(
        hosted, name=name, grid=(nj, ni), in_specs=in_specs + [any_spec] * n_ci,
        out_specs=[out_spec] + [any_spec] * n_co, out_shape=[out_shape] + list(comm.outs),
        scratch_shapes=list(comm.sems), compiler_params=_params(),
    )(*args, *comm.ins)
    return res[0], list(res[1:])


def ew(name, body, nrow, ncol, tm, tn, row_ins, vec_ins, row_outs, sum_outs=(), ctx_tiles=1, pass_i=False):
    def rmap(roff):
        return roff if callable(roff) else (lambda i: i + roff)

    in_specs = []
    for arr, roff, coff in row_ins:
        in_specs.append(pl.BlockSpec((tm, tn), functools.partial(lambda j, i, r, c: (r(i), j + c), r=rmap(roff), c=coff)))
    for arr, coff in vec_ins:
        in_specs.append(pl.BlockSpec((SUBLANES, tn), functools.partial(lambda j, i, c: (0, j + c), c=coff)))
    out_specs = [pl.BlockSpec((tm, tn), lambda j, i: (i, j)) for _ in row_outs]
    out_specs += [pl.BlockSpec((SUBLANES, tn), lambda j, i: (0, j)) for _ in sum_outs]
    out_shape = [jax.ShapeDtypeStruct((r, c), dt) for dt, r, c in row_outs]
    out_shape += [jax.ShapeDtypeStruct((SUBLANES, c), F32) for c, _ in sum_outs]
    n_row, n_vec, n_ro = len(row_ins), len(vec_ins), len(row_outs)

    def kern(*refs):
        j, i = pl.program_id(0), pl.program_id(1)
        rows = [r[...] for r in refs[:n_row]]
        vecs = [r[...] for r in refs[n_row:n_row + n_vec]]
        outs = refs[n_row + n_vec:]
        is_ctx = i < ctx_tiles
        res, sums = body(i if pass_i else is_ctx, j, rows, vecs)
        for ref, val in zip(outs[:n_ro], res):
            ref[...] = val.astype(ref.dtype)
        for ref, val, (_, by_class) in zip(outs[n_ro:], sums, sum_outs):
            @pl.when(i == 0)
            def _():
                ref[...] = jnp.zeros_like(ref)
            if by_class:
                ref[0:1, :] += jnp.where(is_ctx, val, 0.0)
                ref[1:2, :] += jnp.where(is_ctx, 0.0, val)
            else:
                ref[0:1, :] += val

    outs = pl.pallas_call(
        kern, name=name, grid=(ncol, nrow), in_specs=in_specs, out_specs=out_specs, out_shape=out_shape,
        compiler_params=_params(),
    )(*[a for a, _, _ in row_ins], *[a for a, _ in vec_ins])
    return list(outs)


def _cls(vec, is_ctx):
    return jnp.where(is_ctx, vec[0:1, :], vec[1:2, :])


def _colsum(x):
    return jnp.sum(x, axis=0, keepdims=True)


def _chunk_map(cfg, rev):
    nctx, nc = cfg.CL // CHUNK, cfg.R // CHUNK
    if not rev:
        return lambda s: s
    return lambda s: jnp.where(s < nctx, nctx - 1 - s, nctx + nc - 1 - s)


def _gla_chunk(q_ref, k_ref, a_ref, wd_ref, bd_ref, rev, scale):
    q = q_ref[...] * scale
    k = k_ref[...]
    z = _dot_3x(a_ref[...], wd_ref[...], NN) + bd_ref[0:1, :]
    la = (jnp.minimum(z, 0.0) - jnp.log(1.0 + jnp.exp(-jnp.abs(z)))) * (1.0 / GATE_NORM)
    r = lax.broadcasted_iota(jnp.int32, (CHUNK, CHUNK), 0)
    c = lax.broadcasted_iota(jnp.int32, (CHUNK, CHUNK), 1)
    keep = (r <= c) if rev else (r >= c)
    tri = keep.astype(F32)
    cum = _dot_mask(tri, la, NN)
    mid = CHUNK // 2 if rev else CHUNK // 2 - 1
    end = 0 if rev else CHUNK - 1
    ref = cum[mid:mid + 1, :]
    last = cum[end:end + 1, :]
    return dict(q=q, k=k, z=z, keep=keep, tri=tri, q_in=q * jnp.exp(cum - ref), k_in=k * jnp.exp(ref - cum),
                e_q=jnp.exp(cum), e_k=jnp.exp(last - cum), e_inq=jnp.exp(cum - ref), e_ink=jnp.exp(ref - cum),
                e_last=jnp.exp(last))


def _gla_in_specs(cfg, rows_of):
    dk, dv = cfg.DK, cfg.DV
    return [
        pl.BlockSpec((CHUNK, dk), lambda s: (rows_of(s), 0)),
        pl.BlockSpec((CHUNK, dk), lambda s: (rows_of(s), 1)),
        pl.BlockSpec((CHUNK, dv), lambda s: (rows_of(s), 2 * dk // dv)),
        pl.BlockSpec((CHUNK, ALR_W), lambda s: (rows_of(s), (cfg.NP - ALR_W) // ALR_W)),
        pl.BlockSpec((ALR_W, dk), lambda s: (0, 0)),
        pl.BlockSpec((SUBLANES, dk), lambda s: (0, 0)),
    ]


def gla_fwd(name, proj, wd, bd, cfg, rev, comm=None):
    hk, hv = cfg.HK, cfg.HV
    nc = cfg.R // CHUNK
    cmap = _chunk_map(cfg, rev)
    scale = hk ** -0.5
    n_ci, n_co = (len(comm.ins), len(comm.outs)) if comm else (0, 0)

    def body(*refs):
        q_ref, k_ref, v_ref, a_ref, wd_ref, bd_ref = refs[:6]
        c_in = refs[6:6 + n_ci]
        o_ref, ss_ref = refs[6 + n_ci:8 + n_ci]
        c_out = refs[8 + n_ci:8 + n_ci + n_co]
        st_scr = refs[8 + n_ci + n_co]
        sems = refs[9 + n_ci + n_co:]
        s = pl.program_id(0)

        @pl.when(s == 0)
        def _():
            st_scr[...] = jnp.zeros_like(st_scr)
            if comm:
                comm.start(c_in, c_out, sems)

        t = _gla_chunk(q_ref, k_ref, a_ref, wd_ref, bd_ref, rev, scale)
        q_int, k_st = t['q'] * t['e_q'], t['k'] * t['e_k']
        for h in range(N_HEADS):
            ks, vs = slice(h * hk, (h + 1) * hk), slice(h * hv, (h + 1) * hv)
            v = v_ref[:, vs]
            st = st_scr[h]
            ss_ref[0, h] = st
            a = jnp.where(t['keep'], _dot(t['q_in'][:, ks], t['k_in'][:, ks], NT), 0.0)
            o_ref[:, vs] = _dot(a, v, NN) + _dot(q_int[:, ks], st, NT)
            st_scr[h] = st * t['e_last'][:, ks] + _dot(v, k_st[:, ks], TN)

        if comm:
            @pl.when(s == nc - 1)
            def _():
                comm.finish(c_in, c_out, sems)

    any_spec = pl.BlockSpec(memory_space=pl.ANY)
    res = pl.pallas_call(
        body, name=name, grid=(nc,), in_specs=_gla_in_specs(cfg, cmap) + [any_spec] * n_ci,
        out_specs=[pl.BlockSpec((CHUNK, cfg.DV), lambda s: (cmap(s), 0)),
                   pl.BlockSpec((1, N_HEADS, hv, hk), lambda s: (s, 0, 0, 0))] + [any_spec] * n_co,
        out_shape=[jax.ShapeDtypeStruct((cfg.R, cfg.DV), F32),
                   jax.ShapeDtypeStruct((nc, N_HEADS, hv, hk), F32)] + (list(comm.outs) if comm else []),
        scratch_shapes=[pltpu.VMEM((N_HEADS, hv, hk), F32)] + (list(comm.sems) if comm else []),
        compiler_params=_params(),
    )(proj, proj, proj, proj, wd, bd, *(comm.ins if comm else []))
    return res[0], res[1], list(res[2:])


def gla_bwd(name, proj, wd, bd, states, d_o, cfg, rev, addends=None):
    hk, hv = cfg.HK, cfg.HV
    nc = cfg.R // CHUNK
    cmap = _chunk_map(cfg, rev)
    rows_of = lambda g: cmap(nc - 1 - g)
    scale = hk ** -0.5
    n_add = 0 if addends is None else 4

    def body(*refs):
        q_ref, k_ref, v_ref, a_ref, wd_ref, bd_ref, ss_ref, do_ref = refs[:8]
        adds = refs[8:8 + n_add]
        dq_ref, dk_ref, dv_ref, da_ref, dwd_ref, dbd_ref, dst_scr = refs[8 + n_add:]
        g = pl.program_id(0)

        @pl.when(g == 0)
        def _():
            dst_scr[...] = jnp.zeros_like(dst_scr)
            dwd_ref[...] = jnp.zeros_like(dwd_ref)
            dbd_ref[...] = jnp.zeros_like(dbd_ref)

        t = _gla_chunk(q_ref, k_ref, a_ref, wd_ref, bd_ref, rev, scale)
        q_int, k_st = t['q'] * t['e_q'], t['k'] * t['e_k']
        dq_h, dk_h, carry_h = [], [], []
        for h in range(N_HEADS):
            ks, vs = slice(h * hk, (h + 1) * hk), slice(h * hv, (h + 1) * hv)
            v = v_ref[:, vs]
            d_out = do_ref[:, vs]
            st = ss_ref[0, h]
            dst = dst_scr[h]
            a = jnp.where(t['keep'], _dot(t['q_in'][:, ks], t['k_in'][:, ks], NT), 0.0)
            da = jnp.where(t['keep'], _dot(d_out, v, NT), 0.0)
            dv = _dot(a, d_out, TN) + _dot(k_st[:, ks], dst, NT)
            dq_h.append(_dot(d_out, st, NN) * t['e_q'][:, ks] + _dot(da, t['k_in'][:, ks], NN) * t['e_inq'][:, ks])
            dk_h.append(_dot(v, dst, NN) * t['e_k'][:, ks] + _dot(da, t['q_in'][:, ks], TN) * t['e_ink'][:, ks])
            dst_scr[h] = dst * t['e_last'][:, ks] + _dot(d_out, q_int[:, ks], TN)
            st_end = st * t['e_last'][:, ks] + _dot(v, k_st[:, ks], TN)
            carry_h.append(_colsum(dst * st_end))
            dv_ref[:, vs] = dv + adds[2][:, vs] if n_add else dv
        dq = jnp.concatenate(dq_h, axis=-1)
        dk = jnp.concatenate(dk_h, axis=-1)
        dg = t['q'] * dq - t['k'] * dk
        dla = _dot_mask(t['tri'], dg, TN) + jnp.concatenate(carry_h, axis=-1)
        dz = dla * (1.0 / GATE_NORM) * _sigmoid(-t['z'])
        dalr = _dot_3x(dz, wd_ref[...], NT)
        dwd_ref[...] += _dot_3x(a_ref[...], dz, TN)
        dbd_ref[...] += jnp.broadcast_to(_colsum(dz), dbd_ref.shape)
        dq = dq * scale
        if n_add:
            dq, dk, dalr = dq + adds[0][...], dk + adds[1][...], dalr + adds[3][...]
        dq_ref[...] = dq
        dk_ref[...] = dk
        da_ref[...] = dalr

    qk_spec = pl.BlockSpec((CHUNK, cfg.DK), lambda g: (rows_of(g), 0))
    v_spec = pl.BlockSpec((CHUNK, cfg.DV), lambda g: (rows_of(g), 0))
    a_spec = pl.BlockSpec((CHUNK, ALR_W), lambda g: (rows_of(g), 0))
    in_specs = _gla_in_specs(cfg, rows_of) + [
        pl.BlockSpec((1, N_HEADS, hv, hk), lambda g: (nc - 1 - g, 0, 0, 0)), v_spec]
    args = [proj, proj, proj, proj, wd, bd, states, d_o]
    if n_add:
        in_specs += [qk_spec, qk_spec, v_spec, a_spec]
        args += list(addends)
    return pl.pallas_call(
        body, name=name, grid=(nc,), in_specs=in_specs,
        out_specs=[qk_spec, qk_spec, v_spec, a_spec,
                   pl.BlockSpec((ALR_W, cfg.DK), lambda g: (0, 0)),
                   pl.BlockSpec((SUBLANES, cfg.DK), lambda g: (0, 0))],
        out_shape=[jax.ShapeDtypeStruct((cfg.R, cfg.DK), F32), jax.ShapeDtypeStruct((cfg.R, cfg.DK), F32),
                   jax.ShapeDtypeStruct((cfg.R, cfg.DV), F32), jax.ShapeDtypeStruct((cfg.R, ALR_W), F32),
                   jax.ShapeDtypeStruct((ALR_W, cfg.DK), F32),
                   jax.ShapeDtypeStruct((SUBLANES, cfg.DK), F32)],
        scratch_shapes=[pltpu.VMEM((N_HEADS, hv, hk), F32)],
        compiler_params=_params(),
    )(*args)


def pool_mix(name, src, coff, cfg, transpose):
    tm, pg = cfg.TM, cfg.PG
    mt = cfg.R // tm
    reach = -(-(max(2 ** N_POOL // 2, 1) * GRID_W) // tm)
    nk = 2 * reach + 1
    img_rows = cfg.S // GRID_W
    shift = GRID_W.bit_length() - 1

    def ktile(m, d):
        return jnp.where(m == 0, 0, jnp.clip(m + d - reach, 1, mt - 1))

    def counts(idx, is_ctx, lo, hi):
        ctx_n = jnp.minimum(idx + hi + 1, cfg.CL) - jnp.maximum(idx - lo, 0)
        r, c = idx >> shift, idx & (GRID_W - 1)
        lat_n = ((jnp.minimum(r + hi + 1, img_rows) - jnp.maximum(r - lo, 0))
                 * (jnp.minimum(c + hi + 1, GRID_W) - jnp.maximum(c - lo, 0)))
        return jnp.where(is_ctx, ctx_n, lat_n).astype(F32)

    a = np.arange(tm)[:, None]
    b = np.arange(tm)[None, :]
    masks = np.zeros((N_POOL, nk + 1, tm, tm), np.float32)
    for g_ in range(N_POOL):
        lo_ = 2 ** g_
        hi_ = lo_ - 1
        for d_ in range(nk):
            dr = (d_ - reach) * (tm // GRID_W) + (b >> shift) - (a >> shift)
            dc = (b & (GRID_W - 1)) - (a & (GRID_W - 1))
            masks[g_, d_] = (dr >= -lo_) & (dr <= hi_) & (dc >= -lo_) & (dc <= hi_)
        masks[g_, nk] = (b - a >= -lo_) & (b - a <= hi_)
    masks = jnp.asarray(masks, MXU_DTYPE)

    def body(src_ref, self_ref, mask_ref, o_ref, acc):
        g, m, d = pl.program_id(0), pl.program_id(1), pl.program_id(2)
        lo = jnp.left_shift(1, g)
        hi = lo - 1
        is_ctx = m == 0
        kt = m + d - reach
        valid = jnp.where(is_ctx, d == reach, (kt >= 1) & (kt <= mt - 1))
        seg = jnp.where(is_ctx, 0, cfg.CL)

        @pl.when(d == 0)
        def _():
            acc[...] = jnp.zeros_like(acc)

        @pl.when(valid)
        def _():
            x = src_ref[...]
            if transpose:
                kidx = lax.broadcasted_iota(jnp.int32, (tm, 1), 0) + (kt * tm - seg)
                acc[...] += _dot_mask(mask_ref[0, 0], x / counts(kidx, is_ctx, lo, hi), TN)
            else:
                acc[...] += _dot_mask(mask_ref[0, 0], x, NN)

        @pl.when(d == nk - 1)
        def _():
            res = acc[...]
            if not transpose:
                midx = lax.broadcasted_iota(jnp.int32, (tm, 1), 0) + (m * tm - seg)
                res = res / counts(midx, is_ctx, lo, hi)
            o_ref[...] = (res - self_ref[...]).astype(o_ref.dtype)

    which = (lambda d: 2 * reach - d) if transpose else (lambda d: d)
    return pl.pallas_call(
        body, name=name, grid=(N_POOL, mt, nk),
        in_specs=[pl.BlockSpec((tm, pg), lambda g, m, d: (ktile(m, d), coff + g)),
                  pl.BlockSpec((tm, pg), lambda g, m, d: (m, coff + g)),
                  pl.BlockSpec((1, 1, tm, tm), lambda g, m, d: (g, jnp.where(m == 0, nk, which(d)), 0, 0))],
        out_specs=pl.BlockSpec((tm, pg), lambda g, m, d: (m, g)),
        out_shape=jax.ShapeDtypeStruct((cfg.R, cfg.DP), F32),
        scratch_shapes=[pltpu.VMEM((tm, pg), F32)], compiler_params=_params(),
    )(src, src, masks)


def _my_place():
    return lax.axis_index("x"), lax.axis_index("y"), lax.axis_index("c")


def _flip(v, bit):
    return 1 - v if bit else v


def all_gather8(name, block):
    rows, w = block.shape

    def body(x_ref, out_ref, send_sems, recv_sems, local_sem):
        x, y, c = _my_place()
        me = 4 * x + 2 * y + c
        mine = pltpu.make_async_copy(x_ref, out_ref.at[me], local_sem)
        mine.start()
        sends = []
        for k in range(1, N_DEV):
            peer = (_flip(x, k & 4), _flip(y, k & 2), _flip(c, k & 1))
            cp = pltpu.make_async_remote_copy(src_ref=x_ref, dst_ref=out_ref.at[me], send_sem=send_sems.at[k - 1],
                                              recv_sem=recv_sems.at[k - 1], device_id=peer, device_id_type=MESH)
            cp.start()
            sends.append(cp)
        for k in range(1, N_DEV):
            peer = (_flip(x, k & 4), _flip(y, k & 2), _flip(c, k & 1))
            slot = 4 * peer[0] + 2 * peer[1] + peer[2]
            pltpu.make_async_remote_copy(src_ref=x_ref, dst_ref=out_ref.at[slot], send_sem=send_sems.at[k - 1],
                                         recv_sem=recv_sems.at[k - 1], device_id=peer, device_id_type=MESH).wait_recv()
        for cp in sends:
            cp.wait_send()
        mine.wait()

    return pl.pallas_call(
        body, name=name, out_shape=jax.ShapeDtypeStruct((N_DEV, rows, w), block.dtype),
        in_specs=[pl.BlockSpec(memory_space=pl.ANY)], out_specs=pl.BlockSpec(memory_space=pl.ANY),
        scratch_shapes=[pltpu.SemaphoreType.DMA((N_DEV - 1,)), pltpu.SemaphoreType.DMA((N_DEV - 1,)),
                        pltpu.SemaphoreType.DMA],
    )(block)


class _Comm:
    def __init__(self, ins, outs, sems, start, finish):
        self.ins, self.outs, self.sems, self.start, self.finish = list(ins), list(outs), list(sems), start, finish


def run_comm(name, comm):
    def body(*refs):
        n_i, n_o = len(comm.ins), len(comm.outs)
        comm.start(refs[:n_i], refs[n_i:n_i + n_o], refs[n_i + n_o:])
        comm.finish(refs[:n_i], refs[n_i:n_i + n_o], refs[n_i + n_o:])

    any_spec = pl.BlockSpec(memory_space=pl.ANY)
    return list(pl.pallas_call(body, name=name, out_shape=comm.outs, in_specs=[any_spec] * len(comm.ins),
                               out_specs=[any_spec] * len(comm.outs), scratch_shapes=comm.sems)(*comm.ins))


def gather_plan(shards, n_split):
    n = len(shards)

    def part(ref, t, core):
        if t >= n_split:
            return ref
        h = shards[t].shape[0] // 2
        return ref.at[pl.ds(core * h, h)]

    def ici(srcs, dsts, sems, k, t, slot, place):
        x, y, c = place
        return pltpu.make_async_remote_copy(
            src_ref=part(srcs[t], t, c), dst_ref=part(dsts[t].at[slot], t, c), send_sem=sems[0].at[t * 3 + k - 1],
            recv_sem=sems[1].at[t * 3 + k - 1], device_id=(_flip(x, k & 2), _flip(y, k & 1), c), device_id_type=MESH)

    def handed(dsts, sems, k, t, slot, place, core):
        x, y, c = place
        half = part(dsts[t].at[slot], t, core)
        return pltpu.make_async_remote_copy(
            src_ref=half, dst_ref=half, send_sem=sems[2].at[t * 3 + k - 1], recv_sem=sems[3].at[t * 3 + k - 1],
            device_id=(x, y, 1 - c), device_id_type=MESH)

    def start(srcs, dsts, sems):
        place = _my_place()
        me = 2 * place[0] + place[1]
        for t in range(n):
            pltpu.make_async_copy(srcs[t], dsts[t].at[me], sems[4].at[t]).start()
        for k in range(1, N_CHIPS):
            for t in range(n):
                ici(srcs, dsts, sems, k, t, me, place).start()

    def finish(srcs, dsts, sems):
        place = _my_place()
        x, y, c = place
        me = 2 * x + y
        for k in range(1, N_CHIPS):
            slot = 2 * _flip(x, k & 2) + _flip(y, k & 1)
            for t in range(n):
                ici(srcs, dsts, sems, k, t, slot, place).wait_recv()
                if t < n_split:
                    handed(dsts, sems, k, t, slot, place, c).start()
        for k in range(1, N_CHIPS):
            slot = 2 * _flip(x, k & 2) + _flip(y, k & 1)
            for t in range(n_split):
                handed(dsts, sems, k, t, slot, place, 1 - c).wait_recv()
        for k in range(1, N_CHIPS):
            slot = 2 * _flip(x, k & 2) + _flip(y, k & 1)
            for t in range(n):
                ici(srcs, dsts, sems, k, t, me, place).wait_send()
                if t < n_split:
                    handed(dsts, sems, k, t, slot, place, c).wait_send()
        for t in range(n):
            pltpu.make_async_copy(srcs[t], dsts[t].at[me], sems[4].at[t]).wait()

    outs = [jax.ShapeDtypeStruct((N_CHIPS,) + s.shape, s.dtype) for s in shards]
    sems = [pltpu.SemaphoreType.DMA((3 * n,))] * 4 + [pltpu.SemaphoreType.DMA((n,))]
    return _Comm(shards, outs, sems, start, finish)


def scatter_plan(bufs):
    n = len(bufs)

    def copies(srcs, dsts, sems):
        x, y, c = _my_place()
        return [pltpu.make_async_remote_copy(
            src_ref=srcs[t].at[k - 1], dst_ref=dsts[t].at[k - 1], send_sem=sems[0].at[t * 3 + k - 1],
            recv_sem=sems[1].at[t * 3 + k - 1], device_id=(_flip(x, k & 2), _flip(y, k & 1), c), device_id_type=MESH)
            for k in range(1, N_CHIPS) for t in range(n)]

    def start(srcs, dsts, sems):
        for cp in copies(srcs, dsts, sems):
            cp.start()

    def finish(srcs, dsts, sems):
        for cp in copies(srcs, dsts, sems):
            cp.wait_recv()
        for cp in copies(srcs, dsts, sems):
            cp.wait_send()

    return _Comm(bufs, [jax.ShapeDtypeStruct(b.shape, b.dtype) for b in bufs],
                 [pltpu.SemaphoreType.DMA((3 * n,)), pltpu.SemaphoreType.DMA((3 * n,))], start, finish)


def swap_plan(bufs):
    n = len(bufs)

    def copies(srcs, dsts, sems):
        x, y, c = _my_place()
        return [pltpu.make_async_remote_copy(src_ref=srcs[t], dst_ref=dsts[t], send_sem=sems[0].at[t],
                                             recv_sem=sems[1].at[t], device_id=(x, y, 1 - c), device_id_type=MESH)
                for t in range(n)]

    def start(srcs, dsts, sems):
        for cp in copies(srcs, dsts, sems):
            cp.start()

    def finish(srcs, dsts, sems):
        for cp in copies(srcs, dsts, sems):
            cp.wait_recv()
        for cp in copies(srcs, dsts, sems):
            cp.wait_send()

    return _Comm(bufs, [jax.ShapeDtypeStruct(b.shape, b.dtype) for b in bufs],
                 [pltpu.SemaphoreType.DMA((n,)), pltpu.SemaphoreType.DMA((n,))], start, finish)


def _vec8(*rows):
    w = rows[0].shape[-1]
    out = jnp.zeros((SUBLANES, w), F32)
    for r, v in enumerate(rows):
        out = out.at[r].set(v.reshape(w).astype(F32))
    return out


def _pack(arrays):
    parts = []
    for a in arrays:
        flat = a.reshape(-1).astype(F32)
        pad = (-flat.shape[0]) % PACK_W
        parts.append(jnp.pad(flat, (0, pad)))
    flat = jnp.concatenate(parts)
    pad = (-flat.shape[0]) % (PACK_W * SUBLANES)
    return jnp.pad(flat, (0, pad)).reshape(-1, PACK_W)


def _unpack(packed, shapes):
    flat = packed.reshape(-1)
    out, pos = [], 0
    for shp in shapes:
        size = 1
        for d in shp:
            size *= d
        out.append(flat[pos:pos + size].reshape(shp))
        pos += size + (-size) % PACK_W
    return out


def _rows2d(a):
    return a.reshape(-1, a.shape[-1])


class _Cfg:
    pass


def _adam(name, w, grads, m, v, half_rows=None, core=None):
    rows, width = w.shape
    c1 = 1.0 - ADAM_B1 ** ADAM_STEP
    c2 = 1.0 - ADAM_B2 ** ADAM_STEP

    def update(wv, mv, vv, g):
        m_new = ADAM_B1 * mv + (1.0 - ADAM_B1) * g
        v_new = ADAM_B2 * vv + (1.0 - ADAM_B2) * (g * g)
        delta = -ADAM_LR * ((m_new / c1) / (jnp.sqrt(v_new / c2) + ADAM_EPS) + ADAM_WD * wv)
        return [g, delta, m_new, v_new], []

    if half_rows is None:
        tm = _pick(rows, (128, 64, 32, 16, 8))

        def body(i, j, rows_, vecs):
            g = rows_[3]
            for extra in rows_[4:]:
                g = g + extra
            return update(rows_[0], rows_[1], rows_[2], g)

        return ew(name, body, rows // tm, 1, tm, width, [(w, 0, 0), (m, 0, 0), (v, 0, 0)] + [(g, 0, 0) for g in grads], [],
                  [(F32, rows, width)] * 4)

    tm = _pick(half_rows, (128, 64, 32, 16, 8))
    nb = half_rows // tm
    run_tile = lambda i: (i // (2 * nb)) * nb + i % nb
    core_vec = jnp.broadcast_to(core.astype(F32), (SUBLANES, width))

    def body(i, j, rows_, vecs):
        owner = ((i // nb) % 2).astype(F32)
        g = jnp.where(vecs[0][0:1, :] == owner, rows_[3], rows_[4])
        return update(rows_[0], rows_[1], rows_[2], g)

    return ew(name, body, rows // tm, 1, tm, width,
              [(w, 0, 0), (m, 0, 0), (v, 0, 0), (grads[0], run_tile, 0), (grads[1], run_tile, 0)], [(core_vec, 0)],
              [(F32, rows, width)] * 4, pass_i=True)


def _sum_rows(name, arr, nparts, rows, width, dtype=F32):
    tm = _pick(rows, (256, 128, 64, 32, 16, 8))
    nblk = rows // tm

    def body(i, j, rows_, vecs):
        acc = rows_[0].astype(F32)
        for r in rows_[1:]:
            acc = acc + r.astype(F32)
        return [acc], []

    return ew(name, body, nblk, 1, tm, width, [(arr, p * nblk, 0) for p in range(nparts)], [], [(dtype, rows, width)])[0]


def kernel(x, c, ctx, c_ctx, w_ada, b_ada, w_in, w_decay_up, b_decay_up, gla_norm_gain, w_pool_group, pool_scale, w_gla_out, w_pool_out, w_out, ln_mix_gain, ln_mix_bias, w_ffn_in, w_ffn_out, ln_ffn_gain, ln_ffn_bias, loss_target, m_c_ctx, m_w_ada, m_b_ada, m_w_in, m_w_decay_up, m_b_decay_up, m_gla_norm_gain, m_w_pool_group, m_pool_scale, m_w_gla_out, m_w_pool_out, m_w_out, m_ln_mix_gain, m_ln_mix_bias, m_w_ffn_in, m_w_ffn_out, m_ln_ffn_gain, m_ln_ffn_bias, v_c_ctx, v_w_ada, v_b_ada, v_w_in, v_w_decay_up, v_b_decay_up, v_gla_norm_gain, v_w_pool_group, v_pool_scale, v_w_gla_out, v_w_pool_out, v_w_out, v_ln_mix_gain, v_ln_mix_bias, v_w_ffn_in, v_w_ffn_out, v_ln_ffn_gain, v_ln_ffn_bias):
    weights = dict(c_ctx=c_ctx, w_ada=w_ada, b_ada=b_ada, w_in=w_in, w_decay_up=w_decay_up, b_decay_up=b_decay_up,
                   gla_norm_gain=gla_norm_gain, w_pool_group=w_pool_group, pool_scale=pool_scale, w_gla_out=w_gla_out,
                   w_pool_out=w_pool_out, w_out=w_out, ln_mix_gain=ln_mix_gain, ln_mix_bias=ln_mix_bias,
                   w_ffn_in=w_ffn_in, w_ffn_out=w_ffn_out, ln_ffn_gain=ln_ffn_gain, ln_ffn_bias=ln_ffn_bias)
    mom1 = dict(c_ctx=m_c_ctx, w_ada=m_w_ada, b_ada=m_b_ada, w_in=m_w_in, w_decay_up=m_w_decay_up, b_decay_up=m_b_decay_up,
                gla_norm_gain=m_gla_norm_gain, w_pool_group=m_w_pool_group, pool_scale=m_pool_scale, w_gla_out=m_w_gla_out,
                w_pool_out=m_w_pool_out, w_out=m_w_out, ln_mix_gain=m_ln_mix_gain, ln_mix_bias=m_ln_mix_bias,
                w_ffn_in=m_w_ffn_in, w_ffn_out=m_w_ffn_out, ln_ffn_gain=m_ln_ffn_gain, ln_ffn_bias=m_ln_ffn_bias)
    mom2 = dict(c_ctx=v_c_ctx, w_ada=v_w_ada, b_ada=v_b_ada, w_in=v_w_in, w_decay_up=v_w_decay_up, b_decay_up=v_b_decay_up,
                gla_norm_gain=v_gla_norm_gain, w_pool_group=v_w_pool_group, pool_scale=v_pool_scale, w_gla_out=v_w_gla_out,
                w_pool_out=v_w_pool_out, w_out=v_w_out, ln_mix_gain=v_ln_mix_gain, ln_mix_bias=v_ln_mix_bias,
                w_ffn_in=v_w_ffn_in, w_ffn_out=v_w_ffn_out, ln_ffn_gain=v_ln_ffn_gain, ln_ffn_bias=v_ln_ffn_bias)
    names = list(weights)

    cfg = _Cfg()
    L, D = w_ada.shape[0], x.shape[-1]
    S, CL = x.shape[1], ctx.shape[1]
    cfg.L, cfg.D, cfg.S, cfg.CL, cfg.R, cfg.TM = L, D, S, CL, S + CL, CL
    DK, DV, DP = D // 2, D, D // 2
    cfg.DK, cfg.DV, cfg.DP = DK, DV, DP
    cfg.HK, cfg.HV, cfg.PG = DK // N_HEADS, DV // N_HEADS, DP // N_POOL
    DFF = w_ffn_out.shape[1] * N_CHIPS
    NP = 2 * DK + 2 * DV + DP + 2 * D + ALR_W
    cfg.NP, cfg.DFF = NP, DFF
    R, TM, HK, HV, PG = cfg.R, cfg.TM, cfg.HK, cfg.HV, cfg.PG
    MT = R // TM
    alpha = (2.0 * L) ** 0.25
    assert S % TM == 0 and TM % CHUNK == 0 and S % GRID_W == 0 and TM % GRID_W == 0
    OFF_G, OFF_P, OFF_BGA, OFF_BGB = 2 * DK + DV, 2 * DK + 2 * DV, 2 * DK + 2 * DV + DP, 2 * DK + 2 * DV + DP + D
    ALR0 = 2 * DK + 2 * DV
    TE = 512
    TL = TM // 2
    assert D % TE == 0 and DP % TE == 0

    xi, yi, ci = _my_place()
    chip = 2 * xi + yi
    dev = 4 * xi + 2 * yi + ci

    n_ada = w_ada.shape[-1]
    c_all = all_gather8("ag_cond", jnp.pad(c.reshape(1, D), ((0, SUBLANES - 1), (0, 0))))[:, 0, :]
    cond = jnp.concatenate([c_all, c_ctx.reshape(1, D), jnp.zeros((16 - N_DEV - 1, D), F32)], axis=0)

    def silu_body(i, j, rows_, vecs):
        return [_silu(rows_[0]), _dsilu(rows_[0])], []

    act, dact = ew("cond_silu", silu_body, 1, 1, 16, D, [(cond, 0, 0)], [], [(F32, 16, D)] * 2)
    w_ada2 = w_ada.reshape(L * D, n_ada)
    b_ada_mine = lax.dynamic_slice_in_dim(b_ada, chip * n_ada, n_ada, axis=1)
    tn_ada = _pick(n_ada, (1024, 512, 256, 128))
    mods = [matmul("ada_fwd", act, w_ada2, 'nn', 16, n_ada, D, tm=16, tn=tn_ada, b_off=(l, 0),
                   bias=_vec8(b_ada_mine[l]), precise=True) for l in range(L)]
    mods_all = all_gather8("ag_mods", jnp.concatenate(mods, axis=0))
    mods_all = mods_all[0::2].reshape(N_CHIPS, L, 16, n_ada).transpose(1, 2, 0, 3).reshape(L, 16, N_MOD * D)
    modv = [_vec8(mods_all[l, N_DEV], lax.dynamic_index_in_dim(mods_all[l], dev, 0, keepdims=False)) for l in range(L)]
    MB = D // TE

    big = ['w_in', 'w_gla_out', 'w_pool_out', 'w_out', 'w_ffn_in', 'w_ffn_out', 'w_pool_group']
    hosts = [['w_in'], ['w_gla_out', 'w_out', 'w_pool_out', 'w_pool_group'], ['w_ffn_out'], ['w_ffn_in']]
    all_hosted = sum(hosts, [])
    layer_shards = lambda nms, l: [weights[nm][l].astype(WIRE_DTYPE) for nm in nms]

    def cols_together(g):
        return jnp.moveaxis(g, 0, -2).reshape(g.shape[1:-1] + (N_CHIPS * g.shape[-1],))

    def rows_together(g):
        return g.reshape((N_CHIPS * g.shape[1], g.shape[2]))

    te_ff = _pick(DFF, (1408, 1024, 512, 256, 128))

    def ffn_tiles(w, to_tiles):
        shape = (D, 2, DFF // te_ff, te_ff) if to_tiles else (D, DFF // te_ff, 2, te_ff)
        return jnp.swapaxes(w.reshape(shape), 1, 2).reshape(D, 2 * DFF)

    def assemble(gw):
        win_ref = cols_together(gw['w_in'])
        win = jnp.concatenate([win_ref[:, :ALR0], win_ref[:, ALR0 + 2 * GATE_RANK:], win_ref[:, ALR0:ALR0 + 2 * GATE_RANK],
                               jnp.zeros((D, ALR_W - 2 * GATE_RANK), WIRE_DTYPE)], axis=-1)
        wpg = jnp.moveaxis(gw['w_pool_group'], 0, 1).reshape(N_POOL, PG, PG)
        wpg_bd = jnp.zeros((N_POOL, PG, N_POOL, PG), WIRE_DTYPE)
        for g in range(N_POOL):
            wpg_bd = wpg_bd.at[g, :, g, :].set(wpg[g])
        win_bwd = jnp.pad(win_ref, ((0, 0), (0, NP - win_ref.shape[1])))
        return dict(win=win, win_bwd=win_bwd, wgo=rows_together(gw['w_gla_out']), wpo=cols_together(gw['w_pool_out']),
                    wout=rows_together(gw['w_out']), wfi=ffn_tiles(cols_together(gw['w_ffn_in']), True), wfo=rows_together(gw['w_ffn_out']),
                    wpg_bd=wpg_bd.reshape(DP, DP))

    first = run_comm("gather_weights", gather_plan(layer_shards(all_hosted, 0) + [w_decay_up, b_decay_up],
                                                   n_split=len(big)))
    W = [assemble(dict(zip(all_hosted, first[:len(big)])))] + [None] * (L - 1)
    next_plan = lambda l, k: gather_plan(layer_shards(hosts[k], l + 1), n_split=len(hosts[k]))
    wdu = cols_together(first[len(big)])
    bdu = cols_together(first[len(big) + 1])
    wd_pad = [[jnp.zeros((ALR_W, DK), F32).at[d * GATE_RANK:(d + 1) * GATE_RANK].set(wdu[l, d]) for d in range(2)]
              for l in range(L)]
    bd_pad = [[_vec8(bdu[l, d]) for d in range(2)] for l in range(L)]

    tn_np = _pick(NP, (1280, 1024, 768, 512, 256, 128))
    tn_d = _pick(D, (1024, 512, 256, 128))
    tn_ff2 = _pick(2 * DFF, (1024, 512, 256, 128))
    tn_ff = _pick(DFF, (2816, 1408, 1024, 512, 256, 128))
    tn_dp = _pick(DP, (1024, 512, 256, 128))
    tw = lambda n_: _pick(n_, (512, 256, 128))

    gain8 = lambda v_: _vec8(v_)

    xs = jnp.concatenate([ctx.reshape(CL, D), x.reshape(S, D)], axis=0)
    saved = []
    for l in range(L):
        sv = {}
        mv = modv[l]

        def mod_body(i, j, rows_, vecs):
            return [rows_[0] * (1.0 + _cls(vecs[1], i)) + _cls(vecs[0], i)], []

        h1 = ew("modulate", mod_body, MT, D // tn_d, TM, tn_d, [(xs, 0, 0)],
                [(mv, 0 * (D // tn_d)), (mv, 1 * (D // tn_d))], [(MXU_DTYPE, R, D)])[0]
        Wl = W[l]
        if l + 1 < L:
            proj, got0 = matmul("proj_and_gather", h1, Wl['win'], 'nn', R, NP, D, tm=TM, tn=tn_np, comm=next_plan(l, 0))
            o_f, st_f, got1 = gla_fwd("gla_fwd_and_gather", proj, wd_pad[l][0], bd_pad[l][0], cfg, False, next_plan(l, 1))
            o_b, st_b, got2 = gla_fwd("gla_rev_and_gather", proj, wd_pad[l][1], bd_pad[l][1], cfg, True, next_plan(l, 2))
        else:
            proj = matmul("proj", h1, Wl['win'], 'nn', R, NP, D, tm=TM, tn=tn_np)
            o_f, st_f, _ = gla_fwd("gla_fwd", proj, wd_pad[l][0], bd_pad[l][0], cfg, False)
            o_b, st_b, _ = gla_fwd("gla_rev", proj, wd_pad[l][1], bd_pad[l][1], cfg, True)

        def post_body(i, j, rows_, vecs):
            o = rows_[0] + rows_[1]
            on = o * lax.rsqrt(jnp.mean(o * o, axis=-1, keepdims=True) + RMS_EPS)
            return [on * vecs[0][0:1, :] * _silu(rows_[2])], []

        a_gla = ew("gla_post", post_body, MT, N_HEADS, TM, HV, [(o_f, 0, 0), (o_b, 0, 0), (proj, 0, OFF_G // HV)],
                   [(gain8(gla_norm_gain[l]), 0)], [(MXU_DTYPE, R, DV)])[0]
        y_gla = matmul("gla_out", a_gla, Wl['wgo'], 'nn', R, D, DV, tm=TM, tn=tn_d)
        u_pool = pool_mix("pool_fwd", proj, OFF_P // PG, cfg, False)
        t_pool = matmul("pool_group", u_pool, Wl['wpg_bd'], 'nn', R, DP, DP, tm=TM, tn=tn_dp)

        def scale_body(i, j, rows_, vecs):
            return [rows_[0] * vecs[0][0:1, :]], []

        ts_pool = ew("pool_scale", scale_body, MT, DP // TE, TM, TE, [(t_pool, 0, 0)], [(gain8(pool_scale[l]), 0)],
                     [(MXU_DTYPE, R, DP)])[0]
        y_pool = matmul("pool_out", ts_pool, Wl['wpo'], 'nn', R, D, DP, tm=TM, tn=tn_d)

        def merge_body(i, j, rows_, vecs):
            return [_sigmoid(rows_[2]) * rows_[0] + _sigmoid(rows_[3]) * rows_[1]], []

        mg = ew("merge", merge_body, MT, D // TE, TM, TE,
                [(y_gla, 0, 0), (y_pool, 0, 0), (proj, 0, OFF_BGA // TE), (proj, 0, OFF_BGB // TE)], [],
                [(MXU_DTYPE, R, D)])[0]
        y_mix = matmul("mix_out", mg, Wl['wout'], 'nn', R, D, D, tm=TM, tn=tn_d)

        def ln_body(i, j, rows_, vecs):
            r = alpha * rows_[0] + _cls(vecs[0], i) * rows_[1]
            mu = jnp.mean(r, axis=-1, keepdims=True)
            rc = r - mu
            var = jnp.mean(rc * rc, axis=-1, keepdims=True)
            return [rc * lax.rsqrt(var + LN_EPS) * vecs[1][0:1, :] + vecs[2][0:1, :]], []

        x1 = ew("resid_ln", ln_body, R // TL, 1, TL, D, [(xs, 0, 0), (y_mix, 0, 0)],
                [(mv, 2), (gain8(ln_mix_gain[l]), 0), (gain8(ln_mix_bias[l]), 0)], [(F32, R, D)], ctx_tiles=CL // TL)[0]
        h2 = ew("modulate", mod_body, MT, D // tn_d, TM, tn_d, [(x1, 0, 0)],
                [(mv, 3 * (D // tn_d)), (mv, 4 * (D // tn_d))], [(MXU_DTYPE, R, D)])[0]
        if l + 1 < L:
            u_ffn, got3 = matmul("ffn_in_and_gather", h2, Wl['wfi'], 'nn', R, 2 * DFF, D, tm=TM, tn=tn_ff2,
                                 comm=next_plan(l, 3))
            W[l + 1] = assemble(dict(zip(all_hosted, got0 + got1 + got2 + got3)))
        else:
            u_ffn = matmul("ffn_in", h2, Wl['wfi'], 'nn', R, 2 * DFF, D, tm=TM, tn=tn_ff2)
        s_ffn = swiglu("swiglu", u_ffn, MT, DFF // te_ff, TM, te_ff)
        f_ffn = matmul("ffn_out", s_ffn, Wl['wfo'], 'nn', R, D, DFF, tm=TM, tn=tn_d)
        x2 = ew("resid_ln", ln_body, R // TL, 1, TL, D, [(x1, 0, 0), (f_ffn, 0, 0)],
                [(mv, 5), (gain8(ln_ffn_gain[l]), 0), (gain8(ln_ffn_bias[l]), 0)], [(F32, R, D)], ctx_tiles=CL // TL)[0]
        sv.update(xs=xs, h1=h1, proj=proj, o_f=o_f, o_b=o_b, st_f=st_f, st_b=st_b, a_gla=a_gla, y_gla=y_gla, u_pool=u_pool,
                  t_pool=t_pool, ts_pool=ts_pool, y_pool=y_pool, mg=mg, y_mix=y_mix, x1=x1, h2=h2, u_ffn=u_ffn, s_ffn=s_ffn,
                  f_ffn=f_ffn)
        saved.append(sv)
        xs = x2

    tgt = loss_target.reshape(S, D)

    def loss_body(i, j, rows_, vecs):
        d = jnp.where(i, 0.0, rows_[0] - rows_[1])
        return [d * (1.0 / D)], [_colsum(d * d)]

    d_x, sq = ew("loss", loss_body, MT, 1, TM, D, [(xs, 0, 0), (tgt, lambda i: jnp.maximum(i - 1, 0), 0)], [],
                 [(F32, R, D)], [(D, False)])

    def total_body(i, j, rows_, vecs):
        return [jnp.broadcast_to(jnp.sum(rows_[0], axis=-1, keepdims=True), (SUBLANES, D)) * (0.5 / D)], []

    loss_local = ew("loss_total", total_body, 1, 1, SUBLANES, D, [(sq, 0, 0)], [], [(F32, SUBLANES, D)])[0][0, 0]
    loss = lax.psum(loss_local, ("x", "y", "c"))

    gsmall = {nm: [None] * L for nm in ['gla_norm_gain', 'pool_scale', 'ln_mix_gain', 'ln_mix_bias', 'ln_ffn_gain',
                                        'ln_ffn_bias', 'w_decay_up', 'b_decay_up']}
    gbig = {nm: [None] * L for nm in big}
    dmod = [None] * L

    def ln_bwd(name, x_in, br, d_out, mv, gt_blk, gain):
        def body(i, j, rows_, vecs):
            gt = _cls(vecs[0], i)
            r = alpha * rows_[0] + gt * rows_[1]
            mu = jnp.mean(r, axis=-1, keepdims=True)
            rc = r - mu
            rstd = lax.rsqrt(jnp.mean(rc * rc, axis=-1, keepdims=True) + LN_EPS)
            xhat = rc * rstd
            dxh = rows_[2] * vecs[1][0:1, :]
            dr = rstd * (dxh - jnp.mean(dxh, axis=-1, keepdims=True) - xhat * jnp.mean(dxh * xhat, axis=-1, keepdims=True))
            return [dr, gt * dr], [_colsum(rows_[2] * xhat), _colsum(rows_[2]), _colsum(dr * rows_[1])]

        return ew(name, body, R // TL, 1, TL, D, [(x_in, 0, 0), (br, 0, 0), (d_out, 0, 0)], [(mv, gt_blk), (gain8(gain), 0)],
                  [(F32, R, D), (MXU_DTYPE, R, D)], [(D, False), (D, False), (D, True)], ctx_tiles=CL // TL)

    def mod_bwd(name, d_h, x_in, d_r, mv, sc_blk):
        def body(i, j, rows_, vecs):
            return ([rows_[0] * (1.0 + _cls(vecs[0], i)) + alpha * rows_[2]],
                    [_colsum(rows_[0] * rows_[1]), _colsum(rows_[0])])

        return ew(name, body, MT, D // tn_d, TM, tn_d, [(d_h, 0, 0), (x_in, 0, 0), (d_r, 0, 0)],
                  [(mv, sc_blk * (D // tn_d))], [(F32, R, D)], [(D, True), (D, True)])

    shard_axis = {'w_in': 1, 'w_gla_out': 0, 'w_pool_out': 1, 'w_out': 0, 'w_ffn_in': 1, 'w_ffn_out': 0, 'w_pool_group': 1}
    scatter_groups = [['w_in'], ['w_ffn_in'], ['w_gla_out', 'w_pool_out', 'w_out', 'w_ffn_out', 'w_pool_group']]

    def halves(a):
        h = a.shape[0] // 2
        return (lax.dynamic_slice_in_dim(a, ci * h, h, axis=0), lax.dynamic_slice_in_dim(a, (1 - ci) * h, h, axis=0))

    def cut_layer(l):
        cut = dict(keep_r={}, give_r={}, keep_o={}, give_o={})
        for nm in big:
            g = gbig[nm][l]
            g = ffn_tiles(g, False) if nm == 'w_ffn_in' else g
            ax = shard_axis[nm]
            width = weights[nm].shape[ax + 1]
            piece = lambda which: lax.dynamic_slice_in_dim(g, which * width, width, axis=ax)
            parts = [halves(piece(jnp.bitwise_xor(chip, k)).astype(WIRE_DTYPE)) for k in range(1, N_CHIPS)]
            cut['keep_r'][nm] = jnp.stack([p_[0] for p_ in parts])
            cut['give_r'][nm] = jnp.stack([p_[1] for p_ in parts])
            cut['keep_o'][nm], cut['give_o'][nm] = halves(piece(chip))
        return cut

    def swap_partials_plan(cut):
        return swap_plan([cut['give_r'][nm] for nm in big] + [cut['give_o'][nm] for nm in big])

    def add_partials(cut, got):
        cut['got_o'] = dict(zip(big, got[len(big):]))
        cut['send_r'] = {}
        for nm, theirs in zip(big, got[:len(big)]):
            a2, b2 = _rows2d(cut['keep_r'][nm]), _rows2d(theirs)
            tm_ = _pick(a2.shape[0], (256, 128, 64, 32, 16, 8))

            def add2_body(i, j, rows__, vecs):
                return [rows__[0].astype(F32) + rows__[1].astype(F32)], []

            cut['send_r'][nm] = ew("add_partials", add2_body, a2.shape[0] // tm_, 1, tm_, a2.shape[1],
                                   [(a2, 0, 0), (b2, 0, 0)], [], [(WIRE_DTYPE, a2.shape[0], a2.shape[1])]
                                   )[0].reshape(cut['keep_r'][nm].shape)
        cut['recvd'] = {}

    def sum_grads(cut):
        done = {}
        for nm in big:
            a2, b2 = _rows2d(cut['keep_o'][nm]), _rows2d(cut['got_o'][nm])
            rows_, width = a2.shape
            tm_ = _pick(rows_, (256, 128, 64, 32, 16, 8))

            def psum_body(i, j, rows__, vecs):
                return [(rows__[0] + rows__[1]) + rows__[2].astype(F32) + rows__[3].astype(F32) + rows__[4].astype(F32)], []

            rb2 = cut['recvd'][nm].reshape(3 * rows_, width)
            done[nm] = ew("sum_grads", psum_body, rows_ // tm_, 1, tm_, width,
                          [(a2, 0, 0), (b2, 0, 0)] + [(rb2, k * (rows_ // tm_), 0) for k in range(3)], [],
                          [(F32, rows_, width)])[0]
        return done

    def scatter_group_plan(cut, group):
        return scatter_plan([cut['send_r'][nm] for nm in group])

    finished = [None] * L
    pending = None
    for l in reversed(range(L)):
        sv = saved[l]
        mv = modv[l]
        Wl = W[l]
        proj = sv['proj']
        dr2, d_f, g_gain, g_bias, g_gt_f = ln_bwd("ln_bwd", sv['x1'], sv['f_ffn'], d_x, mv, 5, ln_ffn_gain[l])
        gsmall['ln_ffn_gain'][l], gsmall['ln_ffn_bias'][l] = g_gain[0], g_bias[0]
        if pending is not None:
            d_s, got = matmul("ffn_out_dx_and_swap", d_f, Wl['wfo'], 'nt', R, DFF, D, tm=TM, tn=tn_ff,
                              comm=swap_partials_plan(pending))
            add_partials(pending, got)
        else:
            d_s = matmul("ffn_out_dx", d_f, Wl['wfo'], 'nt', R, DFF, D, tm=TM, tn=tn_ff)
        gbig['w_ffn_out'][l] = matmul("ffn_out_dw", sv['s_ffn'], d_f, 'tn', DFF, D, R, tm=tw(DFF), tn=tn_d)
        d_u = swiglu_bwd("swiglu_bwd", d_s, sv['u_ffn'], MT, DFF // te_ff, TM, te_ff)
        tn_x = 512
        if pending is not None:
            d_h2, got = matmul("ffn_in_dx_and_scatter", d_u, Wl['wfi'], 'nt', R, D, 2 * DFF, tm=TM, tn=tn_x,
                               comm=scatter_group_plan(pending, scatter_groups[0]))
            pending['recvd'].update(zip(scatter_groups[0], got))
            gbig['w_ffn_in'][l], got = matmul("ffn_in_dw_and_scatter", sv['h2'], d_u, 'tn', D, 2 * DFF, R, tm=tw(D),
                                              tn=tn_ff2, comm=scatter_group_plan(pending, scatter_groups[1]))
            pending['recvd'].update(zip(scatter_groups[1], got))
        else:
            d_h2 = matmul("ffn_in_dx", d_u, Wl['wfi'], 'nt', R, D, 2 * DFF, tm=TM, tn=tn_x)
            gbig['w_ffn_in'][l] = matmul("ffn_in_dw", sv['h2'], d_u, 'tn', D, 2 * DFF, R, tm=tw(D), tn=tn_ff2)
        d_x1, g_sc_f, g_sh_f = mod_bwd("mod_bwd", d_h2, sv['x1'], dr2, mv, 4)
        dr1, d_y, g_gain, g_bias, g_gt_m = ln_bwd("ln_bwd", sv['xs'], sv['y_mix'], d_x1, mv, 2, ln_mix_gain[l])
        gsmall['ln_mix_gain'][l], gsmall['ln_mix_bias'][l] = g_gain[0], g_bias[0]
        d_mg = matmul("mix_out_dx", d_y, Wl['wout'], 'nt', R, D, D, tm=TM, tn=tn_d)
        gbig['w_out'][l] = matmul("mix_out_dw", sv['mg'], d_y, 'tn', D, D, R, tm=tw(D), tn=tn_d)

        def merge_bwd_body(i, j, rows_, vecs):
            d_m, yg, yp, ba, bb = rows_
            sa, sb = _sigmoid(ba), _sigmoid(bb)
            return [d_m * sa, d_m * sb, d_m * yg * sa * (1.0 - sa), d_m * yp * sb * (1.0 - sb)], []

        d_yg, d_yp, d_bga, d_bgb = ew(
            "merge_bwd", merge_bwd_body, MT, D // TE, TM, TE,
            [(d_mg, 0, 0), (sv['y_gla'], 0, 0), (sv['y_pool'], 0, 0), (proj, 0, OFF_BGA // TE), (proj, 0, OFF_BGB // TE)], [],
            [(MXU_DTYPE, R, D), (MXU_DTYPE, R, D), (F32, R, D), (F32, R, D)])
        d_ts = matmul("pool_out_dx", d_yp, Wl['wpo'], 'nt', R, DP, D, tm=TM, tn=tn_dp)
        gbig['w_pool_out'][l] = matmul("pool_out_dw", sv['ts_pool'], d_yp, 'tn', DP, D, R, tm=tw(DP), tn=tn_d)

        def scale_bwd_body(i, j, rows_, vecs):
            return [rows_[0] * vecs[0][0:1, :]], [_colsum(rows_[0] * rows_[1])]

        d_t, g_ps = ew("pool_scale_bwd", scale_bwd_body, MT, DP // TE, TM, TE, [(d_ts, 0, 0), (sv['t_pool'], 0, 0)],
                       [(gain8(pool_scale[l]), 0)], [(MXU_DTYPE, R, DP)], [(DP, False)])
        gsmall['pool_scale'][l] = g_ps[0]
        d_u_pool = matmul("pool_group_dx", d_t, Wl['wpg_bd'], 'nt', R, DP, DP, tm=TM, tn=tn_dp)
        g_bd = matmul("pool_group_dw", sv['u_pool'], d_t, 'tn', DP, DP, R, tm=tw(DP), tn=tn_dp)
        gbig['w_pool_group'][l] = jnp.stack([g_bd[g * PG:(g + 1) * PG, g * PG:(g + 1) * PG] for g in range(N_POOL)])
        d_p = pool_mix("pool_bwd", d_u_pool, 0, cfg, True)
        d_a = matmul("gla_out_dx", d_yg, Wl['wgo'], 'nt', R, DV, D, tm=TM, tn=tn_d)
        gbig['w_gla_out'][l] = matmul("gla_out_dw", sv['a_gla'], d_yg, 'tn', DV, D, R, tm=tw(DV), tn=tn_d)

        def post_bwd_body(i, j, rows_, vecs):
            d_a_, o_f_, o_b_, g_ = rows_
            gain = vecs[0][0:1, :]
            o = o_f_ + o_b_
            rstd = lax.rsqrt(jnp.mean(o * o, axis=-1, keepdims=True) + RMS_EPS)
            on = o * rstd
            sg = _silu(g_)
            d_on = d_a_ * gain * sg
            d_o_ = rstd * (d_on - on * jnp.mean(d_on * on, axis=-1, keepdims=True))
            return [d_o_, d_a_ * on * gain * _dsilu(g_)], [_colsum(d_a_ * on * sg)]

        d_o, d_g, g_gng = ew("gla_post_bwd", post_bwd_body, MT, N_HEADS, TM, HV,
                             [(d_a, 0, 0), (sv['o_f'], 0, 0), (sv['o_b'], 0, 0), (proj, 0, OFF_G // HV)],
                             [(gain8(gla_norm_gain[l]), 0)], [(F32, R, DV), (F32, R, DV)], [(DV, False)])
        gsmall['gla_norm_gain'][l] = g_gng[0]
        part = gla_bwd("gla_bwd", proj, wd_pad[l][0], bd_pad[l][0], sv['st_f'], d_o, cfg, False)
        full = gla_bwd("gla_rev_bwd", proj, wd_pad[l][1], bd_pad[l][1], sv['st_b'], d_o, cfg, True, addends=part[:4])
        d_q, d_k, d_v, d_alr = full[:4]
        g_wd = [res[4][d * GATE_RANK:(d + 1) * GATE_RANK] for d, res in enumerate((part, full))]
        g_bd_ = [res[5][0] for res in (part, full)]
        gsmall['w_decay_up'][l], gsmall['b_decay_up'][l] = jnp.stack(g_wd), jnp.stack(g_bd_)
        d_proj = jnp.concatenate([t_.astype(MXU_DTYPE) for t_ in (d_q, d_k, d_v, d_g, d_alr[:, :2 * GATE_RANK], d_p, d_bga, d_bgb,
                                                                   jnp.zeros((R, ALR_W - 2 * GATE_RANK), F32))], axis=-1)
        if pending is not None:
            d_h1, got = matmul("proj_dx_and_scatter", d_proj, Wl['win_bwd'], 'nt', R, D, NP, tm=TM, tn=tn_x,
                               comm=scatter_group_plan(pending, scatter_groups[2]))
            pending['recvd'].update(zip(scatter_groups[2], got))
            finished[l + 1] = sum_grads(pending)
        else:
            d_h1 = matmul("proj_dx", d_proj, Wl['win_bwd'], 'nt', R, D, NP, tm=TM, tn=tn_x)
        gbig['w_in'][l] = matmul("proj_dw", sv['h1'], d_proj, 'tn', D, NP, R, tm=tw(D), tn=tn_np)
        d_x, g_sc_m, g_sh_m = mod_bwd("mod_bwd", d_h1, sv['xs'], dr1, mv, 1)
        dmod[l] = jnp.concatenate([g_sh_m[:2], g_sc_m[:2], g_gt_m[:2], g_sh_f[:2], g_sc_f[:2], g_gt_f[:2]], axis=-1)
        pending = cut_layer(l)

    grad_x = d_x[CL:].reshape(x.shape)
    add_partials(pending, run_comm("swap_partials", swap_partials_plan(pending)))
    pending['recvd'] = dict(zip(big, run_comm("scatter_grads", scatter_plan([pending['send_r'][nm] for nm in big]))))
    finished[0] = sum_grads(pending)

    dmod = jnp.stack(dmod)
    summed = [dmod[:, 0]] + [jnp.stack(gsmall[nm]) for nm in
                             ['ln_mix_gain', 'ln_mix_bias', 'ln_ffn_gain', 'ln_ffn_bias', 'gla_norm_gain', 'pool_scale',
                              'w_decay_up', 'b_decay_up']]
    pack = _pack([dmod[:, 1]] + summed)
    prow = pack.shape[0]
    packs = all_gather8("ag_small", pack)
    tot = _sum_rows("sum_small", packs.reshape(N_DEV * prow, PACK_W), N_DEV, prow, PACK_W)
    shapes = [(L, N_MOD * D)] + [a.shape for a in summed]
    tot = _unpack(tot, shapes)
    dmod_ctx = tot[1]
    g_rep = dict(zip(['ln_mix_gain', 'ln_mix_bias', 'ln_ffn_gain', 'ln_ffn_bias', 'gla_norm_gain', 'pool_scale'], tot[2:8]))
    g_wdu_full, g_bdu_full = tot[8], tot[9]
    dmod_lat = jnp.stack([_unpack(packs[d_], shapes[:1])[0] for d_ in range(N_DEV)], axis=1)
    dm_all = jnp.concatenate([dmod_lat, dmod_ctx[:, None, :], jnp.zeros((L, 16 - N_DEV - 1, N_MOD * D), F32)], axis=1)

    dm_mine = lax.dynamic_slice_in_dim(dm_all, chip * n_ada, n_ada, axis=2)
    g_w_ada = jnp.stack([matmul("ada_dw", act, dm_mine[l], 'tn', D, n_ada, 16, tm=tw(D), tn=tn_ada, precise=True)
                         for l in range(L)])

    def bsum_body(i, j, rows_, vecs):
        return [jnp.broadcast_to(_colsum(rows_[0]), rows_[0].shape)], []

    g_b_ada = jnp.stack([ew("ada_db", bsum_body, 1, 1, 16, N_MOD * D, [(dm_all[l], 0, 0)], [],
                            [(F32, 16, N_MOD * D)])[0][0] for l in range(L)])
    part_c = [matmul("ada_dc", dm_mine[l], w_ada2, 'nt', 16, D, n_ada, tm=16, tn=tn_d, b_off=(l * (D // tn_d), 0),
                     precise=True)
              for l in range(L)]
    parts_c = all_gather8("ag_dcond", jnp.concatenate(part_c, axis=0))
    dc_rows = parts_c[0::2].reshape(N_CHIPS * L * 16, D)

    def dc_body(i, j, rows_, vecs):
        acc = rows_[0]
        for r_ in rows_[1:-1]:
            acc = acc + r_
        return [acc * rows_[-1]], []

    g_c_ctx = ew("dcond", dc_body, 1, 1, 16, D, [(dc_rows, p_, 0) for p_ in range(N_CHIPS * L)] + [(dact, 0, 0)], [],
                 [(F32, 16, D)])[0][N_DEV]

    mine_all = [jnp.concatenate([finished[l][nm] for l in range(L)], axis=0) for nm in big]
    other = run_comm("swap_grads", swap_plan(mine_all))

    out_g, out_d, out_m, out_v = {}, {}, {}, {}
    for nm, mine, theirs in zip(big, mine_all, other):
        shp = weights[nm].shape
        half_rows = (shp[1] // 2) * (shp[2] if len(shp) == 4 else 1)
        res = _adam("adam_" + nm, _rows2d(weights[nm]), [mine, theirs], _rows2d(mom1[nm]), _rows2d(mom2[nm]),
                    half_rows=half_rows, core=ci)
        out_g[nm], out_d[nm], out_m[nm], out_v[nm] = [r_.reshape(shp) for r_ in res]
    res = _adam("adam_w_ada", _rows2d(w_ada), [_rows2d(g_w_ada)], _rows2d(m_w_ada), _rows2d(v_w_ada))
    out_g['w_ada'], out_d['w_ada'], out_m['w_ada'], out_v['w_ada'] = [r_.reshape(w_ada.shape) for r_ in res]
    n_wd, n_bd = w_decay_up.shape[-1], b_decay_up.shape[-1]
    small_g = dict(g_rep, c_ctx=g_c_ctx, b_ada=g_b_ada,
                   w_decay_up=lax.dynamic_slice_in_dim(g_wdu_full, chip * n_wd, n_wd, axis=3),
                   b_decay_up=lax.dynamic_slice_in_dim(g_bdu_full, chip * n_bd, n_bd, axis=2))
    small = [nm for nm in names if nm not in big and nm != 'w_ada']
    res = _adam("adam_small", _pack([weights[nm] for nm in small]), [_pack([small_g[nm] for nm in small])],
                _pack([mom1[nm] for nm in small]), _pack([mom2[nm] for nm in small]))
    small_shapes = [weights[nm].shape for nm in small]
    for dst, packed in zip((out_g, out_d, out_m, out_v), res):
        for nm, val in zip(small, _unpack(packed, small_shapes)):
            dst[nm] = val

    return (loss, grad_x, *[out_g[nm] for nm in names], *[out_d[nm] for nm in names],
            *[out_m[nm] for nm in names], *[out_v[nm] for nm in names])


def swiglu(name, u, nrow, nh, tm, tn):
    def kern(u_ref, o_ref):
        o_ref[...] = (_silu(u_ref[:, :tn]) * u_ref[:, tn:]).astype(o_ref.dtype)

    return pl.pallas_call(
        kern, name=name, grid=(nh, nrow), in_specs=[pl.BlockSpec((tm, 2 * tn), lambda j, i: (i, j))],
        out_specs=pl.BlockSpec((tm, tn), lambda j, i: (i, j)),
        out_shape=jax.ShapeDtypeStruct((u.shape[0], nh * tn), MXU_DTYPE), compiler_params=_params(),
    )(u)


def swiglu_bwd(name, d_s, u, nrow, nh, tm, tn):
    def kern(ds_ref, u_ref, o_ref):
        gate, up, d_s_ = u_ref[:, :tn], u_ref[:, tn:], ds_ref[...]
        o_ref[:, :tn] = (d_s_ * up * _dsilu(gate)).astype(o_ref.dtype)
        o_ref[:, tn:] = (d_s_ * _silu(gate)).astype(o_ref.dtype)

    return pl.pallas_call(
        kern, name=name, grid=(nh, nrow),
        in_specs=[pl.BlockSpec((tm, tn), lambda j, i: (i, j)), pl.BlockSpec((tm, 2 * tn), lambda j, i: (i, j))],
        out_specs=pl.BlockSpec((tm, 2 * tn), lambda j, i: (i, j)),
        out_shape=jax.ShapeDtypeStruct((u.shape[0], 2 * nh * tn), MXU_DTYPE), compiler_params=_params(),
    )(d_s, u)
```

```python
import functools

import jax
import jax.numpy as jnp
import numpy as np
from jax import lax
from jax.experimental import pallas as pl
from jax.experimental.pallas import tpu as pltpu

F32 = jnp.float32
MXU_DTYPE = jnp.bfloat16
WIRE_DTYPE = jnp.bfloat16

GRID_W = 64
CHUNK = 64
N_HEADS = 4
GATE_RANK = 16
GATE_NORM = 16.0
N_MOD = 6
N_POOL = 4
LN_EPS = 1e-5
RMS_EPS = 1e-6
ADAM_LR = 0.001
ADAM_B1 = 0.9
ADAM_B2 = 0.999
ADAM_EPS = 1e-08
ADAM_WD = 0.01
ADAM_STEP = 10

LANES = 128
SUBLANES = 8
ALR_W = 256
PACK_W = 2048
VMEM_LIMIT = 56 * 1024 * 1024
N_CHIPS = 4
N_DEV = 8
MESH = pl.DeviceIdType.MESH

NN = ((1,), (0,))
NT = ((1,), (1,))
TN = ((0,), (0,))


def _dot(a, b, dims):
    return lax.dot_general(a.astype(MXU_DTYPE), b.astype(MXU_DTYPE), (dims, ((), ())),
                           preferred_element_type=F32)


def _dot_f32(a, b, dims):
    return lax.dot_general(a.astype(F32), b.astype(F32), (dims, ((), ())),
                           precision=lax.Precision.HIGHEST, preferred_element_type=F32)


def _dot_mask(mask, x, dims):
    m = mask.astype(MXU_DTYPE)
    if MXU_DTYPE == F32:
        return lax.dot_general(m, x, (dims, ((), ())), preferred_element_type=F32)
    acc = None
    rest = x
    for _ in range(3):
        piece = rest.astype(MXU_DTYPE)
        rest = rest - piece.astype(F32)
        part = lax.dot_general(m, piece, (dims, ((), ())), preferred_element_type=F32)
        acc = part if acc is None else acc + part
    return acc


def _dot_3x(a, b, dims):
    if MXU_DTYPE == F32:
        return lax.dot_general(a, b, (dims, ((), ())), preferred_element_type=F32)
    a_hi, b_hi = a.astype(MXU_DTYPE), b.astype(MXU_DTYPE)
    a_lo = (a - a_hi.astype(F32)).astype(MXU_DTYPE)
    b_lo = (b - b_hi.astype(F32)).astype(MXU_DTYPE)
    dot = lambda u, w: lax.dot_general(u, w, (dims, ((), ())), preferred_element_type=F32)
    return dot(a_hi, b_hi) + (dot(a_lo, b_hi) + dot(a_hi, b_lo))


def _pick(n, cands):
    for c in cands:
        if n % c == 0:
            return c
    return n


def _params():
    return pltpu.CompilerParams(vmem_limit_bytes=VMEM_LIMIT)


def _sigmoid(x):
    return 0.5 + 0.5 * jnp.tanh(0.5 * x)


def _silu(x):
    return x * _sigmoid(x)


def _dsilu(x):
    s = _sigmoid(x)
    return s * (1.0 + x * (1.0 - s))


def matmul(name, a, b, form, m, n, k, *, tm, tn, out_dtype=F32, a_off=(0, 0), b_off=(0, 0), bias=None, bias_off=0,
           precise=False, comm=None, second=None):
    assert m % tm == 0 and n % tn == 0, (name, m, n, tm, tn)
    if form == 'tn':
        a_spec = pl.BlockSpec((k, tm), lambda j, i: (a_off[0], i + a_off[1]))
    else:
        a_spec = pl.BlockSpec((tm, k), lambda j, i: (i + a_off[0], a_off[1]))
    if form == 'nt':
        b_spec = pl.BlockSpec((tn, k), lambda j, i: (j + b_off[0], b_off[1]))
    else:
        b_spec = pl.BlockSpec((k, tn), lambda j, i: (b_off[0], j + b_off[1]))
    dims = {'nn': NN, 'nt': NT, 'tn': TN}[form]
    in_specs = [a_spec, b_spec]
    args = [a, b]
    if bias is not None:
        in_specs.append(pl.BlockSpec((SUBLANES, tn), lambda j, i: (0, j + bias_off)))
        args.append(bias)

    if second is not None:
        a2, b2_off = second
        assert form == 'nt'
        in_specs += [pl.BlockSpec((tm, k), lambda j, i: (i, 0)), pl.BlockSpec((tn, k), lambda j, i: (j + b2_off[0], b2_off[1]))]
        args += [a2, b]
    n_own = len(args)
    nj, ni = n // tn, m // tm
    out_spec = pl.BlockSpec((tm, tn), lambda j, i: (i, j))
    out_shape = jax.ShapeDtypeStruct((m, n), out_dtype)

    def product(refs):
        acc = (_dot_f32 if precise else _dot)(refs[0][...], refs[1][...], dims)
        if bias is not None:
            acc = acc + refs[2][0:1, :]
        if second is not None:
            acc = acc + _dot(refs[n_own - 2][...], refs[n_own - 1][...], dims)
        return acc

    if comm is None:
        def body(*refs):
            refs[-1][...] = product(refs).astype(refs[-1].dtype)

        return pl.pallas_call(body, name=name, grid=(nj, ni), in_specs=in_specs, out_specs=out_spec,
                              out_shape=out_shape, compiler_params=_params())(*args)

    n_ci, n_co = len(comm.ins), len(comm.outs)

    def hosted(*refs):
        c_in = refs[n_own:n_own + n_ci]
        o_ref = refs[n_own + n_ci]
        c_out = refs[n_own + n_ci + 1:n_own + n_ci + 1 + n_co]
        sems = refs[n_own + n_ci + 1 + n_co:]
        j, i = pl.program_id(0), pl.program_id(1)

        @pl.when((j == 0) & (i == 0))
        def _():
            comm.start(c_in, c_out, sems)

        o_ref[...] = product(refs).astype(o_ref.dtype)

        @pl.when((j == nj - 1) & (i == ni - 1))
        def _():
            comm.finish(c_in, c_out, sems)

    any_spec = pl.BlockSpec(memory_space=pl.ANY)
    res = pl.pallas_call(
        hosted, name=name, grid=(nj, ni), in_specs=in_specs + [any_spec] * n_ci,
        out_specs=[out_spec] + [any_spec] * n_co, out_shape=[out_shape] + list(comm.outs),
        scratch_shapes=list(comm.sems), compiler_params=_params(),
    )(*args, *comm.ins)
    return res[0], list(res[1:])


def ew(name, body, nrow, ncol, tm, tn, row_ins, vec_ins, row_outs, sum_outs=(), ctx_tiles=1, pass_i=False):
    def rmap(roff):
        return roff if callable(roff) else (lambda i: i + roff)

    in_specs = []
    for arr, roff, coff in row_ins:
        in_specs.append(pl.BlockSpec((tm, tn), functools.partial(lambda j, i, r, c: (r(i), j + c), r=rmap(roff), c=coff)))
    for arr, coff in vec_ins:
        in_specs.append(pl.BlockSpec((SUBLANES, tn), functools.partial(lambda j, i, c: (0, j + c), c=coff)))
    out_specs = [pl.BlockSpec((tm, tn), lambda j, i: (i, j)) for _ in row_outs]
    out_specs += [pl.BlockSpec((SUBLANES, tn), lambda j, i: (0, j)) for _ in sum_outs]
    out_shape = [jax.ShapeDtypeStruct((r, c), dt) for dt, r, c in row_outs]
    out_shape += [jax.ShapeDtypeStruct((SUBLANES, c), F32) for c, _ in sum_outs]
    n_row, n_vec, n_ro = len(row_ins), len(vec_ins), len(row_outs)

    def kern(*refs):
        j, i = pl.program_id(0), pl.program_id(1)
        rows = [r[...] for r in refs[:n_row]]
        vecs = [r[...] for r in refs[n_row:n_row + n_vec]]
        outs = refs[n_row + n_vec:]
        is_ctx = i < ctx_tiles
        res, sums = body(i if pass_i else is_ctx, j, rows, vecs)
        for ref, val in zip(outs[:n_ro], res):
            ref[...] = val.astype(ref.dtype)
        for ref, val, (_, by_class) in zip(outs[n_ro:], sums, sum_outs):
            @pl.when(i == 0)
            def _():
                ref[...] = jnp.zeros_like(ref)
            if by_class:
                ref[0:1, :] += jnp.where(is_ctx, val, 0.0)
                ref[1:2, :] += jnp.where(is_ctx, 0.0, val)
            else:
                ref[0:1, :] += val

    outs = pl.pallas_call(
        kern, name=name, grid=(ncol, nrow), in_specs=in_specs, out_specs=out_specs, out_shape=out_shape,
        compiler_params=_params(),
    )(*[a for a, _, _ in row_ins], *[a for a, _ in vec_ins])
    return list(outs)


def _cls(vec, is_ctx):
    return jnp.where(is_ctx, vec[0:1, :], vec[1:2, :])


def _colsum(x):
    return jnp.sum(x, axis=0, keepdims=True)


def _chunk_map(cfg, rev):
    nctx, nc = cfg.CL // CHUNK, cfg.R // CHUNK
    if not rev:
        return lambda s: s
    return lambda s: jnp.where(s < nctx, nctx - 1 - s, nctx + nc - 1 - s)


def _gla_chunk(q_ref, k_ref, a_ref, wd_ref, bd_ref, rev, scale):
    q = q_ref[...] * scale
    k = k_ref[...]
    z = _dot_3x(a_ref[...], wd_ref[...], NN) + bd_ref[0:1, :]
    la = (jnp.minimum(z, 0.0) - jnp.log(1.0 + jnp.exp(-jnp.abs(z)))) * (1.0 / GATE_NORM)
    r = lax.broadcasted_iota(jnp.int32, (CHUNK, CHUNK), 0)
    c = lax.broadcasted_iota(jnp.int32, (CHUNK, CHUNK), 1)
    keep = (r <= c) if rev else (r >= c)
    tri = keep.astype(F32)
    cum = _dot_mask(tri, la, NN)
    mid = CHUNK // 2 if rev else CHUNK // 2 - 1
    end = 0 if rev else CHUNK - 1
    ref = cum[mid:mid + 1, :]
    last = cum[end:end + 1, :]
    return dict(q=q, k=k, z=z, keep=keep, tri=tri, q_in=q * jnp.exp(cum - ref), k_in=k * jnp.exp(ref - cum),
                e_q=jnp.exp(cum), e_k=jnp.exp(last - cum), e_inq=jnp.exp(cum - ref), e_ink=jnp.exp(ref - cum),
                e_last=jnp.exp(last))


def _gla_in_specs(cfg, rows_of):
    dk, dv = cfg.DK, cfg.DV
    return [
        pl.BlockSpec((CHUNK, dk), lambda s: (rows_of(s), 0)),
        pl.BlockSpec((CHUNK, dk), lambda s: (rows_of(s), 1)),
        pl.BlockSpec((CHUNK, dv), lambda s: (rows_of(s), 2 * dk // dv)),
        pl.BlockSpec((CHUNK, ALR_W), lambda s: (rows_of(s), (cfg.NP - ALR_W) // ALR_W)),
        pl.BlockSpec((ALR_W, dk), lambda s: (0, 0)),
        pl.BlockSpec((SUBLANES, dk), lambda s: (0, 0)),
    ]


def gla_fwd(name, proj, wd, bd, cfg, rev, comm=None):
    hk, hv = cfg.HK, cfg.HV
    nc = cfg.R // CHUNK
    cmap = _chunk_map(cfg, rev)
    scale = hk ** -0.5
    n_ci, n_co = (len(comm.ins), len(comm.outs)) if comm else (0, 0)

    def body(*refs):
        q_ref, k_ref, v_ref, a_ref, wd_ref, bd_ref = refs[:6]
        c_in = refs[6:6 + n_ci]
        o_ref, ss_ref = refs[6 + n_ci:8 + n_ci]
        c_out = refs[8 + n_ci:8 + n_ci + n_co]
        st_scr = refs[8 + n_ci + n_co]
        sems = refs[9 + n_ci + n_co:]
        s = pl.program_id(0)

        @pl.when(s == 0)
        def _():
            st_scr[...] = jnp.zeros_like(st_scr)
            if comm:
                comm.start(c_in, c_out, sems)

        t = _gla_chunk(q_ref, k_ref, a_ref, wd_ref, bd_ref, rev, scale)
        q_int, k_st = t['q'] * t['e_q'], t['k'] * t['e_k']
        for h in range(N_HEADS):
            ks, vs = slice(h * hk, (h + 1) * hk), slice(h * hv, (h + 1) * hv)
            v = v_ref[:, vs]
            st = st_scr[h]
            ss_ref[0, h] = st
            a = jnp.where(t['keep'], _dot(t['q_in'][:, ks], t['k_in'][:, ks], NT), 0.0)
            o_ref[:, vs] = _dot(a, v, NN) + _dot(q_int[:, ks], st, NT)
            st_scr[h] = st * t['e_last'][:, ks] + _dot(v, k_st[:, ks], TN)

        if comm:
            @pl.when(s == nc - 1)
            def _():
                comm.finish(c_in, c_out, sems)

    any_spec = pl.BlockSpec(memory_space=pl.ANY)
    res = pl.pallas_call(
        body, name=name, grid=(nc,), in_specs=_gla_in_specs(cfg, cmap) + [any_spec] * n_ci,
        out_specs=[pl.BlockSpec((CHUNK, cfg.DV), lambda s: (cmap(s), 0)),
                   pl.BlockSpec((1, N_HEADS, hv, hk), lambda s: (s, 0, 0, 0))] + [any_spec] * n_co,
        out_shape=[jax.ShapeDtypeStruct((cfg.R, cfg.DV), F32),
                   jax.ShapeDtypeStruct((nc, N_HEADS, hv, hk), F32)] + (list(comm.outs) if comm else []),
        scratch_shapes=[pltpu.VMEM((N_HEADS, hv, hk), F32)] + (list(comm.sems) if comm else []),
        compiler_params=_params(),
    )(proj, proj, proj, proj, wd, bd, *(comm.ins if comm else []))
    return res[0], res[1], list(res[2:])


def gla_bwd(name, proj, wd, bd, states, d_o, cfg, rev, addends=None):
    hk, hv = cfg.HK, cfg.HV
    nc = cfg.R // CHUNK
    cmap = _chunk_map(cfg, rev)
    rows_of = lambda g: cmap(nc - 1 - g)
    scale = hk ** -0.5
    n_add = 0 if addends is None else 4

    def body(*refs):
        q_ref, k_ref, v_ref, a_ref, wd_ref, bd_ref, ss_ref, do_ref = refs[:8]
        adds = refs[8:8 + n_add]
        dq_ref, dk_ref, dv_ref, da_ref, dwd_ref, dbd_ref, dst_scr = refs[8 + n_add:]
        g = pl.program_id(0)

        @pl.when(g == 0)
        def _():
            dst_scr[...] = jnp.zeros_like(dst_scr)
            dwd_ref[...] = jnp.zeros_like(dwd_ref)
            dbd_ref[...] = jnp.zeros_like(dbd_ref)

        t = _gla_chunk(q_ref, k_ref, a_ref, wd_ref, bd_ref, rev, scale)
        q_int, k_st = t['q'] * t['e_q'], t['k'] * t['e_k']
        dq_h, dk_h, carry_h = [], [], []
        for h in range(N_HEADS):
            ks, vs = slice(h * hk, (h + 1) * hk), slice(h * hv, (h + 1) * hv)
            v = v_ref[:, vs]
            d_out = do_ref[:, vs]
            st = ss_ref[0, h]
            dst = dst_scr[h]
            a = jnp.where(t['keep'], _dot(t['q_in'][:, ks], t['k_in'][:, ks], NT), 0.0)
            da = jnp.where(t['keep'], _dot(d_out, v, NT), 0.0)
            dv = _dot(a, d_out, TN) + _dot(k_st[:, ks], dst, NT)
            dq_h.append(_dot(d_out, st, NN) * t['e_q'][:, ks] + _dot(da, t['k_in'][:, ks], NN) * t['e_inq'][:, ks])
            dk_h.append(_dot(v, dst, NN) * t['e_k'][:, ks] + _dot(da, t['q_in'][:, ks], TN) * t['e_ink'][:, ks])
            dst_scr[h] = dst * t['e_last'][:, ks] + _dot(d_out, q_int[:, ks], TN)
            st_end = st * t['e_last'][:, ks] + _dot(v, k_st[:, ks], TN)
            carry_h.append(_colsum(dst * st_end))
            dv_ref[:, vs] = dv + adds[2][:, vs] if n_add else dv
        dq = jnp.concatenate(dq_h, axis=-1)
        dk = jnp.concatenate(dk_h, axis=-1)
        dg = t['q'] * dq - t['k'] * dk
        dla = _dot_mask(t['tri'], dg, TN) + jnp.concatenate(carry_h, axis=-1)
        dz = dla * (1.0 / GATE_NORM) * _sigmoid(-t['z'])
        dalr = _dot_3x(dz, wd_ref[...], NT)
        dwd_ref[...] += _dot_3x(a_ref[...], dz, TN)
        dbd_ref[...] += jnp.broadcast_to(_colsum(dz), dbd_ref.shape)
        dq = dq * scale
        if n_add:
            dq, dk, dalr = dq + adds[0][...], dk + adds[1][...], dalr + adds[3][...]
        dq_ref[...] = dq
        dk_ref[...] = dk
        da_ref[...] = dalr

    qk_spec = pl.BlockSpec((CHUNK, cfg.DK), lambda g: (rows_of(g), 0))
    v_spec = pl.BlockSpec((CHUNK, cfg.DV), lambda g: (rows_of(g), 0))
    a_spec = pl.BlockSpec((CHUNK, ALR_W), lambda g: (rows_of(g), 0))
    in_specs = _gla_in_specs(cfg, rows_of) + [
        pl.BlockSpec((1, N_HEADS, hv, hk), lambda g: (nc - 1 - g, 0, 0, 0)), v_spec]
    args = [proj, proj, proj, proj, wd, bd, states, d_o]
    if n_add:
        in_specs += [qk_spec, qk_spec, v_spec, a_spec]
        args += list(addends)
    return pl.pallas_call(
        body, name=name, grid=(nc,), in_specs=in_specs,
        out_specs=[qk_spec, qk_spec, v_spec, a_spec,
                   pl.BlockSpec((ALR_W, cfg.DK), lambda g: (0, 0)),
                   pl.BlockSpec((SUBLANES, cfg.DK), lambda g: (0, 0))],
        out_shape=[jax.ShapeDtypeStruct((cfg.R, cfg.DK), F32), jax.ShapeDtypeStruct((cfg.R, cfg.DK), F32),
                   jax.ShapeDtypeStruct((cfg.R, cfg.DV), F32), jax.ShapeDtypeStruct((cfg.R, ALR_W), F32),
                   jax.ShapeDtypeStruct((ALR_W, cfg.DK), F32),
                   jax.ShapeDtypeStruct((SUBLANES, cfg.DK), F32)],
        scratch_shapes=[pltpu.VMEM((N_HEADS, hv, hk), F32)],
        compiler_params=_params(),
    )(*args)


def pool_mix(name, src, coff, cfg, transpose):
    tm, pg = cfg.TM, cfg.PG
    mt = cfg.R // tm
    reach = -(-(max(2 ** N_POOL // 2, 1) * GRID_W) // tm)
    nk = 2 * reach + 1
    img_rows = cfg.S // GRID_W
    shift = GRID_W.bit_length() - 1

    def ktile(m, d):
        return jnp.where(m == 0, 0, jnp.clip(m + d - reach, 1, mt - 1))

    def counts(idx, is_ctx, lo, hi):
        ctx_n = jnp.minimum(idx + hi + 1, cfg.CL) - jnp.maximum(idx - lo, 0)
        r, c = idx >> shift, idx & (GRID_W - 1)
        lat_n = ((jnp.minimum(r + hi + 1, img_rows) - jnp.maximum(r - lo, 0))
                 * (jnp.minimum(c + hi + 1, GRID_W) - jnp.maximum(c - lo, 0)))
        return jnp.where(is_ctx, ctx_n, lat_n).astype(F32)

    a = np.arange(tm)[:, None]
    b = np.arange(tm)[None, :]
    masks = np.zeros((N_POOL, nk + 1, tm, tm), np.float32)
    for g_ in range(N_POOL):
        lo_ = 2 ** g_
        hi_ = lo_ - 1
        for d_ in range(nk):
            dr = (d_ - reach) * (tm // GRID_W) + (b >> shift) - (a >> shift)
            dc = (b & (GRID_W - 1)) - (a & (GRID_W - 1))
            masks[g_, d_] = (dr >= -lo_) & (dr <= hi_) & (dc >= -lo_) & (dc <= hi_)
        masks[g_, nk] = (b - a >= -lo_) & (b - a <= hi_)
    masks = jnp.asarray(masks, MXU_DTYPE)

    def body(src_ref, self_ref, mask_ref, o_ref, acc):
        m, d = pl.program_id(0), pl.program_id(1)
        is_ctx = m == 0
        kt = m + d - reach
        valid = jnp.where(is_ctx, d == reach, (kt >= 1) & (kt <= mt - 1))
        seg = jnp.where(is_ctx, 0, cfg.CL)

        @pl.when(d == 0)
        def _():
            acc[...] = jnp.zeros_like(acc)

        @pl.when(valid)
        def _():
            for g in range(N_POOL):
                cols = slice(g * pg, (g + 1) * pg)
                x = src_ref[:, cols]
                if transpose:
                    kidx = lax.broadcasted_iota(jnp.int32, (tm, 1), 0) + (kt * tm - seg)
                    acc[:, cols] += _dot_mask(mask_ref[g, 0], x / counts(kidx, is_ctx, 2 ** g, 2 ** g - 1), TN)
                else:
                    acc[:, cols] += _dot_mask(mask_ref[g, 0], x, NN)

        @pl.when(d == nk - 1)
        def _():
            for g in range(N_POOL):
                cols = slice(g * pg, (g + 1) * pg)
                res = acc[:, cols]
                if not transpose:
                    midx = lax.broadcasted_iota(jnp.int32, (tm, 1), 0) + (m * tm - seg)
                    res = res / counts(midx, is_ctx, 2 ** g, 2 ** g - 1)
                o_ref[:, cols] = (res - self_ref[:, cols]).astype(o_ref.dtype)

    which = (lambda d: 2 * reach - d) if transpose else (lambda d: d)
    assert coff % N_POOL == 0
    dp = N_POOL * pg
    return pl.pallas_call(
        body, name=name, grid=(mt, nk),
        in_specs=[pl.BlockSpec((tm, dp), lambda m, d: (ktile(m, d), coff // N_POOL)),
                  pl.BlockSpec((tm, dp), lambda m, d: (m, coff // N_POOL)),
                  pl.BlockSpec((N_POOL, 1, tm, tm), lambda m, d: (0, jnp.where(m == 0, nk, which(d)), 0, 0))],
        out_specs=pl.BlockSpec((tm, dp), lambda m, d: (m, 0)),
        out_shape=jax.ShapeDtypeStruct((cfg.R, cfg.DP), F32),
        scratch_shapes=[pltpu.VMEM((tm, dp), F32)], compiler_params=_params(),
    )(src, src, masks)


def _my_place():
    return lax.axis_index("x"), lax.axis_index("y"), lax.axis_index("c")


def _flip(v, bit):
    return 1 - v if bit else v


def all_gather8(name, block):
    rows, w = block.shape

    def body(x_ref, out_ref, send_sems, recv_sems, local_sem):
        x, y, c = _my_place()
        me = 4 * x + 2 * y + c
        mine = pltpu.make_async_copy(x_ref, out_ref.at[me], local_sem)
        mine.start()
        sends = []
        for k in range(1, N_DEV):
            peer = (_flip(x, k & 4), _flip(y, k & 2), _flip(c, k & 1))
            cp = pltpu.make_async_remote_copy(src_ref=x_ref, dst_ref=out_ref.at[me], send_sem=send_sems.at[k - 1],
                                              recv_sem=recv_sems.at[k - 1], device_id=peer, device_id_type=MESH)
            cp.start()
            sends.append(cp)
        for k in range(1, N_DEV):
            peer = (_flip(x, k & 4), _flip(y, k & 2), _flip(c, k & 1))
            slot = 4 * peer[0] + 2 * peer[1] + peer[2]
            pltpu.make_async_remote_copy(src_ref=x_ref, dst_ref=out_ref.at[slot], send_sem=send_sems.at[k - 1],
                                         recv_sem=recv_sems.at[k - 1], device_id=peer, device_id_type=MESH).wait_recv()
        for cp in sends:
            cp.wait_send()
        mine.wait()

    return pl.pallas_call(
        body, name=name, out_shape=jax.ShapeDtypeStruct((N_DEV, rows, w), block.dtype),
        in_specs=[pl.BlockSpec(memory_space=pl.ANY)], out_specs=pl.BlockSpec(memory_space=pl.ANY),
        scratch_shapes=[pltpu.SemaphoreType.DMA((N_DEV - 1,)), pltpu.SemaphoreType.DMA((N_DEV - 1,)),
                        pltpu.SemaphoreType.DMA],
    )(block)


class _Comm:
    def __init__(self, ins, outs, sems, start, finish):
        self.ins, self.outs, self.sems, self.start, self.finish = list(ins), list(outs), list(sems), start, finish


def run_comm(name, comm):
    def body(*refs):
        n_i, n_o = len(comm.ins), len(comm.outs)
        comm.start(refs[:n_i], refs[n_i:n_i + n_o], refs[n_i + n_o:])
        comm.finish(refs[:n_i], refs[n_i:n_i + n_o], refs[n_i + n_o:])

    any_spec = pl.BlockSpec(memory_space=pl.ANY)
    return list(pl.pallas_call(body, name=name, out_shape=comm.outs, in_specs=[any_spec] * len(comm.ins),
                               out_specs=[any_spec] * len(comm.outs), scratch_shapes=comm.sems)(*comm.ins))


def gather_plan(shards, n_split):
    n = len(shards)

    def part(ref, t, core):
        if t >= n_split:
            return ref
        h = shards[t].shape[0] // 2
        return ref.at[pl.ds(core * h, h)]

    def ici(srcs, dsts, sems, k, t, slot, place):
        x, y, c = place
        return pltpu.make_async_remote_copy(
            src_ref=part(srcs[t], t, c), dst_ref=part(dsts[t].at[slot], t, c), send_sem=sems[0].at[t * 3 + k - 1],
            recv_sem=sems[1].at[t * 3 + k - 1], device_id=(_flip(x, k & 2), _flip(y, k & 1), c), device_id_type=MESH)

    def handed(dsts, sems, k, t, slot, place, core):
        x, y, c = place
        half = part(dsts[t].at[slot], t, core)
        return pltpu.make_async_remote_copy(
            src_ref=half, dst_ref=half, send_sem=sems[2].at[t * 3 + k - 1], recv_sem=sems[3].at[t * 3 + k - 1],
            device_id=(x, y, 1 - c), device_id_type=MESH)

    def start(srcs, dsts, sems):
        place = _my_place()
        me = 2 * place[0] + place[1]
        for t in range(n):
            pltpu.make_async_copy(srcs[t], dsts[t].at[me], sems[4].at[t]).start()
        for k in range(1, N_CHIPS):
            for t in range(n):
                ici(srcs, dsts, sems, k, t, me, place).start()

    def finish(srcs, dsts, sems):
        place = _my_place()
        x, y, c = place
        me = 2 * x + y
        for k in range(1, N_CHIPS):
            slot = 2 * _flip(x, k & 2) + _flip(y, k & 1)
            for t in range(n):
                ici(srcs, dsts, sems, k, t, slot, place).wait_recv()
                if t < n_split:
                    handed(dsts, sems, k, t, slot, place, c).start()
        for k in range(1, N_CHIPS):
            slot = 2 * _flip(x, k & 2) + _flip(y, k & 1)
            for t in range(n_split):
                handed(dsts, sems, k, t, slot, place, 1 - c).wait_recv()
        for k in range(1, N_CHIPS):
            slot = 2 * _flip(x, k & 2) + _flip(y, k & 1)
            for t in range(n):
                ici(srcs, dsts, sems, k, t, me, place).wait_send()
                if t < n_split:
                    handed(dsts, sems, k, t, slot, place, c).wait_send()
        for t in range(n):
            pltpu.make_async_copy(srcs[t], dsts[t].at[me], sems[4].at[t]).wait()

    outs = [jax.ShapeDtypeStruct((N_CHIPS,) + s.shape, s.dtype) for s in shards]
    sems = [pltpu.SemaphoreType.DMA((3 * n,))] * 4 + [pltpu.SemaphoreType.DMA((n,))]
    return _Comm(shards, outs, sems, start, finish)


def scatter_plan(bufs):
    n = len(bufs)

    def copies(srcs, dsts, sems):
        x, y, c = _my_place()
        return [pltpu.make_async_remote_copy(
            src_ref=srcs[t].at[k - 1], dst_ref=dsts[t].at[k - 1], send_sem=sems[0].at[t * 3 + k - 1],
            recv_sem=sems[1].at[t * 3 + k - 1], device_id=(_flip(x, k & 2), _flip(y, k & 1), c), device_id_type=MESH)
            for k in range(1, N_CHIPS) for t in range(n)]

    def start(srcs, dsts, sems):
        for cp in copies(srcs, dsts, sems):
            cp.start()

    def finish(srcs, dsts, sems):
        for cp in copies(srcs, dsts, sems):
            cp.wait_recv()
        for cp in copies(srcs, dsts, sems):
            cp.wait_send()

    return _Comm(bufs, [jax.ShapeDtypeStruct(b.shape, b.dtype) for b in bufs],
                 [pltpu.SemaphoreType.DMA((3 * n,)), pltpu.SemaphoreType.DMA((3 * n,))], start, finish)


def swap_plan(bufs):
    n = len(bufs)

    def copies(srcs, dsts, sems):
        x, y, c = _my_place()
        return [pltpu.make_async_remote_copy(src_ref=srcs[t], dst_ref=dsts[t], send_sem=sems[0].at[t],
                                             recv_sem=sems[1].at[t], device_id=(x, y, 1 - c), device_id_type=MESH)
                for t in range(n)]

    def start(srcs, dsts, sems):
        for cp in copies(srcs, dsts, sems):
            cp.start()

    def finish(srcs, dsts, sems):
        for cp in copies(srcs, dsts, sems):
            cp.wait_recv()
        for cp in copies(srcs, dsts, sems):
            cp.wait_send()

    return _Comm(bufs, [jax.ShapeDtypeStruct(b.shape, b.dtype) for b in bufs],
                 [pltpu.SemaphoreType.DMA((n,)), pltpu.SemaphoreType.DMA((n,))], start, finish)


def _vec8(*rows):
    w = rows[0].shape[-1]
    out = jnp.zeros((SUBLANES, w), F32)
    for r, v in enumerate(rows):
        out = out.at[r].set(v.reshape(w).astype(F32))
    return out


def _pack(arrays):
    parts = []
    for a in arrays:
        flat = a.reshape(-1).astype(F32)
        pad = (-flat.shape[0]) % PACK_W
        parts.append(jnp.pad(flat, (0, pad)))
    flat = jnp.concatenate(parts)
    pad = (-flat.shape[0]) % (PACK_W * SUBLANES)
    return jnp.pad(flat, (0, pad)).reshape(-1, PACK_W)


def _unpack(packed, shapes):
    flat = packed.reshape(-1)
    out, pos = [], 0
    for shp in shapes:
        size = 1
        for d in shp:
            size *= d
        out.append(flat[pos:pos + size].reshape(shp))
        pos += size + (-size) % PACK_W
    return out


def _rows2d(a):
    return a.reshape(-1, a.shape[-1])


class _Cfg:
    pass


def _adam(name, w, grads, m, v, half_rows=None, core=None):
    rows, width = w.shape
    c1 = 1.0 - ADAM_B1 ** ADAM_STEP
    c2 = 1.0 - ADAM_B2 ** ADAM_STEP

    def update(wv, mv, vv, g):
        m_new = ADAM_B1 * mv + (1.0 - ADAM_B1) * g
        v_new = ADAM_B2 * vv + (1.0 - ADAM_B2) * (g * g)
        delta = -ADAM_LR * ((m_new / c1) / (jnp.sqrt(v_new / c2) + ADAM_EPS) + ADAM_WD * wv)
        return [g, delta, m_new, v_new], []

    if half_rows is None:
        tm = _pick(rows, (128, 64, 32, 16, 8))

        def body(i, j, rows_, vecs):
            g = rows_[3]
            for extra in rows_[4:]:
                g = g + extra
            return update(rows_[0], rows_[1], rows_[2], g)

        return ew(name, body, rows // tm, 1, tm, width, [(w, 0, 0), (m, 0, 0), (v, 0, 0)] + [(g, 0, 0) for g in grads], [],
                  [(F32, rows, width)] * 4)

    tm = _pick(half_rows, (128, 64, 32, 16, 8))
    nb = half_rows // tm
    run_tile = lambda i: (i // (2 * nb)) * nb + i % nb
    core_vec = jnp.broadcast_to(core.astype(F32), (SUBLANES, width))

    def body(i, j, rows_, vecs):
        owner = ((i // nb) % 2).astype(F32)
        g = jnp.where(vecs[0][0:1, :] == owner, rows_[3], rows_[4])
        return update(rows_[0], rows_[1], rows_[2], g)

    return ew(name, body, rows // tm, 1, tm, width,
              [(w, 0, 0), (m, 0, 0), (v, 0, 0), (grads[0], run_tile, 0), (grads[1], run_tile, 0)], [(core_vec, 0)],
              [(F32, rows, width)] * 4, pass_i=True)


def _sum_rows(name, arr, nparts, rows, width, dtype=F32):
    tm = _pick(rows, (256, 128, 64, 32, 16, 8))
    nblk = rows // tm

    def body(i, j, rows_, vecs):
        acc = rows_[0].astype(F32)
        for r in rows_[1:]:
            acc = acc + r.astype(F32)
        return [acc], []

    return ew(name, body, nblk, 1, tm, width, [(arr, p * nblk, 0) for p in range(nparts)], [], [(dtype, rows, width)])[0]


def kernel(x, c, ctx, c_ctx, w_ada, b_ada, w_in, w_decay_up, b_decay_up, gla_norm_gain, w_pool_group, pool_scale, w_gla_out, w_pool_out, w_out, ln_mix_gain, ln_mix_bias, w_ffn_in, w_ffn_out, ln_ffn_gain, ln_ffn_bias, loss_target, m_c_ctx, m_w_ada, m_b_ada, m_w_in, m_w_decay_up, m_b_decay_up, m_gla_norm_gain, m_w_pool_group, m_pool_scale, m_w_gla_out, m_w_pool_out, m_w_out, m_ln_mix_gain, m_ln_mix_bias, m_w_ffn_in, m_w_ffn_out, m_ln_ffn_gain, m_ln_ffn_bias, v_c_ctx, v_w_ada, v_b_ada, v_w_in, v_w_decay_up, v_b_decay_up, v_gla_norm_gain, v_w_pool_group, v_pool_scale, v_w_gla_out, v_w_pool_out, v_w_out, v_ln_mix_gain, v_ln_mix_bias, v_w_ffn_in, v_w_ffn_out, v_ln_ffn_gain, v_ln_ffn_bias):
    weights = dict(c_ctx=c_ctx, w_ada=w_ada, b_ada=b_ada, w_in=w_in, w_decay_up=w_decay_up, b_decay_up=b_decay_up,
                   gla_norm_gain=gla_norm_gain, w_pool_group=w_pool_group, pool_scale=pool_scale, w_gla_out=w_gla_out,
                   w_pool_out=w_pool_out, w_out=w_out, ln_mix_gain=ln_mix_gain, ln_mix_bias=ln_mix_bias,
                   w_ffn_in=w_ffn_in, w_ffn_out=w_ffn_out, ln_ffn_gain=ln_ffn_gain, ln_ffn_bias=ln_ffn_bias)
    mom1 = dict(c_ctx=m_c_ctx, w_ada=m_w_ada, b_ada=m_b_ada, w_in=m_w_in, w_decay_up=m_w_decay_up, b_decay_up=m_b_decay_up,
                gla_norm_gain=m_gla_norm_gain, w_pool_group=m_w_pool_group, pool_scale=m_pool_scale, w_gla_out=m_w_gla_out,
                w_pool_out=m_w_pool_out, w_out=m_w_out, ln_mix_gain=m_ln_mix_gain, ln_mix_bias=m_ln_mix_bias,
                w_ffn_in=m_w_ffn_in, w_ffn_out=m_w_ffn_out, ln_ffn_gain=m_ln_ffn_gain, ln_ffn_bias=m_ln_ffn_bias)
    mom2 = dict(c_ctx=v_c_ctx, w_ada=v_w_ada, b_ada=v_b_ada, w_in=v_w_in, w_decay_up=v_w_decay_up, b_decay_up=v_b_decay_up,
                gla_norm_gain=v_gla_norm_gain, w_pool_group=v_w_pool_group, pool_scale=v_pool_scale, w_gla_out=v_w_gla_out,
                w_pool_out=v_w_pool_out, w_out=v_w_out, ln_mix_gain=v_ln_mix_gain, ln_mix_bias=v_ln_mix_bias,
                w_ffn_in=v_w_ffn_in, w_ffn_out=v_w_ffn_out, ln_ffn_gain=v_ln_ffn_gain, ln_ffn_bias=v_ln_ffn_bias)
    names = list(weights)

    cfg = _Cfg()
    L, D = w_ada.shape[0], x.shape[-1]
    S, CL = x.shape[1], ctx.shape[1]
    cfg.L, cfg.D, cfg.S, cfg.CL, cfg.R, cfg.TM = L, D, S, CL, S + CL, CL
    DK, DV, DP = D // 2, D, D // 2
    cfg.DK, cfg.DV, cfg.DP = DK, DV, DP
    cfg.HK, cfg.HV, cfg.PG = DK // N_HEADS, DV // N_HEADS, DP // N_POOL
    DFF = w_ffn_out.shape[1] * N_CHIPS
    NP = 2 * DK + 2 * DV + DP + 2 * D + ALR_W
    cfg.NP, cfg.DFF = NP, DFF
    R, TM, HK, HV, PG = cfg.R, cfg.TM, cfg.HK, cfg.HV, cfg.PG
    MT = R // TM
    alpha = (2.0 * L) ** 0.25
    assert S % TM == 0 and TM % CHUNK == 0 and S % GRID_W == 0 and TM % GRID_W == 0
    OFF_G, OFF_P, OFF_BGA, OFF_BGB = 2 * DK + DV, 2 * DK + 2 * DV, 2 * DK + 2 * DV + DP, 2 * DK + 2 * DV + DP + D
    ALR0 = 2 * DK + 2 * DV
    TE = 512
    TL = TM // 2
    assert D % TE == 0 and DP % TE == 0

    xi, yi, ci = _my_place()
    chip = 2 * xi + yi
    dev = 4 * xi + 2 * yi + ci

    n_ada = w_ada.shape[-1]
    c_all = all_gather8("ag_cond", jnp.pad(c.reshape(1, D), ((0, SUBLANES - 1), (0, 0))))[:, 0, :]
    cond = jnp.concatenate([c_all, c_ctx.reshape(1, D), jnp.zeros((16 - N_DEV - 1, D), F32)], axis=0)

    def silu_body(i, j, rows_, vecs):
        return [_silu(rows_[0]), _dsilu(rows_[0])], []

    act, dact = ew("cond_silu", silu_body, 1, 1, 16, D, [(cond, 0, 0)], [], [(F32, 16, D)] * 2)
    w_ada2 = w_ada.reshape(L * D, n_ada)
    b_ada_mine = lax.dynamic_slice_in_dim(b_ada, chip * n_ada, n_ada, axis=1)
    tn_ada = _pick(n_ada, (1024, 512, 256, 128))
    mods = [matmul("ada_fwd", act, w_ada2, 'nn', 16, n_ada, D, tm=16, tn=tn_ada, b_off=(l, 0),
                   bias=_vec8(b_ada_mine[l]), precise=True) for l in range(L)]
    mods_all = all_gather8("ag_mods", jnp.concatenate(mods, axis=0))
    mods_all = mods_all[0::2].reshape(N_CHIPS, L, 16, n_ada).transpose(1, 2, 0, 3).reshape(L, 16, N_MOD * D)
    modv = [_vec8(mods_all[l, N_DEV], lax.dynamic_index_in_dim(mods_all[l], dev, 0, keepdims=False)) for l in range(L)]
    MB = D // TE

    big = ['w_in', 'w_gla_out', 'w_pool_out', 'w_out', 'w_ffn_in', 'w_ffn_out', 'w_pool_group']
    hosts = [['w_in'], ['w_gla_out', 'w_out', 'w_pool_out', 'w_pool_group'], ['w_ffn_out'], ['w_ffn_in']]
    all_hosted = sum(hosts, [])
    layer_shards = lambda nms, l: [weights[nm][l].astype(WIRE_DTYPE) for nm in nms]

    def cols_together(g):
        return jnp.moveaxis(g, 0, -2).reshape(g.shape[1:-1] + (N_CHIPS * g.shape[-1],))

    def rows_together(g):
        return g.reshape((N_CHIPS * g.shape[1], g.shape[2]))

    te_ff = _pick(DFF, (1408, 1024, 512, 256, 128))

    def assemble(gw):
        win_ref = cols_together(gw['w_in'])
        win = jnp.concatenate([win_ref[:, :ALR0], win_ref[:, ALR0 + 2 * GATE_RANK:], win_ref[:, ALR0:ALR0 + 2 * GATE_RANK],
                               jnp.zeros((D, ALR_W - 2 * GATE_RANK), WIRE_DTYPE)], axis=-1)
        wpg = jnp.moveaxis(gw['w_pool_group'], 0, 1).reshape(N_POOL, PG, PG)
        wpg_bd = jnp.zeros((N_POOL, PG, N_POOL, PG), WIRE_DTYPE)
        for g in range(N_POOL):
            wpg_bd = wpg_bd.at[g, :, g, :].set(wpg[g])
        win_bwd = jnp.pad(win_ref, ((0, 0), (0, NP - win_ref.shape[1])))
        return dict(win=win, win_bwd=win_bwd, wgo=rows_together(gw['w_gla_out']), wpo=cols_together(gw['w_pool_out']),
                    wout=rows_together(gw['w_out']), wfi=cols_together(gw['w_ffn_in']), wfo=rows_together(gw['w_ffn_out']),
                    wpg_bd=wpg_bd.reshape(DP, DP))

    first = run_comm("gather_weights", gather_plan(layer_shards(all_hosted, 0) + [w_decay_up, b_decay_up],
                                                   n_split=len(big)))
    W = [assemble(dict(zip(all_hosted, first[:len(big)])))] + [None] * (L - 1)
    next_plan = lambda l, k: gather_plan(layer_shards(hosts[k], l + 1), n_split=len(hosts[k]))
    wdu = cols_together(first[len(big)])
    bdu = cols_together(first[len(big) + 1])
    wd_pad = [[jnp.zeros((ALR_W, DK), F32).at[d * GATE_RANK:(d + 1) * GATE_RANK].set(wdu[l, d]) for d in range(2)]
              for l in range(L)]
    bd_pad = [[_vec8(bdu[l, d]) for d in range(2)] for l in range(L)]

    tn_np = _pick(NP, (1280, 1024, 768, 512, 256, 128))
    tn_d = _pick(D, (1024, 512, 256, 128))
    tn_ff2 = _pick(2 * DFF, (1024, 512, 256, 128))
    tn_ff = _pick(DFF, (2816, 1408, 1024, 512, 256, 128))
    tn_dp = _pick(DP, (1024, 512, 256, 128))
    tw = lambda n_: _pick(n_, (512, 256, 128))

    gain8 = lambda v_: _vec8(v_)

    xs = jnp.concatenate([ctx.reshape(CL, D), x.reshape(S, D)], axis=0)
    saved = []
    for l in range(L):
        sv = {}
        mv = modv[l]

        def mod_body(i, j, rows_, vecs):
            return [rows_[0] * (1.0 + _cls(vecs[1], i)) + _cls(vecs[0], i)], []

        h1 = ew("modulate", mod_body, MT, D // tn_d, TM, tn_d, [(xs, 0, 0)],
                [(mv, 0 * (D // tn_d)), (mv, 1 * (D // tn_d))], [(MXU_DTYPE, R, D)])[0]
        Wl = W[l]
        if l + 1 < L:
            proj, got0 = matmul("proj_and_gather", h1, Wl['win'], 'nn', R, NP, D, tm=TM, tn=tn_np, comm=next_plan(l, 0))
            o_f, st_f, got1 = gla_fwd("gla_fwd_and_gather", proj, wd_pad[l][0], bd_pad[l][0], cfg, False, next_plan(l, 1))
            o_b, st_b, got2 = gla_fwd("gla_rev_and_gather", proj, wd_pad[l][1], bd_pad[l][1], cfg, True, next_plan(l, 2))
        else:
            proj = matmul("proj", h1, Wl['win'], 'nn', R, NP, D, tm=TM, tn=tn_np)
            o_f, st_f, _ = gla_fwd("gla_fwd", proj, wd_pad[l][0], bd_pad[l][0], cfg, False)
            o_b, st_b, _ = gla_fwd("gla_rev", proj, wd_pad[l][1], bd_pad[l][1], cfg, True)

        def post_body(i, j, rows_, vecs):
            o = rows_[0] + rows_[1]
            on = o * lax.rsqrt(jnp.mean(o * o, axis=-1, keepdims=True) + RMS_EPS)
            return [on * vecs[0][0:1, :] * _silu(rows_[2])], []

        a_gla = ew("gla_post", post_body, MT, N_HEADS, TM, HV, [(o_f, 0, 0), (o_b, 0, 0), (proj, 0, OFF_G // HV)],
                   [(gain8(gla_norm_gain[l]), 0)], [(MXU_DTYPE, R, DV)])[0]
        y_gla = matmul("gla_out", a_gla, Wl['wgo'], 'nn', R, D, DV, tm=TM, tn=tn_d)
        u_pool = pool_mix("pool_fwd", proj, OFF_P // PG, cfg, False)
        t_pool = matmul("pool_group", u_pool, Wl['wpg_bd'], 'nn', R, DP, DP, tm=TM, tn=tn_dp)

        def scale_body(i, j, rows_, vecs):
            return [rows_[0] * vecs[0][0:1, :]], []

        ts_pool = ew("pool_scale", scale_body, MT, DP // TE, TM, TE, [(t_pool, 0, 0)], [(gain8(pool_scale[l]), 0)],
                     [(MXU_DTYPE, R, DP)])[0]
        y_pool = matmul("pool_out", ts_pool, Wl['wpo'], 'nn', R, D, DP, tm=TM, tn=tn_d)

        def merge_body(i, j, rows_, vecs):
            return [_sigmoid(rows_[2]) * rows_[0] + _sigmoid(rows_[3]) * rows_[1]], []

        mg = ew("merge", merge_body, MT, D // TE, TM, TE,
                [(y_gla, 0, 0), (y_pool, 0, 0), (proj, 0, OFF_BGA // TE), (proj, 0, OFF_BGB // TE)], [],
                [(MXU_DTYPE, R, D)])[0]
        y_mix = matmul("mix_out", mg, Wl['wout'], 'nn', R, D, D, tm=TM, tn=tn_d)

        def ln_body(i, j, rows_, vecs):
            r = alpha * rows_[0] + _cls(vecs[0], i) * rows_[1]
            mu = jnp.mean(r, axis=-1, keepdims=True)
            rc = r - mu
            var = jnp.mean(rc * rc, axis=-1, keepdims=True)
            return [rc * lax.rsqrt(var + LN_EPS) * vecs[1][0:1, :] + vecs[2][0:1, :]], []

        x1 = ew("resid_ln", ln_body, R // TL, 1, TL, D, [(xs, 0, 0), (y_mix, 0, 0)],
                [(mv, 2), (gain8(ln_mix_gain[l]), 0), (gain8(ln_mix_bias[l]), 0)], [(F32, R, D)], ctx_tiles=CL // TL)[0]
        h2 = ew("modulate", mod_body, MT, D // tn_d, TM, tn_d, [(x1, 0, 0)],
                [(mv, 3 * (D // tn_d)), (mv, 4 * (D // tn_d))], [(MXU_DTYPE, R, D)])[0]
        if l + 1 < L:
            u_ffn, got3 = matmul("ffn_in_and_gather", h2, Wl['wfi'], 'nn', R, 2 * DFF, D, tm=TM, tn=tn_ff2,
                                 comm=next_plan(l, 3))
            W[l + 1] = assemble(dict(zip(all_hosted, got0 + got1 + got2 + got3)))
        else:
            u_ffn = matmul("ffn_in", h2, Wl['wfi'], 'nn', R, 2 * DFF, D, tm=TM, tn=tn_ff2)
        s_ffn = swiglu("swiglu", u_ffn, MT, DFF // te_ff, TM, te_ff)
        f_ffn = matmul("ffn_out", s_ffn, Wl['wfo'], 'nn', R, D, DFF, tm=TM, tn=tn_d)
        x2 = ew("resid_ln", ln_body, R // TL, 1, TL, D, [(x1, 0, 0), (f_ffn, 0, 0)],
                [(mv, 5), (gain8(ln_ffn_gain[l]), 0), (gain8(ln_ffn_bias[l]), 0)], [(F32, R, D)], ctx_tiles=CL // TL)[0]
        sv.update(xs=xs, h1=h1, proj=proj, o_f=o_f, o_b=o_b, st_f=st_f, st_b=st_b, a_gla=a_gla, y_gla=y_gla, u_pool=u_pool,
                  t_pool=t_pool, ts_pool=ts_pool, y_pool=y_pool, mg=mg, y_mix=y_mix, x1=x1, h2=h2, u_ffn=u_ffn, s_ffn=s_ffn,
                  f_ffn=f_ffn)
        saved.append(sv)
        xs = x2

    tgt = loss_target.reshape(S, D)

    def loss_body(i, j, rows_, vecs):
        d = jnp.where(i, 0.0, rows_[0] - rows_[1])
        return [d * (1.0 / D)], [_colsum(d * d)]

    d_x, sq = ew("loss", loss_body, MT, 1, TM, D, [(xs, 0, 0), (tgt, lambda i: jnp.maximum(i - 1, 0), 0)], [],
                 [(F32, R, D)], [(D, False)])

    def total_body(i, j, rows_, vecs):
        return [jnp.broadcast_to(jnp.sum(rows_[0], axis=-1, keepdims=True), (SUBLANES, D)) * (0.5 / D)], []

    loss_local = ew("loss_total", total_body, 1, 1, SUBLANES, D, [(sq, 0, 0)], [], [(F32, SUBLANES, D)])[0][0, 0]
    loss = lax.psum(loss_local, ("x", "y", "c"))

    gsmall = {nm: [None] * L for nm in ['gla_norm_gain', 'pool_scale', 'ln_mix_gain', 'ln_mix_bias', 'ln_ffn_gain',
                                        'ln_ffn_bias', 'w_decay_up', 'b_decay_up']}
    gbig = {nm: [None] * L for nm in big}
    dmod = [None] * L

    def ln_bwd(name, x_in, br, d_out, mv, gt_blk, gain):
        def body(i, j, rows_, vecs):
            gt = _cls(vecs[0], i)
            r = alpha * rows_[0] + gt * rows_[1]
            mu = jnp.mean(r, axis=-1, keepdims=True)
            rc = r - mu
            rstd = lax.rsqrt(jnp.mean(rc * rc, axis=-1, keepdims=True) + LN_EPS)
            xhat = rc * rstd
            dxh = rows_[2] * vecs[1][0:1, :]
            dr = rstd * (dxh - jnp.mean(dxh, axis=-1, keepdims=True) - xhat * jnp.mean(dxh * xhat, axis=-1, keepdims=True))
            return [dr, gt * dr], [_colsum(rows_[2] * xhat), _colsum(rows_[2]), _colsum(dr * rows_[1])]

        return ew(name, body, R // TL, 1, TL, D, [(x_in, 0, 0), (br, 0, 0), (d_out, 0, 0)], [(mv, gt_blk), (gain8(gain), 0)],
                  [(F32, R, D), (MXU_DTYPE, R, D)], [(D, False), (D, False), (D, True)], ctx_tiles=CL // TL)

    def mod_bwd(name, d_h, x_in, d_r, mv, sc_blk):
        def body(i, j, rows_, vecs):
            return ([rows_[0] * (1.0 + _cls(vecs[0], i)) + alpha * rows_[2]],
                    [_colsum(rows_[0] * rows_[1]), _colsum(rows_[0])])

        return ew(name, body, MT, D // tn_d, TM, tn_d, [(d_h, 0, 0), (x_in, 0, 0), (d_r, 0, 0)],
                  [(mv, sc_blk * (D // tn_d))], [(F32, R, D)], [(D, True), (D, True)])

    shard_axis = {'w_in': 1, 'w_gla_out': 0, 'w_pool_out': 1, 'w_out': 0, 'w_ffn_in': 1, 'w_ffn_out': 0, 'w_pool_group': 1}
    scatter_groups = [['w_in'], ['w_ffn_in'], ['w_gla_out', 'w_pool_out', 'w_out', 'w_ffn_out', 'w_pool_group']]

    def halves(a):
        h = a.shape[0] // 2
        return (lax.dynamic_slice_in_dim(a, ci * h, h, axis=0), lax.dynamic_slice_in_dim(a, (1 - ci) * h, h, axis=0))

    def cut_layer(l):
        cut = dict(keep_r={}, give_r={}, keep_o={}, give_o={})
        for nm in big:
            g = gbig[nm][l]
            ax = shard_axis[nm]
            width = weights[nm].shape[ax + 1]
            if nm == 'w_ffn_in':
                assert 2 * width == DFF and N_CHIPS == 4
                piece = lambda which: jnp.where(which < 2, lax.dynamic_slice_in_dim(g[0], (which % 2) * width, width, axis=1),
                                                lax.dynamic_slice_in_dim(g[1], (which % 2) * width, width, axis=1))
            else:
                piece = lambda which: lax.dynamic_slice_in_dim(g, which * width, width, axis=ax)
            parts = [halves(piece(jnp.bitwise_xor(chip, k)).astype(WIRE_DTYPE)) for k in range(1, N_CHIPS)]
            cut['keep_r'][nm] = jnp.stack([p_[0] for p_ in parts])
            cut['give_r'][nm] = jnp.stack([p_[1] for p_ in parts])
            cut['keep_o'][nm], cut['give_o'][nm] = halves(piece(chip))
        return cut

    def swap_partials_plan(cut):
        return swap_plan([cut['give_r'][nm] for nm in big] + [cut['give_o'][nm] for nm in big])

    def add_partials(cut, got):
        cut['got_o'] = dict(zip(big, got[len(big):]))
        cut['send_r'] = {}
        for nm, theirs in zip(big, got[:len(big)]):
            a2, b2 = _rows2d(cut['keep_r'][nm]), _rows2d(theirs)
            tm_ = _pick(a2.shape[0], (256, 128, 64, 32, 16, 8))

            def add2_body(i, j, rows__, vecs):
                return [rows__[0].astype(F32) + rows__[1].astype(F32)], []

            cut['send_r'][nm] = ew("add_partials", add2_body, a2.shape[0] // tm_, 1, tm_, a2.shape[1],
                                   [(a2, 0, 0), (b2, 0, 0)], [], [(WIRE_DTYPE, a2.shape[0], a2.shape[1])]
                                   )[0].reshape(cut['keep_r'][nm].shape)
        cut['recvd'] = {}

    def sum_grads(cut):
        done = {}
        for nm in big:
            a2, b2 = _rows2d(cut['keep_o'][nm]), _rows2d(cut['got_o'][nm])
            rows_, width = a2.shape
            tm_ = _pick(rows_, (256, 128, 64, 32, 16, 8))

            def psum_body(i, j, rows__, vecs):
                return [(rows__[0] + rows__[1]) + rows__[2].astype(F32) + rows__[3].astype(F32) + rows__[4].astype(F32)], []

            rb2 = cut['recvd'][nm].reshape(3 * rows_, width)
            done[nm] = ew("sum_grads", psum_body, rows_ // tm_, 1, tm_, width,
                          [(a2, 0, 0), (b2, 0, 0)] + [(rb2, k * (rows_ // tm_), 0) for k in range(3)], [],
                          [(F32, rows_, width)])[0]
        return done

    def scatter_group_plan(cut, group):
        return scatter_plan([cut['send_r'][nm] for nm in group])

    finished = [None] * L
    pending = None
    for l in reversed(range(L)):
        sv = saved[l]
        mv = modv[l]
        Wl = W[l]
        proj = sv['proj']
        dr2, d_f, g_gain, g_bias, g_gt_f = ln_bwd("ln_bwd", sv['x1'], sv['f_ffn'], d_x, mv, 5, ln_ffn_gain[l])
        gsmall['ln_ffn_gain'][l], gsmall['ln_ffn_bias'][l] = g_gain[0], g_bias[0]
        if pending is not None:
            d_s, got = matmul("ffn_out_dx_and_swap", d_f, Wl['wfo'], 'nt', R, DFF, D, tm=TM, tn=tn_ff,
                              comm=swap_partials_plan(pending))
            add_partials(pending, got)
        else:
            d_s = matmul("ffn_out_dx", d_f, Wl['wfo'], 'nt', R, DFF, D, tm=TM, tn=tn_ff)
        gbig['w_ffn_out'][l] = matmul("ffn_out_dw", sv['s_ffn'], d_f, 'tn', DFF, D, R, tm=tw(DFF), tn=tn_d)
        d_gate, d_up = swiglu_bwd("swiglu_bwd", d_s, sv['u_ffn'], MT, DFF // te_ff, TM, te_ff)
        tn_x = 512
        if pending is not None:
            d_h2, got = matmul("ffn_in_dx_and_scatter", d_gate, Wl['wfi'], 'nt', R, D, DFF, tm=TM, tn=tn_x,
                               second=(d_up, (0, 1)), comm=scatter_group_plan(pending, scatter_groups[0]))
            pending['recvd'].update(zip(scatter_groups[0], got))
            g_gate, got = matmul("ffn_in_dw_and_scatter", sv['h2'], d_gate, 'tn', D, DFF, R, tm=tw(D), tn=te_ff,
                                 comm=scatter_group_plan(pending, scatter_groups[1]))
            pending['recvd'].update(zip(scatter_groups[1], got))
        else:
            d_h2 = matmul("ffn_in_dx", d_gate, Wl['wfi'], 'nt', R, D, DFF, tm=TM, tn=tn_x, second=(d_up, (0, 1)))
            g_gate = matmul("ffn_in_dw", sv['h2'], d_gate, 'tn', D, DFF, R, tm=tw(D), tn=te_ff)
        gbig['w_ffn_in'][l] = (g_gate, matmul("ffn_in_dw_up", sv['h2'], d_up, 'tn', D, DFF, R, tm=tw(D), tn=te_ff))
        d_x1, g_sc_f, g_sh_f = mod_bwd("mod_bwd", d_h2, sv['x1'], dr2, mv, 4)
        dr1, d_y, g_gain, g_bias, g_gt_m = ln_bwd("ln_bwd", sv['xs'], sv['y_mix'], d_x1, mv, 2, ln_mix_gain[l])
        gsmall['ln_mix_gain'][l], gsmall['ln_mix_bias'][l] = g_gain[0], g_bias[0]
        d_mg = matmul("mix_out_dx", d_y, Wl['wout'], 'nt', R, D, D, tm=TM, tn=tn_d)
        gbig['w_out'][l] = matmul("mix_out_dw", sv['mg'], d_y, 'tn', D, D, R, tm=tw(D), tn=tn_d)

        def merge_bwd_body(i, j, rows_, vecs):
            d_m, yg, yp, ba, bb = rows_
            sa, sb = _sigmoid(ba), _sigmoid(bb)
            return [d_m * sa, d_m * sb, d_m * yg * sa * (1.0 - sa), d_m * yp * sb * (1.0 - sb)], []

        d_yg, d_yp, d_bga, d_bgb = ew(
            "merge_bwd", merge_bwd_body, MT, D // TE, TM, TE,
            [(d_mg, 0, 0), (sv['y_gla'], 0, 0), (sv['y_pool'], 0, 0), (proj, 0, OFF_BGA // TE), (proj, 0, OFF_BGB // TE)], [],
            [(MXU_DTYPE, R, D), (MXU_DTYPE, R, D), (F32, R, D), (F32, R, D)])
        d_ts = matmul("pool_out_dx", d_yp, Wl['wpo'], 'nt', R, DP, D, tm=TM, tn=tn_dp)
        gbig['w_pool_out'][l] = matmul("pool_out_dw", sv['ts_pool'], d_yp, 'tn', DP, D, R, tm=tw(DP), tn=tn_d)

        def scale_bwd_body(i, j, rows_, vecs):
            return [rows_[0] * vecs[0][0:1, :]], [_colsum(rows_[0] * rows_[1])]

        d_t, g_ps = ew("pool_scale_bwd", scale_bwd_body, MT, DP // TE, TM, TE, [(d_ts, 0, 0), (sv['t_pool'], 0, 0)],
                       [(gain8(pool_scale[l]), 0)], [(MXU_DTYPE, R, DP)], [(DP, False)])
        gsmall['pool_scale'][l] = g_ps[0]
        d_u_pool = matmul("pool_group_dx", d_t, Wl['wpg_bd'], 'nt', R, DP, DP, tm=TM, tn=tn_dp)
        g_bd = matmul("pool_group_dw", sv['u_pool'], d_t, 'tn', DP, DP, R, tm=tw(DP), tn=tn_dp)
        gbig['w_pool_group'][l] = jnp.stack([g_bd[g * PG:(g + 1) * PG, g * PG:(g + 1) * PG] for g in range(N_POOL)])
        d_p = pool_mix("pool_bwd", d_u_pool, 0, cfg, True)
        d_a = matmul("gla_out_dx", d_yg, Wl['wgo'], 'nt', R, DV, D, tm=TM, tn=tn_d)
        gbig['w_gla_out'][l] = matmul("gla_out_dw", sv['a_gla'], d_yg, 'tn', DV, D, R, tm=tw(DV), tn=tn_d)

        def post_bwd_body(i, j, rows_, vecs):
            d_a_, o_f_, o_b_, g_ = rows_
            gain = vecs[0][0:1, :]
            o = o_f_ + o_b_
            rstd = lax.rsqrt(jnp.mean(o * o, axis=-1, keepdims=True) + RMS_EPS)
            on = o * rstd
            sg = _silu(g_)
            d_on = d_a_ * gain * sg
            d_o_ = rstd * (d_on - on * jnp.mean(d_on * on, axis=-1, keepdims=True))
            return [d_o_, d_a_ * on * gain * _dsilu(g_)], [_colsum(d_a_ * on * sg)]

        d_o, d_g, g_gng = ew("gla_post_bwd", post_bwd_body, MT, N_HEADS, TM, HV,
                             [(d_a, 0, 0), (sv['o_f'], 0, 0), (sv['o_b'], 0, 0), (proj, 0, OFF_G // HV)],
                             [(gain8(gla_norm_gain[l]), 0)], [(F32, R, DV), (F32, R, DV)], [(DV, False)])
        gsmall['gla_norm_gain'][l] = g_gng[0]
        part = gla_bwd("gla_bwd", proj, wd_pad[l][0], bd_pad[l][0], sv['st_f'], d_o, cfg, False)
        full = gla_bwd("gla_rev_bwd", proj, wd_pad[l][1], bd_pad[l][1], sv['st_b'], d_o, cfg, True, addends=part[:4])
        d_q, d_k, d_v, d_alr = full[:4]
        g_wd = [res[4][d * GATE_RANK:(d + 1) * GATE_RANK] for d, res in enumerate((part, full))]
        g_bd_ = [res[5][0] for res in (part, full)]
        gsmall['w_decay_up'][l], gsmall['b_decay_up'][l] = jnp.stack(g_wd), jnp.stack(g_bd_)
        d_proj = jnp.concatenate([t_.astype(MXU_DTYPE) for t_ in (d_q, d_k, d_v, d_g, d_alr[:, :2 * GATE_RANK], d_p, d_bga, d_bgb,
                                                                   jnp.zeros((R, ALR_W - 2 * GATE_RANK), F32))], axis=-1)
        if pending is not None:
            d_h1, got = matmul("proj_dx_and_scatter", d_proj, Wl['win_bwd'], 'nt', R, D, NP, tm=TM, tn=tn_x,
                               comm=scatter_group_plan(pending, scatter_groups[2]))
            pending['recvd'].update(zip(scatter_groups[2], got))
            finished[l + 1] = sum_grads(pending)
        else:
            d_h1 = matmul("proj_dx", d_proj, Wl['win_bwd'], 'nt', R, D, NP, tm=TM, tn=tn_x)
        gbig['w_in'][l] = matmul("proj_dw", sv['h1'], d_proj, 'tn', D, NP, R, tm=tw(D), tn=tn_np)
        d_x, g_sc_m, g_sh_m = mod_bwd("mod_bwd", d_h1, sv['xs'], dr1, mv, 1)
        dmod[l] = jnp.concatenate([g_sh_m[:2], g_sc_m[:2], g_gt_m[:2], g_sh_f[:2], g_sc_f[:2], g_gt_f[:2]], axis=-1)
        pending = cut_layer(l)

    grad_x = d_x[CL:].reshape(x.shape)
    add_partials(pending, run_comm("swap_partials", swap_partials_plan(pending)))
    pending['recvd'] = dict(zip(big, run_comm("scatter_grads", scatter_plan([pending['send_r'][nm] for nm in big]))))
    finished[0] = sum_grads(pending)

    dmod = jnp.stack(dmod)
    summed = [dmod[:, 0]] + [jnp.stack(gsmall[nm]) for nm in
                             ['ln_mix_gain', 'ln_mix_bias', 'ln_ffn_gain', 'ln_ffn_bias', 'gla_norm_gain', 'pool_scale',
                              'w_decay_up', 'b_decay_up']]
    pack = _pack([dmod[:, 1]] + summed)
    prow = pack.shape[0]
    packs = all_gather8("ag_small", pack)
    tot = _sum_rows("sum_small", packs.reshape(N_DEV * prow, PACK_W), N_DEV, prow, PACK_W)
    shapes = [(L, N_MOD * D)] + [a.shape for a in summed]
    tot = _unpack(tot, shapes)
    dmod_ctx = tot[1]
    g_rep = dict(zip(['ln_mix_gain', 'ln_mix_bias', 'ln_ffn_gain', 'ln_ffn_bias', 'gla_norm_gain', 'pool_scale'], tot[2:8]))
    g_wdu_full, g_bdu_full = tot[8], tot[9]
    dmod_lat = jnp.stack([_unpack(packs[d_], shapes[:1])[0] for d_ in range(N_DEV)], axis=1)
    dm_all = jnp.concatenate([dmod_lat, dmod_ctx[:, None, :], jnp.zeros((L, 16 - N_DEV - 1, N_MOD * D), F32)], axis=1)

    dm_mine = lax.dynamic_slice_in_dim(dm_all, chip * n_ada, n_ada, axis=2)
    g_w_ada = jnp.stack([matmul("ada_dw", act, dm_mine[l], 'tn', D, n_ada, 16, tm=tw(D), tn=tn_ada, precise=True)
                         for l in range(L)])

    def bsum_body(i, j, rows_, vecs):
        return [jnp.broadcast_to(_colsum(rows_[0]), rows_[0].shape)], []

    g_b_ada = jnp.stack([ew("ada_db", bsum_body, 1, 1, 16, N_MOD * D, [(dm_all[l], 0, 0)], [],
                            [(F32, 16, N_MOD * D)])[0][0] for l in range(L)])
    part_c = [matmul("ada_dc", dm_mine[l], w_ada2, 'nt', 16, D, n_ada, tm=16, tn=tn_d, b_off=(l * (D // tn_d), 0),
                     precise=True)
              for l in range(L)]
    parts_c = all_gather8("ag_dcond", jnp.concatenate(part_c, axis=0))
    dc_rows = parts_c[0::2].reshape(N_CHIPS * L * 16, D)

    def dc_body(i, j, rows_, vecs):
        acc = rows_[0]
        for r_ in rows_[1:-1]:
            acc = acc + r_
        return [acc * rows_[-1]], []

    g_c_ctx = ew("dcond", dc_body, 1, 1, 16, D, [(dc_rows, p_, 0) for p_ in range(N_CHIPS * L)] + [(dact, 0, 0)], [],
                 [(F32, 16, D)])[0][N_DEV]

    mine_all = [jnp.concatenate([finished[l][nm] for l in range(L)], axis=0) for nm in big]
    other = run_comm("swap_grads", swap_plan(mine_all))

    out_g, out_d, out_m, out_v = {}, {}, {}, {}
    for nm, mine, theirs in zip(big, mine_all, other):
        shp = weights[nm].shape
        half_rows = (shp[1] // 2) * (shp[2] if len(shp) == 4 else 1)
        res = _adam("adam_" + nm, _rows2d(weights[nm]), [mine, theirs], _rows2d(mom1[nm]), _rows2d(mom2[nm]),
                    half_rows=half_rows, core=ci)
        out_g[nm], out_d[nm], out_m[nm], out_v[nm] = [r_.reshape(shp) for r_ in res]
    res = _adam("adam_w_ada", _rows2d(w_ada), [_rows2d(g_w_ada)], _rows2d(m_w_ada), _rows2d(v_w_ada))
    out_g['w_ada'], out_d['w_ada'], out_m['w_ada'], out_v['w_ada'] = [r_.reshape(w_ada.shape) for r_ in res]
    n_wd, n_bd = w_decay_up.shape[-1], b_decay_up.shape[-1]
    small_g = dict(g_rep, c_ctx=g_c_ctx, b_ada=g_b_ada,
                   w_decay_up=lax.dynamic_slice_in_dim(g_wdu_full, chip * n_wd, n_wd, axis=3),
                   b_decay_up=lax.dynamic_slice_in_dim(g_bdu_full, chip * n_bd, n_bd, axis=2))
    small = [nm for nm in names if nm not in big and nm != 'w_ada']
    res = _adam("adam_small", _pack([weights[nm] for nm in small]), [_pack([small_g[nm] for nm in small])],
                _pack([mom1[nm] for nm in small]), _pack([mom2[nm] for nm in small]))
    small_shapes = [weights[nm].shape for nm in small]
    for dst, packed in zip((out_g, out_d, out_m, out_v), res):
        for nm, val in zip(small, _unpack(packed, small_shapes)):
            dst[nm] = val

    return (loss, grad_x, *[out_g[nm] for nm in names], *[out_d[nm] for nm in names],
            *[out_m[nm] for nm in names], *[out_v[nm] for nm in names])


def swiglu(name, u, nrow, nh, tm, tn):
    def kern(gate_ref, up_ref, o_ref):
        o_ref[...] = (_silu(gate_ref[...]) * up_ref[...]).astype(o_ref.dtype)

    return pl.pallas_call(
        kern, name=name, grid=(nh, nrow),
        in_specs=[pl.BlockSpec((tm, tn), lambda j, i: (i, j)), pl.BlockSpec((tm, tn), lambda j, i: (i, nh + j))],
        out_specs=pl.BlockSpec((tm, tn), lambda j, i: (i, j)),
        out_shape=jax.ShapeDtypeStruct((u.shape[0], nh * tn), MXU_DTYPE), compiler_params=_params(),
    )(u, u)


def swiglu_bwd(name, d_s, u, nrow, nh, tm, tn):
    def kern(ds_ref, gate_ref, up_ref, dg_ref, du_ref):
        gate, d_s_ = gate_ref[...], ds_ref[...]
        dg_ref[...] = (d_s_ * up_ref[...] * _dsilu(gate)).astype(dg_ref.dtype)
        du_ref[...] = (d_s_ * _silu(gate)).astype(du_ref.dtype)

    tile = pl.BlockSpec((tm, tn), lambda j, i: (i, j))
    return pl.pallas_call(
        kern, name=name, grid=(nh, nrow), in_specs=[tile, tile, pl.BlockSpec((tm, tn), lambda j, i: (i, nh + j))],
        out_specs=[tile, tile], out_shape=[jax.ShapeDtypeStruct((u.shape[0], nh * tn), MXU_DTYPE)] * 2,
        compiler_params=_params(),
    )(d_s, u, u)
```

```python
import functools

import jax
import jax.numpy as jnp
import numpy as np
from jax import lax
from jax.experimental import pallas as pl
from jax.experimental.pallas import tpu as pltpu

F32 = jnp.float32
MXU_DTYPE = jnp.bfloat16
WIRE_DTYPE = jnp.bfloat16

GRID_W = 64
CHUNK = 64
N_HEADS = 4
GATE_RANK = 16
GATE_NORM = 16.0
N_MOD = 6
N_POOL = 4
LN_EPS = 1e-5
RMS_EPS = 1e-6
ADAM_LR = 0.001
ADAM_B1 = 0.9
ADAM_B2 = 0.999
ADAM_EPS = 1e-08
ADAM_WD = 0.01
ADAM_STEP = 10

LANES = 128
SUBLANES = 8
ALR_W = 256
PACK_W = 2048
VMEM_LIMIT = 56 * 1024 * 1024
N_CHIPS = 4
N_DEV = 8
MESH = pl.DeviceIdType.MESH

NN = ((1,), (0,))
NT = ((1,), (1,))
TN = ((0,), (0,))


def _dot(a, b, dims):
    return lax.dot_general(a.astype(MXU_DTYPE), b.astype(MXU_DTYPE), (dims, ((), ())),
                           preferred_element_type=F32)


def _dot_f32(a, b, dims):
    return lax.dot_general(a.astype(F32), b.astype(F32), (dims, ((), ())),
                           precision=lax.Precision.HIGHEST, preferred_element_type=F32)


def _dot_mask(mask, x, dims):
    m = mask.astype(MXU_DTYPE)
    if MXU_DTYPE == F32:
        return lax.dot_general(m, x, (dims, ((), ())), preferred_element_type=F32)
    acc = None
    rest = x
    for _ in range(3):
        piece = rest.astype(MXU_DTYPE)
        rest = rest - piece.astype(F32)
        part = lax.dot_general(m, piece, (dims, ((), ())), preferred_element_type=F32)
        acc = part if acc is None else acc + part
    return acc


def _dot_3x(a, b, dims):
    if MXU_DTYPE == F32:
        return lax.dot_general(a, b, (dims, ((), ())), preferred_element_type=F32)
    a_hi, b_hi = a.astype(MXU_DTYPE), b.astype(MXU_DTYPE)
    a_lo = (a - a_hi.astype(F32)).astype(MXU_DTYPE)
    b_lo = (b - b_hi.astype(F32)).astype(MXU_DTYPE)
    dot = lambda u, w: lax.dot_general(u, w, (dims, ((), ())), preferred_element_type=F32)
    return dot(a_hi, b_hi) + (dot(a_lo, b_hi) + dot(a_hi, b_lo))


def _pick(n, cands):
    for c in cands:
        if n % c == 0:
            return c
    return n


def _params():
    return pltpu.CompilerParams(vmem_limit_bytes=VMEM_LIMIT)


def _sigmoid(x):
    return 0.5 + 0.5 * jnp.tanh(0.5 * x)


def _silu(x):
    return x * _sigmoid(x)


def _dsilu(x):
    s = _sigmoid(x)
    return s * (1.0 + x * (1.0 - s))


def matmul(name, a, b, form, m, n, k, *, tm, tn, out_dtype=F32, a_off=(0, 0), b_off=(0, 0), bias=None, bias_off=0,
           precise=False, comm=None, second=None):
    assert m % tm == 0 and n % tn == 0, (name, m, n, tm, tn)
    if form == 'tn':
        a_spec = pl.BlockSpec((k, tm), lambda j, i: (a_off[0], i + a_off[1]))
    else:
        a_spec = pl.BlockSpec((tm, k), lambda j, i: (i + a_off[0], a_off[1]))
    if form == 'nt':
        b_spec = pl.BlockSpec((tn, k), lambda j, i: (j + b_off[0], b_off[1]))
    else:
        b_spec = pl.BlockSpec((k, tn), lambda j, i: (b_off[0], j + b_off[1]))
    dims = {'nn': NN, 'nt': NT, 'tn': TN}[form]
    in_specs = [a_spec, b_spec]
    args = [a, b]
    if bias is not None:
        in_specs.append(pl.BlockSpec((SUBLANES, tn), lambda j, i: (0, j + bias_off)))
        args.append(bias)

    if second is not None:
        a2, b2_off = second
        assert form == 'nt'
        in_specs += [pl.BlockSpec((tm, k), lambda j, i: (i, 0)), pl.BlockSpec((tn, k), lambda j, i: (j + b2_off[0], b2_off[1]))]
        args += [a2, b]
    n_own = len(args)
    nj, ni = n // tn, m // tm
    out_spec = pl.BlockSpec((tm, tn), lambda j, i: (i, j))
    out_shape = jax.ShapeDtypeStruct((m, n), out_dtype)

    def product(refs):
        acc = (_dot_f32 if precise else _dot)(refs[0][...], refs[1][...], dims)
        if bias is not None:
            acc = acc + refs[2][0:1, :]
        if second is not None:
            acc = acc + _dot(refs[n_own - 2][...], refs[n_own - 1][...], dims)
        return acc

    if comm is None:
        def body(*refs):
            refs[-1][...] = product(refs).astype(refs[-1].dtype)

        return pl.pallas_call(body, name=name, grid=(nj, ni), in_specs=in_specs, out_specs=out_spec,
                              out_shape=out_shape, compiler_params=_params())(*args)

    n_ci, n_co = len(comm.ins), len(comm.outs)

    def hosted(*refs):
        c_in = refs[n_own:n_own + n_ci]
        o_ref = refs[n_own + n_ci]
        c_out = refs[n_own + n_ci + 1:n_own + n_ci + 1 + n_co]
        sems = refs[n_own + n_ci + 1 + n_co:]
        j, i = pl.program_id(0), pl.program_id(1)

        @pl.when((j == 0) & (i == 0))
        def _():
            comm.start(c_in, c_out, sems)

        o_ref[...] = product(refs).astype(o_ref.dtype)

        @pl.when((j == nj - 1) & (i == ni - 1))
        def _():
            comm.finish(c_in, c_out, sems)

    any_spec = pl.BlockSpec(memory_space=pl.ANY)
    res = pl.pallas_call(
        hosted, name=name, grid=(nj, ni), in_specs=in_specs + [any_spec] * n_ci,
        out_specs=[out_spec] + [any_spec] * n_co, out_shape=[out_shape] + list(comm.outs),
        scratch_shapes=list(comm.sems), compiler_params=_params(),
    )(*args, *comm.ins)
    return res[0], list(res[1:])


def ew(name, body, nrow, ncol, tm, tn, row_ins, vec_ins, row_outs, sum_outs=(), ctx_tiles=1, pass_i=False):
    def rmap(roff):
        return roff if callable(roff) else (lambda i: i + roff)

    in_specs = []
    for arr, roff, coff in row_ins:
        in_specs.append(pl.BlockSpec((tm, tn), functools.partial(lambda j, i, r, c: (r(i), j + c), r=rmap(roff), c=coff)))
    for arr, coff in vec_ins:
        in_specs.append(pl.BlockSpec((SUBLANES, tn), functools.partial(lambda j, i, c: (0, j + c), c=coff)))
    out_specs = [pl.BlockSpec((tm, tn), lambda j, i: (i, j)) for _ in row_outs]
    out_specs += [pl.BlockSpec((SUBLANES, tn), lambda j, i: (0, j)) for _ in sum_outs]
    out_shape = [jax.ShapeDtypeStruct((r, c), dt) for dt, r, c in row_outs]
    out_shape += [jax.ShapeDtypeStruct((SUBLANES, c), F32) for c, _ in sum_outs]
    n_row, n_vec, n_ro = len(row_ins), len(vec_ins), len(row_outs)

    def kern(*refs):
        j, i = pl.program_id(0), pl.program_id(1)
        rows = [r[...] for r in refs[:n_row]]
        vecs = [r[...] for r in refs[n_row:n_row + n_vec]]
        outs = refs[n_row + n_vec:]
        is_ctx = i < ctx_tiles
        res, sums = body(i if pass_i else is_ctx, j, rows, vecs)
        for ref, val in zip(outs[:n_ro], res):
            ref[...] = val.astype(ref.dtype)
        for ref, val, (_, by_class) in zip(outs[n_ro:], sums, sum_outs):
            @pl.when(i == 0)
            def _():
                ref[...] = jnp.zeros_like(ref)
            if by_class:
                ref[0:1, :] += jnp.where(is_ctx, val, 0.0)
                ref[1:2, :] += jnp.where(is_ctx, 0.0, val)
            else:
                ref[0:1, :] += val

    outs = pl.pallas_call(
        kern, name=name, grid=(ncol, nrow), in_specs=in_specs, out_specs=out_specs, out_shape=out_shape,
        compiler_params=_params(),
    )(*[a for a, _, _ in row_ins], *[a for a, _ in vec_ins])
    return list(outs)


def _cls(vec, is_ctx):
    return jnp.where(is_ctx, vec[0:1, :], vec[1:2, :])


def _colsum(x):
    return jnp.sum(x, axis=0, keepdims=True)


def _chunk_map(cfg, rev):
    nctx, nc = cfg.CL // CHUNK, cfg.R // CHUNK
    if not rev:
        return lambda s: s
    return lambda s: jnp.where(s < nctx, nctx - 1 - s, nctx + nc - 1 - s)


def _gla_chunk(q_ref, k_ref, a_ref, wd_ref, bd_ref, rev, scale):
    q = q_ref[...] * scale
    k = k_ref[...]
    z = _dot_3x(a_ref[...], wd_ref[...], NN) + bd_ref[0:1, :]
    la = (jnp.minimum(z, 0.0) - jnp.log(1.0 + jnp.exp(-jnp.abs(z)))) * (1.0 / GATE_NORM)
    r = lax.broadcasted_iota(jnp.int32, (CHUNK, CHUNK), 0)
    c = lax.broadcasted_iota(jnp.int32, (CHUNK, CHUNK), 1)
    keep = (r <= c) if rev else (r >= c)
    tri = keep.astype(F32)
    cum = _dot_mask(tri, la, NN)
    mid = CHUNK // 2 if rev else CHUNK // 2 - 1
    end = 0 if rev else CHUNK - 1
    ref = cum[mid:mid + 1, :]
    last = cum[end:end + 1, :]
    return dict(q=q, k=k, z=z, keep=keep, tri=tri, q_in=q * jnp.exp(cum - ref), k_in=k * jnp.exp(ref - cum),
                e_q=jnp.exp(cum), e_k=jnp.exp(last - cum), e_inq=jnp.exp(cum - ref), e_ink=jnp.exp(ref - cum),
                e_last=jnp.exp(last))


def _gla_in_specs(cfg, rows_of):
    dk, dv = cfg.DK, cfg.DV
    return [
        pl.BlockSpec((CHUNK, dk), lambda s: (rows_of(s), 0)),
        pl.BlockSpec((CHUNK, dk), lambda s: (rows_of(s), 1)),
        pl.BlockSpec((CHUNK, dv), lambda s: (rows_of(s), 2 * dk // dv)),
        pl.BlockSpec((CHUNK, ALR_W), lambda s: (rows_of(s), (cfg.NP - ALR_W) // ALR_W)),
        pl.BlockSpec((ALR_W, dk), lambda s: (0, 0)),
        pl.BlockSpec((SUBLANES, dk), lambda s: (0, 0)),
    ]


def gla_fwd(name, proj, wd, bd, cfg, rev, comm=None):
    hk, hv = cfg.HK, cfg.HV
    nc = cfg.R // CHUNK
    cmap = _chunk_map(cfg, rev)
    scale = hk ** -0.5
    n_ci, n_co = (len(comm.ins), len(comm.outs)) if comm else (0, 0)

    def body(*refs):
        q_ref, k_ref, v_ref, a_ref, wd_ref, bd_ref = refs[:6]
        c_in = refs[6:6 + n_ci]
        o_ref, ss_ref = refs[6 + n_ci:8 + n_ci]
        c_out = refs[8 + n_ci:8 + n_ci + n_co]
        st_scr = refs[8 + n_ci + n_co]
        sems = refs[9 + n_ci + n_co:]
        s = pl.program_id(0)

        @pl.when(s == 0)
        def _():
            st_scr[...] = jnp.zeros_like(st_scr)
            if comm:
                comm.start(c_in, c_out, sems)

        t = _gla_chunk(q_ref, k_ref, a_ref, wd_ref, bd_ref, rev, scale)
        q_int, k_st = t['q'] * t['e_q'], t['k'] * t['e_k']
        for h in range(N_HEADS):
            ks, vs = slice(h * hk, (h + 1) * hk), slice(h * hv, (h + 1) * hv)
            v = v_ref[:, vs]
            st = st_scr[h]
            ss_ref[0, h] = st
            a = jnp.where(t['keep'], _dot(t['q_in'][:, ks], t['k_in'][:, ks], NT), 0.0)
            o_ref[:, vs] = _dot(a, v, NN) + _dot(q_int[:, ks], st, NT)
            st_scr[h] = st * t['e_last'][:, ks] + _dot(v, k_st[:, ks], TN)

        if comm:
            @pl.when(s == nc - 1)
            def _():
                comm.finish(c_in, c_out, sems)

    any_spec = pl.BlockSpec(memory_space=pl.ANY)
    res = pl.pallas_call(
        body, name=name, grid=(nc,), in_specs=_gla_in_specs(cfg, cmap) + [any_spec] * n_ci,
        out_specs=[pl.BlockSpec((CHUNK, cfg.DV), lambda s: (cmap(s), 0)),
                   pl.BlockSpec((1, N_HEADS, hv, hk), lambda s: (s, 0, 0, 0))] + [any_spec] * n_co,
        out_shape=[jax.ShapeDtypeStruct((cfg.R, cfg.DV), F32),
                   jax.ShapeDtypeStruct((nc, N_HEADS, hv, hk), F32)] + (list(comm.outs) if comm else []),
        scratch_shapes=[pltpu.VMEM((N_HEADS, hv, hk), F32)] + (list(comm.sems) if comm else []),
        compiler_params=_params(),
    )(proj, proj, proj, proj, wd, bd, *(comm.ins if comm else []))
    return res[0], res[1], list(res[2:])


def gla_bwd(name, proj, wd, bd, states, d_o, cfg, rev, addends=None):
    hk, hv = cfg.HK, cfg.HV
    nc = cfg.R // CHUNK
    cmap = _chunk_map(cfg, rev)
    rows_of = lambda g: cmap(nc - 1 - g)
    scale = hk ** -0.5
    n_add = 0 if addends is None else 4

    def body(*refs):
        q_ref, k_ref, v_ref, a_ref, wd_ref, bd_ref, ss_ref, do_ref = refs[:8]
        adds = refs[8:8 + n_add]
        dq_ref, dk_ref, dv_ref, da_ref, dwd_ref, dbd_ref, dst_scr = refs[8 + n_add:]
        g = pl.program_id(0)

        @pl.when(g == 0)
        def _():
            dst_scr[...] = jnp.zeros_like(dst_scr)
            dwd_ref[...] = jnp.zeros_like(dwd_ref)
            dbd_ref[...] = jnp.zeros_like(dbd_ref)

        t = _gla_chunk(q_ref, k_ref, a_ref, wd_ref, bd_ref, rev, scale)
        q_int, k_st = t['q'] * t['e_q'], t['k'] * t['e_k']
        dq_h, dk_h, carry_h = [], [], []
        for h in range(N_HEADS):
            ks, vs = slice(h * hk, (h + 1) * hk), slice(h * hv, (h + 1) * hv)
            v = v_ref[:, vs]
            d_out = do_ref[:, vs]
            st = ss_ref[0, h]
            dst = dst_scr[h]
            a = jnp.where(t['keep'], _dot(t['q_in'][:, ks], t['k_in'][:, ks], NT), 0.0)
            da = jnp.where(t['keep'], _dot(d_out, v, NT), 0.0)
            dv = _dot(a, d_out, TN) + _dot(k_st[:, ks], dst, NT)
            dq_h.append(_dot(d_out, st, NN) * t['e_q'][:, ks] + _dot(da, t['k_in'][:, ks], NN) * t['e_inq'][:, ks])
            dk_h.append(_dot(v, dst, NN) * t['e_k'][:, ks] + _dot(da, t['q_in'][:, ks], TN) * t['e_ink'][:, ks])
            dst_scr[h] = dst * t['e_last'][:, ks] + _dot(d_out, q_int[:, ks], TN)
            st_end = st * t['e_last'][:, ks] + _dot(v, k_st[:, ks], TN)
            carry_h.append(_colsum(dst * st_end))
            dv_ref[:, vs] = dv + adds[2][:, vs] if n_add else dv
        dq = jnp.concatenate(dq_h, axis=-1)
        dk = jnp.concatenate(dk_h, axis=-1)
        dg = t['q'] * dq - t['k'] * dk
        dla = _dot_mask(t['tri'], dg, TN) + jnp.concatenate(carry_h, axis=-1)
        dz = dla * (1.0 / GATE_NORM) * _sigmoid(-t['z'])
        dalr = _dot_3x(dz, wd_ref[...], NT)
        dwd_ref[...] += _dot_3x(a_ref[...], dz, TN)
        dbd_ref[...] += jnp.broadcast_to(_colsum(dz), dbd_ref.shape)
        dq = dq * scale
        if n_add:
            dq, dk, dalr = dq + adds[0][...], dk + adds[1][...], dalr + adds[3][...]
        dq_ref[...] = dq
        dk_ref[...] = dk
        da_ref[...] = dalr

    qk_spec = pl.BlockSpec((CHUNK, cfg.DK), lambda g: (rows_of(g), 0))
    v_spec = pl.BlockSpec((CHUNK, cfg.DV), lambda g: (rows_of(g), 0))
    a_spec = pl.BlockSpec((CHUNK, ALR_W), lambda g: (rows_of(g), 0))
    in_specs = _gla_in_specs(cfg, rows_of) + [
        pl.BlockSpec((1, N_HEADS, hv, hk), lambda g: (nc - 1 - g, 0, 0, 0)), v_spec]
    args = [proj, proj, proj, proj, wd, bd, states, d_o]
    if n_add:
        in_specs += [qk_spec, qk_spec, v_spec, a_spec]
        args += list(addends)
    return pl.pallas_call(
        body, name=name, grid=(nc,), in_specs=in_specs,
        out_specs=[qk_spec, qk_spec, v_spec, a_spec,
                   pl.BlockSpec((ALR_W, cfg.DK), lambda g: (0, 0)),
                   pl.BlockSpec((SUBLANES, cfg.DK), lambda g: (0, 0))],
        out_shape=[jax.ShapeDtypeStruct((cfg.R, cfg.DK), F32), jax.ShapeDtypeStruct((cfg.R, cfg.DK), F32),
                   jax.ShapeDtypeStruct((cfg.R, cfg.DV), F32), jax.ShapeDtypeStruct((cfg.R, ALR_W), F32),
                   jax.ShapeDtypeStruct((ALR_W, cfg.DK), F32),
                   jax.ShapeDtypeStruct((SUBLANES, cfg.DK), F32)],
        scratch_shapes=[pltpu.VMEM((N_HEADS, hv, hk), F32)],
        compiler_params=_params(),
    )(*args)


def pool_mix(name, src, coff, cfg, transpose):
    tm, pg = cfg.TM, cfg.PG
    mt = cfg.R // tm
    reach = -(-(max(2 ** N_POOL // 2, 1) * GRID_W) // tm)
    nk = 2 * reach + 1
    img_rows = cfg.S // GRID_W
    shift = GRID_W.bit_length() - 1

    def ktile(m, d):
        return jnp.where(m == 0, 0, jnp.clip(m + d - reach, 1, mt - 1))

    def counts(idx, is_ctx, lo, hi):
        ctx_n = jnp.minimum(idx + hi + 1, cfg.CL) - jnp.maximum(idx - lo, 0)
        r, c = idx >> shift, idx & (GRID_W - 1)
        lat_n = ((jnp.minimum(r + hi + 1, img_rows) - jnp.maximum(r - lo, 0))
                 * (jnp.minimum(c + hi + 1, GRID_W) - jnp.maximum(c - lo, 0)))
        return jnp.where(is_ctx, ctx_n, lat_n).astype(F32)

    a = np.arange(tm)[:, None]
    b = np.arange(tm)[None, :]
    masks = np.zeros((N_POOL, nk + 1, tm, tm), np.float32)
    for g_ in range(N_POOL):
        lo_ = 2 ** g_
        hi_ = lo_ - 1
        for d_ in range(nk):
            dr = (d_ - reach) * (tm // GRID_W) + (b >> shift) - (a >> shift)
            dc = (b & (GRID_W - 1)) - (a & (GRID_W - 1))
            masks[g_, d_] = (dr >= -lo_) & (dr <= hi_) & (dc >= -lo_) & (dc <= hi_)
        masks[g_, nk] = (b - a >= -lo_) & (b - a <= hi_)
    masks = jnp.asarray(masks, MXU_DTYPE)

    def body(src_ref, self_ref, mask_ref, o_ref, acc):
        m, d = pl.program_id(0), pl.program_id(1)
        is_ctx = m == 0
        kt = m + d - reach
        valid = jnp.where(is_ctx, d == reach, (kt >= 1) & (kt <= mt - 1))
        seg = jnp.where(is_ctx, 0, cfg.CL)

        @pl.when(d == 0)
        def _():
            acc[...] = jnp.zeros_like(acc)

        @pl.when(valid)
        def _():
            for g in range(N_POOL):
                cols = slice(g * pg, (g + 1) * pg)
                x = src_ref[:, cols]
                if transpose:
                    kidx = lax.broadcasted_iota(jnp.int32, (tm, 1), 0) + (kt * tm - seg)
                    acc[:, cols] += _dot_mask(mask_ref[g, 0], x / counts(kidx, is_ctx, 2 ** g, 2 ** g - 1), TN)
                else:
                    acc[:, cols] += _dot_mask(mask_ref[g, 0], x, NN)

        @pl.when(d == nk - 1)
        def _():
            for g in range(N_POOL):
                cols = slice(g * pg, (g + 1) * pg)
                res = acc[:, cols]
                if not transpose:
                    midx = lax.broadcasted_iota(jnp.int32, (tm, 1), 0) + (m * tm - seg)
                    res = res / counts(midx, is_ctx, 2 ** g, 2 ** g - 1)
                o_ref[:, cols] = (res - self_ref[:, cols]).astype(o_ref.dtype)

    which = (lambda d: 2 * reach - d) if transpose else (lambda d: d)
    assert coff % N_POOL == 0
    dp = N_POOL * pg
    return pl.pallas_call(
        body, name=name, grid=(mt, nk),
        in_specs=[pl.BlockSpec((tm, dp), lambda m, d: (ktile(m, d), coff // N_POOL)),
                  pl.BlockSpec((tm, dp), lambda m, d: (m, coff // N_POOL)),
                  pl.BlockSpec((N_POOL, 1, tm, tm), lambda m, d: (0, jnp.where(m == 0, nk, which(d)), 0, 0))],
        out_specs=pl.BlockSpec((tm, dp), lambda m, d: (m, 0)),
        out_shape=jax.ShapeDtypeStruct((cfg.R, cfg.DP), F32),
        scratch_shapes=[pltpu.VMEM((tm, dp), F32)], compiler_params=_params(),
    )(src, src, masks)


def _my_place():
    return lax.axis_index("x"), lax.axis_index("y"), lax.axis_index("c")


def _flip(v, bit):
    return 1 - v if bit else v


def all_gather8(name, block):
    rows, w = block.shape

    def body(x_ref, out_ref, send_sems, recv_sems, local_sem):
        x, y, c = _my_place()
        me = 4 * x + 2 * y + c
        mine = pltpu.make_async_copy(x_ref, out_ref.at[me], local_sem)
        mine.start()
        sends = []
        for k in range(1, N_DEV):
            peer = (_flip(x, k & 4), _flip(y, k & 2), _flip(c, k & 1))
            cp = pltpu.make_async_remote_copy(src_ref=x_ref, dst_ref=out_ref.at[me], send_sem=send_sems.at[k - 1],
                                              recv_sem=recv_sems.at[k - 1], device_id=peer, device_id_type=MESH)
            cp.start()
            sends.append(cp)
        for k in range(1, N_DEV):
            peer = (_flip(x, k & 4), _flip(y, k & 2), _flip(c, k & 1))
            slot = 4 * peer[0] + 2 * peer[1] + peer[2]
            pltpu.make_async_remote_copy(src_ref=x_ref, dst_ref=out_ref.at[slot], send_sem=send_sems.at[k - 1],
                                         recv_sem=recv_sems.at[k - 1], device_id=peer, device_id_type=MESH).wait_recv()
        for cp in sends:
            cp.wait_send()
        mine.wait()

    return pl.pallas_call(
        body, name=name, out_shape=jax.ShapeDtypeStruct((N_DEV, rows, w), block.dtype),
        in_specs=[pl.BlockSpec(memory_space=pl.ANY)], out_specs=pl.BlockSpec(memory_space=pl.ANY),
        scratch_shapes=[pltpu.SemaphoreType.DMA((N_DEV - 1,)), pltpu.SemaphoreType.DMA((N_DEV - 1,)),
                        pltpu.SemaphoreType.DMA],
    )(block)


class _Comm:
    def __init__(self, ins, outs, sems, start, finish):
        self.ins, self.outs, self.sems, self.start, self.finish = list(ins), list(outs), list(sems), start, finish


def run_comm(name, comm):
    def body(*refs):
        n_i, n_o = len(comm.ins), len(comm.outs)
        comm.start(refs[:n_i], refs[n_i:n_i + n_o], refs[n_i + n_o:])
        comm.finish(refs[:n_i], refs[n_i:n_i + n_o], refs[n_i + n_o:])

    any_spec = pl.BlockSpec(memory_space=pl.ANY)
    return list(pl.pallas_call(body, name=name, out_shape=comm.outs, in_specs=[any_spec] * len(comm.ins),
                               out_specs=[any_spec] * len(comm.outs), scratch_shapes=comm.sems)(*comm.ins))


def gather_plan(shards, n_split):
    n = len(shards)

    def part(ref, t, core):
        if t >= n_split:
            return ref
        h = shards[t].shape[0] // 2
        return ref.at[pl.ds(core * h, h)]

    def ici(srcs, dsts, sems, k, t, slot, place):
        x, y, c = place
        return pltpu.make_async_remote_copy(
            src_ref=part(srcs[t], t, c), dst_ref=part(dsts[t].at[slot], t, c), send_sem=sems[0].at[t * 3 + k - 1],
            recv_sem=sems[1].at[t * 3 + k - 1], device_id=(_flip(x, k & 2), _flip(y, k & 1), c), device_id_type=MESH)

    def handed(dsts, sems, k, t, slot, place, core):
        x, y, c = place
        half = part(dsts[t].at[slot], t, core)
        return pltpu.make_async_remote_copy(
            src_ref=half, dst_ref=half, send_sem=sems[2].at[t * 3 + k - 1], recv_sem=sems[3].at[t * 3 + k - 1],
            device_id=(x, y, 1 - c), device_id_type=MESH)

    def start(srcs, dsts, sems):
        place = _my_place()
        me = 2 * place[0] + place[1]
        for t in range(n):
            pltpu.make_async_copy(srcs[t], dsts[t].at[me], sems[4].at[t]).start()
        for k in range(1, N_CHIPS):
            for t in range(n):
                ici(srcs, dsts, sems, k, t, me, place).start()

    def finish(srcs, dsts, sems):
        place = _my_place()
        x, y, c = place
        me = 2 * x + y
        for k in range(1, N_CHIPS):
            slot = 2 * _flip(x, k & 2) + _flip(y, k & 1)
            for t in range(n):
                ici(srcs, dsts, sems, k, t, slot, place).wait_recv()
                if t < n_split:
                    handed(dsts, sems, k, t, slot, place, c).start()
        for k in range(1, N_CHIPS):
            slot = 2 * _flip(x, k & 2) + _flip(y, k & 1)
            for t in range(n_split):
                handed(dsts, sems, k, t, slot, place, 1 - c).wait_recv()
        for k in range(1, N_CHIPS):
            slot = 2 * _flip(x, k & 2) + _flip(y, k & 1)
            for t in range(n):
                ici(srcs, dsts, sems, k, t, me, place).wait_send()
                if t < n_split:
                    handed(dsts, sems, k, t, slot, place, c).wait_send()
        for t in range(n):
            pltpu.make_async_copy(srcs[t], dsts[t].at[me], sems[4].at[t]).wait()

    outs = [jax.ShapeDtypeStruct((N_CHIPS,) + s.shape, s.dtype) for s in shards]
    sems = [pltpu.SemaphoreType.DMA((3 * n,))] * 4 + [pltpu.SemaphoreType.DMA((n,))]
    return _Comm(shards, outs, sems, start, finish)


def scatter_plan(triples):
    n = len(triples)
    bufs = [b for tr in triples for b in tr]

    def copies(srcs, dsts, sems):
        x, y, c = _my_place()
        return [pltpu.make_async_remote_copy(
            src_ref=srcs[t * 3 + k - 1], dst_ref=dsts[t * 3 + k - 1], send_sem=sems[0].at[t * 3 + k - 1],
            recv_sem=sems[1].at[t * 3 + k - 1], device_id=(_flip(x, k & 2), _flip(y, k & 1), c), device_id_type=MESH)
            for k in range(1, N_CHIPS) for t in range(n)]

    def start(srcs, dsts, sems):
        for cp in copies(srcs, dsts, sems):
            cp.start()

    def finish(srcs, dsts, sems):
        for cp in copies(srcs, dsts, sems):
            cp.wait_recv()
        for cp in copies(srcs, dsts, sems):
            cp.wait_send()

    return _Comm(bufs, [jax.ShapeDtypeStruct(b.shape, b.dtype) for b in bufs],
                 [pltpu.SemaphoreType.DMA((3 * n,)), pltpu.SemaphoreType.DMA((3 * n,))], start, finish)


def swap_plan(bufs):
    n = len(bufs)

    def copies(srcs, dsts, sems):
        x, y, c = _my_place()
        return [pltpu.make_async_remote_copy(src_ref=srcs[t], dst_ref=dsts[t], send_sem=sems[0].at[t],
                                             recv_sem=sems[1].at[t], device_id=(x, y, 1 - c), device_id_type=MESH)
                for t in range(n)]

    def start(srcs, dsts, sems):
        for cp in copies(srcs, dsts, sems):
            cp.start()

    def finish(srcs, dsts, sems):
        for cp in copies(srcs, dsts, sems):
            cp.wait_recv()
        for cp in copies(srcs, dsts, sems):
            cp.wait_send()

    return _Comm(bufs, [jax.ShapeDtypeStruct(b.shape, b.dtype) for b in bufs],
                 [pltpu.SemaphoreType.DMA((n,)), pltpu.SemaphoreType.DMA((n,))], start, finish)


def _vec8(*rows):
    w = rows[0].shape[-1]
    out = jnp.zeros((SUBLANES, w), F32)
    for r, v in enumerate(rows):
        out = out.at[r].set(v.reshape(w).astype(F32))
    return out


def _pack(arrays):
    parts = []
    for a in arrays:
        flat = a.reshape(-1).astype(F32)
        pad = (-flat.shape[0]) % PACK_W
        parts.append(jnp.pad(flat, (0, pad)))
    flat = jnp.concatenate(parts)
    pad = (-flat.shape[0]) % (PACK_W * SUBLANES)
    return jnp.pad(flat, (0, pad)).reshape(-1, PACK_W)


def _unpack(packed, shapes):
    flat = packed.reshape(-1)
    out, pos = [], 0
    for shp in shapes:
        size = 1
        for d in shp:
            size *= d
        out.append(flat[pos:pos + size].reshape(shp))
        pos += size + (-size) % PACK_W
    return out


def _rows2d(a):
    return a.reshape(-1, a.shape[-1])


class _Cfg:
    pass


def _adam(name, w, grads, m, v, half_rows=None, core=None):
    rows, width = w.shape
    c1 = 1.0 - ADAM_B1 ** ADAM_STEP
    c2 = 1.0 - ADAM_B2 ** ADAM_STEP

    def update(wv, mv, vv, g):
        m_new = ADAM_B1 * mv + (1.0 - ADAM_B1) * g
        v_new = ADAM_B2 * vv + (1.0 - ADAM_B2) * (g * g)
        delta = -ADAM_LR * ((m_new / c1) / (jnp.sqrt(v_new / c2) + ADAM_EPS) + ADAM_WD * wv)
        return [g, delta, m_new, v_new], []

    if half_rows is None:
        tm = _pick(rows, (128, 64, 32, 16, 8))

        def body(i, j, rows_, vecs):
            g = rows_[3]
            for extra in rows_[4:]:
                g = g + extra
            return update(rows_[0], rows_[1], rows_[2], g)

        return ew(name, body, rows // tm, 1, tm, width, [(w, 0, 0), (m, 0, 0), (v, 0, 0)] + [(g, 0, 0) for g in grads], [],
                  [(F32, rows, width)] * 4)

    tm = _pick(half_rows, (128, 64, 32, 16, 8))
    nb = half_rows // tm
    run_tile = lambda i: (i // (2 * nb)) * nb + i % nb
    core_vec = jnp.broadcast_to(core.astype(F32), (SUBLANES, width))

    def body(i, j, rows_, vecs):
        owner = ((i // nb) % 2).astype(F32)
        g = jnp.where(vecs[0][0:1, :] == owner, rows_[3], rows_[4])
        return update(rows_[0], rows_[1], rows_[2], g)

    return ew(name, body, rows // tm, 1, tm, width,
              [(w, 0, 0), (m, 0, 0), (v, 0, 0), (grads[0], run_tile, 0), (grads[1], run_tile, 0)], [(core_vec, 0)],
              [(F32, rows, width)] * 4, pass_i=True)


def _sum_rows(name, arr, nparts, rows, width, dtype=F32):
    tm = _pick(rows, (256, 128, 64, 32, 16, 8))
    nblk = rows // tm

    def body(i, j, rows_, vecs):
        acc = rows_[0].astype(F32)
        for r in rows_[1:]:
            acc = acc + r.astype(F32)
        return [acc], []

    return ew(name, body, nblk, 1, tm, width, [(arr, p * nblk, 0) for p in range(nparts)], [], [(dtype, rows, width)])[0]


def kernel(x, c, ctx, c_ctx, w_ada, b_ada, w_in, w_decay_up, b_decay_up, gla_norm_gain, w_pool_group, pool_scale, w_gla_out, w_pool_out, w_out, ln_mix_gain, ln_mix_bias, w_ffn_in, w_ffn_out, ln_ffn_gain, ln_ffn_bias, loss_target, m_c_ctx, m_w_ada, m_b_ada, m_w_in, m_w_decay_up, m_b_decay_up, m_gla_norm_gain, m_w_pool_group, m_pool_scale, m_w_gla_out, m_w_pool_out, m_w_out, m_ln_mix_gain, m_ln_mix_bias, m_w_ffn_in, m_w_ffn_out, m_ln_ffn_gain, m_ln_ffn_bias, v_c_ctx, v_w_ada, v_b_ada, v_w_in, v_w_decay_up, v_b_decay_up, v_gla_norm_gain, v_w_pool_group, v_pool_scale, v_w_gla_out, v_w_pool_out, v_w_out, v_ln_mix_gain, v_ln_mix_bias, v_w_ffn_in, v_w_ffn_out, v_ln_ffn_gain, v_ln_ffn_bias):
    weights = dict(c_ctx=c_ctx, w_ada=w_ada, b_ada=b_ada, w_in=w_in, w_decay_up=w_decay_up, b_decay_up=b_decay_up,
                   gla_norm_gain=gla_norm_gain, w_pool_group=w_pool_group, pool_scale=pool_scale, w_gla_out=w_gla_out,
                   w_pool_out=w_pool_out, w_out=w_out, ln_mix_gain=ln_mix_gain, ln_mix_bias=ln_mix_bias,
                   w_ffn_in=w_ffn_in, w_ffn_out=w_ffn_out, ln_ffn_gain=ln_ffn_gain, ln_ffn_bias=ln_ffn_bias)
    mom1 = dict(c_ctx=m_c_ctx, w_ada=m_w_ada, b_ada=m_b_ada, w_in=m_w_in, w_decay_up=m_w_decay_up, b_decay_up=m_b_decay_up,
                gla_norm_gain=m_gla_norm_gain, w_pool_group=m_w_pool_group, pool_scale=m_pool_scale, w_gla_out=m_w_gla_out,
                w_pool_out=m_w_pool_out, w_out=m_w_out, ln_mix_gain=m_ln_mix_gain, ln_mix_bias=m_ln_mix_bias,
                w_ffn_in=m_w_ffn_in, w_ffn_out=m_w_ffn_out, ln_ffn_gain=m_ln_ffn_gain, ln_ffn_bias=m_ln_ffn_bias)
    mom2 = dict(c_ctx=v_c_ctx, w_ada=v_w_ada, b_ada=v_b_ada, w_in=v_w_in, w_decay_up=v_w_decay_up, b_decay_up=v_b_decay_up,
                gla_norm_gain=v_gla_norm_gain, w_pool_group=v_w_pool_group, pool_scale=v_pool_scale, w_gla_out=v_w_gla_out,
                w_pool_out=v_w_pool_out, w_out=v_w_out, ln_mix_gain=v_ln_mix_gain, ln_mix_bias=v_ln_mix_bias,
                w_ffn_in=v_w_ffn_in, w_ffn_out=v_w_ffn_out, ln_ffn_gain=v_ln_ffn_gain, ln_ffn_bias=v_ln_ffn_bias)
    names = list(weights)

    cfg = _Cfg()
    L, D = w_ada.shape[0], x.shape[-1]
    S, CL = x.shape[1], ctx.shape[1]
    cfg.L, cfg.D, cfg.S, cfg.CL, cfg.R, cfg.TM = L, D, S, CL, S + CL, CL
    DK, DV, DP = D // 2, D, D // 2
    cfg.DK, cfg.DV, cfg.DP = DK, DV, DP
    cfg.HK, cfg.HV, cfg.PG = DK // N_HEADS, DV // N_HEADS, DP // N_POOL
    DFF = w_ffn_out.shape[1] * N_CHIPS
    NP = 2 * DK + 2 * DV + DP + 2 * D + ALR_W
    cfg.NP, cfg.DFF = NP, DFF
    R, TM, HK, HV, PG = cfg.R, cfg.TM, cfg.HK, cfg.HV, cfg.PG
    MT = R // TM
    alpha = (2.0 * L) ** 0.25
    assert S % TM == 0 and TM % CHUNK == 0 and S % GRID_W == 0 and TM % GRID_W == 0
    OFF_G, OFF_P, OFF_BGA, OFF_BGB = 2 * DK + DV, 2 * DK + 2 * DV, 2 * DK + 2 * DV + DP, 2 * DK + 2 * DV + DP + D
    ALR0 = 2 * DK + 2 * DV
    TE = 512
    TL = TM // 2
    assert D % TE == 0 and DP % TE == 0

    xi, yi, ci = _my_place()
    chip = 2 * xi + yi
    dev = 4 * xi + 2 * yi + ci

    n_ada = w_ada.shape[-1]
    c_all = all_gather8("ag_cond", jnp.pad(c.reshape(1, D), ((0, SUBLANES - 1), (0, 0))))[:, 0, :]
    cond = jnp.concatenate([c_all, c_ctx.reshape(1, D), jnp.zeros((16 - N_DEV - 1, D), F32)], axis=0)

    def silu_body(i, j, rows_, vecs):
        return [_silu(rows_[0]), _dsilu(rows_[0])], []

    act, dact = ew("cond_silu", silu_body, 1, 1, 16, D, [(cond, 0, 0)], [], [(F32, 16, D)] * 2)
    w_ada2 = w_ada.reshape(L * D, n_ada)
    b_ada_mine = lax.dynamic_slice_in_dim(b_ada, chip * n_ada, n_ada, axis=1)
    tn_ada = _pick(n_ada, (1024, 512, 256, 128))
    mods = [matmul("ada_fwd", act, w_ada2, 'nn', 16, n_ada, D, tm=16, tn=tn_ada, b_off=(l, 0),
                   bias=_vec8(b_ada_mine[l]), precise=True) for l in range(L)]
    mods_all = all_gather8("ag_mods", jnp.concatenate(mods, axis=0))
    mods_all = mods_all[0::2].reshape(N_CHIPS, L, 16, n_ada).transpose(1, 2, 0, 3).reshape(L, 16, N_MOD * D)
    modv = [_vec8(mods_all[l, N_DEV], lax.dynamic_index_in_dim(mods_all[l], dev, 0, keepdims=False)) for l in range(L)]
    MB = D // TE

    big = ['w_in', 'w_gla_out', 'w_pool_out', 'w_out', 'w_ffn_in', 'w_ffn_out', 'w_pool_group']
    hosts = [['w_in'], ['w_gla_out', 'w_out', 'w_pool_out', 'w_pool_group'], ['w_ffn_out'], ['w_ffn_in']]
    all_hosted = sum(hosts, [])
    layer_shards = lambda nms, l: [weights[nm][l].astype(WIRE_DTYPE) for nm in nms]

    def cols_together(g):
        return jnp.moveaxis(g, 0, -2).reshape(g.shape[1:-1] + (N_CHIPS * g.shape[-1],))

    def rows_together(g):
        return g.reshape((N_CHIPS * g.shape[1], g.shape[2]))

    te_ff = _pick(DFF, (1408, 1024, 512, 256, 128))

    def assemble(gw):
        win_ref = cols_together(gw['w_in'])
        win = jnp.concatenate([win_ref[:, :ALR0], win_ref[:, ALR0 + 2 * GATE_RANK:], win_ref[:, ALR0:ALR0 + 2 * GATE_RANK],
                               jnp.zeros((D, ALR_W - 2 * GATE_RANK), WIRE_DTYPE)], axis=-1)
        wpg = jnp.moveaxis(gw['w_pool_group'], 0, 1).reshape(N_POOL, PG, PG)
        wpg_bd = jnp.zeros((N_POOL, PG, N_POOL, PG), WIRE_DTYPE)
        for g in range(N_POOL):
            wpg_bd = wpg_bd.at[g, :, g, :].set(wpg[g])
        win_bwd = jnp.pad(win_ref, ((0, 0), (0, NP - win_ref.shape[1])))
        return dict(win=win, win_bwd=win_bwd, wgo=rows_together(gw['w_gla_out']), wpo=cols_together(gw['w_pool_out']),
                    wout=rows_together(gw['w_out']), wfi=cols_together(gw['w_ffn_in']), wfo=rows_together(gw['w_ffn_out']),
                    wpg_bd=wpg_bd.reshape(DP, DP))

    first = run_comm("gather_weights", gather_plan(layer_shards(all_hosted, 0) + [w_decay_up, b_decay_up],
                                                   n_split=len(big)))
    W = [assemble(dict(zip(all_hosted, first[:len(big)])))] + [None] * (L - 1)
    next_plan = lambda l, k: gather_plan(layer_shards(hosts[k], l + 1), n_split=len(hosts[k]))
    wdu = cols_together(first[len(big)])
    bdu = cols_together(first[len(big) + 1])
    wd_pad = [[jnp.zeros((ALR_W, DK), F32).at[d * GATE_RANK:(d + 1) * GATE_RANK].set(wdu[l, d]) for d in range(2)]
              for l in range(L)]
    bd_pad = [[_vec8(bdu[l, d]) for d in range(2)] for l in range(L)]

    tn_np = _pick(NP, (1280, 1024, 768, 512, 256, 128))
    tn_d = _pick(D, (1024, 512, 256, 128))
    tn_ff2 = _pick(2 * DFF, (1024, 512, 256, 128))
    tn_ff = _pick(DFF, (2816, 1408, 1024, 512, 256, 128))
    tn_dp = _pick(DP, (1024, 512, 256, 128))
    tw = lambda n_: _pick(n_, (512, 256, 128))

    gain8 = lambda v_: _vec8(v_)

    xs = jnp.concatenate([ctx.reshape(CL, D), x.reshape(S, D)], axis=0)
    saved = []
    for l in range(L):
        sv = {}
        mv = modv[l]

        def mod_body(i, j, rows_, vecs):
            return [rows_[0] * (1.0 + _cls(vecs[1], i)) + _cls(vecs[0], i)], []

        h1 = ew("modulate", mod_body, MT, D // tn_d, TM, tn_d, [(xs, 0, 0)],
                [(mv, 0 * (D // tn_d)), (mv, 1 * (D // tn_d))], [(MXU_DTYPE, R, D)])[0]
        Wl = W[l]
        if l + 1 < L:
            proj, got0 = matmul("proj_and_gather", h1, Wl['win'], 'nn', R, NP, D, tm=TM, tn=tn_np, comm=next_plan(l, 0))
            o_f, st_f, got1 = gla_fwd("gla_fwd_and_gather", proj, wd_pad[l][0], bd_pad[l][0], cfg, False, next_plan(l, 1))
            o_b, st_b, got2 = gla_fwd("gla_rev_and_gather", proj, wd_pad[l][1], bd_pad[l][1], cfg, True, next_plan(l, 2))
        else:
            proj = matmul("proj", h1, Wl['win'], 'nn', R, NP, D, tm=TM, tn=tn_np)
            o_f, st_f, _ = gla_fwd("gla_fwd", proj, wd_pad[l][0], bd_pad[l][0], cfg, False)
            o_b, st_b, _ = gla_fwd("gla_rev", proj, wd_pad[l][1], bd_pad[l][1], cfg, True)

        def post_body(i, j, rows_, vecs):
            o = rows_[0] + rows_[1]
            on = o * lax.rsqrt(jnp.mean(o * o, axis=-1, keepdims=True) + RMS_EPS)
            return [on * vecs[0][0:1, :] * _silu(rows_[2])], []

        a_gla = ew("gla_post", post_body, MT, N_HEADS, TM, HV, [(o_f, 0, 0), (o_b, 0, 0), (proj, 0, OFF_G // HV)],
                   [(gain8(gla_norm_gain[l]), 0)], [(MXU_DTYPE, R, DV)])[0]
        y_gla = matmul("gla_out", a_gla, Wl['wgo'], 'nn', R, D, DV, tm=TM, tn=tn_d)
        u_pool = pool_mix("pool_fwd", proj, OFF_P // PG, cfg, False)
        t_pool = matmul("pool_group", u_pool, Wl['wpg_bd'], 'nn', R, DP, DP, tm=TM, tn=tn_dp)

        def scale_body(i, j, rows_, vecs):
            return [rows_[0] * vecs[0][0:1, :]], []

        ts_pool = ew("pool_scale", scale_body, MT, DP // TE, TM, TE, [(t_pool, 0, 0)], [(gain8(pool_scale[l]), 0)],
                     [(MXU_DTYPE, R, DP)])[0]
        y_pool = matmul("pool_out", ts_pool, Wl['wpo'], 'nn', R, D, DP, tm=TM, tn=tn_d)

        def merge_body(i, j, rows_, vecs):
            return [_sigmoid(rows_[2]) * rows_[0] + _sigmoid(rows_[3]) * rows_[1]], []

        mg = ew("merge", merge_body, MT, D // TE, TM, TE,
                [(y_gla, 0, 0), (y_pool, 0, 0), (proj, 0, OFF_BGA // TE), (proj, 0, OFF_BGB // TE)], [],
                [(MXU_DTYPE, R, D)])[0]
        y_mix = matmul("mix_out", mg, Wl['wout'], 'nn', R, D, D, tm=TM, tn=tn_d)

        def ln_body(i, j, rows_, vecs):
            r = alpha * rows_[0] + _cls(vecs[0], i) * rows_[1]
            mu = jnp.mean(r, axis=-1, keepdims=True)
            rc = r - mu
            var = jnp.mean(rc * rc, axis=-1, keepdims=True)
            return [rc * lax.rsqrt(var + LN_EPS) * vecs[1][0:1, :] + vecs[2][0:1, :]], []

        x1 = ew("resid_ln", ln_body, R // TL, 1, TL, D, [(xs, 0, 0), (y_mix, 0, 0)],
                [(mv, 2), (gain8(ln_mix_gain[l]), 0), (gain8(ln_mix_bias[l]), 0)], [(F32, R, D)], ctx_tiles=CL // TL)[0]
        h2 = ew("modulate", mod_body, MT, D // tn_d, TM, tn_d, [(x1, 0, 0)],
                [(mv, 3 * (D // tn_d)), (mv, 4 * (D // tn_d))], [(MXU_DTYPE, R, D)])[0]
        if l + 1 < L:
            u_ffn, got3 = matmul("ffn_in_and_gather", h2, Wl['wfi'], 'nn', R, 2 * DFF, D, tm=TM, tn=tn_ff2,
                                 comm=next_plan(l, 3))
            W[l + 1] = assemble(dict(zip(all_hosted, got0 + got1 + got2 + got3)))
        else:
            u_ffn = matmul("ffn_in", h2, Wl['wfi'], 'nn', R, 2 * DFF, D, tm=TM, tn=tn_ff2)
        s_ffn = swiglu("swiglu", u_ffn, MT, DFF // te_ff, TM, te_ff)
        f_ffn = matmul("ffn_out", s_ffn, Wl['wfo'], 'nn', R, D, DFF, tm=TM, tn=tn_d)
        x2 = ew("resid_ln", ln_body, R // TL, 1, TL, D, [(x1, 0, 0), (f_ffn, 0, 0)],
                [(mv, 5), (gain8(ln_ffn_gain[l]), 0), (gain8(ln_ffn_bias[l]), 0)], [(F32, R, D)], ctx_tiles=CL // TL)[0]
        sv.update(xs=xs, h1=h1, proj=proj, o_f=o_f, o_b=o_b, st_f=st_f, st_b=st_b, a_gla=a_gla, y_gla=y_gla, u_pool=u_pool,
                  t_pool=t_pool, ts_pool=ts_pool, y_pool=y_pool, mg=mg, y_mix=y_mix, x1=x1, h2=h2, u_ffn=u_ffn, s_ffn=s_ffn,
                  f_ffn=f_ffn)
        saved.append(sv)
        xs = x2

    tgt = loss_target.reshape(S, D)

    def loss_body(i, j, rows_, vecs):
        d = jnp.where(i, 0.0, rows_[0] - rows_[1])
        return [d * (1.0 / D)], [_colsum(d * d)]

    d_x, sq = ew("loss", loss_body, MT, 1, TM, D, [(xs, 0, 0), (tgt, lambda i: jnp.maximum(i - 1, 0), 0)], [],
                 [(F32, R, D)], [(D, False)])

    def total_body(i, j, rows_, vecs):
        return [jnp.broadcast_to(jnp.sum(rows_[0], axis=-1, keepdims=True), (SUBLANES, D)) * (0.5 / D)], []

    loss_local = ew("loss_total", total_body, 1, 1, SUBLANES, D, [(sq, 0, 0)], [], [(F32, SUBLANES, D)])[0][0, 0]
    loss = lax.psum(loss_local, ("x", "y", "c"))

    gsmall = {nm: [None] * L for nm in ['gla_norm_gain', 'pool_scale', 'ln_mix_gain', 'ln_mix_bias', 'ln_ffn_gain',
                                        'ln_ffn_bias', 'w_decay_up', 'b_decay_up']}
    gbig = {nm: [None] * L for nm in big}
    dmod = [None] * L

    def ln_bwd(name, x_in, br, d_out, mv, gt_blk, gain):
        def body(i, j, rows_, vecs):
            gt = _cls(vecs[0], i)
            r = alpha * rows_[0] + gt * rows_[1]
            mu = jnp.mean(r, axis=-1, keepdims=True)
            rc = r - mu
            rstd = lax.rsqrt(jnp.mean(rc * rc, axis=-1, keepdims=True) + LN_EPS)
            xhat = rc * rstd
            dxh = rows_[2] * vecs[1][0:1, :]
            dr = rstd * (dxh - jnp.mean(dxh, axis=-1, keepdims=True) - xhat * jnp.mean(dxh * xhat, axis=-1, keepdims=True))
            return [dr, gt * dr], [_colsum(rows_[2] * xhat), _colsum(rows_[2]), _colsum(dr * rows_[1])]

        return ew(name, body, R // TL, 1, TL, D, [(x_in, 0, 0), (br, 0, 0), (d_out, 0, 0)], [(mv, gt_blk), (gain8(gain), 0)],
                  [(F32, R, D), (MXU_DTYPE, R, D)], [(D, False), (D, False), (D, True)], ctx_tiles=CL // TL)

    def mod_bwd(name, d_h, x_in, d_r, mv, sc_blk):
        def body(i, j, rows_, vecs):
            return ([rows_[0] * (1.0 + _cls(vecs[0], i)) + alpha * rows_[2]],
                    [_colsum(rows_[0] * rows_[1]), _colsum(rows_[0])])

        return ew(name, body, MT, D // tn_d, TM, tn_d, [(d_h, 0, 0), (x_in, 0, 0), (d_r, 0, 0)],
                  [(mv, sc_blk * (D // tn_d))], [(F32, R, D)], [(D, True), (D, True)])

    shard_axis = {'w_in': 1, 'w_gla_out': 0, 'w_pool_out': 1, 'w_out': 0, 'w_ffn_in': 1, 'w_ffn_out': 0, 'w_pool_group': 1}
    scatter_groups = [['w_in'], ['w_ffn_in'], ['w_gla_out', 'w_pool_out', 'w_out', 'w_ffn_out', 'w_pool_group']]

    def cut_block(nm, g, which, core, dtype):
        ax = shard_axis[nm]
        width = weights[nm].shape[ax + 1]
        starts, sizes = [0] * g.ndim, list(g.shape)
        if ax == 0:
            sizes[0] = width // 2
            starts[0] = which * width + core * sizes[0]
        else:
            sizes[0], sizes[ax] = g.shape[0] // 2, width
            starts[0], starts[ax] = core * sizes[0], which * width
        return lax.dynamic_slice(g, starts, sizes).astype(dtype)

    def cut_layer(l):
        cut = dict(keep_r={}, give_r={}, keep_o={}, give_o={})
        for nm in big:
            g = gbig[nm][l]
            if nm == 'w_ffn_in':
                assert 2 * weights[nm].shape[2] == DFF and N_CHIPS == 4
                blk = lambda which, core, dt: jnp.where(which < 2, cut_block(nm, g[0], which % 2, core, dt),
                                                        cut_block(nm, g[1], which % 2, core, dt))
            else:
                blk = lambda which, core, dt: cut_block(nm, g, which, core, dt)
            others = [jnp.bitwise_xor(chip, k) for k in range(1, N_CHIPS)]
            cut['keep_r'][nm] = [blk(o_, ci, WIRE_DTYPE) for o_ in others]
            cut['give_r'][nm] = [blk(o_, 1 - ci, WIRE_DTYPE) for o_ in others]
            cut['keep_o'][nm], cut['give_o'][nm] = blk(chip, ci, F32), blk(chip, 1 - ci, F32)
        return cut

    def swap_partials_plan(cut):
        return swap_plan([a_ for nm in big for a_ in cut['give_r'][nm]] + [cut['give_o'][nm] for nm in big])

    def add_partials(cut, got):
        cut['got_o'] = dict(zip(big, got[3 * len(big):]))
        cut['send_r'] = {}
        for t, nm in enumerate(big):
            kept = [_rows2d(a_) for a_ in cut['keep_r'][nm]]
            theirs = [_rows2d(a_) for a_ in got[3 * t:3 * t + 3]]
            rows_, width = kept[0].shape
            tm_ = _pick(rows_, (256, 128, 64, 32, 16, 8))

            def add2_body(i, j, rows__, vecs):
                return [rows__[k].astype(F32) + rows__[3 + k].astype(F32) for k in range(3)], []

            cut['send_r'][nm] = ew("add_partials", add2_body, rows_ // tm_, 1, tm_, width,
                                   [(a_, 0, 0) for a_ in kept + theirs], [], [(WIRE_DTYPE, rows_, width)] * 3)
        cut['recvd'] = {}

    def sum_grads(cut):
        done = {}
        for nm in big:
            a2, b2 = _rows2d(cut['keep_o'][nm]), _rows2d(cut['got_o'][nm])
            rows_, width = a2.shape
            tm_ = _pick(rows_, (256, 128, 64, 32, 16, 8))

            def psum_body(i, j, rows__, vecs):
                return [(rows__[0] + rows__[1]) + rows__[2].astype(F32) + rows__[3].astype(F32) + rows__[4].astype(F32)], []

            done[nm] = ew("sum_grads", psum_body, rows_ // tm_, 1, tm_, width,
                          [(a2, 0, 0), (b2, 0, 0)] + [(r_, 0, 0) for r_ in cut['recvd'][nm]], [], [(F32, rows_, width)])[0]
        return done

    def scatter_group_plan(cut, group):
        return scatter_plan([cut['send_r'][nm] for nm in group])

    def received(cut, group, got):
        for t, nm in enumerate(group):
            cut['recvd'][nm] = got[3 * t:3 * t + 3]

    finished = [None] * L
    pending = None
    for l in reversed(range(L)):
        sv = saved[l]
        mv = modv[l]
        Wl = W[l]
        proj = sv['proj']
        dr2, d_f, g_gain, g_bias, g_gt_f = ln_bwd("ln_bwd", sv['x1'], sv['f_ffn'], d_x, mv, 5, ln_ffn_gain[l])
        gsmall['ln_ffn_gain'][l], gsmall['ln_ffn_bias'][l] = g_gain[0], g_bias[0]
        if pending is not None:
            d_s, got = matmul("ffn_out_dx_and_swap", d_f, Wl['wfo'], 'nt', R, DFF, D, tm=TM, tn=tn_ff,
                              comm=swap_partials_plan(pending))
            add_partials(pending, got)
        else:
            d_s = matmul("ffn_out_dx", d_f, Wl['wfo'], 'nt', R, DFF, D, tm=TM, tn=tn_ff)
        gbig['w_ffn_out'][l] = matmul("ffn_out_dw", sv['s_ffn'], d_f, 'tn', DFF, D, R, tm=tw(DFF), tn=tn_d)
        d_gate, d_up = swiglu_bwd("swiglu_bwd", d_s, sv['u_ffn'], MT, DFF // te_ff, TM, te_ff)
        tn_x = 512
        if pending is not None:
            d_h2, got = matmul("ffn_in_dx_and_scatter", d_gate, Wl['wfi'], 'nt', R, D, DFF, tm=TM, tn=tn_x,
                               second=(d_up, (0, 1)), comm=scatter_group_plan(pending, scatter_groups[0]))
            received(pending, scatter_groups[0], got)
        else:
            d_h2 = matmul("ffn_in_dx", d_gate, Wl['wfi'], 'nt', R, D, DFF, tm=TM, tn=tn_x, second=(d_up, (0, 1)))
        gbig['w_ffn_in'][l] = (matmul("ffn_in_dw", sv['h2'], d_gate, 'tn', D, DFF, R, tm=tw(D), tn=te_ff),
                               matmul("ffn_in_dw_up", sv['h2'], d_up, 'tn', D, DFF, R, tm=tw(D), tn=te_ff))
        d_x1, g_sc_f, g_sh_f = mod_bwd("mod_bwd", d_h2, sv['x1'], dr2, mv, 4)
        dr1, d_y, g_gain, g_bias, g_gt_m = ln_bwd("ln_bwd", sv['xs'], sv['y_mix'], d_x1, mv, 2, ln_mix_gain[l])
        gsmall['ln_mix_gain'][l], gsmall['ln_mix_bias'][l] = g_gain[0], g_bias[0]
        d_mg = matmul("mix_out_dx", d_y, Wl['wout'], 'nt', R, D, D, tm=TM, tn=tn_d)
        gbig['w_out'][l] = matmul("mix_out_dw", sv['mg'], d_y, 'tn', D, D, R, tm=tw(D), tn=tn_d)

        def merge_bwd_body(i, j, rows_, vecs):
            d_m, yg, yp, ba, bb = rows_
            sa, sb = _sigmoid(ba), _sigmoid(bb)
            return [d_m * sa, d_m * sb, d_m * yg * sa * (1.0 - sa), d_m * yp * sb * (1.0 - sb)], []

        d_yg, d_yp, d_bga, d_bgb = ew(
            "merge_bwd", merge_bwd_body, MT, D // TE, TM, TE,
            [(d_mg, 0, 0), (sv['y_gla'], 0, 0), (sv['y_pool'], 0, 0), (proj, 0, OFF_BGA // TE), (proj, 0, OFF_BGB // TE)], [],
            [(MXU_DTYPE, R, D), (MXU_DTYPE, R, D), (F32, R, D), (F32, R, D)])
        d_ts = matmul("pool_out_dx", d_yp, Wl['wpo'], 'nt', R, DP, D, tm=TM, tn=tn_dp)
        gbig['w_pool_out'][l] = matmul("pool_out_dw", sv['ts_pool'], d_yp, 'tn', DP, D, R, tm=tw(DP), tn=tn_d)

        def scale_bwd_body(i, j, rows_, vecs):
            return [rows_[0] * vecs[0][0:1, :]], [_colsum(rows_[0] * rows_[1])]

        d_t, g_ps = ew("pool_scale_bwd", scale_bwd_body, MT, DP // TE, TM, TE, [(d_ts, 0, 0), (sv['t_pool'], 0, 0)],
                       [(gain8(pool_scale[l]), 0)], [(MXU_DTYPE, R, DP)], [(DP, False)])
        gsmall['pool_scale'][l] = g_ps[0]
        d_u_pool = matmul("pool_group_dx", d_t, Wl['wpg_bd'], 'nt', R, DP, DP, tm=TM, tn=tn_dp)
        g_bd = matmul("pool_group_dw", sv['u_pool'], d_t, 'tn', DP, DP, R, tm=tw(DP), tn=tn_dp)
        gbig['w_pool_group'][l] = jnp.stack([g_bd[g * PG:(g + 1) * PG, g * PG:(g + 1) * PG] for g in range(N_POOL)])
        d_p = pool_mix("pool_bwd", d_u_pool, 0, cfg, True)
        d_a = matmul("gla_out_dx", d_yg, Wl['wgo'], 'nt', R, DV, D, tm=TM, tn=tn_d)
        gbig['w_gla_out'][l] = matmul("gla_out_dw", sv['a_gla'], d_yg, 'tn', DV, D, R, tm=tw(DV), tn=tn_d)

        def post_bwd_body(i, j, rows_, vecs):
            d_a_, o_f_, o_b_, g_ = rows_
            gain = vecs[0][0:1, :]
            o = o_f_ + o_b_
            rstd = lax.rsqrt(jnp.mean(o * o, axis=-1, keepdims=True) + RMS_EPS)
            on = o * rstd
            sg = _silu(g_)
            d_on = d_a_ * gain * sg
            d_o_ = rstd * (d_on - on * jnp.mean(d_on * on, axis=-1, keepdims=True))
            return [d_o_, d_a_ * on * gain * _dsilu(g_)], [_colsum(d_a_ * on * sg)]

        d_o, d_g, g_gng = ew("gla_post_bwd", post_bwd_body, MT, N_HEADS, TM, HV,
                             [(d_a, 0, 0), (sv['o_f'], 0, 0), (sv['o_b'], 0, 0), (proj, 0, OFF_G // HV)],
                             [(gain8(gla_norm_gain[l]), 0)], [(F32, R, DV), (F32, R, DV)], [(DV, False)])
        gsmall['gla_norm_gain'][l] = g_gng[0]
        part = gla_bwd("gla_bwd", proj, wd_pad[l][0], bd_pad[l][0], sv['st_f'], d_o, cfg, False)
        full = gla_bwd("gla_rev_bwd", proj, wd_pad[l][1], bd_pad[l][1], sv['st_b'], d_o, cfg, True, addends=part[:4])
        d_q, d_k, d_v, d_alr = full[:4]
        g_wd = [res[4][d * GATE_RANK:(d + 1) * GATE_RANK] for d, res in enumerate((part, full))]
        g_bd_ = [res[5][0] for res in (part, full)]
        gsmall['w_decay_up'][l], gsmall['b_decay_up'][l] = jnp.stack(g_wd), jnp.stack(g_bd_)
        d_proj = jnp.concatenate([t_.astype(MXU_DTYPE) for t_ in (d_q, d_k, d_v, d_g, d_alr[:, :2 * GATE_RANK], d_p, d_bga, d_bgb,
                                                                   jnp.zeros((R, ALR_W - 2 * GATE_RANK), F32))], axis=-1)
        if pending is not None:
            d_h1, got = matmul("proj_dx_and_scatter", d_proj, Wl['win_bwd'], 'nt', R, D, NP, tm=TM, tn=tn_x,
                               comm=scatter_group_plan(pending, scatter_groups[2]))
            received(pending, scatter_groups[2], got)
            gbig['w_in'][l], got = matmul("proj_dw_and_scatter", sv['h1'], d_proj, 'tn', D, NP, R, tm=tw(D), tn=tn_np,
                                          comm=scatter_group_plan(pending, scatter_groups[1]))
            received(pending, scatter_groups[1], got)
            finished[l + 1] = sum_grads(pending)
        else:
            d_h1 = matmul("proj_dx", d_proj, Wl['win_bwd'], 'nt', R, D, NP, tm=TM, tn=tn_x)
            gbig['w_in'][l] = matmul("proj_dw", sv['h1'], d_proj, 'tn', D, NP, R, tm=tw(D), tn=tn_np)
        d_x, g_sc_m, g_sh_m = mod_bwd("mod_bwd", d_h1, sv['xs'], dr1, mv, 1)
        dmod[l] = jnp.concatenate([g_sh_m[:2], g_sc_m[:2], g_gt_m[:2], g_sh_f[:2], g_sc_f[:2], g_gt_f[:2]], axis=-1)
        pending = cut_layer(l)

    grad_x = d_x[CL:].reshape(x.shape)
    add_partials(pending, run_comm("swap_partials", swap_partials_plan(pending)))
    received(pending, big, run_comm("scatter_grads", scatter_group_plan(pending, big)))
    finished[0] = sum_grads(pending)

    dmod = jnp.stack(dmod)
    summed = [dmod[:, 0]] + [jnp.stack(gsmall[nm]) for nm in
                             ['ln_mix_gain', 'ln_mix_bias', 'ln_ffn_gain', 'ln_ffn_bias', 'gla_norm_gain', 'pool_scale',
                              'w_decay_up', 'b_decay_up']]
    pack = _pack([dmod[:, 1]] + summed)
    prow = pack.shape[0]
    packs = all_gather8("ag_small", pack)
    tot = _sum_rows("sum_small", packs.reshape(N_DEV * prow, PACK_W), N_DEV, prow, PACK_W)
    shapes = [(L, N_MOD * D)] + [a.shape for a in summed]
    tot = _unpack(tot, shapes)
    dmod_ctx = tot[1]
    g_rep = dict(zip(['ln_mix_gain', 'ln_mix_bias', 'ln_ffn_gain', 'ln_ffn_bias', 'gla_norm_gain', 'pool_scale'], tot[2:8]))
    g_wdu_full, g_bdu_full = tot[8], tot[9]
    dmod_lat = jnp.stack([_unpack(packs[d_], shapes[:1])[0] for d_ in range(N_DEV)], axis=1)
    dm_all = jnp.concatenate([dmod_lat, dmod_ctx[:, None, :], jnp.zeros((L, 16 - N_DEV - 1, N_MOD * D), F32)], axis=1)

    dm_mine = lax.dynamic_slice_in_dim(dm_all, chip * n_ada, n_ada, axis=2)
    g_w_ada = jnp.stack([matmul("ada_dw", act, dm_mine[l], 'tn', D, n_ada, 16, tm=tw(D), tn=tn_ada, precise=True)
                         for l in range(L)])

    def bsum_body(i, j, rows_, vecs):
        return [jnp.broadcast_to(_colsum(rows_[0]), rows_[0].shape)], []

    g_b_ada = jnp.stack([ew("ada_db", bsum_body, 1, 1, 16, N_MOD * D, [(dm_all[l], 0, 0)], [],
                            [(F32, 16, N_MOD * D)])[0][0] for l in range(L)])
    part_c = [matmul("ada_dc", dm_mine[l], w_ada2, 'nt', 16, D, n_ada, tm=16, tn=tn_d, b_off=(l * (D // tn_d), 0),
                     precise=True)
              for l in range(L)]
    parts_c = all_gather8("ag_dcond", jnp.concatenate(part_c, axis=0))
    dc_rows = parts_c[0::2].reshape(N_CHIPS * L * 16, D)

    def dc_body(i, j, rows_, vecs):
        acc = rows_[0]
        for r_ in rows_[1:-1]:
            acc = acc + r_
        return [acc * rows_[-1]], []

    g_c_ctx = ew("dcond", dc_body, 1, 1, 16, D, [(dc_rows, p_, 0) for p_ in range(N_CHIPS * L)] + [(dact, 0, 0)], [],
                 [(F32, 16, D)])[0][N_DEV]

    mine_all = [jnp.concatenate([finished[l][nm] for l in range(L)], axis=0) for nm in big]
    other = run_comm("swap_grads", swap_plan(mine_all))

    out_g, out_d, out_m, out_v = {}, {}, {}, {}
    for nm, mine, theirs in zip(big, mine_all, other):
        shp = weights[nm].shape
        half_rows = (shp[1] // 2) * (shp[2] if len(shp) == 4 else 1)
        res = _adam("adam_" + nm, _rows2d(weights[nm]), [mine, theirs], _rows2d(mom1[nm]), _rows2d(mom2[nm]),
                    half_rows=half_rows, core=ci)
        out_g[nm], out_d[nm], out_m[nm], out_v[nm] = [r_.reshape(shp) for r_ in res]
    res = _adam("adam_w_ada", _rows2d(w_ada), [_rows2d(g_w_ada)], _rows2d(m_w_ada), _rows2d(v_w_ada))
    out_g['w_ada'], out_d['w_ada'], out_m['w_ada'], out_v['w_ada'] = [r_.reshape(w_ada.shape) for r_ in res]
    n_wd, n_bd = w_decay_up.shape[-1], b_decay_up.shape[-1]
    small_g = dict(g_rep, c_ctx=g_c_ctx, b_ada=g_b_ada,
                   w_decay_up=lax.dynamic_slice_in_dim(g_wdu_full, chip * n_wd, n_wd, axis=3),
                   b_decay_up=lax.dynamic_slice_in_dim(g_bdu_full, chip * n_bd, n_bd, axis=2))
    small = [nm for nm in names if nm not in big and nm != 'w_ada']
    res = _adam("adam_small", _pack([weights[nm] for nm in small]), [_pack([small_g[nm] for nm in small])],
                _pack([mom1[nm] for nm in small]), _pack([mom2[nm] for nm in small]))
    small_shapes = [weights[nm].shape for nm in small]
    for dst, packed in zip((out_g, out_d, out_m, out_v), res):
        for nm, val in zip(small, _unpack(packed, small_shapes)):
            dst[nm] = val

    return (loss, grad_x, *[out_g[nm] for nm in names], *[out_d[nm] for nm in names],
            *[out_m[nm] for nm in names], *[out_v[nm] for nm in names])


def swiglu(name, u, nrow, nh, tm, tn):
    def kern(gate_ref, up_ref, o_ref):
        o_ref[...] = (_silu(gate_ref[...]) * up_ref[...]).astype(o_ref.dtype)

    return pl.pallas_call(
        kern, name=name, grid=(nh, nrow),
        in_specs=[pl.BlockSpec((tm, tn), lambda j, i: (i, j)), pl.BlockSpec((tm, tn), lambda j, i: (i, nh + j))],
        out_specs=pl.BlockSpec((tm, tn), lambda j, i: (i, j)),
        out_shape=jax.ShapeDtypeStruct((u.shape[0], nh * tn), MXU_DTYPE), compiler_params=_params(),
    )(u, u)


def swiglu_bwd(name, d_s, u, nrow, nh, tm, tn):
    def kern(ds_ref, gate_ref, up_ref, dg_ref, du_ref):
        gate, d_s_ = gate_ref[...], ds_ref[...]
        dg_ref[...] = (d_s_ * up_ref[...] * _dsilu(gate)).astype(dg_ref.dtype)
        du_ref[...] = (d_s_ * _silu(gate)).astype(du_ref.dtype)

    tile = pl.BlockSpec((tm, tn), lambda j, i: (i, j))
    return pl.pallas_call(
        kern, name=name, grid=(nh, nrow), in_specs=[tile, tile, pl.BlockSpec((tm, tn), lambda j, i: (i, nh + j))],
        out_specs=[tile, tile], out_shape=[jax.ShapeDtypeStruct((u.shape[0], nh * tn), MXU_DTYPE)] * 2,
        compiler_params=_params(),
    )(d_s, u, u)
```

```python
import functools

import jax
import jax.numpy as jnp
import numpy as np
from jax import lax
from jax.experimental import pallas as pl
from jax.experimental.pallas import tpu as pltpu

F32 = jnp.float32
MXU_DTYPE = jnp.bfloat16
WIRE_DTYPE = jnp.bfloat16

GRID_W = 64
CHUNK = 64
N_HEADS = 4
GATE_RANK = 16
GATE_NORM = 16.0
N_MOD = 6
N_POOL = 4
LN_EPS = 1e-5
RMS_EPS = 1e-6
ADAM_LR = 0.001
ADAM_B1 = 0.9
ADAM_B2 = 0.999
ADAM_EPS = 1e-08
ADAM_WD = 0.01
ADAM_STEP = 10

LANES = 128
SUBLANES = 8
ALR_W = 256
PACK_W = 2048
VMEM_LIMIT = 56 * 1024 * 1024
N_CHIPS = 4
N_DEV = 8
MESH = pl.DeviceIdType.MESH

NN = ((1,), (0,))
NT = ((1,), (1,))
TN = ((0,), (0,))


def _dot(a, b, dims):
    return lax.dot_general(a.astype(MXU_DTYPE), b.astype(MXU_DTYPE), (dims, ((), ())),
                           preferred_element_type=F32)


def _dot_f32(a, b, dims):
    return lax.dot_general(a.astype(F32), b.astype(F32), (dims, ((), ())),
                           precision=lax.Precision.HIGHEST, preferred_element_type=F32)


def _dot_mask(mask, x, dims):
    m = mask.astype(MXU_DTYPE)
    if MXU_DTYPE == F32:
        return lax.dot_general(m, x, (dims, ((), ())), preferred_element_type=F32)
    acc = None
    rest = x
    for _ in range(3):
        piece = rest.astype(MXU_DTYPE)
        rest = rest - piece.astype(F32)
        part = lax.dot_general(m, piece, (dims, ((), ())), preferred_element_type=F32)
        acc = part if acc is None else acc + part
    return acc


def _dot_3x(a, b, dims):
    if MXU_DTYPE == F32:
        return lax.dot_general(a, b, (dims, ((), ())), preferred_element_type=F32)
    a_hi, b_hi = a.astype(MXU_DTYPE), b.astype(MXU_DTYPE)
    a_lo = (a - a_hi.astype(F32)).astype(MXU_DTYPE)
    b_lo = (b - b_hi.astype(F32)).astype(MXU_DTYPE)
    dot = lambda u, w: lax.dot_general(u, w, (dims, ((), ())), preferred_element_type=F32)
    return dot(a_hi, b_hi) + (dot(a_lo, b_hi) + dot(a_hi, b_lo))


def _pick(n, cands):
    for c in cands:
        if n % c == 0:
            return c
    return n


def _params():
    return pltpu.CompilerParams(vmem_limit_bytes=VMEM_LIMIT)


def _sigmoid(x):
    return 0.5 + 0.5 * jnp.tanh(0.5 * x)


def _silu(x):
    return x * _sigmoid(x)


def _dsilu(x):
    s = _sigmoid(x)
    return s * (1.0 + x * (1.0 - s))


def matmul(name, a, b, form, m, n, k, *, tm, tn, out_dtype=F32, a_off=(0, 0), b_off=(0, 0), bias=None, bias_off=0,
           precise=False, comm=None, second=None):
    assert m % tm == 0 and n % tn == 0, (name, m, n, tm, tn)
    if form == 'tn':
        a_spec = pl.BlockSpec((k, tm), lambda j, i: (a_off[0], i + a_off[1]))
    else:
        a_spec = pl.BlockSpec((tm, k), lambda j, i: (i + a_off[0], a_off[1]))
    if form == 'nt':
        b_spec = pl.BlockSpec((tn, k), lambda j, i: (j + b_off[0], b_off[1]))
    else:
        b_spec = pl.BlockSpec((k, tn), lambda j, i: (b_off[0], j + b_off[1]))
    dims = {'nn': NN, 'nt': NT, 'tn': TN}[form]
    in_specs = [a_spec, b_spec]
    args = [a, b]
    if bias is not None:
        in_specs.append(pl.BlockSpec((SUBLANES, tn), lambda j, i: (0, j + bias_off)))
        args.append(bias)

    if second is not None:
        a2, b2_off = second
        assert form == 'nt'
        in_specs += [pl.BlockSpec((tm, k), lambda j, i: (i, 0)), pl.BlockSpec((tn, k), lambda j, i: (j + b2_off[0], b2_off[1]))]
        args += [a2, b]
    n_own = len(args)
    nj, ni = n // tn, m // tm
    out_spec = pl.BlockSpec((tm, tn), lambda j, i: (i, j))
    out_shape = jax.ShapeDtypeStruct((m, n), out_dtype)

    def product(refs):
        acc = (_dot_f32 if precise else _dot)(refs[0][...], refs[1][...], dims)
        if bias is not None:
            acc = acc + refs[2][0:1, :]
        if second is not None:
            acc = acc + _dot(refs[n_own - 2][...], refs[n_own - 1][...], dims)
        return acc

    if comm is None:
        def body(*refs):
            refs[-1][...] = product(refs).astype(refs[-1].dtype)

        return pl.pallas_call(body, name=name, grid=(nj, ni), in_specs=in_specs, out_specs=out_spec,
                              out_shape=out_shape, compiler_params=_params())(*args)

    n_ci, n_co = len(comm.ins), len(comm.outs)

    def hosted(*refs):
        c_in = refs[n_own:n_own + n_ci]
        o_ref = refs[n_own + n_ci]
        c_out = refs[n_own + n_ci + 1:n_own + n_ci + 1 + n_co]
        sems = refs[n_own + n_ci + 1 + n_co:]
        j, i = pl.program_id(0), pl.program_id(1)

        @pl.when((j == 0) & (i == 0))
        def _():
            comm.start(c_in, c_out, sems)

        o_ref[...] = product(refs).astype(o_ref.dtype)

        @pl.when((j == nj - 1) & (i == ni - 1))
        def _():
            comm.finish(c_in, c_out, sems)

    any_spec = pl.BlockSpec(memory_space=pl.ANY)
    res = pl.pallas_call(
        hosted, name=name, grid=(nj, ni), in_specs=in_specs + [any_spec] * n_ci,
        out_specs=[out_spec] + [any_spec] * n_co, out_shape=[out_shape] + list(comm.outs),
        scratch_shapes=list(comm.sems), compiler_params=_params(),
    )(*args, *comm.ins)
    return res[0], list(res[1:])


def ew(name, body, nrow, ncol, tm, tn, row_ins, vec_ins, row_outs, sum_outs=(), ctx_tiles=1, pass_i=False):
    def rmap(roff):
        return roff if callable(roff) else (lambda i: i + roff)

    in_specs = []
    for arr, roff, coff in row_ins:
        in_specs.append(pl.BlockSpec((tm, tn), functools.partial(lambda j, i, r, c: (r(i), j + c), r=rmap(roff), c=coff)))
    for arr, coff in vec_ins:
        in_specs.append(pl.BlockSpec((SUBLANES, tn), functools.partial(lambda j, i, c: (0, j + c), c=coff)))
    out_specs = [pl.BlockSpec((tm, tn), lambda j, i: (i, j)) for _ in row_outs]
    out_specs += [pl.BlockSpec((SUBLANES, tn), lambda j, i: (0, j)) for _ in sum_outs]
    out_shape = [jax.ShapeDtypeStruct((r, c), dt) for dt, r, c in row_outs]
    out_shape += [jax.ShapeDtypeStruct((SUBLANES, c), F32) for c, _ in sum_outs]
    n_row, n_vec, n_ro = len(row_ins), len(vec_ins), len(row_outs)

    def kern(*refs):
        j, i = pl.program_id(0), pl.program_id(1)
        rows = [r[...] for r in refs[:n_row]]
        vecs = [r[...] for r in refs[n_row:n_row + n_vec]]
        outs = refs[n_row + n_vec:]
        is_ctx = i < ctx_tiles
        res, sums = body(i if pass_i else is_ctx, j, rows, vecs)
        for ref, val in zip(outs[:n_ro], res):
            ref[...] = val.astype(ref.dtype)
        for ref, val, (_, by_class) in zip(outs[n_ro:], sums, sum_outs):
            @pl.when(i == 0)
            def _():
                ref[...] = jnp.zeros_like(ref)
            if by_class:
                ref[0:1, :] += jnp.where(is_ctx, val, 0.0)
                ref[1:2, :] += jnp.where(is_ctx, 0.0, val)
            else:
                ref[0:1, :] += val

    outs = pl.pallas_call(
        kern, name=name, grid=(ncol, nrow), in_specs=in_specs, out_specs=out_specs, out_shape=out_shape,
        compiler_params=_params(),
    )(*[a for a, _, _ in row_ins], *[a for a, _ in vec_ins])
    return list(outs)


def _cls(vec, is_ctx):
    return jnp.where(is_ctx, vec[0:1, :], vec[1:2, :])


def _colsum(x):
    return jnp.sum(x, axis=0, keepdims=True)


def _chunk_map(cfg, rev):
    nctx, nc = cfg.CL // CHUNK, cfg.R // CHUNK
    if not rev:
        return lambda s: s
    return lambda s: jnp.where(s < nctx, nctx - 1 - s, nctx + nc - 1 - s)


def _gla_chunk(q_ref, k_ref, a_ref, wd_ref, bd_ref, rev, scale):
    q = q_ref[...] * scale
    k = k_ref[...]
    z = _dot_3x(a_ref[...], wd_ref[...], NN) + bd_ref[0:1, :]
    la = (jnp.minimum(z, 0.0) - jnp.log(1.0 + jnp.exp(-jnp.abs(z)))) * (1.0 / GATE_NORM)
    r = lax.broadcasted_iota(jnp.int32, (CHUNK, CHUNK), 0)
    c = lax.broadcasted_iota(jnp.int32, (CHUNK, CHUNK), 1)
    keep = (r <= c) if rev else (r >= c)
    tri = keep.astype(F32)
    cum = _dot_mask(tri, la, NN)
    mid = CHUNK // 2 if rev else CHUNK // 2 - 1
    end = 0 if rev else CHUNK - 1
    ref = cum[mid:mid + 1, :]
    last = cum[end:end + 1, :]
    return dict(q=q, k=k, z=z, keep=keep, tri=tri, q_in=q * jnp.exp(cum - ref), k_in=k * jnp.exp(ref - cum),
                e_q=jnp.exp(cum), e_k=jnp.exp(last - cum), e_inq=jnp.exp(cum - ref), e_ink=jnp.exp(ref - cum),
                e_last=jnp.exp(last))


def _gla_in_specs(cfg, rows_of):
    dk, dv = cfg.DK, cfg.DV
    return [
        pl.BlockSpec((CHUNK, dk), lambda s: (rows_of(s), 0)),
        pl.BlockSpec((CHUNK, dk), lambda s: (rows_of(s), 1)),
        pl.BlockSpec((CHUNK, dv), lambda s: (rows_of(s), 2 * dk // dv)),
        pl.BlockSpec((CHUNK, ALR_W), lambda s: (rows_of(s), (cfg.NP - ALR_W) // ALR_W)),
        pl.BlockSpec((ALR_W, dk), lambda s: (0, 0)),
        pl.BlockSpec((SUBLANES, dk), lambda s: (0, 0)),
    ]


def gla_fwd(name, proj, wd, bd, cfg, rev, comm=None):
    hk, hv = cfg.HK, cfg.HV
    nc = cfg.R // CHUNK
    cmap = _chunk_map(cfg, rev)
    scale = hk ** -0.5
    n_ci, n_co = (len(comm.ins), len(comm.outs)) if comm else (0, 0)

    def body(*refs):
        q_ref, k_ref, v_ref, a_ref, wd_ref, bd_ref = refs[:6]
        c_in = refs[6:6 + n_ci]
        o_ref, ss_ref = refs[6 + n_ci:8 + n_ci]
        c_out = refs[8 + n_ci:8 + n_ci + n_co]
        st_scr = refs[8 + n_ci + n_co]
        sems = refs[9 + n_ci + n_co:]
        s = pl.program_id(0)

        @pl.when(s == 0)
        def _():
            st_scr[...] = jnp.zeros_like(st_scr)
            if comm:
                comm.start(c_in, c_out, sems)

        t = _gla_chunk(q_ref, k_ref, a_ref, wd_ref, bd_ref, rev, scale)
        q_int, k_st = t['q'] * t['e_q'], t['k'] * t['e_k']
        for h in range(N_HEADS):
            ks, vs = slice(h * hk, (h + 1) * hk), slice(h * hv, (h + 1) * hv)
            v = v_ref[:, vs]
            st = st_scr[h]
            ss_ref[0, h] = st
            a = jnp.where(t['keep'], _dot(t['q_in'][:, ks], t['k_in'][:, ks], NT), 0.0)
            o_ref[:, vs] = _dot(a, v, NN) + _dot(q_int[:, ks], st, NT)
            st_scr[h] = st * t['e_last'][:, ks] + _dot(v, k_st[:, ks], TN)

        if comm:
            @pl.when(s == nc - 1)
            def _():
                comm.finish(c_in, c_out, sems)

    any_spec = pl.BlockSpec(memory_space=pl.ANY)
    res = pl.pallas_call(
        body, name=name, grid=(nc,), in_specs=_gla_in_specs(cfg, cmap) + [any_spec] * n_ci,
        out_specs=[pl.BlockSpec((CHUNK, cfg.DV), lambda s: (cmap(s), 0)),
                   pl.BlockSpec((1, N_HEADS, hv, hk), lambda s: (s, 0, 0, 0))] + [any_spec] * n_co,
        out_shape=[jax.ShapeDtypeStruct((cfg.R, cfg.DV), F32),
                   jax.ShapeDtypeStruct((nc, N_HEADS, hv, hk), F32)] + (list(comm.outs) if comm else []),
        scratch_shapes=[pltpu.VMEM((N_HEADS, hv, hk), F32)] + (list(comm.sems) if comm else []),
        compiler_params=_params(),
    )(proj, proj, proj, proj, wd, bd, *(comm.ins if comm else []))
    return res[0], res[1], list(res[2:])


def gla_bwd(name, proj, wd, bd, states, d_o, cfg, rev, addends=None):
    hk, hv = cfg.HK, cfg.HV
    nc = cfg.R // CHUNK
    cmap = _chunk_map(cfg, rev)
    rows_of = lambda g: cmap(nc - 1 - g)
    scale = hk ** -0.5
    n_add = 0 if addends is None else 4

    def body(*refs):
        q_ref, k_ref, v_ref, a_ref, wd_ref, bd_ref, ss_ref, do_ref = refs[:8]
        adds = refs[8:8 + n_add]
        dq_ref, dk_ref, dv_ref, da_ref, dwd_ref, dbd_ref, dst_scr = refs[8 + n_add:]
        g = pl.program_id(0)

        @pl.when(g == 0)
        def _():
            dst_scr[...] = jnp.zeros_like(dst_scr)
            dwd_ref[...] = jnp.zeros_like(dwd_ref)
            dbd_ref[...] = jnp.zeros_like(dbd_ref)

        t = _gla_chunk(q_ref, k_ref, a_ref, wd_ref, bd_ref, rev, scale)
        q_int, k_st = t['q'] * t['e_q'], t['k'] * t['e_k']
        dq_h, dk_h, carry_h = [], [], []
        for h in range(N_HEADS):
            ks, vs = slice(h * hk, (h + 1) * hk), slice(h * hv, (h + 1) * hv)
            v = v_ref[:, vs]
            d_out = do_ref[:, vs]
            st = ss_ref[0, h]
            dst = dst_scr[h]
            a = jnp.where(t['keep'], _dot(t['q_in'][:, ks], t['k_in'][:, ks], NT), 0.0)
            da = jnp.where(t['keep'], _dot(d_out, v, NT), 0.0)
            dv = _dot(a, d_out, TN) + _dot(k_st[:, ks], dst, NT)
            dq_h.append(_dot(d_out, st, NN) * t['e_q'][:, ks] + _dot(da, t['k_in'][:, ks], NN) * t['e_inq'][:, ks])
            dk_h.append(_dot(v, dst, NN) * t['e_k'][:, ks] + _dot(da, t['q_in'][:, ks], TN) * t['e_ink'][:, ks])
            dst_scr[h] = dst * t['e_last'][:, ks] + _dot(d_out, q_int[:, ks], TN)
            st_end = st * t['e_last'][:, ks] + _dot(v, k_st[:, ks], TN)
            carry_h.append(_colsum(dst * st_end))
            dv_ref[:, vs] = dv + adds[2][:, vs] if n_add else dv
        dq = jnp.concatenate(dq_h, axis=-1)
        dk = jnp.concatenate(dk_h, axis=-1)
        dg = t['q'] * dq - t['k'] * dk
        dla = _dot_mask(t['tri'], dg, TN) + jnp.concatenate(carry_h, axis=-1)
        dz = dla * (1.0 / GATE_NORM) * _sigmoid(-t['z'])
        dalr = _dot_3x(dz, wd_ref[...], NT)
        dwd_ref[...] += _dot_3x(a_ref[...], dz, TN)
        dbd_ref[...] += jnp.broadcast_to(_colsum(dz), dbd_ref.shape)
        dq = dq * scale
        if n_add:
            dq, dk, dalr = dq + adds[0][...], dk + adds[1][...], dalr + adds[3][...]
        dq_ref[...] = dq
        dk_ref[...] = dk
        da_ref[...] = dalr

    qk_spec = pl.BlockSpec((CHUNK, cfg.DK), lambda g: (rows_of(g), 0))
    v_spec = pl.BlockSpec((CHUNK, cfg.DV), lambda g: (rows_of(g), 0))
    a_spec = pl.BlockSpec((CHUNK, ALR_W), lambda g: (rows_of(g), 0))
    in_specs = _gla_in_specs(cfg, rows_of) + [
        pl.BlockSpec((1, N_HEADS, hv, hk), lambda g: (nc - 1 - g, 0, 0, 0)), v_spec]
    args = [proj, proj, proj, proj, wd, bd, states, d_o]
    if n_add:
        in_specs += [qk_spec, qk_spec, v_spec, a_spec]
        args += list(addends)
    return pl.pallas_call(
        body, name=name, grid=(nc,), in_specs=in_specs,
        out_specs=[qk_spec, qk_spec, v_spec, a_spec,
                   pl.BlockSpec((ALR_W, cfg.DK), lambda g: (0, 0)),
                   pl.BlockSpec((SUBLANES, cfg.DK), lambda g: (0, 0))],
        out_shape=[jax.ShapeDtypeStruct((cfg.R, cfg.DK), F32), jax.ShapeDtypeStruct((cfg.R, cfg.DK), F32),
                   jax.ShapeDtypeStruct((cfg.R, cfg.DV), F32), jax.ShapeDtypeStruct((cfg.R, ALR_W), F32),
                   jax.ShapeDtypeStruct((ALR_W, cfg.DK), F32),
                   jax.ShapeDtypeStruct((SUBLANES, cfg.DK), F32)],
        scratch_shapes=[pltpu.VMEM((N_HEADS, hv, hk), F32)],
        compiler_params=_params(),
    )(*args)


def pool_mix(name, src, coff, cfg, transpose):
    tm, pg = cfg.TM, cfg.PG
    mt = cfg.R // tm
    reach = -(-(max(2 ** N_POOL // 2, 1) * GRID_W) // tm)
    nk = 2 * reach + 1
    img_rows = cfg.S // GRID_W
    shift = GRID_W.bit_length() - 1

    def ktile(m, d):
        return jnp.where(m == 0, 0, jnp.clip(m + d - reach, 1, mt - 1))

    def counts(idx, is_ctx, lo, hi):
        ctx_n = jnp.minimum(idx + hi + 1, cfg.CL) - jnp.maximum(idx - lo, 0)
        r, c = idx >> shift, idx & (GRID_W - 1)
        lat_n = ((jnp.minimum(r + hi + 1, img_rows) - jnp.maximum(r - lo, 0))
                 * (jnp.minimum(c + hi + 1, GRID_W) - jnp.maximum(c - lo, 0)))
        return jnp.where(is_ctx, ctx_n, lat_n).astype(F32)

    a = np.arange(tm)[:, None]
    b = np.arange(tm)[None, :]
    masks = np.zeros((N_POOL, nk + 1, tm, tm), np.float32)
    for g_ in range(N_POOL):
        lo_ = 2 ** g_
        hi_ = lo_ - 1
        for d_ in range(nk):
            dr = (d_ - reach) * (tm // GRID_W) + (b >> shift) - (a >> shift)
            dc = (b & (GRID_W - 1)) - (a & (GRID_W - 1))
            masks[g_, d_] = (dr >= -lo_) & (dr <= hi_) & (dc >= -lo_) & (dc <= hi_)
        masks[g_, nk] = (b - a >= -lo_) & (b - a <= hi_)
    masks = jnp.asarray(masks, MXU_DTYPE)

    def body(src_ref, self_ref, mask_ref, o_ref, acc):
        m, d = pl.program_id(0), pl.program_id(1)
        is_ctx = m == 0
        kt = m + d - reach
        valid = jnp.where(is_ctx, d == reach, (kt >= 1) & (kt <= mt - 1))
        seg = jnp.where(is_ctx, 0, cfg.CL)

        @pl.when(d == 0)
        def _():
            acc[...] = jnp.zeros_like(acc)

        @pl.when(valid)
        def _():
            for g in range(N_POOL):
                cols = slice(g * pg, (g + 1) * pg)
                x = src_ref[:, cols]
                if transpose:
                    kidx = lax.broadcasted_iota(jnp.int32, (tm, 1), 0) + (kt * tm - seg)
                    acc[:, cols] += _dot_mask(mask_ref[g, 0], x / counts(kidx, is_ctx, 2 ** g, 2 ** g - 1), TN)
                else:
                    acc[:, cols] += _dot_mask(mask_ref[g, 0], x, NN)

        @pl.when(d == nk - 1)
        def _():
            for g in range(N_POOL):
                cols = slice(g * pg, (g + 1) * pg)
                res = acc[:, cols]
                if not transpose:
                    midx = lax.broadcasted_iota(jnp.int32, (tm, 1), 0) + (m * tm - seg)
                    res = res / counts(midx, is_ctx, 2 ** g, 2 ** g - 1)
                o_ref[:, cols] = (res - self_ref[:, cols]).astype(o_ref.dtype)

    which = (lambda d: 2 * reach - d) if transpose else (lambda d: d)
    assert coff % N_POOL == 0
    dp = N_POOL * pg
    return pl.pallas_call(
        body, name=name, grid=(mt, nk),
        in_specs=[pl.BlockSpec((tm, dp), lambda m, d: (ktile(m, d), coff // N_POOL)),
                  pl.BlockSpec((tm, dp), lambda m, d: (m, coff // N_POOL)),
                  pl.BlockSpec((N_POOL, 1, tm, tm), lambda m, d: (0, jnp.where(m == 0, nk, which(d)), 0, 0))],
        out_specs=pl.BlockSpec((tm, dp), lambda m, d: (m, 0)),
        out_shape=jax.ShapeDtypeStruct((cfg.R, cfg.DP), F32),
        scratch_shapes=[pltpu.VMEM((tm, dp), F32)], compiler_params=_params(),
    )(src, src, masks)


def _my_place():
    return lax.axis_index("x"), lax.axis_index("y"), lax.axis_index("c")


def _flip(v, bit):
    return 1 - v if bit else v


def all_gather8(name, block):
    rows, w = block.shape

    def body(x_ref, out_ref, send_sems, recv_sems, local_sem):
        x, y, c = _my_place()
        me = 4 * x + 2 * y + c
        mine = pltpu.make_async_copy(x_ref, out_ref.at[me], local_sem)
        mine.start()
        sends = []
        for k in range(1, N_DEV):
            peer = (_flip(x, k & 4), _flip(y, k & 2), _flip(c, k & 1))
            cp = pltpu.make_async_remote_copy(src_ref=x_ref, dst_ref=out_ref.at[me], send_sem=send_sems.at[k - 1],
                                              recv_sem=recv_sems.at[k - 1], device_id=peer, device_id_type=MESH)
            cp.start()
            sends.append(cp)
        for k in range(1, N_DEV):
            peer = (_flip(x, k & 4), _flip(y, k & 2), _flip(c, k & 1))
            slot = 4 * peer[0] + 2 * peer[1] + peer[2]
            pltpu.make_async_remote_copy(src_ref=x_ref, dst_ref=out_ref.at[slot], send_sem=send_sems.at[k - 1],
                                         recv_sem=recv_sems.at[k - 1], device_id=peer, device_id_type=MESH).wait_recv()
        for cp in sends:
            cp.wait_send()
        mine.wait()

    return pl.pallas_call(
        body, name=name, out_shape=jax.ShapeDtypeStruct((N_DEV, rows, w), block.dtype),
        in_specs=[pl.BlockSpec(memory_space=pl.ANY)], out_specs=pl.BlockSpec(memory_space=pl.ANY),
        scratch_shapes=[pltpu.SemaphoreType.DMA((N_DEV - 1,)), pltpu.SemaphoreType.DMA((N_DEV - 1,)),
                        pltpu.SemaphoreType.DMA],
    )(block)


class _Comm:
    def __init__(self, ins, outs, sems, start, finish):
        self.ins, self.outs, self.sems, self.start, self.finish = list(ins), list(outs), list(sems), start, finish


def run_comm(name, comm):
    def body(*refs):
        n_i, n_o = len(comm.ins), len(comm.outs)
        comm.start(refs[:n_i], refs[n_i:n_i + n_o], refs[n_i + n_o:])
        comm.finish(refs[:n_i], refs[n_i:n_i + n_o], refs[n_i + n_o:])

    any_spec = pl.BlockSpec(memory_space=pl.ANY)
    return list(pl.pallas_call(body, name=name, out_shape=comm.outs, in_specs=[any_spec] * len(comm.ins),
                               out_specs=[any_spec] * len(comm.outs), scratch_shapes=comm.sems)(*comm.ins))


def gather_plan(shards, n_split):
    n = len(shards)

    def part(ref, t, core):
        if t >= n_split:
            return ref
        h = shards[t].shape[0] // 2
        return ref.at[pl.ds(core * h, h)]

    def ici(srcs, dsts, sems, k, t, slot, place):
        x, y, c = place
        return pltpu.make_async_remote_copy(
            src_ref=part(srcs[t], t, c), dst_ref=part(dsts[t].at[slot], t, c), send_sem=sems[0].at[t * 3 + k - 1],
            recv_sem=sems[1].at[t * 3 + k - 1], device_id=(_flip(x, k & 2), _flip(y, k & 1), c), device_id_type=MESH)

    def handed(dsts, sems, k, t, slot, place, core):
        x, y, c = place
        half = part(dsts[t].at[slot], t, core)
        return pltpu.make_async_remote_copy(
            src_ref=half, dst_ref=half, send_sem=sems[2].at[t * 3 + k - 1], recv_sem=sems[3].at[t * 3 + k - 1],
            device_id=(x, y, 1 - c), device_id_type=MESH)

    def start(srcs, dsts, sems):
        place = _my_place()
        me = 2 * place[0] + place[1]
        for t in range(n):
            pltpu.make_async_copy(srcs[t], dsts[t].at[me], sems[4].at[t]).start()
        for k in range(1, N_CHIPS):
            for t in range(n):
                ici(srcs, dsts, sems, k, t, me, place).start()

    def finish(srcs, dsts, sems):
        place = _my_place()
        x, y, c = place
        me = 2 * x + y
        for k in range(1, N_CHIPS):
            slot = 2 * _flip(x, k & 2) + _flip(y, k & 1)
            for t in range(n):
                ici(srcs, dsts, sems, k, t, slot, place).wait_recv()
                if t < n_split:
                    handed(dsts, sems, k, t, slot, place, c).start()
        for k in range(1, N_CHIPS):
            slot = 2 * _flip(x, k & 2) + _flip(y, k & 1)
            for t in range(n_split):
                handed(dsts, sems, k, t, slot, place, 1 - c).wait_recv()
        for k in range(1, N_CHIPS):
            slot = 2 * _flip(x, k & 2) + _flip(y, k & 1)
            for t in range(n):
                ici(srcs, dsts, sems, k, t, me, place).wait_send()
                if t < n_split:
                    handed(dsts, sems, k, t, slot, place, c).wait_send()
        for t in range(n):
            pltpu.make_async_copy(srcs[t], dsts[t].at[me], sems[4].at[t]).wait()

    outs = [jax.ShapeDtypeStruct((N_CHIPS,) + s.shape, s.dtype) for s in shards]
    sems = [pltpu.SemaphoreType.DMA((3 * n,))] * 4 + [pltpu.SemaphoreType.DMA((n,))]
    return _Comm(shards, outs, sems, start, finish)


def scatter_plan(triples):
    n = len(triples)
    bufs = [b for tr in triples for b in tr]

    def copies(srcs, dsts, sems):
        x, y, c = _my_place()
        return [pltpu.make_async_remote_copy(
            src_ref=srcs[t * 3 + k - 1], dst_ref=dsts[t * 3 + k - 1], send_sem=sems[0].at[t * 3 + k - 1],
            recv_sem=sems[1].at[t * 3 + k - 1], device_id=(_flip(x, k & 2), _flip(y, k & 1), c), device_id_type=MESH)
            for k in range(1, N_CHIPS) for t in range(n)]

    def start(srcs, dsts, sems):
        for cp in copies(srcs, dsts, sems):
            cp.start()

    def finish(srcs, dsts, sems):
        for cp in copies(srcs, dsts, sems):
            cp.wait_recv()
        for cp in copies(srcs, dsts, sems):
            cp.wait_send()

    return _Comm(bufs, [jax.ShapeDtypeStruct(b.shape, b.dtype) for b in bufs],
                 [pltpu.SemaphoreType.DMA((3 * n,)), pltpu.SemaphoreType.DMA((3 * n,))], start, finish)


def swap_plan(bufs):
    n = len(bufs)

    def copies(srcs, dsts, sems):
        x, y, c = _my_place()
        return [pltpu.make_async_remote_copy(src_ref=srcs[t], dst_ref=dsts[t], send_sem=sems[0].at[t],
                                             recv_sem=sems[1].at[t], device_id=(x, y, 1 - c), device_id_type=MESH)
                for t in range(n)]

    def start(srcs, dsts, sems):
        for cp in copies(srcs, dsts, sems):
            cp.start()

    def finish(srcs, dsts, sems):
        for cp in copies(srcs, dsts, sems):
            cp.wait_recv()
        for cp in copies(srcs, dsts, sems):
            cp.wait_send()

    return _Comm(bufs, [jax.ShapeDtypeStruct(b.shape, b.dtype) for b in bufs],
                 [pltpu.SemaphoreType.DMA((n,)), pltpu.SemaphoreType.DMA((n,))], start, finish)


def _vec8(*rows):
    w = rows[0].shape[-1]
    out = jnp.zeros((SUBLANES, w), F32)
    for r, v in enumerate(rows):
        out = out.at[r].set(v.reshape(w).astype(F32))
    return out


def _pack(arrays):
    parts = []
    for a in arrays:
        flat = a.reshape(-1).astype(F32)
        pad = (-flat.shape[0]) % PACK_W
        parts.append(jnp.pad(flat, (0, pad)))
    flat = jnp.concatenate(parts)
    pad = (-flat.shape[0]) % (PACK_W * SUBLANES)
    return jnp.pad(flat, (0, pad)).reshape(-1, PACK_W)


def _unpack(packed, shapes):
    flat = packed.reshape(-1)
    out, pos = [], 0
    for shp in shapes:
        size = 1
        for d in shp:
            size *= d
        out.append(flat[pos:pos + size].reshape(shp))
        pos += size + (-size) % PACK_W
    return out


def _rows2d(a):
    return a.reshape(-1, a.shape[-1])


class _Cfg:
    pass


def _adam(name, w, grads, m, v, half_rows=None, core=None):
    rows, width = w.shape
    c1 = 1.0 - ADAM_B1 ** ADAM_STEP
    c2 = 1.0 - ADAM_B2 ** ADAM_STEP

    def update(wv, mv, vv, g):
        m_new = ADAM_B1 * mv + (1.0 - ADAM_B1) * g
        v_new = ADAM_B2 * vv + (1.0 - ADAM_B2) * (g * g)
        delta = -ADAM_LR * ((m_new / c1) / (jnp.sqrt(v_new / c2) + ADAM_EPS) + ADAM_WD * wv)
        return [g, delta, m_new, v_new], []

    if half_rows is None:
        tm = _pick(rows, (128, 64, 32, 16, 8))

        def body(i, j, rows_, vecs):
            g = rows_[3]
            for extra in rows_[4:]:
                g = g + extra
            return update(rows_[0], rows_[1], rows_[2], g)

        return ew(name, body, rows // tm, 1, tm, width, [(w, 0, 0), (m, 0, 0), (v, 0, 0)] + [(g, 0, 0) for g in grads], [],
                  [(F32, rows, width)] * 4)

    tm = _pick(half_rows, (128, 64, 32, 16, 8))
    nb = half_rows // tm
    mine, theirs = grads
    n_layers = len(mine)
    core_vec = jnp.broadcast_to(core.astype(F32), (SUBLANES, width))

    def layer_tile(l):
        return lambda i: jnp.clip(i - l * 2 * nb, 0, 2 * nb - 1) % nb

    def body(i, j, rows_, vecs):
        is_mine = vecs[0][0:1, :] == ((i // nb) % 2).astype(F32)
        layer = i // (2 * nb)
        g = jnp.where(is_mine, rows_[3], rows_[3 + n_layers])
        for l in range(1, n_layers):
            g = jnp.where(layer == l, jnp.where(is_mine, rows_[3 + l], rows_[3 + n_layers + l]), g)
        return update(rows_[0], rows_[1], rows_[2], g)

    return ew(name, body, rows // tm, 1, tm, width,
              [(w, 0, 0), (m, 0, 0), (v, 0, 0)] + [(a_, layer_tile(l), 0) for l, a_ in enumerate(mine)]
              + [(a_, layer_tile(l), 0) for l, a_ in enumerate(theirs)], [(core_vec, 0)],
              [(F32, rows, width)] * 4, pass_i=True)


def _sum_rows(name, arr, nparts, rows, width, dtype=F32):
    tm = _pick(rows, (256, 128, 64, 32, 16, 8))
    nblk = rows // tm

    def body(i, j, rows_, vecs):
        acc = rows_[0].astype(F32)
        for r in rows_[1:]:
            acc = acc + r.astype(F32)
        return [acc], []

    return ew(name, body, nblk, 1, tm, width, [(arr, p * nblk, 0) for p in range(nparts)], [], [(dtype, rows, width)])[0]


def kernel(x, c, ctx, c_ctx, w_ada, b_ada, w_in, w_decay_up, b_decay_up, gla_norm_gain, w_pool_group, pool_scale, w_gla_out, w_pool_out, w_out, ln_mix_gain, ln_mix_bias, w_ffn_in, w_ffn_out, ln_ffn_gain, ln_ffn_bias, loss_target, m_c_ctx, m_w_ada, m_b_ada, m_w_in, m_w_decay_up, m_b_decay_up, m_gla_norm_gain, m_w_pool_group, m_pool_scale, m_w_gla_out, m_w_pool_out, m_w_out, m_ln_mix_gain, m_ln_mix_bias, m_w_ffn_in, m_w_ffn_out, m_ln_ffn_gain, m_ln_ffn_bias, v_c_ctx, v_w_ada, v_b_ada, v_w_in, v_w_decay_up, v_b_decay_up, v_gla_norm_gain, v_w_pool_group, v_pool_scale, v_w_gla_out, v_w_pool_out, v_w_out, v_ln_mix_gain, v_ln_mix_bias, v_w_ffn_in, v_w_ffn_out, v_ln_ffn_gain, v_ln_ffn_bias):
    weights = dict(c_ctx=c_ctx, w_ada=w_ada, b_ada=b_ada, w_in=w_in, w_decay_up=w_decay_up, b_decay_up=b_decay_up,
                   gla_norm_gain=gla_norm_gain, w_pool_group=w_pool_group, pool_scale=pool_scale, w_gla_out=w_gla_out,
                   w_pool_out=w_pool_out, w_out=w_out, ln_mix_gain=ln_mix_gain, ln_mix_bias=ln_mix_bias,
                   w_ffn_in=w_ffn_in, w_ffn_out=w_ffn_out, ln_ffn_gain=ln_ffn_gain, ln_ffn_bias=ln_ffn_bias)
    mom1 = dict(c_ctx=m_c_ctx, w_ada=m_w_ada, b_ada=m_b_ada, w_in=m_w_in, w_decay_up=m_w_decay_up, b_decay_up=m_b_decay_up,
                gla_norm_gain=m_gla_norm_gain, w_pool_group=m_w_pool_group, pool_scale=m_pool_scale, w_gla_out=m_w_gla_out,
                w_pool_out=m_w_pool_out, w_out=m_w_out, ln_mix_gain=m_ln_mix_gain, ln_mix_bias=m_ln_mix_bias,
                w_ffn_in=m_w_ffn_in, w_ffn_out=m_w_ffn_out, ln_ffn_gain=m_ln_ffn_gain, ln_ffn_bias=m_ln_ffn_bias)
    mom2 = dict(c_ctx=v_c_ctx, w_ada=v_w_ada, b_ada=v_b_ada, w_in=v_w_in, w_decay_up=v_w_decay_up, b_decay_up=v_b_decay_up,
                gla_norm_gain=v_gla_norm_gain, w_pool_group=v_w_pool_group, pool_scale=v_pool_scale, w_gla_out=v_w_gla_out,
                w_pool_out=v_w_pool_out, w_out=v_w_out, ln_mix_gain=v_ln_mix_gain, ln_mix_bias=v_ln_mix_bias,
                w_ffn_in=v_w_ffn_in, w_ffn_out=v_w_ffn_out, ln_ffn_gain=v_ln_ffn_gain, ln_ffn_bias=v_ln_ffn_bias)
    names = list(weights)

    cfg = _Cfg()
    L, D = w_ada.shape[0], x.shape[-1]
    S, CL = x.shape[1], ctx.shape[1]
    cfg.L, cfg.D, cfg.S, cfg.CL, cfg.R, cfg.TM = L, D, S, CL, S + CL, CL
    DK, DV, DP = D // 2, D, D // 2
    cfg.DK, cfg.DV, cfg.DP = DK, DV, DP
    cfg.HK, cfg.HV, cfg.PG = DK // N_HEADS, DV // N_HEADS, DP // N_POOL
    DFF = w_ffn_out.shape[1] * N_CHIPS
    NP = 2 * DK + 2 * DV + DP + 2 * D + ALR_W
    cfg.NP, cfg.DFF = NP, DFF
    R, TM, HK, HV, PG = cfg.R, cfg.TM, cfg.HK, cfg.HV, cfg.PG
    MT = R // TM
    alpha = (2.0 * L) ** 0.25
    assert S % TM == 0 and TM % CHUNK == 0 and S % GRID_W == 0 and TM % GRID_W == 0
    OFF_G, OFF_P, OFF_BGA, OFF_BGB = 2 * DK + DV, 2 * DK + 2 * DV, 2 * DK + 2 * DV + DP, 2 * DK + 2 * DV + DP + D
    ALR0 = 2 * DK + 2 * DV
    TE = 512
    TL = TM // 2
    assert D % TE == 0 and DP % TE == 0

    xi, yi, ci = _my_place()
    chip = 2 * xi + yi
    dev = 4 * xi + 2 * yi + ci

    n_ada = w_ada.shape[-1]
    c_all = all_gather8("ag_cond", jnp.pad(c.reshape(1, D), ((0, SUBLANES - 1), (0, 0))))[:, 0, :]
    cond = jnp.concatenate([c_all, c_ctx.reshape(1, D), jnp.zeros((16 - N_DEV - 1, D), F32)], axis=0)

    def silu_body(i, j, rows_, vecs):
        return [_silu(rows_[0]), _dsilu(rows_[0])], []

    act, dact = ew("cond_silu", silu_body, 1, 1, 16, D, [(cond, 0, 0)], [], [(F32, 16, D)] * 2)
    w_ada2 = w_ada.reshape(L * D, n_ada)
    b_ada_mine = lax.dynamic_slice_in_dim(b_ada, chip * n_ada, n_ada, axis=1)
    tn_ada = _pick(n_ada, (1024, 512, 256, 128))
    mods = [matmul("ada_fwd", act, w_ada2, 'nn', 16, n_ada, D, tm=16, tn=tn_ada, b_off=(l, 0),
                   bias=_vec8(b_ada_mine[l]), precise=True) for l in range(L)]
    mods_all = all_gather8("ag_mods", jnp.concatenate(mods, axis=0))
    mods_all = mods_all[0::2].reshape(N_CHIPS, L, 16, n_ada).transpose(1, 2, 0, 3).reshape(L, 16, N_MOD * D)
    modv = [_vec8(mods_all[l, N_DEV], lax.dynamic_index_in_dim(mods_all[l], dev, 0, keepdims=False)) for l in range(L)]
    MB = D // TE

    big = ['w_in', 'w_gla_out', 'w_pool_out', 'w_out', 'w_ffn_in', 'w_ffn_out', 'w_pool_group']
    hosts = [['w_in'], ['w_gla_out', 'w_out', 'w_pool_out', 'w_pool_group'], ['w_ffn_out'], ['w_ffn_in']]
    all_hosted = sum(hosts, [])
    layer_shards = lambda nms, l: [weights[nm][l].astype(WIRE_DTYPE) for nm in nms]

    def cols_together(g):
        return jnp.moveaxis(g, 0, -2).reshape(g.shape[1:-1] + (N_CHIPS * g.shape[-1],))

    def rows_together(g):
        return g.reshape((N_CHIPS * g.shape[1], g.shape[2]))

    te_ff = _pick(DFF, (1408, 1024, 512, 256, 128))

    def assemble(gw):
        win_ref = cols_together(gw['w_in'])
        win = jnp.concatenate([win_ref[:, :ALR0], win_ref[:, ALR0 + 2 * GATE_RANK:], win_ref[:, ALR0:ALR0 + 2 * GATE_RANK],
                               jnp.zeros((D, ALR_W - 2 * GATE_RANK), WIRE_DTYPE)], axis=-1)
        wpg = jnp.moveaxis(gw['w_pool_group'], 0, 1).reshape(N_POOL, PG, PG)
        wpg_bd = jnp.zeros((N_POOL, PG, N_POOL, PG), WIRE_DTYPE)
        for g in range(N_POOL):
            wpg_bd = wpg_bd.at[g, :, g, :].set(wpg[g])
        win_bwd = jnp.pad(win_ref, ((0, 0), (0, NP - win_ref.shape[1])))
        return dict(win=win, win_bwd=win_bwd, wgo=rows_together(gw['w_gla_out']), wpo=cols_together(gw['w_pool_out']),
                    wout=rows_together(gw['w_out']), wfi=cols_together(gw['w_ffn_in']), wfo=rows_together(gw['w_ffn_out']),
                    wpg_bd=wpg_bd.reshape(DP, DP))

    first = run_comm("gather_weights", gather_plan(layer_shards(all_hosted, 0) + [w_decay_up, b_decay_up],
                                                   n_split=len(big)))
    W = [assemble(dict(zip(all_hosted, first[:len(big)])))] + [None] * (L - 1)
    next_plan = lambda l, k: gather_plan(layer_shards(hosts[k], l + 1), n_split=len(hosts[k]))
    wdu = cols_together(first[len(big)])
    bdu = cols_together(first[len(big) + 1])
    wd_pad = [[jnp.zeros((ALR_W, DK), F32).at[d * GATE_RANK:(d + 1) * GATE_RANK].set(wdu[l, d]) for d in range(2)]
              for l in range(L)]
    bd_pad = [[_vec8(bdu[l, d]) for d in range(2)] for l in range(L)]

    tn_np = _pick(NP, (1280, 1024, 768, 512, 256, 128))
    tn_d = _pick(D, (1024, 512, 256, 128))
    tn_ff2 = _pick(2 * DFF, (1024, 512, 256, 128))
    tn_ff = _pick(DFF, (2816, 1408, 1024, 512, 256, 128))
    tn_dp = _pick(DP, (1024, 512, 256, 128))
    tw = lambda n_: _pick(n_, (512, 256, 128))

    gain8 = lambda v_: _vec8(v_)

    xs = jnp.concatenate([ctx.reshape(CL, D), x.reshape(S, D)], axis=0)
    saved = []
    for l in range(L):
        sv = {}
        mv = modv[l]

        def mod_body(i, j, rows_, vecs):
            return [rows_[0] * (1.0 + _cls(vecs[1], i)) + _cls(vecs[0], i)], []

        if l == 0:
            h1 = ew("modulate", mod_body, MT, D // tn_d, TM, tn_d, [(xs, 0, 0)],
                    [(mv, 0 * (D // tn_d)), (mv, 1 * (D // tn_d))], [(MXU_DTYPE, R, D)])[0]
        Wl = W[l]
        if l + 1 < L:
            proj, got0 = matmul("proj_and_gather", h1, Wl['win'], 'nn', R, NP, D, tm=TM, tn=tn_np, comm=next_plan(l, 0))
            o_f, st_f, got1 = gla_fwd("gla_fwd_and_gather", proj, wd_pad[l][0], bd_pad[l][0], cfg, False, next_plan(l, 1))
            o_b, st_b, got2 = gla_fwd("gla_rev_and_gather", proj, wd_pad[l][1], bd_pad[l][1], cfg, True, next_plan(l, 2))
        else:
            proj = matmul("proj", h1, Wl['win'], 'nn', R, NP, D, tm=TM, tn=tn_np)
            o_f, st_f, _ = gla_fwd("gla_fwd", proj, wd_pad[l][0], bd_pad[l][0], cfg, False)
            o_b, st_b, _ = gla_fwd("gla_rev", proj, wd_pad[l][1], bd_pad[l][1], cfg, True)

        def post_body(i, j, rows_, vecs):
            o = rows_[0] + rows_[1]
            on = o * lax.rsqrt(jnp.mean(o * o, axis=-1, keepdims=True) + RMS_EPS)
            return [on * vecs[0][0:1, :] * _silu(rows_[2])], []

        a_gla = ew("gla_post", post_body, MT, N_HEADS, TM, HV, [(o_f, 0, 0), (o_b, 0, 0), (proj, 0, OFF_G // HV)],
                   [(gain8(gla_norm_gain[l]), 0)], [(MXU_DTYPE, R, DV)])[0]
        y_gla = matmul("gla_out", a_gla, Wl['wgo'], 'nn', R, D, DV, tm=TM, tn=tn_d)
        u_pool = pool_mix("pool_fwd", proj, OFF_P // PG, cfg, False)
        t_pool = matmul("pool_group", u_pool, Wl['wpg_bd'], 'nn', R, DP, DP, tm=TM, tn=tn_dp)

        def scale_body(i, j, rows_, vecs):
            return [rows_[0] * vecs[0][0:1, :]], []

        ts_pool = ew("pool_scale", scale_body, MT, DP // TE, TM, TE, [(t_pool, 0, 0)], [(gain8(pool_scale[l]), 0)],
                     [(MXU_DTYPE, R, DP)])[0]
        y_pool = matmul("pool_out", ts_pool, Wl['wpo'], 'nn', R, D, DP, tm=TM, tn=tn_d)

        def merge_body(i, j, rows_, vecs):
            return [_sigmoid(rows_[2]) * rows_[0] + _sigmoid(rows_[3]) * rows_[1]], []

        mg = ew("merge", merge_body, MT, D // TE, TM, TE,
                [(y_gla, 0, 0), (y_pool, 0, 0), (proj, 0, OFF_BGA // TE), (proj, 0, OFF_BGB // TE)], [],
                [(MXU_DTYPE, R, D)])[0]
        y_mix = matmul("mix_out", mg, Wl['wout'], 'nn', R, D, D, tm=TM, tn=tn_d)

        def ln_body(i, j, rows_, vecs):
            r = alpha * rows_[0] + _cls(vecs[0], i) * rows_[1]
            mu = jnp.mean(r, axis=-1, keepdims=True)
            rc = r - mu
            var = jnp.mean(rc * rc, axis=-1, keepdims=True)
            out = rc * lax.rsqrt(var + LN_EPS) * vecs[1][0:1, :] + vecs[2][0:1, :]
            if len(vecs) == 3:
                return [out], []
            return [out, out * (1.0 + _cls(vecs[4], i)) + _cls(vecs[3], i)], []

        x1, h2 = ew("resid_ln_mod", ln_body, R // TL, 1, TL, D, [(xs, 0, 0), (y_mix, 0, 0)],
                    [(mv, 2), (gain8(ln_mix_gain[l]), 0), (gain8(ln_mix_bias[l]), 0), (mv, 3), (mv, 4)],
                    [(F32, R, D), (MXU_DTYPE, R, D)], ctx_tiles=CL // TL)
        if l + 1 < L:
            u_ffn, got3 = matmul("ffn_in_and_gather", h2, Wl['wfi'], 'nn', R, 2 * DFF, D, tm=TM, tn=tn_ff2,
                                 comm=next_plan(l, 3))
            W[l + 1] = assemble(dict(zip(all_hosted, got0 + got1 + got2 + got3)))
        else:
            u_ffn = matmul("ffn_in", h2, Wl['wfi'], 'nn', R, 2 * DFF, D, tm=TM, tn=tn_ff2)
        s_ffn = swiglu("swiglu", u_ffn, MT, DFF // te_ff, TM, te_ff)
        f_ffn = matmul("ffn_out", s_ffn, Wl['wfo'], 'nn', R, D, DFF, tm=TM, tn=tn_d)
        if l + 1 < L:
            x2, h_next = ew("resid_ln_mod", ln_body, R // TL, 1, TL, D, [(x1, 0, 0), (f_ffn, 0, 0)],
                            [(mv, 5), (gain8(ln_ffn_gain[l]), 0), (gain8(ln_ffn_bias[l]), 0), (modv[l + 1], 0), (modv[l + 1], 1)],
                            [(F32, R, D), (MXU_DTYPE, R, D)], ctx_tiles=CL // TL)
        else:
            x2 = ew("resid_ln", ln_body, R // TL, 1, TL, D, [(x1, 0, 0), (f_ffn, 0, 0)],
                    [(mv, 5), (gain8(ln_ffn_gain[l]), 0), (gain8(ln_ffn_bias[l]), 0)], [(F32, R, D)], ctx_tiles=CL // TL)[0]
        sv.update(xs=xs, h1=h1, proj=proj, o_f=o_f, o_b=o_b, st_f=st_f, st_b=st_b, a_gla=a_gla, y_gla=y_gla, u_pool=u_pool,
                  t_pool=t_pool, ts_pool=ts_pool, y_pool=y_pool, mg=mg, y_mix=y_mix, x1=x1, h2=h2, u_ffn=u_ffn, s_ffn=s_ffn,
                  f_ffn=f_ffn)
        saved.append(sv)
        xs = x2
        if l + 1 < L:
            h1 = h_next

    tgt = loss_target.reshape(S, D)

    def loss_body(i, j, rows_, vecs):
        d = jnp.where(i, 0.0, rows_[0] - rows_[1])
        return [d * (1.0 / D)], [_colsum(d * d)]

    d_x, sq = ew("loss", loss_body, MT, 1, TM, D, [(xs, 0, 0), (tgt, lambda i: jnp.maximum(i - 1, 0), 0)], [],
                 [(F32, R, D)], [(D, False)])

    def total_body(i, j, rows_, vecs):
        return [jnp.broadcast_to(jnp.sum(rows_[0], axis=-1, keepdims=True), (SUBLANES, D)) * (0.5 / D)], []

    loss_local = ew("loss_total", total_body, 1, 1, SUBLANES, D, [(sq, 0, 0)], [], [(F32, SUBLANES, D)])[0][0, 0]
    loss = lax.psum(loss_local, ("x", "y", "c"))

    gsmall = {nm: [None] * L for nm in ['gla_norm_gain', 'pool_scale', 'ln_mix_gain', 'ln_mix_bias', 'ln_ffn_gain',
                                        'ln_ffn_bias', 'w_decay_up', 'b_decay_up']}
    gbig = {nm: [None] * L for nm in big}
    dmod = [None] * L

    def ln_bwd(name, x_in, br, d_out, mv, gt_blk, gain):
        def body(i, j, rows_, vecs):
            gt = _cls(vecs[0], i)
            r = alpha * rows_[0] + gt * rows_[1]
            mu = jnp.mean(r, axis=-1, keepdims=True)
            rc = r - mu
            rstd = lax.rsqrt(jnp.mean(rc * rc, axis=-1, keepdims=True) + LN_EPS)
            xhat = rc * rstd
            dxh = rows_[2] * vecs[1][0:1, :]
            dr = rstd * (dxh - jnp.mean(dxh, axis=-1, keepdims=True) - xhat * jnp.mean(dxh * xhat, axis=-1, keepdims=True))
            return [dr, gt * dr], [_colsum(rows_[2] * xhat), _colsum(rows_[2]), _colsum(dr * rows_[1])]

        return ew(name, body, R // TL, 1, TL, D, [(x_in, 0, 0), (br, 0, 0), (d_out, 0, 0)], [(mv, gt_blk), (gain8(gain), 0)],
                  [(F32, R, D), (MXU_DTYPE, R, D)], [(D, False), (D, False), (D, True)], ctx_tiles=CL // TL)

    def mod_bwd(name, d_h, x_in, d_r, mv, sc_blk):
        def body(i, j, rows_, vecs):
            return ([rows_[0] * (1.0 + _cls(vecs[0], i)) + alpha * rows_[2]],
                    [_colsum(rows_[0] * rows_[1]), _colsum(rows_[0])])

        return ew(name, body, MT, D // tn_d, TM, tn_d, [(d_h, 0, 0), (x_in, 0, 0), (d_r, 0, 0)],
                  [(mv, sc_blk * (D // tn_d))], [(F32, R, D)], [(D, True), (D, True)])

    shard_axis = {'w_in': 1, 'w_gla_out': 0, 'w_pool_out': 1, 'w_out': 0, 'w_ffn_in': 1, 'w_ffn_out': 0, 'w_pool_group': 1}
    scatter_groups = [['w_in'], ['w_ffn_in'], ['w_gla_out', 'w_pool_out', 'w_out', 'w_ffn_out', 'w_pool_group']]

    def cut_block(nm, g, which, core, dtype):
        ax = shard_axis[nm]
        width = weights[nm].shape[ax + 1]
        starts, sizes = [0] * g.ndim, list(g.shape)
        if ax == 0:
            sizes[0] = width // 2
            starts[0] = which * width + core * sizes[0]
        else:
            sizes[0], sizes[ax] = g.shape[0] // 2, width
            starts[0], starts[ax] = core * sizes[0], which * width
        return lax.dynamic_slice(g, starts, sizes).astype(dtype)

    def cut_layer(l):
        cut = dict(keep_r={}, give_r={}, keep_o={}, give_o={})
        for nm in big:
            g = gbig[nm][l]
            if nm == 'w_ffn_in':
                assert 2 * weights[nm].shape[2] == DFF and N_CHIPS == 4
                blk = lambda which, core, dt: jnp.where(which < 2, cut_block(nm, g[0], which % 2, core, dt),
                                                        cut_block(nm, g[1], which % 2, core, dt))
            else:
                blk = lambda which, core, dt: cut_block(nm, g, which, core, dt)
            others = [jnp.bitwise_xor(chip, k) for k in range(1, N_CHIPS)]
            cut['keep_r'][nm] = [blk(o_, ci, WIRE_DTYPE) for o_ in others]
            cut['give_r'][nm] = [blk(o_, 1 - ci, WIRE_DTYPE) for o_ in others]
            cut['keep_o'][nm], cut['give_o'][nm] = blk(chip, ci, F32), blk(chip, 1 - ci, F32)
        return cut

    def swap_partials_plan(cut):
        return swap_plan([a_ for nm in big for a_ in cut['give_r'][nm]] + [cut['give_o'][nm] for nm in big])

    def add_partials(cut, got):
        cut['got_o'] = dict(zip(big, got[3 * len(big):]))
        cut['send_r'] = {}
        for t, nm in enumerate(big):
            kept = [_rows2d(a_) for a_ in cut['keep_r'][nm]]
            theirs = [_rows2d(a_) for a_ in got[3 * t:3 * t + 3]]
            rows_, width = kept[0].shape
            tm_ = _pick(rows_, (256, 128, 64, 32, 16, 8))

            def add2_body(i, j, rows__, vecs):
                return [rows__[k].astype(F32) + rows__[3 + k].astype(F32) for k in range(3)], []

            cut['send_r'][nm] = ew("add_partials", add2_body, rows_ // tm_, 1, tm_, width,
                                   [(a_, 0, 0) for a_ in kept + theirs], [], [(WIRE_DTYPE, rows_, width)] * 3)
        cut['recvd'] = {}

    def sum_grads(cut):
        done = {}
        for nm in big:
            a2, b2 = _rows2d(cut['keep_o'][nm]), _rows2d(cut['got_o'][nm])
            rows_, width = a2.shape
            tm_ = _pick(rows_, (256, 128, 64, 32, 16, 8))

            def psum_body(i, j, rows__, vecs):
                return [(rows__[0] + rows__[1]) + rows__[2].astype(F32) + rows__[3].astype(F32) + rows__[4].astype(F32)], []

            done[nm] = ew("sum_grads", psum_body, rows_ // tm_, 1, tm_, width,
                          [(a2, 0, 0), (b2, 0, 0)] + [(r_, 0, 0) for r_ in cut['recvd'][nm]], [], [(F32, rows_, width)])[0]
        return done

    def scatter_group_plan(cut, group):
        return scatter_plan([cut['send_r'][nm] for nm in group])

    def received(cut, group, got):
        for t, nm in enumerate(group):
            cut['recvd'][nm] = got[3 * t:3 * t + 3]

    finished = [None] * L
    pending = None
    for l in reversed(range(L)):
        sv = saved[l]
        mv = modv[l]
        Wl = W[l]
        proj = sv['proj']
        dr2, d_f, g_gain, g_bias, g_gt_f = ln_bwd("ln_bwd", sv['x1'], sv['f_ffn'], d_x, mv, 5, ln_ffn_gain[l])
        gsmall['ln_ffn_gain'][l], gsmall['ln_ffn_bias'][l] = g_gain[0], g_bias[0]
        if pending is not None:
            d_s, got = matmul("ffn_out_dx_and_swap", d_f, Wl['wfo'], 'nt', R, DFF, D, tm=TM, tn=tn_ff,
                              comm=swap_partials_plan(pending))
            add_partials(pending, got)
        else:
            d_s = matmul("ffn_out_dx", d_f, Wl['wfo'], 'nt', R, DFF, D, tm=TM, tn=tn_ff)
        gbig['w_ffn_out'][l] = matmul("ffn_out_dw", sv['s_ffn'], d_f, 'tn', DFF, D, R, tm=tw(DFF), tn=tn_d)
        d_gate, d_up = swiglu_bwd("swiglu_bwd", d_s, sv['u_ffn'], MT, DFF // te_ff, TM, te_ff)
        tn_x = 512
        if pending is not None:
            d_h2, got = matmul("ffn_in_dx_and_scatter", d_gate, Wl['wfi'], 'nt', R, D, DFF, tm=TM, tn=tn_x,
                               second=(d_up, (0, 1)), comm=scatter_group_plan(pending, scatter_groups[0]))
            received(pending, scatter_groups[0], got)
        else:
            d_h2 = matmul("ffn_in_dx", d_gate, Wl['wfi'], 'nt', R, D, DFF, tm=TM, tn=tn_x, second=(d_up, (0, 1)))
        gbig['w_ffn_in'][l] = (matmul("ffn_in_dw", sv['h2'], d_gate, 'tn', D, DFF, R, tm=tw(D), tn=te_ff),
                               matmul("ffn_in_dw_up", sv['h2'], d_up, 'tn', D, DFF, R, tm=tw(D), tn=te_ff))
        d_x1, g_sc_f, g_sh_f = mod_bwd("mod_bwd", d_h2, sv['x1'], dr2, mv, 4)
        dr1, d_y, g_gain, g_bias, g_gt_m = ln_bwd("ln_bwd", sv['xs'], sv['y_mix'], d_x1, mv, 2, ln_mix_gain[l])
        gsmall['ln_mix_gain'][l], gsmall['ln_mix_bias'][l] = g_gain[0], g_bias[0]
        d_mg = matmul("mix_out_dx", d_y, Wl['wout'], 'nt', R, D, D, tm=TM, tn=tn_d)
        gbig['w_out'][l] = matmul("mix_out_dw", sv['mg'], d_y, 'tn', D, D, R, tm=tw(D), tn=tn_d)

        def merge_bwd_body(i, j, rows_, vecs):
            d_m, yg, yp, ba, bb = rows_
            sa, sb = _sigmoid(ba), _sigmoid(bb)
            return [d_m * sa, d_m * sb, d_m * yg * sa * (1.0 - sa), d_m * yp * sb * (1.0 - sb)], []

        d_yg, d_yp, d_bga, d_bgb = ew(
            "merge_bwd", merge_bwd_body, MT, D // TE, TM, TE,
            [(d_mg, 0, 0), (sv['y_gla'], 0, 0), (sv['y_pool'], 0, 0), (proj, 0, OFF_BGA // TE), (proj, 0, OFF_BGB // TE)], [],
            [(MXU_DTYPE, R, D), (MXU_DTYPE, R, D), (F32, R, D), (F32, R, D)])
        d_ts = matmul("pool_out_dx", d_yp, Wl['wpo'], 'nt', R, DP, D, tm=TM, tn=tn_dp)
        gbig['w_pool_out'][l] = matmul("pool_out_dw", sv['ts_pool'], d_yp, 'tn', DP, D, R, tm=tw(DP), tn=tn_d)

        def scale_bwd_body(i, j, rows_, vecs):
            return [rows_[0] * vecs[0][0:1, :]], [_colsum(rows_[0] * rows_[1])]

        d_t, g_ps = ew("pool_scale_bwd", scale_bwd_body, MT, DP // TE, TM, TE, [(d_ts, 0, 0), (sv['t_pool'], 0, 0)],
                       [(gain8(pool_scale[l]), 0)], [(MXU_DTYPE, R, DP)], [(DP, False)])
        gsmall['pool_scale'][l] = g_ps[0]
        d_u_pool = matmul("pool_group_dx", d_t, Wl['wpg_bd'], 'nt', R, DP, DP, tm=TM, tn=tn_dp)
        g_bd = matmul("pool_group_dw", sv['u_pool'], d_t, 'tn', DP, DP, R, tm=tw(DP), tn=tn_dp)
        gbig['w_pool_group'][l] = jnp.stack([g_bd[g * PG:(g + 1) * PG, g * PG:(g + 1) * PG] for g in range(N_POOL)])
        d_p = pool_mix("pool_bwd", d_u_pool, 0, cfg, True)
        d_a = matmul("gla_out_dx", d_yg, Wl['wgo'], 'nt', R, DV, D, tm=TM, tn=tn_d)
        gbig['w_gla_out'][l] = matmul("gla_out_dw", sv['a_gla'], d_yg, 'tn', DV, D, R, tm=tw(DV), tn=tn_d)

        def post_bwd_body(i, j, rows_, vecs):
            d_a_, o_f_, o_b_, g_ = rows_
            gain = vecs[0][0:1, :]
            o = o_f_ + o_b_
            rstd = lax.rsqrt(jnp.mean(o * o, axis=-1, keepdims=True) + RMS_EPS)
            on = o * rstd
            sg = _silu(g_)
            d_on = d_a_ * gain * sg
            d_o_ = rstd * (d_on - on * jnp.mean(d_on * on, axis=-1, keepdims=True))
            return [d_o_, d_a_ * on * gain * _dsilu(g_)], [_colsum(d_a_ * on * sg)]

        d_o, d_g, g_gng = ew("gla_post_bwd", post_bwd_body, MT, N_HEADS, TM, HV,
                             [(d_a, 0, 0), (sv['o_f'], 0, 0), (sv['o_b'], 0, 0), (proj, 0, OFF_G // HV)],
                             [(gain8(gla_norm_gain[l]), 0)], [(F32, R, DV), (F32, R, DV)], [(DV, False)])
        gsmall['gla_norm_gain'][l] = g_gng[0]
        part = gla_bwd("gla_bwd", proj, wd_pad[l][0], bd_pad[l][0], sv['st_f'], d_o, cfg, False)
        full = gla_bwd("gla_rev_bwd", proj, wd_pad[l][1], bd_pad[l][1], sv['st_b'], d_o, cfg, True, addends=part[:4])
        d_q, d_k, d_v, d_alr = full[:4]
        g_wd = [res[4][d * GATE_RANK:(d + 1) * GATE_RANK] for d, res in enumerate((part, full))]
        g_bd_ = [res[5][0] for res in (part, full)]
        gsmall['w_decay_up'][l], gsmall['b_decay_up'][l] = jnp.stack(g_wd), jnp.stack(g_bd_)
        d_proj = jnp.concatenate([t_.astype(MXU_DTYPE) for t_ in (d_q, d_k, d_v, d_g, d_alr[:, :2 * GATE_RANK], d_p, d_bga, d_bgb,
                                                                   jnp.zeros((R, ALR_W - 2 * GATE_RANK), F32))], axis=-1)
        if pending is not None:
            d_h1, got = matmul("proj_dx_and_scatter", d_proj, Wl['win_bwd'], 'nt', R, D, NP, tm=TM, tn=tn_x,
                               comm=scatter_group_plan(pending, scatter_groups[2]))
            received(pending, scatter_groups[2], got)
            gbig['w_in'][l], got = matmul("proj_dw_and_scatter", sv['h1'], d_proj, 'tn', D, NP, R, tm=tw(D), tn=tn_np,
                                          comm=scatter_group_plan(pending, scatter_groups[1]))
            received(pending, scatter_groups[1], got)
            finished[l + 1] = sum_grads(pending)
        else:
            d_h1 = matmul("proj_dx", d_proj, Wl['win_bwd'], 'nt', R, D, NP, tm=TM, tn=tn_x)
            gbig['w_in'][l] = matmul("proj_dw", sv['h1'], d_proj, 'tn', D, NP, R, tm=tw(D), tn=tn_np)
        d_x, g_sc_m, g_sh_m = mod_bwd("mod_bwd", d_h1, sv['xs'], dr1, mv, 1)
        dmod[l] = jnp.concatenate([g_sh_m[:2], g_sc_m[:2], g_gt_m[:2], g_sh_f[:2], g_sc_f[:2], g_gt_f[:2]], axis=-1)
        pending = cut_layer(l)

    grad_x = d_x[CL:].reshape(x.shape)
    add_partials(pending, run_comm("swap_partials", swap_partials_plan(pending)))
    received(pending, big, run_comm("scatter_grads", scatter_group_plan(pending, big)))
    finished[0] = sum_grads(pending)

    dmod = jnp.stack(dmod)
    summed = [dmod[:, 0]] + [jnp.stack(gsmall[nm]) for nm in
                             ['ln_mix_gain', 'ln_mix_bias', 'ln_ffn_gain', 'ln_ffn_bias', 'gla_norm_gain', 'pool_scale',
                              'w_decay_up', 'b_decay_up']]
    pack = _pack([dmod[:, 1]] + summed)
    prow = pack.shape[0]
    packs = all_gather8("ag_small", pack)
    tot = _sum_rows("sum_small", packs.reshape(N_DEV * prow, PACK_W), N_DEV, prow, PACK_W)
    shapes = [(L, N_MOD * D)] + [a.shape for a in summed]
    tot = _unpack(tot, shapes)
    dmod_ctx = tot[1]
    g_rep = dict(zip(['ln_mix_gain', 'ln_mix_bias', 'ln_ffn_gain', 'ln_ffn_bias', 'gla_norm_gain', 'pool_scale'], tot[2:8]))
    g_wdu_full, g_bdu_full = tot[8], tot[9]
    dmod_lat = jnp.stack([_unpack(packs[d_], shapes[:1])[0] for d_ in range(N_DEV)], axis=1)
    dm_all = jnp.concatenate([dmod_lat, dmod_ctx[:, None, :], jnp.zeros((L, 16 - N_DEV - 1, N_MOD * D), F32)], axis=1)

    dm_mine = lax.dynamic_slice_in_dim(dm_all, chip * n_ada, n_ada, axis=2)
    g_w_ada = jnp.stack([matmul("ada_dw", act, dm_mine[l], 'tn', D, n_ada, 16, tm=tw(D), tn=tn_ada, precise=True)
                         for l in range(L)])

    def bsum_body(i, j, rows_, vecs):
        return [jnp.broadcast_to(_colsum(rows_[0]), rows_[0].shape)], []

    g_b_ada = jnp.stack([ew("ada_db", bsum_body, 1, 1, 16, N_MOD * D, [(dm_all[l], 0, 0)], [],
                            [(F32, 16, N_MOD * D)])[0][0] for l in range(L)])
    part_c = [matmul("ada_dc", dm_mine[l], w_ada2, 'nt', 16, D, n_ada, tm=16, tn=tn_d, b_off=(l * (D // tn_d), 0),
                     precise=True)
              for l in range(L)]
    parts_c = all_gather8("ag_dcond", jnp.concatenate(part_c, axis=0))
    dc_rows = parts_c[0::2].reshape(N_CHIPS * L * 16, D)

    def dc_body(i, j, rows_, vecs):
        acc = rows_[0]
        for r_ in rows_[1:-1]:
            acc = acc + r_
        return [acc * rows_[-1]], []

    g_c_ctx = ew("dcond", dc_body, 1, 1, 16, D, [(dc_rows, p_, 0) for p_ in range(N_CHIPS * L)] + [(dact, 0, 0)], [],
                 [(F32, 16, D)])[0][N_DEV]

    other = run_comm("swap_grads", swap_plan([finished[l][nm] for nm in big for l in range(L)]))

    out_g, out_d, out_m, out_v = {}, {}, {}, {}
    for t, nm in enumerate(big):
        shp = weights[nm].shape
        half_rows = (shp[1] // 2) * (shp[2] if len(shp) == 4 else 1)
        res = _adam("adam_" + nm, _rows2d(weights[nm]), ([finished[l][nm] for l in range(L)], other[t * L:(t + 1) * L]),
                    _rows2d(mom1[nm]), _rows2d(mom2[nm]), half_rows=half_rows, core=ci)
        out_g[nm], out_d[nm], out_m[nm], out_v[nm] = [r_.reshape(shp) for r_ in res]
    res = _adam("adam_w_ada", _rows2d(w_ada), [_rows2d(g_w_ada)], _rows2d(m_w_ada), _rows2d(v_w_ada))
    out_g['w_ada'], out_d['w_ada'], out_m['w_ada'], out_v['w_ada'] = [r_.reshape(w_ada.shape) for r_ in res]
    n_wd, n_bd = w_decay_up.shape[-1], b_decay_up.shape[-1]
    small_g = dict(g_rep, c_ctx=g_c_ctx, b_ada=g_b_ada,
                   w_decay_up=lax.dynamic_slice_in_dim(g_wdu_full, chip * n_wd, n_wd, axis=3),
                   b_decay_up=lax.dynamic_slice_in_dim(g_bdu_full, chip * n_bd, n_bd, axis=2))
    small = [nm for nm in names if nm not in big and nm != 'w_ada']
    res = _adam("adam_small", _pack([weights[nm] for nm in small]), [_pack([small_g[nm] for nm in small])],
                _pack([mom1[nm] for nm in small]), _pack([mom2[nm] for nm in small]))
    small_shapes = [weights[nm].shape for nm in small]
    for dst, packed in zip((out_g, out_d, out_m, out_v), res):
        for nm, val in zip(small, _unpack(packed, small_shapes)):
            dst[nm] = val

    return (loss, grad_x, *[out_g[nm] for nm in names], *[out_d[nm] for nm in names],
            *[out_m[nm] for nm in names], *[out_v[nm] for nm in names])


def swiglu(name, u, nrow, nh, tm, tn):
    def kern(gate_ref, up_ref, o_ref):
        o_ref[...] = (_silu(gate_ref[...]) * up_ref[...]).astype(o_ref.dtype)

    return pl.pallas_call(
        kern, name=name, grid=(nh, nrow),
        in_specs=[pl.BlockSpec((tm, tn), lambda j, i: (i, j)), pl.BlockSpec((tm, tn), lambda j, i: (i, nh + j))],
        out_specs=pl.BlockSpec((tm, tn), lambda j, i: (i, j)),
        out_shape=jax.ShapeDtypeStruct((u.shape[0], nh * tn), MXU_DTYPE), compiler_params=_params(),
    )(u, u)


def swiglu_bwd(name, d_s, u, nrow, nh, tm, tn):
    def kern(ds_ref, gate_ref, up_ref, dg_ref, du_ref):
        gate, d_s_ = gate_ref[...], ds_ref[...]
        dg_ref[...] = (d_s_ * up_ref[...] * _dsilu(gate)).astype(dg_ref.dtype)
        du_ref[...] = (d_s_ * _silu(gate)).astype(du_ref.dtype)

    tile = pl.BlockSpec((tm, tn), lambda j, i: (i, j))
    return pl.pallas_call(
        kern, name=name, grid=(nh, nrow), in_specs=[tile, tile, pl.BlockSpec((tm, tn), lambda j, i: (i, nh + j))],
        out_specs=[tile, tile], out_shape=[jax.ShapeDtypeStruct((u.shape[0], nh * tn), MXU_DTYPE)] * 2,
        compiler_params=_params(),
    )(d_s, u, u)
```

```python
import functools

import jax
import jax.numpy as jnp
import numpy as np
from jax import lax
from jax.experimental import pallas as pl
from jax.experimental.pallas import tpu as pltpu

F32 = jnp.float32
MXU_DTYPE = jnp.bfloat16
WIRE_DTYPE = jnp.bfloat16

GRID_W = 64
CHUNK = 64
N_HEADS = 4
GATE_RANK = 16
GATE_NORM = 16.0
N_MOD = 6
N_POOL = 4
LN_EPS = 1e-5
RMS_EPS = 1e-6
ADAM_LR = 0.001
ADAM_B1 = 0.9
ADAM_B2 = 0.999
ADAM_EPS = 1e-08
ADAM_WD = 0.01
ADAM_STEP = 10

LANES = 128
SUBLANES = 8
ALR_W = 256
PACK_W = 2048
VMEM_LIMIT = 56 * 1024 * 1024
N_CHIPS = 4
N_DEV = 8
MESH = pl.DeviceIdType.MESH

NN = ((1,), (0,))
NT = ((1,), (1,))
TN = ((0,), (0,))


def _dot(a, b, dims):
    return lax.dot_general(a.astype(MXU_DTYPE), b.astype(MXU_DTYPE), (dims, ((), ())),
                           preferred_element_type=F32)


def _dot_f32(a, b, dims):
    return lax.dot_general(a.astype(F32), b.astype(F32), (dims, ((), ())),
                           precision=lax.Precision.HIGHEST, preferred_element_type=F32)


def _dot_mask(mask, x, dims):
    m = mask.astype(MXU_DTYPE)
    if MXU_DTYPE == F32:
        return lax.dot_general(m, x, (dims, ((), ())), preferred_element_type=F32)
    acc = None
    rest = x
    for _ in range(3):
        piece = rest.astype(MXU_DTYPE)
        rest = rest - piece.astype(F32)
        part = lax.dot_general(m, piece, (dims, ((), ())), preferred_element_type=F32)
        acc = part if acc is None else acc + part
    return acc


def _dot_3x(a, b, dims):
    if MXU_DTYPE == F32:
        return lax.dot_general(a, b, (dims, ((), ())), preferred_element_type=F32)
    a_hi, b_hi = a.astype(MXU_DTYPE), b.astype(MXU_DTYPE)
    a_lo = (a - a_hi.astype(F32)).astype(MXU_DTYPE)
    b_lo = (b - b_hi.astype(F32)).astype(MXU_DTYPE)
    dot = lambda u, w: lax.dot_general(u, w, (dims, ((), ())), preferred_element_type=F32)
    return dot(a_hi, b_hi) + (dot(a_lo, b_hi) + dot(a_hi, b_lo))


def _pick(n, cands):
    for c in cands:
        if n % c == 0:
            return c
    return n


def _params():
    return pltpu.CompilerParams(vmem_limit_bytes=VMEM_LIMIT)


def _sigmoid(x):
    return 0.5 + 0.5 * jnp.tanh(0.5 * x)


def _silu(x):
    return x * _sigmoid(x)


def _dsilu(x):
    s = _sigmoid(x)
    return s * (1.0 + x * (1.0 - s))


def matmul(name, a, b, form, m, n, k, *, tm, tn, out_dtype=F32, a_off=(0, 0), b_off=(0, 0), bias=None, bias_off=0,
           precise=False, comm=None, second=None):
    assert m % tm == 0 and n % tn == 0, (name, m, n, tm, tn)
    if form == 'tn':
        a_spec = pl.BlockSpec((k, tm), lambda j, i: (a_off[0], i + a_off[1]))
    else:
        a_spec = pl.BlockSpec((tm, k), lambda j, i: (i + a_off[0], a_off[1]))
    if form == 'nt':
        b_spec = pl.BlockSpec((tn, k), lambda j, i: (j + b_off[0], b_off[1]))
    else:
        b_spec = pl.BlockSpec((k, tn), lambda j, i: (b_off[0], j + b_off[1]))
    dims = {'nn': NN, 'nt': NT, 'tn': TN}[form]
    in_specs = [a_spec, b_spec]
    args = [a, b]
    if bias is not None:
        in_specs.append(pl.BlockSpec((SUBLANES, tn), lambda j, i: (0, j + bias_off)))
        args.append(bias)

    if second is not None:
        a2, b2_off = second
        assert form == 'nt'
        in_specs += [pl.BlockSpec((tm, k), lambda j, i: (i, 0)), pl.BlockSpec((tn, k), lambda j, i: (j + b2_off[0], b2_off[1]))]
        args += [a2, b]
    n_own = len(args)
    nj, ni = n // tn, m // tm
    out_spec = pl.BlockSpec((tm, tn), lambda j, i: (i, j))
    out_shape = jax.ShapeDtypeStruct((m, n), out_dtype)

    def product(refs):
        acc = (_dot_f32 if precise else _dot)(refs[0][...], refs[1][...], dims)
        if bias is not None:
            acc = acc + refs[2][0:1, :]
        if second is not None:
            acc = acc + _dot(refs[n_own - 2][...], refs[n_own - 1][...], dims)
        return acc

    if comm is None:
        def body(*refs):
            refs[-1][...] = product(refs).astype(refs[-1].dtype)

        return pl.pallas_call(body, name=name, grid=(nj, ni), in_specs=in_specs, out_specs=out_spec,
                              out_shape=out_shape, compiler_params=_params())(*args)

    n_ci, n_co = len(comm.ins), len(comm.outs)

    def hosted(*refs):
        c_in = refs[n_own:n_own + n_ci]
        o_ref = refs[n_own + n_ci]
        c_out = refs[n_own + n_ci + 1:n_own + n_ci + 1 + n_co]
        sems = refs[n_own + n_ci + 1 + n_co:]
        j, i = pl.program_id(0), pl.program_id(1)

        @pl.when((j == 0) & (i == 0))
        def _():
            comm.start(c_in, c_out, sems)

        if comm.mid is not None:
            @pl.when((j == nj - 1) & (i == 0))
            def _():
                comm.mid(c_in, c_out, sems)

        o_ref[...] = product(refs).astype(o_ref.dtype)

        @pl.when((j == nj - 1) & (i == ni - 1))
        def _():
            comm.finish(c_in, c_out, sems)

    any_spec = pl.BlockSpec(memory_space=pl.ANY)
    res = pl.pallas_call(
        hosted, name=name, grid=(nj, ni), in_specs=in_specs + [any_spec] * n_ci,
        out_specs=[out_spec] + [any_spec] * n_co, out_shape=[out_shape] + list(comm.outs),
        scratch_shapes=list(comm.sems), compiler_params=_params(),
    )(*args, *comm.ins)
    return res[0], list(res[1:])


def ew(name, body, nrow, ncol, tm, tn, row_ins, vec_ins, row_outs, sum_outs=(), ctx_tiles=1, pass_i=False):
    def rmap(roff):
        return roff if callable(roff) else (lambda i: i + roff)

    in_specs = []
    for arr, roff, coff in row_ins:
        in_specs.append(pl.BlockSpec((tm, tn), functools.partial(lambda j, i, r, c: (r(i), j + c), r=rmap(roff), c=coff)))
    for arr, coff in vec_ins:
        in_specs.append(pl.BlockSpec((SUBLANES, tn), functools.partial(lambda j, i, c: (0, j + c), c=coff)))
    out_specs = [pl.BlockSpec((tm, tn), lambda j, i: (i, j)) for _ in row_outs]
    out_specs += [pl.BlockSpec((SUBLANES, tn), lambda j, i: (0, j)) for _ in sum_outs]
    out_shape = [jax.ShapeDtypeStruct((r, c), dt) for dt, r, c in row_outs]
    out_shape += [jax.ShapeDtypeStruct((SUBLANES, c), F32) for c, _ in sum_outs]
    n_row, n_vec, n_ro = len(row_ins), len(vec_ins), len(row_outs)

    def kern(*refs):
        j, i = pl.program_id(0), pl.program_id(1)
        rows = [r[...] for r in refs[:n_row]]
        vecs = [r[...] for r in refs[n_row:n_row + n_vec]]
        outs = refs[n_row + n_vec:]
        is_ctx = i < ctx_tiles
        res, sums = body(i if pass_i else is_ctx, j, rows, vecs)
        for ref, val in zip(outs[:n_ro], res):
            ref[...] = val.astype(ref.dtype)
        for ref, val, (_, by_class) in zip(outs[n_ro:], sums, sum_outs):
            @pl.when(i == 0)
            def _():
                ref[...] = jnp.zeros_like(ref)
            if by_class:
                ref[0:1, :] += jnp.where(is_ctx, val, 0.0)
                ref[1:2, :] += jnp.where(is_ctx, 0.0, val)
            else:
                ref[0:1, :] += val

    outs = pl.pallas_call(
        kern, name=name, grid=(ncol, nrow), in_specs=in_specs, out_specs=out_specs, out_shape=out_shape,
        compiler_params=_params(),
    )(*[a for a, _, _ in row_ins], *[a for a, _ in vec_ins])
    return list(outs)


def _cls(vec, is_ctx):
    return jnp.where(is_ctx, vec[0:1, :], vec[1:2, :])


def _colsum(x):
    return jnp.sum(x, axis=0, keepdims=True)


def _chunk_map(cfg, rev):
    nctx, nc = cfg.CL // CHUNK, cfg.R // CHUNK
    if not rev:
        return lambda s: s
    return lambda s: jnp.where(s < nctx, nctx - 1 - s, nctx + nc - 1 - s)


def _gla_chunk(q_ref, k_ref, a_ref, wd_ref, bd_ref, rev, scale):
    q = q_ref[...] * scale
    k = k_ref[...]
    z = _dot_3x(a_ref[...], wd_ref[...], NN) + bd_ref[0:1, :]
    la = (jnp.minimum(z, 0.0) - jnp.log(1.0 + jnp.exp(-jnp.abs(z)))) * (1.0 / GATE_NORM)
    r = lax.broadcasted_iota(jnp.int32, (CHUNK, CHUNK), 0)
    c = lax.broadcasted_iota(jnp.int32, (CHUNK, CHUNK), 1)
    keep = (r <= c) if rev else (r >= c)
    tri = keep.astype(F32)
    cum = _dot_mask(tri, la, NN)
    mid = CHUNK // 2 if rev else CHUNK // 2 - 1
    end = 0 if rev else CHUNK - 1
    ref = cum[mid:mid + 1, :]
    last = cum[end:end + 1, :]
    return dict(q=q, k=k, z=z, keep=keep, tri=tri, q_in=q * jnp.exp(cum - ref), k_in=k * jnp.exp(ref - cum),
                e_q=jnp.exp(cum), e_k=jnp.exp(last - cum), e_inq=jnp.exp(cum - ref), e_ink=jnp.exp(ref - cum),
                e_last=jnp.exp(last))


def _gla_in_specs(cfg, rows_of):
    dk, dv = cfg.DK, cfg.DV
    return [
        pl.BlockSpec((CHUNK, dk), lambda s: (rows_of(s), 0)),
        pl.BlockSpec((CHUNK, dk), lambda s: (rows_of(s), 1)),
        pl.BlockSpec((CHUNK, dv), lambda s: (rows_of(s), 2 * dk // dv)),
        pl.BlockSpec((CHUNK, ALR_W), lambda s: (rows_of(s), (cfg.NP - ALR_W) // ALR_W)),
        pl.BlockSpec((ALR_W, dk), lambda s: (0, 0)),
        pl.BlockSpec((SUBLANES, dk), lambda s: (0, 0)),
    ]


def gla_fwd(name, proj, wd, bd, cfg, rev, comm=None):
    hk, hv = cfg.HK, cfg.HV
    nc = cfg.R // CHUNK
    cmap = _chunk_map(cfg, rev)
    scale = hk ** -0.5
    n_ci, n_co = (len(comm.ins), len(comm.outs)) if comm else (0, 0)

    def body(*refs):
        q_ref, k_ref, v_ref, a_ref, wd_ref, bd_ref = refs[:6]
        c_in = refs[6:6 + n_ci]
        o_ref, ss_ref = refs[6 + n_ci:8 + n_ci]
        c_out = refs[8 + n_ci:8 + n_ci + n_co]
        st_scr = refs[8 + n_ci + n_co]
        sems = refs[9 + n_ci + n_co:]
        s = pl.program_id(0)

        @pl.when(s == 0)
        def _():
            st_scr[...] = jnp.zeros_like(st_scr)
            if comm:
                comm.start(c_in, c_out, sems)

        t = _gla_chunk(q_ref, k_ref, a_ref, wd_ref, bd_ref, rev, scale)
        q_int, k_st = t['q'] * t['e_q'], t['k'] * t['e_k']
        for h in range(N_HEADS):
            ks, vs = slice(h * hk, (h + 1) * hk), slice(h * hv, (h + 1) * hv)
            v = v_ref[:, vs]
            st = st_scr[h]
            ss_ref[0, h] = st
            a = jnp.where(t['keep'], _dot(t['q_in'][:, ks], t['k_in'][:, ks], NT), 0.0)
            o_ref[:, vs] = _dot(a, v, NN) + _dot(q_int[:, ks], st, NT)
            st_scr[h] = st * t['e_last'][:, ks] + _dot(v, k_st[:, ks], TN)

        if comm:
            if comm.mid is not None:
                @pl.when(s == max(nc - 12, 0))
                def _():
                    comm.mid(c_in, c_out, sems)

            @pl.when(s == nc - 1)
            def _():
                comm.finish(c_in, c_out, sems)

    any_spec = pl.BlockSpec(memory_space=pl.ANY)
    res = pl.pallas_call(
        body, name=name, grid=(nc,), in_specs=_gla_in_specs(cfg, cmap) + [any_spec] * n_ci,
        out_specs=[pl.BlockSpec((CHUNK, cfg.DV), lambda s: (cmap(s), 0)),
                   pl.BlockSpec((1, N_HEADS, hv, hk), lambda s: (s, 0, 0, 0))] + [any_spec] * n_co,
        out_shape=[jax.ShapeDtypeStruct((cfg.R, cfg.DV), F32),
                   jax.ShapeDtypeStruct((nc, N_HEADS, hv, hk), F32)] + (list(comm.outs) if comm else []),
        scratch_shapes=[pltpu.VMEM((N_HEADS, hv, hk), F32)] + (list(comm.sems) if comm else []),
        compiler_params=_params(),
    )(proj, proj, proj, proj, wd, bd, *(comm.ins if comm else []))
    return res[0], res[1], list(res[2:])


def gla_bwd(name, proj, wd, bd, states, d_o, cfg, rev, addends=None):
    hk, hv = cfg.HK, cfg.HV
    nc = cfg.R // CHUNK
    cmap = _chunk_map(cfg, rev)
    rows_of = lambda g: cmap(nc - 1 - g)
    scale = hk ** -0.5
    n_add = 0 if addends is None else 4

    def body(*refs):
        q_ref, k_ref, v_ref, a_ref, wd_ref, bd_ref, ss_ref, do_ref = refs[:8]
        adds = refs[8:8 + n_add]
        dq_ref, dk_ref, dv_ref, da_ref, dwd_ref, dbd_ref, dst_scr = refs[8 + n_add:]
        g = pl.program_id(0)

        @pl.when(g == 0)
        def _():
            dst_scr[...] = jnp.zeros_like(dst_scr)
            dwd_ref[...] = jnp.zeros_like(dwd_ref)
            dbd_ref[...] = jnp.zeros_like(dbd_ref)

        t = _gla_chunk(q_ref, k_ref, a_ref, wd_ref, bd_ref, rev, scale)
        q_int, k_st = t['q'] * t['e_q'], t['k'] * t['e_k']
        dq_h, dk_h, carry_h = [], [], []
        for h in range(N_HEADS):
            ks, vs = slice(h * hk, (h + 1) * hk), slice(h * hv, (h + 1) * hv)
            v = v_ref[:, vs]
            d_out = do_ref[:, vs]
            st = ss_ref[0, h]
            dst = dst_scr[h]
            a = jnp.where(t['keep'], _dot(t['q_in'][:, ks], t['k_in'][:, ks], NT), 0.0)
            da = jnp.where(t['keep'], _dot(d_out, v, NT), 0.0)
            dv = _dot(a, d_out, TN) + _dot(k_st[:, ks], dst, NT)
            dq_h.append(_dot(d_out, st, NN) * t['e_q'][:, ks] + _dot(da, t['k_in'][:, ks], NN) * t['e_inq'][:, ks])
            dk_h.append(_dot(v, dst, NN) * t['e_k'][:, ks] + _dot(da, t['q_in'][:, ks], TN) * t['e_ink'][:, ks])
            dst_scr[h] = dst * t['e_last'][:, ks] + _dot(d_out, q_int[:, ks], TN)
            st_end = st * t['e_last'][:, ks] + _dot(v, k_st[:, ks], TN)
            carry_h.append(_colsum(dst * st_end))
            dv_ref[:, vs] = dv + adds[2][:, vs] if n_add else dv
        dq = jnp.concatenate(dq_h, axis=-1)
        dk = jnp.concatenate(dk_h, axis=-1)
        dg = t['q'] * dq - t['k'] * dk
        dla = _dot_mask(t['tri'], dg, TN) + jnp.concatenate(carry_h, axis=-1)
        dz = dla * (1.0 / GATE_NORM) * _sigmoid(-t['z'])
        dalr = _dot_3x(dz, wd_ref[...], NT)
        dwd_ref[...] += _dot_3x(a_ref[...], dz, TN)
        dbd_ref[...] += jnp.broadcast_to(_colsum(dz), dbd_ref.shape)
        dq = dq * scale
        if n_add:
            dq, dk, dalr = dq + adds[0][...], dk + adds[1][...], dalr + adds[3][...]
        dq_ref[...] = dq
        dk_ref[...] = dk
        da_ref[...] = dalr

    qk_spec = pl.BlockSpec((CHUNK, cfg.DK), lambda g: (rows_of(g), 0))
    v_spec = pl.BlockSpec((CHUNK, cfg.DV), lambda g: (rows_of(g), 0))
    a_spec = pl.BlockSpec((CHUNK, ALR_W), lambda g: (rows_of(g), 0))
    in_specs = _gla_in_specs(cfg, rows_of) + [
        pl.BlockSpec((1, N_HEADS, hv, hk), lambda g: (nc - 1 - g, 0, 0, 0)), v_spec]
    args = [proj, proj, proj, proj, wd, bd, states, d_o]
    if n_add:
        in_specs += [qk_spec, qk_spec, v_spec, a_spec]
        args += list(addends)
    return pl.pallas_call(
        body, name=name, grid=(nc,), in_specs=in_specs,
        out_specs=[qk_spec, qk_spec, v_spec, a_spec,
                   pl.BlockSpec((ALR_W, cfg.DK), lambda g: (0, 0)),
                   pl.BlockSpec((SUBLANES, cfg.DK), lambda g: (0, 0))],
        out_shape=[jax.ShapeDtypeStruct((cfg.R, cfg.DK), F32), jax.ShapeDtypeStruct((cfg.R, cfg.DK), F32),
                   jax.ShapeDtypeStruct((cfg.R, cfg.DV), F32), jax.ShapeDtypeStruct((cfg.R, ALR_W), F32),
                   jax.ShapeDtypeStruct((ALR_W, cfg.DK), F32),
                   jax.ShapeDtypeStruct((SUBLANES, cfg.DK), F32)],
        scratch_shapes=[pltpu.VMEM((N_HEADS, hv, hk), F32)],
        compiler_params=_params(),
    )(*args)


def pool_mix(name, src, coff, cfg, transpose):
    tm, pg = cfg.TM, cfg.PG
    mt = cfg.R // tm
    reach = -(-(max(2 ** N_POOL // 2, 1) * GRID_W) // tm)
    nk = 2 * reach + 1
    img_rows = cfg.S // GRID_W
    shift = GRID_W.bit_length() - 1

    def ktile(m, d):
        return jnp.where(m == 0, 0, jnp.clip(m + d - reach, 1, mt - 1))

    def counts(idx, is_ctx, lo, hi):
        ctx_n = jnp.minimum(idx + hi + 1, cfg.CL) - jnp.maximum(idx - lo, 0)
        r, c = idx >> shift, idx & (GRID_W - 1)
        lat_n = ((jnp.minimum(r + hi + 1, img_rows) - jnp.maximum(r - lo, 0))
                 * (jnp.minimum(c + hi + 1, GRID_W) - jnp.maximum(c - lo, 0)))
        return jnp.where(is_ctx, ctx_n, lat_n).astype(F32)

    a = np.arange(tm)[:, None]
    b = np.arange(tm)[None, :]
    masks = np.zeros((N_POOL, nk + 1, tm, tm), np.float32)
    for g_ in range(N_POOL):
        lo_ = 2 ** g_
        hi_ = lo_ - 1
        for d_ in range(nk):
            dr = (d_ - reach) * (tm // GRID_W) + (b >> shift) - (a >> shift)
            dc = (b & (GRID_W - 1)) - (a & (GRID_W - 1))
            masks[g_, d_] = (dr >= -lo_) & (dr <= hi_) & (dc >= -lo_) & (dc <= hi_)
        masks[g_, nk] = (b - a >= -lo_) & (b - a <= hi_)
    masks = jnp.asarray(masks, MXU_DTYPE)

    def body(src_ref, self_ref, mask_ref, o_ref, acc):
        m, d = pl.program_id(0), pl.program_id(1)
        is_ctx = m == 0
        kt = m + d - reach
        valid = jnp.where(is_ctx, d == reach, (kt >= 1) & (kt <= mt - 1))
        seg = jnp.where(is_ctx, 0, cfg.CL)

        @pl.when(d == 0)
        def _():
            acc[...] = jnp.zeros_like(acc)

        @pl.when(valid)
        def _():
            for g in range(N_POOL):
                cols = slice(g * pg, (g + 1) * pg)
                x = src_ref[:, cols]
                if transpose:
                    kidx = lax.broadcasted_iota(jnp.int32, (tm, 1), 0) + (kt * tm - seg)
                    acc[:, cols] += _dot_mask(mask_ref[g, 0], x / counts(kidx, is_ctx, 2 ** g, 2 ** g - 1), TN)
                else:
                    acc[:, cols] += _dot_mask(mask_ref[g, 0], x, NN)

        @pl.when(d == nk - 1)
        def _():
            for g in range(N_POOL):
                cols = slice(g * pg, (g + 1) * pg)
                res = acc[:, cols]
                if not transpose:
                    midx = lax.broadcasted_iota(jnp.int32, (tm, 1), 0) + (m * tm - seg)
                    res = res / counts(midx, is_ctx, 2 ** g, 2 ** g - 1)
                o_ref[:, cols] = (res - self_ref[:, cols]).astype(o_ref.dtype)

    which = (lambda d: 2 * reach - d) if transpose else (lambda d: d)
    assert coff % N_POOL == 0
    dp = N_POOL * pg
    return pl.pallas_call(
        body, name=name, grid=(mt, nk),
        in_specs=[pl.BlockSpec((tm, dp), lambda m, d: (ktile(m, d), coff // N_POOL)),
                  pl.BlockSpec((tm, dp), lambda m, d: (m, coff // N_POOL)),
                  pl.BlockSpec((N_POOL, 1, tm, tm), lambda m, d: (0, jnp.where(m == 0, nk, which(d)), 0, 0))],
        out_specs=pl.BlockSpec((tm, dp), lambda m, d: (m, 0)),
        out_shape=jax.ShapeDtypeStruct((cfg.R, cfg.DP), F32),
        scratch_shapes=[pltpu.VMEM((tm, dp), F32)], compiler_params=_params(),
    )(src, src, masks)


def _my_place():
    return lax.axis_index("x"), lax.axis_index("y"), lax.axis_index("c")


def _flip(v, bit):
    return 1 - v if bit else v


def all_gather8(name, block):
    rows, w = block.shape

    def body(x_ref, out_ref, send_sems, recv_sems, local_sem):
        x, y, c = _my_place()
        me = 4 * x + 2 * y + c
        mine = pltpu.make_async_copy(x_ref, out_ref.at[me], local_sem)
        mine.start()
        sends = []
        for k in range(1, N_DEV):
            peer = (_flip(x, k & 4), _flip(y, k & 2), _flip(c, k & 1))
            cp = pltpu.make_async_remote_copy(src_ref=x_ref, dst_ref=out_ref.at[me], send_sem=send_sems.at[k - 1],
                                              recv_sem=recv_sems.at[k - 1], device_id=peer, device_id_type=MESH)
            cp.start()
            sends.append(cp)
        for k in range(1, N_DEV):
            peer = (_flip(x, k & 4), _flip(y, k & 2), _flip(c, k & 1))
            slot = 4 * peer[0] + 2 * peer[1] + peer[2]
            pltpu.make_async_remote_copy(src_ref=x_ref, dst_ref=out_ref.at[slot], send_sem=send_sems.at[k - 1],
                                         recv_sem=recv_sems.at[k - 1], device_id=peer, device_id_type=MESH).wait_recv()
        for cp in sends:
            cp.wait_send()
        mine.wait()

    return pl.pallas_call(
        body, name=name, out_shape=jax.ShapeDtypeStruct((N_DEV, rows, w), block.dtype),
        in_specs=[pl.BlockSpec(memory_space=pl.ANY)], out_specs=pl.BlockSpec(memory_space=pl.ANY),
        scratch_shapes=[pltpu.SemaphoreType.DMA((N_DEV - 1,)), pltpu.SemaphoreType.DMA((N_DEV - 1,)),
                        pltpu.SemaphoreType.DMA],
    )(block)


class _Comm:
    def __init__(self, ins, outs, sems, start, finish, mid=None):
        self.ins, self.outs, self.sems, self.start, self.finish = list(ins), list(outs), list(sems), start, finish
        self.mid = mid


def run_comm(name, comm):
    def body(*refs):
        n_i, n_o = len(comm.ins), len(comm.outs)
        comm.start(refs[:n_i], refs[n_i:n_i + n_o], refs[n_i + n_o:])
        if comm.mid is not None:
            comm.mid(refs[:n_i], refs[n_i:n_i + n_o], refs[n_i + n_o:])
        comm.finish(refs[:n_i], refs[n_i:n_i + n_o], refs[n_i + n_o:])

    any_spec = pl.BlockSpec(memory_space=pl.ANY)
    return list(pl.pallas_call(body, name=name, out_shape=comm.outs, in_specs=[any_spec] * len(comm.ins),
                               out_specs=[any_spec] * len(comm.outs), scratch_shapes=comm.sems)(*comm.ins))


def gather_plan(shards, n_split):
    n = len(shards)

    def part(ref, t, core):
        if t >= n_split:
            return ref
        h = shards[t].shape[0] // 2
        return ref.at[pl.ds(core * h, h)]

    def ici(srcs, dsts, sems, k, t, slot, place):
        x, y, c = place
        return pltpu.make_async_remote_copy(
            src_ref=part(srcs[t], t, c), dst_ref=part(dsts[t].at[slot], t, c), send_sem=sems[0].at[t * 3 + k - 1],
            recv_sem=sems[1].at[t * 3 + k - 1], device_id=(_flip(x, k & 2), _flip(y, k & 1), c), device_id_type=MESH)

    def handed(dsts, sems, k, t, slot, place, core):
        x, y, c = place
        half = part(dsts[t].at[slot], t, core)
        return pltpu.make_async_remote_copy(
            src_ref=half, dst_ref=half, send_sem=sems[2].at[t * 3 + k - 1], recv_sem=sems[3].at[t * 3 + k - 1],
            device_id=(x, y, 1 - c), device_id_type=MESH)

    def start(srcs, dsts, sems):
        place = _my_place()
        me = 2 * place[0] + place[1]
        for t in range(n):
            pltpu.make_async_copy(srcs[t], dsts[t].at[me], sems[4].at[t]).start()
        for k in range(1, N_CHIPS):
            for t in range(n):
                ici(srcs, dsts, sems, k, t, me, place).start()

    def mid(srcs, dsts, sems):
        place = _my_place()
        x, y, c = place
        for k in range(1, N_CHIPS):
            slot = 2 * _flip(x, k & 2) + _flip(y, k & 1)
            for t in range(n):
                ici(srcs, dsts, sems, k, t, slot, place).wait_recv()
                if t < n_split:
                    handed(dsts, sems, k, t, slot, place, c).start()

    def finish(srcs, dsts, sems):
        place = _my_place()
        x, y, c = place
        me = 2 * x + y
        for k in range(1, N_CHIPS):
            slot = 2 * _flip(x, k & 2) + _flip(y, k & 1)
            for t in range(n_split):
                handed(dsts, sems, k, t, slot, place, 1 - c).wait_recv()
        for k in range(1, N_CHIPS):
            slot = 2 * _flip(x, k & 2) + _flip(y, k & 1)
            for t in range(n):
                ici(srcs, dsts, sems, k, t, me, place).wait_send()
                if t < n_split:
                    handed(dsts, sems, k, t, slot, place, c).wait_send()
        for t in range(n):
            pltpu.make_async_copy(srcs[t], dsts[t].at[me], sems[4].at[t]).wait()

    outs = [jax.ShapeDtypeStruct((N_CHIPS,) + s.shape, s.dtype) for s in shards]
    sems = [pltpu.SemaphoreType.DMA((3 * n,))] * 4 + [pltpu.SemaphoreType.DMA((n,))]
    return _Comm(shards, outs, sems, start, finish, mid)


def scatter_plan(triples):
    n = len(triples)
    bufs = [b for tr in triples for b in tr]

    def copies(srcs, dsts, sems):
        x, y, c = _my_place()
        return [pltpu.make_async_remote_copy(
            src_ref=srcs[t * 3 + k - 1], dst_ref=dsts[t * 3 + k - 1], send_sem=sems[0].at[t * 3 + k - 1],
            recv_sem=sems[1].at[t * 3 + k - 1], device_id=(_flip(x, k & 2), _flip(y, k & 1), c), device_id_type=MESH)
            for k in range(1, N_CHIPS) for t in range(n)]

    def start(srcs, dsts, sems):
        for cp in copies(srcs, dsts, sems):
            cp.start()

    def finish(srcs, dsts, sems):
        for cp in copies(srcs, dsts, sems):
            cp.wait_recv()
        for cp in copies(srcs, dsts, sems):
            cp.wait_send()

    return _Comm(bufs, [jax.ShapeDtypeStruct(b.shape, b.dtype) for b in bufs],
                 [pltpu.SemaphoreType.DMA((3 * n,)), pltpu.SemaphoreType.DMA((3 * n,))], start, finish)


def swap_plan(bufs):
    n = len(bufs)

    def copies(srcs, dsts, sems):
        x, y, c = _my_place()
        return [pltpu.make_async_remote_copy(src_ref=srcs[t], dst_ref=dsts[t], send_sem=sems[0].at[t],
                                             recv_sem=sems[1].at[t], device_id=(x, y, 1 - c), device_id_type=MESH)
                for t in range(n)]

    def start(srcs, dsts, sems):
        for cp in copies(srcs, dsts, sems):
            cp.start()

    def finish(srcs, dsts, sems):
        for cp in copies(srcs, dsts, sems):
            cp.wait_recv()
        for cp in copies(srcs, dsts, sems):
            cp.wait_send()

    return _Comm(bufs, [jax.ShapeDtypeStruct(b.shape, b.dtype) for b in bufs],
                 [pltpu.SemaphoreType.DMA((n,)), pltpu.SemaphoreType.DMA((n,))], start, finish)


def _vec8(*rows):
    w = rows[0].shape[-1]
    out = jnp.zeros((SUBLANES, w), F32)
    for r, v in enumerate(rows):
        out = out.at[r].set(v.reshape(w).astype(F32))
    return out


def _pack(arrays):
    parts = []
    for a in arrays:
        flat = a.reshape(-1).astype(F32)
        pad = (-flat.shape[0]) % PACK_W
        parts.append(jnp.pad(flat, (0, pad)))
    flat = jnp.concatenate(parts)
    pad = (-flat.shape[0]) % (PACK_W * SUBLANES)
    return jnp.pad(flat, (0, pad)).reshape(-1, PACK_W)


def _unpack(packed, shapes):
    flat = packed.reshape(-1)
    out, pos = [], 0
    for shp in shapes:
        size = 1
        for d in shp:
            size *= d
        out.append(flat[pos:pos + size].reshape(shp))
        pos += size + (-size) % PACK_W
    return out


def _rows2d(a):
    return a.reshape(-1, a.shape[-1])


class _Cfg:
    pass


def _adam(name, w, grads, m, v, half_rows=None, core=None):
    rows, width = w.shape
    c1 = 1.0 - ADAM_B1 ** ADAM_STEP
    c2 = 1.0 - ADAM_B2 ** ADAM_STEP

    def update(wv, mv, vv, g):
        m_new = ADAM_B1 * mv + (1.0 - ADAM_B1) * g
        v_new = ADAM_B2 * vv + (1.0 - ADAM_B2) * (g * g)
        delta = -ADAM_LR * ((m_new / c1) / (jnp.sqrt(v_new / c2) + ADAM_EPS) + ADAM_WD * wv)
        return [g, delta, m_new, v_new], []

    if half_rows is None:
        tm = _pick(rows, (128, 64, 32, 16, 8))

        def body(i, j, rows_, vecs):
            g = rows_[3]
            for extra in rows_[4:]:
                g = g + extra
            return update(rows_[0], rows_[1], rows_[2], g)

        return ew(name, body, rows // tm, 1, tm, width, [(w, 0, 0), (m, 0, 0), (v, 0, 0)] + [(g, 0, 0) for g in grads], [],
                  [(F32, rows, width)] * 4)

    tm = _pick(half_rows, (128, 64, 32, 16, 8))
    nb = half_rows // tm
    mine, theirs = grads
    n_layers = len(mine)
    core_vec = jnp.broadcast_to(core.astype(F32), (SUBLANES, width))

    def layer_tile(l):
        return lambda i: jnp.clip(i - l * 2 * nb, 0, 2 * nb - 1) % nb

    def body(i, j, rows_, vecs):
        is_mine = vecs[0][0:1, :] == ((i // nb) % 2).astype(F32)
        layer = i // (2 * nb)
        g = jnp.where(is_mine, rows_[3], rows_[3 + n_layers])
        for l in range(1, n_layers):
            g = jnp.where(layer == l, jnp.where(is_mine, rows_[3 + l], rows_[3 + n_layers + l]), g)
        return update(rows_[0], rows_[1], rows_[2], g)

    return ew(name, body, rows // tm, 1, tm, width,
              [(w, 0, 0), (m, 0, 0), (v, 0, 0)] + [(a_, layer_tile(l), 0) for l, a_ in enumerate(mine)]
              + [(a_, layer_tile(l), 0) for l, a_ in enumerate(theirs)], [(core_vec, 0)],
              [(F32, rows, width)] * 4, pass_i=True)


def _sum_rows(name, arr, nparts, rows, width, dtype=F32):
    tm = _pick(rows, (256, 128, 64, 32, 16, 8))
    nblk = rows // tm

    def body(i, j, rows_, vecs):
        acc = rows_[0].astype(F32)
        for r in rows_[1:]:
            acc = acc + r.astype(F32)
        return [acc], []

    return ew(name, body, nblk, 1, tm, width, [(arr, p * nblk, 0) for p in range(nparts)], [], [(dtype, rows, width)])[0]


def kernel(x, c, ctx, c_ctx, w_ada, b_ada, w_in, w_decay_up, b_decay_up, gla_norm_gain, w_pool_group, pool_scale, w_gla_out, w_pool_out, w_out, ln_mix_gain, ln_mix_bias, w_ffn_in, w_ffn_out, ln_ffn_gain, ln_ffn_bias, loss_target, m_c_ctx, m_w_ada, m_b_ada, m_w_in, m_w_decay_up, m_b_decay_up, m_gla_norm_gain, m_w_pool_group, m_pool_scale, m_w_gla_out, m_w_pool_out, m_w_out, m_ln_mix_gain, m_ln_mix_bias, m_w_ffn_in, m_w_ffn_out, m_ln_ffn_gain, m_ln_ffn_bias, v_c_ctx, v_w_ada, v_b_ada, v_w_in, v_w_decay_up, v_b_decay_up, v_gla_norm_gain, v_w_pool_group, v_pool_scale, v_w_gla_out, v_w_pool_out, v_w_out, v_ln_mix_gain, v_ln_mix_bias, v_w_ffn_in, v_w_ffn_out, v_ln_ffn_gain, v_ln_ffn_bias):
    weights = dict(c_ctx=c_ctx, w_ada=w_ada, b_ada=b_ada, w_in=w_in, w_decay_up=w_decay_up, b_decay_up=b_decay_up,
                   gla_norm_gain=gla_norm_gain, w_pool_group=w_pool_group, pool_scale=pool_scale, w_gla_out=w_gla_out,
                   w_pool_out=w_pool_out, w_out=w_out, ln_mix_gain=ln_mix_gain, ln_mix_bias=ln_mix_bias,
                   w_ffn_in=w_ffn_in, w_ffn_out=w_ffn_out, ln_ffn_gain=ln_ffn_gain, ln_ffn_bias=ln_ffn_bias)
    mom1 = dict(c_ctx=m_c_ctx, w_ada=m_w_ada, b_ada=m_b_ada, w_in=m_w_in, w_decay_up=m_w_decay_up, b_decay_up=m_b_decay_up,
                gla_norm_gain=m_gla_norm_gain, w_pool_group=m_w_pool_group, pool_scale=m_pool_scale, w_gla_out=m_w_gla_out,
                w_pool_out=m_w_pool_out, w_out=m_w_out, ln_mix_gain=m_ln_mix_gain, ln_mix_bias=m_ln_mix_bias,
                w_ffn_in=m_w_ffn_in, w_ffn_out=m_w_ffn_out, ln_ffn_gain=m_ln_ffn_gain, ln_ffn_bias=m_ln_ffn_bias)
    mom2 = dict(c_ctx=v_c_ctx, w_ada=v_w_ada, b_ada=v_b_ada, w_in=v_w_in, w_decay_up=v_w_decay_up, b_decay_up=v_b_decay_up,
                gla_norm_gain=v_gla_norm_gain, w_pool_group=v_w_pool_group, pool_scale=v_pool_scale, w_gla_out=v_w_gla_out,
                w_pool_out=v_w_pool_out, w_out=v_w_out, ln_mix_gain=v_ln_mix_gain, ln_mix_bias=v_ln_mix_bias,
                w_ffn_in=v_w_ffn_in, w_ffn_out=v_w_ffn_out, ln_ffn_gain=v_ln_ffn_gain, ln_ffn_bias=v_ln_ffn_bias)
    names = list(weights)

    cfg = _Cfg()
    L, D = w_ada.shape[0], x.shape[-1]
    S, CL = x.shape[1], ctx.shape[1]
    cfg.L, cfg.D, cfg.S, cfg.CL, cfg.R, cfg.TM = L, D, S, CL, S + CL, CL
    DK, DV, DP = D // 2, D, D // 2
    cfg.DK, cfg.DV, cfg.DP = DK, DV, DP
    cfg.HK, cfg.HV, cfg.PG = DK // N_HEADS, DV // N_HEADS, DP // N_POOL
    DFF = w_ffn_out.shape[1] * N_CHIPS
    NP = 2 * DK + 2 * DV + DP + 2 * D + ALR_W
    cfg.NP, cfg.DFF = NP, DFF
    R, TM, HK, HV, PG = cfg.R, cfg.TM, cfg.HK, cfg.HV, cfg.PG
    MT = R // TM
    alpha = (2.0 * L) ** 0.25
    assert S % TM == 0 and TM % CHUNK == 0 and S % GRID_W == 0 and TM % GRID_W == 0
    OFF_G, OFF_P, OFF_BGA, OFF_BGB = 2 * DK + DV, 2 * DK + 2 * DV, 2 * DK + 2 * DV + DP, 2 * DK + 2 * DV + DP + D
    ALR0 = 2 * DK + 2 * DV
    TE = 512
    TL = TM // 2
    assert D % TE == 0 and DP % TE == 0

    xi, yi, ci = _my_place()
    chip = 2 * xi + yi
    dev = 4 * xi + 2 * yi + ci

    n_ada = w_ada.shape[-1]
    c_all = all_gather8("ag_cond", jnp.pad(c.reshape(1, D), ((0, SUBLANES - 1), (0, 0))))[:, 0, :]
    cond = jnp.concatenate([c_all, c_ctx.reshape(1, D), jnp.zeros((16 - N_DEV - 1, D), F32)], axis=0)

    def silu_body(i, j, rows_, vecs):
        return [_silu(rows_[0]), _dsilu(rows_[0])], []

    act, dact = ew("cond_silu", silu_body, 1, 1, 16, D, [(cond, 0, 0)], [], [(F32, 16, D)] * 2)
    w_ada2 = w_ada.reshape(L * D, n_ada)
    b_ada_mine = lax.dynamic_slice_in_dim(b_ada, chip * n_ada, n_ada, axis=1)
    tn_ada = _pick(n_ada, (1024, 512, 256, 128))
    mods = [matmul("ada_fwd", act, w_ada2, 'nn', 16, n_ada, D, tm=16, tn=tn_ada, b_off=(l, 0),
                   bias=_vec8(b_ada_mine[l]), precise=True) for l in range(L)]
    mods_all = all_gather8("ag_mods", jnp.concatenate(mods, axis=0))
    mods_all = mods_all[0::2].reshape(N_CHIPS, L, 16, n_ada).transpose(1, 2, 0, 3).reshape(L, 16, N_MOD * D)
    modv = [_vec8(mods_all[l, N_DEV], lax.dynamic_index_in_dim(mods_all[l], dev, 0, keepdims=False)) for l in range(L)]
    MB = D // TE

    big = ['w_in', 'w_gla_out', 'w_pool_out', 'w_out', 'w_ffn_in', 'w_ffn_out', 'w_pool_group']
    hosts = [['w_in'], ['w_gla_out', 'w_out', 'w_pool_out', 'w_pool_group'], ['w_ffn_out'], ['w_ffn_in']]
    all_hosted = sum(hosts, [])
    layer_shards = lambda nms, l: [weights[nm][l].astype(WIRE_DTYPE) for nm in nms]

    def cols_together(g):
        return jnp.moveaxis(g, 0, -2).reshape(g.shape[1:-1] + (N_CHIPS * g.shape[-1],))

    def rows_together(g):
        return g.reshape((N_CHIPS * g.shape[1], g.shape[2]))

    te_ff = _pick(DFF, (1408, 1024, 512, 256, 128))

    def assemble(gw):
        win_ref = cols_together(gw['w_in'])
        win = jnp.concatenate([win_ref[:, :ALR0], win_ref[:, ALR0 + 2 * GATE_RANK:], win_ref[:, ALR0:ALR0 + 2 * GATE_RANK],
                               jnp.zeros((D, ALR_W - 2 * GATE_RANK), WIRE_DTYPE)], axis=-1)
        wpg = jnp.moveaxis(gw['w_pool_group'], 0, 1).reshape(N_POOL, PG, PG)
        wpg_bd = jnp.zeros((N_POOL, PG, N_POOL, PG), WIRE_DTYPE)
        for g in range(N_POOL):
            wpg_bd = wpg_bd.at[g, :, g, :].set(wpg[g])
        win_bwd = jnp.pad(win_ref, ((0, 0), (0, NP - win_ref.shape[1])))
        return dict(win=win, win_bwd=win_bwd, wgo=rows_together(gw['w_gla_out']), wpo=cols_together(gw['w_pool_out']),
                    wout=rows_together(gw['w_out']), wfi=cols_together(gw['w_ffn_in']), wfo=rows_together(gw['w_ffn_out']),
                    wpg_bd=wpg_bd.reshape(DP, DP))

    first = run_comm("gather_weights", gather_plan(layer_shards(all_hosted, 0) + [w_decay_up, b_decay_up],
                                                   n_split=len(big)))
    W = [assemble(dict(zip(all_hosted, first[:len(big)])))] + [None] * (L - 1)
    next_plan = lambda l, k: gather_plan(layer_shards(hosts[k], l + 1), n_split=len(hosts[k]))
    wdu = cols_together(first[len(big)])
    bdu = cols_together(first[len(big) + 1])
    wd_pad = [[jnp.zeros((ALR_W, DK), F32).at[d * GATE_RANK:(d + 1) * GATE_RANK].set(wdu[l, d]) for d in range(2)]
              for l in range(L)]
    bd_pad = [[_vec8(bdu[l, d]) for d in range(2)] for l in range(L)]

    tn_np = _pick(NP, (1280, 1024, 768, 512, 256, 128))
    tn_d = _pick(D, (1024, 512, 256, 128))
    tn_ff2 = _pick(2 * DFF, (1024, 512, 256, 128))
    tn_ff = _pick(DFF, (2816, 1408, 1024, 512, 256, 128))
    tn_dp = _pick(DP, (1024, 512, 256, 128))
    tw = lambda n_: _pick(n_, (512, 256, 128))

    gain8 = lambda v_: _vec8(v_)

    xs = jnp.concatenate([ctx.reshape(CL, D), x.reshape(S, D)], axis=0)
    saved = []
    for l in range(L):
        sv = {}
        mv = modv[l]

        def mod_body(i, j, rows_, vecs):
            return [rows_[0] * (1.0 + _cls(vecs[1], i)) + _cls(vecs[0], i)], []

        if l == 0:
            h1 = ew("modulate", mod_body, MT, D // tn_d, TM, tn_d, [(xs, 0, 0)],
                    [(mv, 0 * (D // tn_d)), (mv, 1 * (D // tn_d))], [(MXU_DTYPE, R, D)])[0]
        Wl = W[l]
        if l + 1 < L:
            proj, got0 = matmul("proj_and_gather", h1, Wl['win'], 'nn', R, NP, D, tm=TM, tn=tn_np, comm=next_plan(l, 0))
            o_f, st_f, got1 = gla_fwd("gla_fwd_and_gather", proj, wd_pad[l][0], bd_pad[l][0], cfg, False, next_plan(l, 1))
            o_b, st_b, got2 = gla_fwd("gla_rev_and_gather", proj, wd_pad[l][1], bd_pad[l][1], cfg, True, next_plan(l, 2))
        else:
            proj = matmul("proj", h1, Wl['win'], 'nn', R, NP, D, tm=TM, tn=tn_np)
            o_f, st_f, _ = gla_fwd("gla_fwd", proj, wd_pad[l][0], bd_pad[l][0], cfg, False)
            o_b, st_b, _ = gla_fwd("gla_rev", proj, wd_pad[l][1], bd_pad[l][1], cfg, True)

        def post_body(i, j, rows_, vecs):
            o = rows_[0] + rows_[1]
            on = o * lax.rsqrt(jnp.mean(o * o, axis=-1, keepdims=True) + RMS_EPS)
            return [on * vecs[0][0:1, :] * _silu(rows_[2])], []

        a_gla = ew("gla_post", post_body, MT, N_HEADS, TM, HV, [(o_f, 0, 0), (o_b, 0, 0), (proj, 0, OFF_G // HV)],
                   [(gain8(gla_norm_gain[l]), 0)], [(MXU_DTYPE, R, DV)])[0]
        y_gla = matmul("gla_out", a_gla, Wl['wgo'], 'nn', R, D, DV, tm=TM, tn=tn_d)
        u_pool = pool_mix("pool_fwd", proj, OFF_P // PG, cfg, False)
        t_pool = matmul("pool_group", u_pool, Wl['wpg_bd'], 'nn', R, DP, DP, tm=TM, tn=tn_dp)

        def scale_body(i, j, rows_, vecs):
            return [rows_[0] * vecs[0][0:1, :]], []

        ts_pool = ew("pool_scale", scale_body, MT, DP // TE, TM, TE, [(t_pool, 0, 0)], [(gain8(pool_scale[l]), 0)],
                     [(MXU_DTYPE, R, DP)])[0]
        y_pool = matmul("pool_out", ts_pool, Wl['wpo'], 'nn', R, D, DP, tm=TM, tn=tn_d)

        def merge_body(i, j, rows_, vecs):
            return [_sigmoid(rows_[2]) * rows_[0] + _sigmoid(rows_[3]) * rows_[1]], []

        mg = ew("merge", merge_body, MT, D // TE, TM, TE,
                [(y_gla, 0, 0), (y_pool, 0, 0), (proj, 0, OFF_BGA // TE), (proj, 0, OFF_BGB // TE)], [],
                [(MXU_DTYPE, R, D)])[0]
        y_mix = matmul("mix_out", mg, Wl['wout'], 'nn', R, D, D, tm=TM, tn=tn_d)

        def ln_body(i, j, rows_, vecs):
            r = alpha * rows_[0] + _cls(vecs[0], i) * rows_[1]
            mu = jnp.mean(r, axis=-1, keepdims=True)
            rc = r - mu
            var = jnp.mean(rc * rc, axis=-1, keepdims=True)
            out = rc * lax.rsqrt(var + LN_EPS) * vecs[1][0:1, :] + vecs[2][0:1, :]
            if len(vecs) == 3:
                return [out], []
            return [out, out * (1.0 + _cls(vecs[4], i)) + _cls(vecs[3], i)], []

        x1, h2 = ew("resid_ln_mod", ln_body, R // TL, 1, TL, D, [(xs, 0, 0), (y_mix, 0, 0)],
                    [(mv, 2), (gain8(ln_mix_gain[l]), 0), (gain8(ln_mix_bias[l]), 0), (mv, 3), (mv, 4)],
                    [(F32, R, D), (MXU_DTYPE, R, D)], ctx_tiles=CL // TL)
        if l + 1 < L:
            u_ffn, got3 = matmul("ffn_in_and_gather", h2, Wl['wfi'], 'nn', R, 2 * DFF, D, tm=TM, tn=tn_ff2,
                                 comm=next_plan(l, 3))
            W[l + 1] = assemble(dict(zip(all_hosted, got0 + got1 + got2 + got3)))
        else:
            u_ffn = matmul("ffn_in", h2, Wl['wfi'], 'nn', R, 2 * DFF, D, tm=TM, tn=tn_ff2)
        s_ffn = swiglu("swiglu", u_ffn, MT, DFF // te_ff, TM, te_ff)
        f_ffn = matmul("ffn_out", s_ffn, Wl['wfo'], 'nn', R, D, DFF, tm=TM, tn=tn_d)
        if l + 1 < L:
            x2, h_next = ew("resid_ln_mod", ln_body, R // TL, 1, TL, D, [(x1, 0, 0), (f_ffn, 0, 0)],
                            [(mv, 5), (gain8(ln_ffn_gain[l]), 0), (gain8(ln_ffn_bias[l]), 0), (modv[l + 1], 0), (modv[l + 1], 1)],
                            [(F32, R, D), (MXU_DTYPE, R, D)], ctx_tiles=CL // TL)
        else:
            x2 = ew("resid_ln", ln_body, R // TL, 1, TL, D, [(x1, 0, 0), (f_ffn, 0, 0)],
                    [(mv, 5), (gain8(ln_ffn_gain[l]), 0), (gain8(ln_ffn_bias[l]), 0)], [(F32, R, D)], ctx_tiles=CL // TL)[0]
        sv.update(xs=xs, h1=h1, proj=proj, o_f=o_f, o_b=o_b, st_f=st_f, st_b=st_b, a_gla=a_gla, y_gla=y_gla, u_pool=u_pool,
                  t_pool=t_pool, ts_pool=ts_pool, y_pool=y_pool, mg=mg, y_mix=y_mix, x1=x1, h2=h2, u_ffn=u_ffn, s_ffn=s_ffn,
                  f_ffn=f_ffn)
        saved.append(sv)
        xs = x2
        if l + 1 < L:
            h1 = h_next

    tgt = loss_target.reshape(S, D)

    def loss_body(i, j, rows_, vecs):
        d = jnp.where(i, 0.0, rows_[0] - rows_[1])
        return [d * (1.0 / D)], [_colsum(d * d)]

    d_x, sq = ew("loss", loss_body, MT, 1, TM, D, [(xs, 0, 0), (tgt, lambda i: jnp.maximum(i - 1, 0), 0)], [],
                 [(F32, R, D)], [(D, False)])

    def total_body(i, j, rows_, vecs):
        return [jnp.broadcast_to(jnp.sum(rows_[0], axis=-1, keepdims=True), (SUBLANES, D)) * (0.5 / D)], []

    loss_local = ew("loss_total", total_body, 1, 1, SUBLANES, D, [(sq, 0, 0)], [], [(F32, SUBLANES, D)])[0][0, 0]
    loss = lax.psum(loss_local, ("x", "y", "c"))

    gsmall = {nm: [None] * L for nm in ['gla_norm_gain', 'pool_scale', 'ln_mix_gain', 'ln_mix_bias', 'ln_ffn_gain',
                                        'ln_ffn_bias', 'w_decay_up', 'b_decay_up']}
    gbig = {nm: [None] * L for nm in big}
    dmod = [None] * L

    def ln_bwd(name, x_in, br, d_out, mv, gt_blk, gain, through=None):
        def body(i, j, rows_, vecs):
            gt = _cls(vecs[0], i)
            r = alpha * rows_[0] + gt * rows_[1]
            mu = jnp.mean(r, axis=-1, keepdims=True)
            rc = r - mu
            rstd = lax.rsqrt(jnp.mean(rc * rc, axis=-1, keepdims=True) + LN_EPS)
            xhat = rc * rstd
            extra = []
            if through is None:
                d_o = rows_[2]
            else:
                d_o = rows_[2] * (1.0 + _cls(vecs[2], i)) + alpha * rows_[4]
                extra = [_colsum(rows_[2] * rows_[3]), _colsum(rows_[2])]
            dxh = d_o * vecs[1][0:1, :]
            dr = rstd * (dxh - jnp.mean(dxh, axis=-1, keepdims=True) - xhat * jnp.mean(dxh * xhat, axis=-1, keepdims=True))
            return [dr, gt * dr], [_colsum(d_o * xhat), _colsum(d_o), _colsum(dr * rows_[1])] + extra

        rows_in = [(x_in, 0, 0), (br, 0, 0)]
        vecs_in = [(mv, gt_blk), (gain8(gain), 0)]
        sums = [(D, False), (D, False), (D, True)]
        if through is None:
            rows_in.append((d_out, 0, 0))
        else:
            rows_in += [(through[0], 0, 0), (through[1], 0, 0), (through[2], 0, 0)]
            vecs_in.append((mv, through[3]))
            sums += [(D, True), (D, True)]
        return ew(name, body, R // TL, 1, TL, D, rows_in, vecs_in, [(F32, R, D), (MXU_DTYPE, R, D)], sums,
                  ctx_tiles=CL // TL)

    def mod_bwd(name, d_h, x_in, d_r, mv, sc_blk):
        def body(i, j, rows_, vecs):
            return ([rows_[0] * (1.0 + _cls(vecs[0], i)) + alpha * rows_[2]],
                    [_colsum(rows_[0] * rows_[1]), _colsum(rows_[0])])

        return ew(name, body, MT, D // tn_d, TM, tn_d, [(d_h, 0, 0), (x_in, 0, 0), (d_r, 0, 0)],
                  [(mv, sc_blk * (D // tn_d))], [(F32, R, D)], [(D, True), (D, True)])

    shard_axis = {'w_in': 1, 'w_gla_out': 0, 'w_pool_out': 1, 'w_out': 0, 'w_ffn_in': 1, 'w_ffn_out': 0, 'w_pool_group': 1}
    scatter_groups = [['w_in'], ['w_ffn_in'], ['w_gla_out', 'w_pool_out', 'w_out', 'w_ffn_out', 'w_pool_group']]

    def cut_block(nm, g, which, core, dtype):
        ax = shard_axis[nm]
        width = weights[nm].shape[ax + 1]
        starts, sizes = [0] * g.ndim, list(g.shape)
        if ax == 0:
            sizes[0] = width // 2
            starts[0] = which * width + core * sizes[0]
        else:
            sizes[0], sizes[ax] = g.shape[0] // 2, width
            starts[0], starts[ax] = core * sizes[0], which * width
        return lax.dynamic_slice(g, starts, sizes).astype(dtype)

    def cut_layer(l):
        cut = dict(keep_r={}, give_r={}, keep_o={}, give_o={})
        for nm in big:
            g = gbig[nm][l]
            if nm == 'w_ffn_in':
                assert 2 * weights[nm].shape[2] == DFF and N_CHIPS == 4
                blk = lambda which, core, dt: jnp.where(which < 2, cut_block(nm, g[0], which % 2, core, dt),
                                                        cut_block(nm, g[1], which % 2, core, dt))
            else:
                blk = lambda which, core, dt: cut_block(nm, g, which, core, dt)
            others = [jnp.bitwise_xor(chip, k) for k in range(1, N_CHIPS)]
            cut['keep_r'][nm] = [blk(o_, ci, WIRE_DTYPE) for o_ in others]
            cut['give_r'][nm] = [blk(o_, 1 - ci, WIRE_DTYPE) for o_ in others]
            cut['keep_o'][nm], cut['give_o'][nm] = blk(chip, ci, F32), blk(chip, 1 - ci, F32)
        return cut

    def swap_partials_plan(cut):
        return swap_plan([a_ for nm in big for a_ in cut['give_r'][nm]] + [cut['give_o'][nm] for nm in big])

    def add_partials(cut, got):
        cut['got_o'] = dict(zip(big, got[3 * len(big):]))
        cut['send_r'] = {}
        for t, nm in enumerate(big):
            kept = [_rows2d(a_) for a_ in cut['keep_r'][nm]]
            theirs = [_rows2d(a_) for a_ in got[3 * t:3 * t + 3]]
            rows_, width = kept[0].shape
            tm_ = _pick(rows_, (256, 128, 64, 32, 16, 8))

            def add2_body(i, j, rows__, vecs):
                return [rows__[k].astype(F32) + rows__[3 + k].astype(F32) for k in range(3)], []

            cut['send_r'][nm] = ew("add_partials", add2_body, rows_ // tm_, 1, tm_, width,
                                   [(a_, 0, 0) for a_ in kept + theirs], [], [(WIRE_DTYPE, rows_, width)] * 3)
        cut['recvd'] = {}

    def sum_grads(cut):
        done = {}
        for nm in big:
            a2, b2 = _rows2d(cut['keep_o'][nm]), _rows2d(cut['got_o'][nm])
            rows_, width = a2.shape
            tm_ = _pick(rows_, (256, 128, 64, 32, 16, 8))

            def psum_body(i, j, rows__, vecs):
                return [(rows__[0] + rows__[1]) + rows__[2].astype(F32) + rows__[3].astype(F32) + rows__[4].astype(F32)], []

            done[nm] = ew("sum_grads", psum_body, rows_ // tm_, 1, tm_, width,
                          [(a2, 0, 0), (b2, 0, 0)] + [(r_, 0, 0) for r_ in cut['recvd'][nm]], [], [(F32, rows_, width)])[0]
        return done

    def scatter_group_plan(cut, group):
        return scatter_plan([cut['send_r'][nm] for nm in group])

    def received(cut, group, got):
        for t, nm in enumerate(group):
            cut['recvd'][nm] = got[3 * t:3 * t + 3]

    finished = [None] * L
    pending = None
    for l in reversed(range(L)):
        sv = saved[l]
        mv = modv[l]
        Wl = W[l]
        proj = sv['proj']
        dr2, d_f, g_gain, g_bias, g_gt_f = ln_bwd("ln_bwd", sv['x1'], sv['f_ffn'], d_x, mv, 5, ln_ffn_gain[l])
        gsmall['ln_ffn_gain'][l], gsmall['ln_ffn_bias'][l] = g_gain[0], g_bias[0]
        if pending is not None:
            d_s, got = matmul("ffn_out_dx_and_swap", d_f, Wl['wfo'], 'nt', R, DFF, D, tm=TM, tn=tn_ff,
                              comm=swap_partials_plan(pending))
            add_partials(pending, got)
        else:
            d_s = matmul("ffn_out_dx", d_f, Wl['wfo'], 'nt', R, DFF, D, tm=TM, tn=tn_ff)
        gbig['w_ffn_out'][l] = matmul("ffn_out_dw", sv['s_ffn'], d_f, 'tn', DFF, D, R, tm=tw(DFF), tn=tn_d)
        d_gate, d_up = swiglu_bwd("swiglu_bwd", d_s, sv['u_ffn'], MT, DFF // te_ff, TM, te_ff)
        tn_x = 512
        if pending is not None:
            d_h2, got = matmul("ffn_in_dx_and_scatter", d_gate, Wl['wfi'], 'nt', R, D, DFF, tm=TM, tn=tn_x,
                               second=(d_up, (0, 1)), comm=scatter_group_plan(pending, scatter_groups[0]))
            received(pending, scatter_groups[0], got)
        else:
            d_h2 = matmul("ffn_in_dx", d_gate, Wl['wfi'], 'nt', R, D, DFF, tm=TM, tn=tn_x, second=(d_up, (0, 1)))
        gbig['w_ffn_in'][l] = (matmul("ffn_in_dw", sv['h2'], d_gate, 'tn', D, DFF, R, tm=tw(D), tn=te_ff),
                               matmul("ffn_in_dw_up", sv['h2'], d_up, 'tn', D, DFF, R, tm=tw(D), tn=te_ff))
        dr1, d_y, g_gain, g_bias, g_gt_m, g_sc_f, g_sh_f = ln_bwd("ln_bwd_through", sv['xs'], sv['y_mix'], None, mv, 2,
                                                                   ln_mix_gain[l], through=(d_h2, sv['x1'], dr2, 4))
        gsmall['ln_mix_gain'][l], gsmall['ln_mix_bias'][l] = g_gain[0], g_bias[0]
        d_mg = matmul("mix_out_dx", d_y, Wl['wout'], 'nt', R, D, D, tm=TM, tn=tn_d)
        gbig['w_out'][l] = matmul("mix_out_dw", sv['mg'], d_y, 'tn', D, D, R, tm=tw(D), tn=tn_d)

        def merge_bwd_body(i, j, rows_, vecs):
            d_m, yg, yp, ba, bb = rows_
            sa, sb = _sigmoid(ba), _sigmoid(bb)
            return [d_m * sa, d_m * sb, d_m * yg * sa * (1.0 - sa), d_m * yp * sb * (1.0 - sb)], []

        d_yg, d_yp, d_bga, d_bgb = ew(
            "merge_bwd", merge_bwd_body, MT, D // TE, TM, TE,
            [(d_mg, 0, 0), (sv['y_gla'], 0, 0), (sv['y_pool'], 0, 0), (proj, 0, OFF_BGA // TE), (proj, 0, OFF_BGB // TE)], [],
            [(MXU_DTYPE, R, D), (MXU_DTYPE, R, D), (F32, R, D), (F32, R, D)])
        d_ts = matmul("pool_out_dx", d_yp, Wl['wpo'], 'nt', R, DP, D, tm=TM, tn=tn_dp)
        gbig['w_pool_out'][l] = matmul("pool_out_dw", sv['ts_pool'], d_yp, 'tn', DP, D, R, tm=tw(DP), tn=tn_d)

        def scale_bwd_body(i, j, rows_, vecs):
            return [rows_[0] * vecs[0][0:1, :]], [_colsum(rows_[0] * rows_[1])]

        d_t, g_ps = ew("pool_scale_bwd", scale_bwd_body, MT, DP // TE, TM, TE, [(d_ts, 0, 0), (sv['t_pool'], 0, 0)],
                       [(gain8(pool_scale[l]), 0)], [(MXU_DTYPE, R, DP)], [(DP, False)])
        gsmall['pool_scale'][l] = g_ps[0]
        d_u_pool = matmul("pool_group_dx", d_t, Wl['wpg_bd'], 'nt', R, DP, DP, tm=TM, tn=tn_dp)
        g_bd = matmul("pool_group_dw", sv['u_pool'], d_t, 'tn', DP, DP, R, tm=tw(DP), tn=tn_dp)
        gbig['w_pool_group'][l] = jnp.stack([g_bd[g * PG:(g + 1) * PG, g * PG:(g + 1) * PG] for g in range(N_POOL)])
        d_p = pool_mix("pool_bwd", d_u_pool, 0, cfg, True)
        d_a = matmul("gla_out_dx", d_yg, Wl['wgo'], 'nt', R, DV, D, tm=TM, tn=tn_d)
        gbig['w_gla_out'][l] = matmul("gla_out_dw", sv['a_gla'], d_yg, 'tn', DV, D, R, tm=tw(DV), tn=tn_d)

        def post_bwd_body(i, j, rows_, vecs):
            d_a_, o_f_, o_b_, g_ = rows_
            gain = vecs[0][0:1, :]
            o = o_f_ + o_b_
            rstd = lax.rsqrt(jnp.mean(o * o, axis=-1, keepdims=True) + RMS_EPS)
            on = o * rstd
            sg = _silu(g_)
            d_on = d_a_ * gain * sg
            d_o_ = rstd * (d_on - on * jnp.mean(d_on * on, axis=-1, keepdims=True))
            return [d_o_, d_a_ * on * gain * _dsilu(g_)], [_colsum(d_a_ * on * sg)]

        d_o, d_g, g_gng = ew("gla_post_bwd", post_bwd_body, MT, N_HEADS, TM, HV,
                             [(d_a, 0, 0), (sv['o_f'], 0, 0), (sv['o_b'], 0, 0), (proj, 0, OFF_G // HV)],
                             [(gain8(gla_norm_gain[l]), 0)], [(F32, R, DV), (F32, R, DV)], [(DV, False)])
        gsmall['gla_norm_gain'][l] = g_gng[0]
        part = gla_bwd("gla_bwd", proj, wd_pad[l][0], bd_pad[l][0], sv['st_f'], d_o, cfg, False)
        full = gla_bwd("gla_rev_bwd", proj, wd_pad[l][1], bd_pad[l][1], sv['st_b'], d_o, cfg, True, addends=part[:4])
        d_q, d_k, d_v, d_alr = full[:4]
        g_wd = [res[4][d * GATE_RANK:(d + 1) * GATE_RANK] for d, res in enumerate((part, full))]
        g_bd_ = [res[5][0] for res in (part, full)]
        gsmall['w_decay_up'][l], gsmall['b_decay_up'][l] = jnp.stack(g_wd), jnp.stack(g_bd_)
        d_proj = jnp.concatenate([t_.astype(MXU_DTYPE) for t_ in (d_q, d_k, d_v, d_g, d_alr[:, :2 * GATE_RANK], d_p, d_bga, d_bgb,
                                                                   jnp.zeros((R, ALR_W - 2 * GATE_RANK), F32))], axis=-1)
        if pending is not None:
            d_h1, got = matmul("proj_dx_and_scatter", d_proj, Wl['win_bwd'], 'nt', R, D, NP, tm=TM, tn=tn_x,
                               comm=scatter_group_plan(pending, scatter_groups[2]))
            received(pending, scatter_groups[2], got)
            gbig['w_in'][l], got = matmul("proj_dw_and_scatter", sv['h1'], d_proj, 'tn', D, NP, R, tm=tw(D), tn=tn_np,
                                          comm=scatter_group_plan(pending, scatter_groups[1]))
            received(pending, scatter_groups[1], got)
            finished[l + 1] = sum_grads(pending)
        else:
            d_h1 = matmul("proj_dx", d_proj, Wl['win_bwd'], 'nt', R, D, NP, tm=TM, tn=tn_x)
            gbig['w_in'][l] = matmul("proj_dw", sv['h1'], d_proj, 'tn', D, NP, R, tm=tw(D), tn=tn_np)
        d_x, g_sc_m, g_sh_m = mod_bwd("mod_bwd", d_h1, sv['xs'], dr1, mv, 1)
        dmod[l] = jnp.concatenate([g_sh_m[:2], g_sc_m[:2], g_gt_m[:2], g_sh_f[:2], g_sc_f[:2], g_gt_f[:2]], axis=-1)
        pending = cut_layer(l)

    grad_x = d_x[CL:].reshape(x.shape)
    add_partials(pending, run_comm("swap_partials", swap_partials_plan(pending)))
    received(pending, big, run_comm("scatter_grads", scatter_group_plan(pending, big)))
    finished[0] = sum_grads(pending)

    dmod = jnp.stack(dmod)
    summed = [dmod[:, 0]] + [jnp.stack(gsmall[nm]) for nm in
                             ['ln_mix_gain', 'ln_mix_bias', 'ln_ffn_gain', 'ln_ffn_bias', 'gla_norm_gain', 'pool_scale',
                              'w_decay_up', 'b_decay_up']]
    pack = _pack([dmod[:, 1]] + summed)
    prow = pack.shape[0]
    packs = all_gather8("ag_small", pack)
    tot = _sum_rows("sum_small", packs.reshape(N_DEV * prow, PACK_W), N_DEV, prow, PACK_W)
    shapes = [(L, N_MOD * D)] + [a.shape for a in summed]
    tot = _unpack(tot, shapes)
    dmod_ctx = tot[1]
    g_rep = dict(zip(['ln_mix_gain', 'ln_mix_bias', 'ln_ffn_gain', 'ln_ffn_bias', 'gla_norm_gain', 'pool_scale'], tot[2:8]))
    g_wdu_full, g_bdu_full = tot[8], tot[9]
    dmod_lat = jnp.stack([_unpack(packs[d_], shapes[:1])[0] for d_ in range(N_DEV)], axis=1)
    dm_all = jnp.concatenate([dmod_lat, dmod_ctx[:, None, :], jnp.zeros((L, 16 - N_DEV - 1, N_MOD * D), F32)], axis=1)

    dm_mine = lax.dynamic_slice_in_dim(dm_all, chip * n_ada, n_ada, axis=2)
    g_w_ada = jnp.stack([matmul("ada_dw", act, dm_mine[l], 'tn', D, n_ada, 16, tm=tw(D), tn=tn_ada, precise=True)
                         for l in range(L)])

    def bsum_body(i, j, rows_, vecs):
        return [jnp.broadcast_to(_colsum(rows_[0]), rows_[0].shape)], []

    g_b_ada = jnp.stack([ew("ada_db", bsum_body, 1, 1, 16, N_MOD * D, [(dm_all[l], 0, 0)], [],
                            [(F32, 16, N_MOD * D)])[0][0] for l in range(L)])
    part_c = [matmul("ada_dc", dm_mine[l], w_ada2, 'nt', 16, D, n_ada, tm=16, tn=tn_d, b_off=(l * (D // tn_d), 0),
                     precise=True)
              for l in range(L)]
    parts_c = all_gather8("ag_dcond", jnp.concatenate(part_c, axis=0))
    dc_rows = parts_c[0::2].reshape(N_CHIPS * L * 16, D)

    def dc_body(i, j, rows_, vecs):
        acc = rows_[0]
        for r_ in rows_[1:-1]:
            acc = acc + r_
        return [acc * rows_[-1]], []

    g_c_ctx = ew("dcond", dc_body, 1, 1, 16, D, [(dc_rows, p_, 0) for p_ in range(N_CHIPS * L)] + [(dact, 0, 0)], [],
                 [(F32, 16, D)])[0][N_DEV]

    other = run_comm("swap_grads", swap_plan([finished[l][nm] for nm in big for l in range(L)]))

    out_g, out_d, out_m, out_v = {}, {}, {}, {}
    for t, nm in enumerate(big):
        shp = weights[nm].shape
        half_rows = (shp[1] // 2) * (shp[2] if len(shp) == 4 else 1)
        res = _adam("adam_" + nm, _rows2d(weights[nm]), ([finished[l][nm] for l in range(L)], other[t * L:(t + 1) * L]),
                    _rows2d(mom1[nm]), _rows2d(mom2[nm]), half_rows=half_rows, core=ci)
        out_g[nm], out_d[nm], out_m[nm], out_v[nm] = [r_.reshape(shp) for r_ in res]
    res = _adam("adam_w_ada", _rows2d(w_ada), [_rows2d(g_w_ada)], _rows2d(m_w_ada), _rows2d(v_w_ada))
    out_g['w_ada'], out_d['w_ada'], out_m['w_ada'], out_v['w_ada'] = [r_.reshape(w_ada.shape) for r_ in res]
    n_wd, n_bd = w_decay_up.shape[-1], b_decay_up.shape[-1]
    small_g = dict(g_rep, c_ctx=g_c_ctx, b_ada=g_b_ada,
                   w_decay_up=lax.dynamic_slice_in_dim(g_wdu_full, chip * n_wd, n_wd, axis=3),
                   b_decay_up=lax.dynamic_slice_in_dim(g_bdu_full, chip * n_bd, n_bd, axis=2))
    small = [nm for nm in names if nm not in big and nm != 'w_ada']
    res = _adam("adam_small", _pack([weights[nm] for nm in small]), [_pack([small_g[nm] for nm in small])],
                _pack([mom1[nm] for nm in small]), _pack([mom2[nm] for nm in small]))
    small_shapes = [weights[nm].shape for nm in small]
    for dst, packed in zip((out_g, out_d, out_m, out_v), res):
        for nm, val in zip(small, _unpack(packed, small_shapes)):
            dst[nm] = val

    return (loss, grad_x, *[out_g[nm] for nm in names], *[out_d[nm] for nm in names],
            *[out_m[nm] for nm in names], *[out_v[nm] for nm in names])


def swiglu(name, u, nrow, nh, tm, tn):
    def kern(gate_ref, up_ref, o_ref):
        o_ref[...] = (_silu(gate_ref[...]) * up_ref[...]).astype(o_ref.dtype)

    return pl.pallas_call(
        kern, name=name, grid=(nh, nrow),
        in_specs=[pl.BlockSpec((tm, tn), lambda j, i: (i, j)), pl.BlockSpec((tm, tn), lambda j, i: (i, nh + j))],
        out_specs=pl.BlockSpec((tm, tn), lambda j, i: (i, j)),
        out_shape=jax.ShapeDtypeStruct((u.shape[0], nh * tn), MXU_DTYPE), compiler_params=_params(),
    )(u, u)


def swiglu_bwd(name, d_s, u, nrow, nh, tm, tn):
    def kern(ds_ref, gate_ref, up_ref, dg_ref, du_ref):
        gate, d_s_ = gate_ref[...], ds_ref[...]
        dg_ref[...] = (d_s_ * up_ref[...] * _dsilu(gate)).astype(dg_ref.dtype)
        du_ref[...] = (d_s_ * _silu(gate)).astype(du_ref.dtype)

    tile = pl.BlockSpec((tm, tn), lambda j, i: (i, j))
    return pl.pallas_call(
        kern, name=name, grid=(nh, nrow), in_specs=[tile, tile, pl.BlockSpec((tm, tn), lambda j, i: (i, nh + j))],
        out_specs=[tile, tile], out_shape=[jax.ShapeDtypeStruct((u.shape[0], nh * tn), MXU_DTYPE)] * 2,
        compiler_params=_params(),
    )(d_s, u, u)
```

```python
import functools

import jax
import jax.numpy as jnp
import numpy as np
from jax import lax
from jax.experimental import pallas as pl
from jax.experimental.pallas import tpu as pltpu

F32 = jnp.float32
MXU_DTYPE = jnp.bfloat16
WIRE_DTYPE = jnp.bfloat16

GRID_W = 64
CHUNK = 64
N_HEADS = 4
GATE_RANK = 16
GATE_NORM = 16.0
N_MOD = 6
N_POOL = 4
LN_EPS = 1e-5
RMS_EPS = 1e-6
ADAM_LR = 0.001
ADAM_B1 = 0.9
ADAM_B2 = 0.999
ADAM_EPS = 1e-08
ADAM_WD = 0.01
ADAM_STEP = 10

LANES = 128
SUBLANES = 8
ALR_W = 256
PACK_W = 2048
VMEM_LIMIT = 56 * 1024 * 1024
N_CHIPS = 4
N_DEV = 8
MESH = pl.DeviceIdType.MESH

NN = ((1,), (0,))
NT = ((1,), (1,))
TN = ((0,), (0,))


def _dot(a, b, dims):
    return lax.dot_general(a.astype(MXU_DTYPE), b.astype(MXU_DTYPE), (dims, ((), ())),
                           preferred_element_type=F32)


def _dot_f32(a, b, dims):
    return lax.dot_general(a.astype(F32), b.astype(F32), (dims, ((), ())),
                           precision=lax.Precision.HIGHEST, preferred_element_type=F32)


def _dot_mask(mask, x, dims):
    m = mask.astype(MXU_DTYPE)
    if MXU_DTYPE == F32:
        return lax.dot_general(m, x, (dims, ((), ())), preferred_element_type=F32)
    acc = None
    rest = x
    for _ in range(3):
        piece = rest.astype(MXU_DTYPE)
        rest = rest - piece.astype(F32)
        part = lax.dot_general(m, piece, (dims, ((), ())), preferred_element_type=F32)
        acc = part if acc is None else acc + part
    return acc


def _dot_3x(a, b, dims):
    if MXU_DTYPE == F32:
        return lax.dot_general(a, b, (dims, ((), ())), preferred_element_type=F32)
    a_hi, b_hi = a.astype(MXU_DTYPE), b.astype(MXU_DTYPE)
    a_lo = (a - a_hi.astype(F32)).astype(MXU_DTYPE)
    b_lo = (b - b_hi.astype(F32)).astype(MXU_DTYPE)
    dot = lambda u, w: lax.dot_general(u, w, (dims, ((), ())), preferred_element_type=F32)
    return dot(a_hi, b_hi) + (dot(a_lo, b_hi) + dot(a_hi, b_lo))


def _pick(n, cands):
    for c in cands:
        if n % c == 0:
            return c
    return n


def _params():
    return pltpu.CompilerParams(vmem_limit_bytes=VMEM_LIMIT)


def _sigmoid(x):
    return 0.5 + 0.5 * jnp.tanh(0.5 * x)


def _silu(x):
    return x * _sigmoid(x)


def _dsilu(x):
    s = _sigmoid(x)
    return s * (1.0 + x * (1.0 - s))


def matmul(name, a, b, form, m, n, k, *, tm, tn, out_dtype=F32, a_off=(0, 0), b_off=(0, 0), bias=None, bias_off=0,
           precise=False, comm=None, second=None):
    assert m % tm == 0 and n % tn == 0, (name, m, n, tm, tn)
    if form == 'tn':
        a_spec = pl.BlockSpec((k, tm), lambda j, i: (a_off[0], i + a_off[1]))
    else:
        a_spec = pl.BlockSpec((tm, k), lambda j, i: (i + a_off[0], a_off[1]))
    if form == 'nt':
        b_spec = pl.BlockSpec((tn, k), lambda j, i: (j + b_off[0], b_off[1]))
    else:
        b_spec = pl.BlockSpec((k, tn), lambda j, i: (b_off[0], j + b_off[1]))
    dims = {'nn': NN, 'nt': NT, 'tn': TN}[form]
    in_specs = [a_spec, b_spec]
    args = [a, b]
    if bias is not None:
        in_specs.append(pl.BlockSpec((SUBLANES, tn), lambda j, i: (0, j + bias_off)))
        args.append(bias)

    if second is not None:
        a2, b2_off = second
        assert form == 'nt'
        in_specs += [pl.BlockSpec((tm, k), lambda j, i: (i, 0)), pl.BlockSpec((tn, k), lambda j, i: (j + b2_off[0], b2_off[1]))]
        args += [a2, b]
    n_own = len(args)
    nj, ni = n // tn, m // tm
    out_spec = pl.BlockSpec((tm, tn), lambda j, i: (i, j))
    out_shape = jax.ShapeDtypeStruct((m, n), out_dtype)

    def product(refs):
        acc = (_dot_f32 if precise else _dot)(refs[0][...], refs[1][...], dims)
        if bias is not None:
            acc = acc + refs[2][0:1, :]
        if second is not None:
            acc = acc + _dot(refs[n_own - 2][...], refs[n_own - 1][...], dims)
        return acc

    if comm is None:
        def body(*refs):
            refs[-1][...] = product(refs).astype(refs[-1].dtype)

        return pl.pallas_call(body, name=name, grid=(nj, ni), in_specs=in_specs, out_specs=out_spec,
                              out_shape=out_shape, compiler_params=_params())(*args)

    n_ci, n_co = len(comm.ins), len(comm.outs)

    def hosted(*refs):
        c_in = refs[n_own:n_own + n_ci]
        o_ref = refs[n_own + n_ci]
        c_out = refs[n_own + n_ci + 1:n_own + n_ci + 1 + n_co]
        sems = refs[n_own + n_ci + 1 + n_co:]
        j, i = pl.program_id(0), pl.program_id(1)

        @pl.when((j == 0) & (i == 0))
        def _():
            comm.start(c_in, c_out, sems)

        if comm.mid is not None:
            @pl.when((j == nj - 1) & (i == 0))
            def _():
                comm.mid(c_in, c_out, sems)

        o_ref[...] = product(refs).astype(o_ref.dtype)

        @pl.when((j == nj - 1) & (i == ni - 1))
        def _():
            comm.finish(c_in, c_out, sems)

    any_spec = pl.BlockSpec(memory_space=pl.ANY)
    res = pl.pallas_call(
        hosted, name=name, grid=(nj, ni), in_specs=in_specs + [any_spec] * n_ci,
        out_specs=[out_spec] + [any_spec] * n_co, out_shape=[out_shape] + list(comm.outs),
        scratch_shapes=list(comm.sems), compiler_params=_params(),
    )(*args, *comm.ins)
    return res[0], list(res[1:])


def ew(name, body, nrow, ncol, tm, tn, row_ins, vec_ins, row_outs, sum_outs=(), ctx_tiles=1, pass_i=False):
    def rmap(roff):
        return roff if callable(roff) else (lambda i: i + roff)

    in_specs = []
    for arr, roff, coff in row_ins:
        in_specs.append(pl.BlockSpec((tm, tn), functools.partial(lambda j, i, r, c: (r(i), j + c), r=rmap(roff), c=coff)))
    for arr, coff in vec_ins:
        in_specs.append(pl.BlockSpec((SUBLANES, tn), functools.partial(lambda j, i, c: (0, j + c), c=coff)))
    out_specs = [pl.BlockSpec((tm, tn), lambda j, i: (i, j)) for _ in row_outs]
    out_specs += [pl.BlockSpec((SUBLANES, tn), lambda j, i: (0, j)) for _ in sum_outs]
    out_shape = [jax.ShapeDtypeStruct((r, c), dt) for dt, r, c in row_outs]
    out_shape += [jax.ShapeDtypeStruct((SUBLANES, c), F32) for c, _ in sum_outs]
    n_row, n_vec, n_ro = len(row_ins), len(vec_ins), len(row_outs)

    def kern(*refs):
        j, i = pl.program_id(0), pl.program_id(1)
        rows = [r[...] for r in refs[:n_row]]
        vecs = [r[...] for r in refs[n_row:n_row + n_vec]]
        outs = refs[n_row + n_vec:]
        is_ctx = i < ctx_tiles
        res, sums = body(i if pass_i else is_ctx, j, rows, vecs)
        for ref, val in zip(outs[:n_ro], res):
            ref[...] = val.astype(ref.dtype)
        for ref, val, (_, by_class) in zip(outs[n_ro:], sums, sum_outs):
            @pl.when(i == 0)
            def _():
                ref[...] = jnp.zeros_like(ref)
            if by_class:
                ref[0:1, :] += jnp.where(is_ctx, val, 0.0)
                ref[1:2, :] += jnp.where(is_ctx, 0.0, val)
            else:
                ref[0:1, :] += val

    outs = pl.pallas_call(
        kern, name=name, grid=(ncol, nrow), in_specs=in_specs, out_specs=out_specs, out_shape=out_shape,
        compiler_params=_params(),
    )(*[a for a, _, _ in row_ins], *[a for a, _ in vec_ins])
    return list(outs)


def _cls(vec, is_ctx):
    return jnp.where(is_ctx, vec[0:1, :], vec[1:2, :])


def _colsum(x):
    return jnp.sum(x, axis=0, keepdims=True)


def _chunk_map(cfg, rev):
    nctx, nc = cfg.CL // CHUNK, cfg.R // CHUNK
    if not rev:
        return lambda s: s
    return lambda s: jnp.where(s < nctx, nctx - 1 - s, nctx + nc - 1 - s)


def _gla_chunk(q_ref, k_ref, a_ref, wd_ref, bd_ref, rev, scale):
    q = q_ref[...] * scale
    k = k_ref[...]
    z = _dot_3x(a_ref[...], wd_ref[...], NN) + bd_ref[0:1, :]
    la = (jnp.minimum(z, 0.0) - jnp.log(1.0 + jnp.exp(-jnp.abs(z)))) * (1.0 / GATE_NORM)
    r = lax.broadcasted_iota(jnp.int32, (CHUNK, CHUNK), 0)
    c = lax.broadcasted_iota(jnp.int32, (CHUNK, CHUNK), 1)
    keep = (r <= c) if rev else (r >= c)
    tri = keep.astype(F32)
    cum = _dot_mask(tri, la, NN)
    mid = CHUNK // 2 if rev else CHUNK // 2 - 1
    end = 0 if rev else CHUNK - 1
    ref = cum[mid:mid + 1, :]
    last = cum[end:end + 1, :]
    return dict(q=q, k=k, z=z, keep=keep, tri=tri, q_in=q * jnp.exp(cum - ref), k_in=k * jnp.exp(ref - cum),
                e_q=jnp.exp(cum), e_k=jnp.exp(last - cum), e_inq=jnp.exp(cum - ref), e_ink=jnp.exp(ref - cum),
                e_last=jnp.exp(last))


def _gla_in_specs(cfg, rows_of):
    dk, dv = cfg.DK, cfg.DV
    return [
        pl.BlockSpec((CHUNK, dk), lambda s: (rows_of(s), 0)),
        pl.BlockSpec((CHUNK, dk), lambda s: (rows_of(s), 1)),
        pl.BlockSpec((CHUNK, dv), lambda s: (rows_of(s), 2 * dk // dv)),
        pl.BlockSpec((CHUNK, ALR_W), lambda s: (rows_of(s), (cfg.NP - ALR_W) // ALR_W)),
        pl.BlockSpec((ALR_W, dk), lambda s: (0, 0)),
        pl.BlockSpec((SUBLANES, dk), lambda s: (0, 0)),
    ]


def gla_fwd(name, proj, wd, bd, cfg, rev, comm=None):
    hk, hv = cfg.HK, cfg.HV
    nc = cfg.R // CHUNK
    cmap = _chunk_map(cfg, rev)
    scale = hk ** -0.5
    n_ci, n_co = (len(comm.ins), len(comm.outs)) if comm else (0, 0)

    def body(*refs):
        q_ref, k_ref, v_ref, a_ref, wd_ref, bd_ref = refs[:6]
        c_in = refs[6:6 + n_ci]
        o_ref, ss_ref = refs[6 + n_ci:8 + n_ci]
        c_out = refs[8 + n_ci:8 + n_ci + n_co]
        st_scr = refs[8 + n_ci + n_co]
        sems = refs[9 + n_ci + n_co:]
        s = pl.program_id(0)

        @pl.when(s == 0)
        def _():
            st_scr[...] = jnp.zeros_like(st_scr)
            if comm:
                comm.start(c_in, c_out, sems)

        t = _gla_chunk(q_ref, k_ref, a_ref, wd_ref, bd_ref, rev, scale)
        q_int, k_st = t['q'] * t['e_q'], t['k'] * t['e_k']
        for h in range(N_HEADS):
            ks, vs = slice(h * hk, (h + 1) * hk), slice(h * hv, (h + 1) * hv)
            v = v_ref[:, vs]
            st = st_scr[h]
            ss_ref[0, h] = st
            a = jnp.where(t['keep'], _dot(t['q_in'][:, ks], t['k_in'][:, ks], NT), 0.0)
            o_ref[:, vs] = _dot(a, v, NN) + _dot(q_int[:, ks], st, NT)
            st_scr[h] = st * t['e_last'][:, ks] + _dot(v, k_st[:, ks], TN)

        if comm:
            if comm.mid is not None:
                @pl.when(s == max(nc - 12, 0))
                def _():
                    comm.mid(c_in, c_out, sems)

            @pl.when(s == nc - 1)
            def _():
                comm.finish(c_in, c_out, sems)

    any_spec = pl.BlockSpec(memory_space=pl.ANY)
    res = pl.pallas_call(
        body, name=name, grid=(nc,), in_specs=_gla_in_specs(cfg, cmap) + [any_spec] * n_ci,
        out_specs=[pl.BlockSpec((CHUNK, cfg.DV), lambda s: (cmap(s), 0)),
                   pl.BlockSpec((1, N_HEADS, hv, hk), lambda s: (s, 0, 0, 0))] + [any_spec] * n_co,
        out_shape=[jax.ShapeDtypeStruct((cfg.R, cfg.DV), F32),
                   jax.ShapeDtypeStruct((nc, N_HEADS, hv, hk), F32)] + (list(comm.outs) if comm else []),
        scratch_shapes=[pltpu.VMEM((N_HEADS, hv, hk), F32)] + (list(comm.sems) if comm else []),
        compiler_params=_params(),
    )(proj, proj, proj, proj, wd, bd, *(comm.ins if comm else []))
    return res[0], res[1], list(res[2:])


def gla_bwd(name, proj, wd, bd, states, d_o, cfg, rev, addends=None):
    hk, hv = cfg.HK, cfg.HV
    nc = cfg.R // CHUNK
    cmap = _chunk_map(cfg, rev)
    rows_of = lambda g: cmap(nc - 1 - g)
    scale = hk ** -0.5
    n_add = 0 if addends is None else 4

    def body(*refs):
        q_ref, k_ref, v_ref, a_ref, wd_ref, bd_ref, ss_ref, do_ref = refs[:8]
        adds = refs[8:8 + n_add]
        dq_ref, dk_ref, dv_ref, da_ref, dwd_ref, dbd_ref, dst_scr = refs[8 + n_add:]
        g = pl.program_id(0)

        @pl.when(g == 0)
        def _():
            dst_scr[...] = jnp.zeros_like(dst_scr)
            dwd_ref[...] = jnp.zeros_like(dwd_ref)
            dbd_ref[...] = jnp.zeros_like(dbd_ref)

        t = _gla_chunk(q_ref, k_ref, a_ref, wd_ref, bd_ref, rev, scale)
        q_int, k_st = t['q'] * t['e_q'], t['k'] * t['e_k']
        dq_h, dk_h, carry_h = [], [], []
        for h in range(N_HEADS):
            ks, vs = slice(h * hk, (h + 1) * hk), slice(h * hv, (h + 1) * hv)
            v = v_ref[:, vs]
            d_out = do_ref[:, vs]
            st = ss_ref[0, h]
            dst = dst_scr[h]
            a = jnp.where(t['keep'], _dot(t['q_in'][:, ks], t['k_in'][:, ks], NT), 0.0)
            da = jnp.where(t['keep'], _dot(d_out, v, NT), 0.0)
            dv = _dot(a, d_out, TN) + _dot(k_st[:, ks], dst, NT)
            dq_h.append(_dot(d_out, st, NN) * t['e_q'][:, ks] + _dot(da, t['k_in'][:, ks], NN) * t['e_inq'][:, ks])
            dk_h.append(_dot(v, dst, NN) * t['e_k'][:, ks] + _dot(da, t['q_in'][:, ks], TN) * t['e_ink'][:, ks])
            dst_scr[h] = dst * t['e_last'][:, ks] + _dot(d_out, q_int[:, ks], TN)
            st_end = st * t['e_last'][:, ks] + _dot(v, k_st[:, ks], TN)
            carry_h.append(_colsum(dst * st_end))
            dv_ref[:, vs] = dv + adds[2][:, vs] if n_add else dv
        dq = jnp.concatenate(dq_h, axis=-1)
        dk = jnp.concatenate(dk_h, axis=-1)
        dg = t['q'] * dq - t['k'] * dk
        dla = _dot_mask(t['tri'], dg, TN) + jnp.concatenate(carry_h, axis=-1)
        dz = dla * (1.0 / GATE_NORM) * _sigmoid(-t['z'])
        dalr = _dot_3x(dz, wd_ref[...], NT)
        dwd_ref[...] += _dot_3x(a_ref[...], dz, TN)
        dbd_ref[...] += jnp.broadcast_to(_colsum(dz), dbd_ref.shape)
        dq = dq * scale
        if n_add:
            dq, dk, dalr = dq + adds[0][...], dk + adds[1][...], dalr + adds[3][...]
        dq_ref[...] = dq
        dk_ref[...] = dk
        da_ref[...] = dalr

    qk_spec = pl.BlockSpec((CHUNK, cfg.DK), lambda g: (rows_of(g), 0))
    v_spec = pl.BlockSpec((CHUNK, cfg.DV), lambda g: (rows_of(g), 0))
    a_spec = pl.BlockSpec((CHUNK, ALR_W), lambda g: (rows_of(g), 0))
    in_specs = _gla_in_specs(cfg, rows_of) + [
        pl.BlockSpec((1, N_HEADS, hv, hk), lambda g: (nc - 1 - g, 0, 0, 0)), v_spec]
    args = [proj, proj, proj, proj, wd, bd, states, d_o]
    if n_add:
        in_specs += [qk_spec, qk_spec, v_spec, a_spec]
        args += list(addends)
    return pl.pallas_call(
        body, name=name, grid=(nc,), in_specs=in_specs,
        out_specs=[qk_spec, qk_spec, v_spec, a_spec,
                   pl.BlockSpec((ALR_W, cfg.DK), lambda g: (0, 0)),
                   pl.BlockSpec((SUBLANES, cfg.DK), lambda g: (0, 0))],
        out_shape=[jax.ShapeDtypeStruct((cfg.R, cfg.DK), F32), jax.ShapeDtypeStruct((cfg.R, cfg.DK), F32),
                   jax.ShapeDtypeStruct((cfg.R, cfg.DV), F32), jax.ShapeDtypeStruct((cfg.R, ALR_W), F32),
                   jax.ShapeDtypeStruct((ALR_W, cfg.DK), F32),
                   jax.ShapeDtypeStruct((SUBLANES, cfg.DK), F32)],
        scratch_shapes=[pltpu.VMEM((N_HEADS, hv, hk), F32)],
        compiler_params=_params(),
    )(*args)


def pool_mix(name, src, coff, cfg, transpose):
    tm, pg = cfg.TM, cfg.PG
    mt = cfg.R // tm
    reach = -(-(max(2 ** N_POOL // 2, 1) * GRID_W) // tm)
    nk = 2 * reach + 1
    img_rows = cfg.S // GRID_W
    shift = GRID_W.bit_length() - 1

    def ktile(m, d):
        return jnp.where(m == 0, 0, jnp.clip(m + d - reach, 1, mt - 1))

    def counts(idx, is_ctx, lo, hi):
        ctx_n = jnp.minimum(idx + hi + 1, cfg.CL) - jnp.maximum(idx - lo, 0)
        r, c = idx >> shift, idx & (GRID_W - 1)
        lat_n = ((jnp.minimum(r + hi + 1, img_rows) - jnp.maximum(r - lo, 0))
                 * (jnp.minimum(c + hi + 1, GRID_W) - jnp.maximum(c - lo, 0)))
        return jnp.where(is_ctx, ctx_n, lat_n).astype(F32)

    a = np.arange(tm)[:, None]
    b = np.arange(tm)[None, :]
    masks = np.zeros((N_POOL, nk + 1, tm, tm), np.float32)
    for g_ in range(N_POOL):
        lo_ = 2 ** g_
        hi_ = lo_ - 1
        for d_ in range(nk):
            dr = (d_ - reach) * (tm // GRID_W) + (b >> shift) - (a >> shift)
            dc = (b & (GRID_W - 1)) - (a & (GRID_W - 1))
            masks[g_, d_] = (dr >= -lo_) & (dr <= hi_) & (dc >= -lo_) & (dc <= hi_)
        masks[g_, nk] = (b - a >= -lo_) & (b - a <= hi_)
    masks = jnp.asarray(masks, MXU_DTYPE)

    def body(src_ref, self_ref, mask_ref, o_ref, acc):
        m, d = pl.program_id(0), pl.program_id(1)
        is_ctx = m == 0
        kt = m + d - reach
        valid = jnp.where(is_ctx, d == reach, (kt >= 1) & (kt <= mt - 1))
        seg = jnp.where(is_ctx, 0, cfg.CL)

        @pl.when(d == 0)
        def _():
            acc[...] = jnp.zeros_like(acc)

        @pl.when(valid)
        def _():
            for g in range(N_POOL):
                cols = slice(g * pg, (g + 1) * pg)
                x = src_ref[:, cols]
                if transpose:
                    kidx = lax.broadcasted_iota(jnp.int32, (tm, 1), 0) + (kt * tm - seg)
                    acc[:, cols] += _dot_mask(mask_ref[g, 0], x / counts(kidx, is_ctx, 2 ** g, 2 ** g - 1), TN)
                else:
                    acc[:, cols] += _dot_mask(mask_ref[g, 0], x, NN)

        @pl.when(d == nk - 1)
        def _():
            for g in range(N_POOL):
                cols = slice(g * pg, (g + 1) * pg)
                res = acc[:, cols]
                if not transpose:
                    midx = lax.broadcasted_iota(jnp.int32, (tm, 1), 0) + (m * tm - seg)
                    res = res / counts(midx, is_ctx, 2 ** g, 2 ** g - 1)
                o_ref[:, cols] = (res - self_ref[:, cols]).astype(o_ref.dtype)

    which = (lambda d: 2 * reach - d) if transpose else (lambda d: d)
    assert coff % N_POOL == 0
    dp = N_POOL * pg
    return pl.pallas_call(
        body, name=name, grid=(mt, nk),
        in_specs=[pl.BlockSpec((tm, dp), lambda m, d: (ktile(m, d), coff // N_POOL)),
                  pl.BlockSpec((tm, dp), lambda m, d: (m, coff // N_POOL)),
                  pl.BlockSpec((N_POOL, 1, tm, tm), lambda m, d: (0, jnp.where(m == 0, nk, which(d)), 0, 0))],
        out_specs=pl.BlockSpec((tm, dp), lambda m, d: (m, 0)),
        out_shape=jax.ShapeDtypeStruct((cfg.R, cfg.DP), F32),
        scratch_shapes=[pltpu.VMEM((tm, dp), F32)], compiler_params=_params(),
    )(src, src, masks)


def _my_place():
    return lax.axis_index("x"), lax.axis_index("y"), lax.axis_index("c")


def _flip(v, bit):
    return 1 - v if bit else v


def all_gather8(name, block):
    rows, w = block.shape

    def body(x_ref, out_ref, send_sems, recv_sems, local_sem):
        x, y, c = _my_place()
        me = 4 * x + 2 * y + c
        mine = pltpu.make_async_copy(x_ref, out_ref.at[me], local_sem)
        mine.start()
        sends = []
        for k in range(1, N_DEV):
            peer = (_flip(x, k & 4), _flip(y, k & 2), _flip(c, k & 1))
            cp = pltpu.make_async_remote_copy(src_ref=x_ref, dst_ref=out_ref.at[me], send_sem=send_sems.at[k - 1],
                                              recv_sem=recv_sems.at[k - 1], device_id=peer, device_id_type=MESH)
            cp.start()
            sends.append(cp)
        for k in range(1, N_DEV):
            peer = (_flip(x, k & 4), _flip(y, k & 2), _flip(c, k & 1))
            slot = 4 * peer[0] + 2 * peer[1] + peer[2]
            pltpu.make_async_remote_copy(src_ref=x_ref, dst_ref=out_ref.at[slot], send_sem=send_sems.at[k - 1],
                                         recv_sem=recv_sems.at[k - 1], device_id=peer, device_id_type=MESH).wait_recv()
        for cp in sends:
            cp.wait_send()
        mine.wait()

    return pl.pallas_call(
        body, name=name, out_shape=jax.ShapeDtypeStruct((N_DEV, rows, w), block.dtype),
        in_specs=[pl.BlockSpec(memory_space=pl.ANY)], out_specs=pl.BlockSpec(memory_space=pl.ANY),
        scratch_shapes=[pltpu.SemaphoreType.DMA((N_DEV - 1,)), pltpu.SemaphoreType.DMA((N_DEV - 1,)),
                        pltpu.SemaphoreType.DMA],
    )(block)


class _Comm:
    def __init__(self, ins, outs, sems, start, finish, mid=None):
        self.ins, self.outs, self.sems, self.start, self.finish = list(ins), list(outs), list(sems), start, finish
        self.mid = mid


def run_comm(name, comm):
    def body(*refs):
        n_i, n_o = len(comm.ins), len(comm.outs)
        comm.start(refs[:n_i], refs[n_i:n_i + n_o], refs[n_i + n_o:])
        if comm.mid is not None:
            comm.mid(refs[:n_i], refs[n_i:n_i + n_o], refs[n_i + n_o:])
        comm.finish(refs[:n_i], refs[n_i:n_i + n_o], refs[n_i + n_o:])

    any_spec = pl.BlockSpec(memory_space=pl.ANY)
    return list(pl.pallas_call(body, name=name, out_shape=comm.outs, in_specs=[any_spec] * len(comm.ins),
                               out_specs=[any_spec] * len(comm.outs), scratch_shapes=comm.sems)(*comm.ins))


def gather_plan(shards, n_split):
    n = len(shards)

    def part(ref, t, core):
        if t >= n_split:
            return ref
        h = shards[t].shape[0] // 2
        return ref.at[pl.ds(core * h, h)]

    def ici(srcs, dsts, sems, k, t, slot, place):
        x, y, c = place
        return pltpu.make_async_remote_copy(
            src_ref=part(srcs[t], t, c), dst_ref=part(dsts[t].at[slot], t, c), send_sem=sems[0].at[t * 3 + k - 1],
            recv_sem=sems[1].at[t * 3 + k - 1], device_id=(_flip(x, k & 2), _flip(y, k & 1), c), device_id_type=MESH)

    def handed(dsts, sems, k, t, slot, place, core):
        x, y, c = place
        half = part(dsts[t].at[slot], t, core)
        return pltpu.make_async_remote_copy(
            src_ref=half, dst_ref=half, send_sem=sems[2].at[t * 3 + k - 1], recv_sem=sems[3].at[t * 3 + k - 1],
            device_id=(x, y, 1 - c), device_id_type=MESH)

    def start(srcs, dsts, sems):
        place = _my_place()
        me = 2 * place[0] + place[1]
        for t in range(n):
            pltpu.make_async_copy(srcs[t], dsts[t].at[me], sems[4].at[t]).start()
        for k in range(1, N_CHIPS):
            for t in range(n):
                ici(srcs, dsts, sems, k, t, me, place).start()

    def mid(srcs, dsts, sems):
        place = _my_place()
        x, y, c = place
        for k in range(1, N_CHIPS):
            slot = 2 * _flip(x, k & 2) + _flip(y, k & 1)
            for t in range(n):
                ici(srcs, dsts, sems, k, t, slot, place).wait_recv()
                if t < n_split:
                    handed(dsts, sems, k, t, slot, place, c).start()

    def finish(srcs, dsts, sems):
        place = _my_place()
        x, y, c = place
        me = 2 * x + y
        for k in range(1, N_CHIPS):
            slot = 2 * _flip(x, k & 2) + _flip(y, k & 1)
            for t in range(n_split):
                handed(dsts, sems, k, t, slot, place, 1 - c).wait_recv()
        for k in range(1, N_CHIPS):
            slot = 2 * _flip(x, k & 2) + _flip(y, k & 1)
            for t in range(n):
                ici(srcs, dsts, sems, k, t, me, place).wait_send()
                if t < n_split:
                    handed(dsts, sems, k, t, slot, place, c).wait_send()
        for t in range(n):
            pltpu.make_async_copy(srcs[t], dsts[t].at[me], sems[4].at[t]).wait()

    outs = [jax.ShapeDtypeStruct((N_CHIPS,) + s.shape, s.dtype) for s in shards]
    sems = [pltpu.SemaphoreType.DMA((3 * n,))] * 4 + [pltpu.SemaphoreType.DMA((n,))]
    return _Comm(shards, outs, sems, start, finish, mid)


def scatter_plan(triples):
    n = len(triples)
    bufs = [b for tr in triples for b in tr]

    def copies(srcs, dsts, sems):
        x, y, c = _my_place()
        return [pltpu.make_async_remote_copy(
            src_ref=srcs[t * 3 + k - 1], dst_ref=dsts[t * 3 + k - 1], send_sem=sems[0].at[t * 3 + k - 1],
            recv_sem=sems[1].at[t * 3 + k - 1], device_id=(_flip(x, k & 2), _flip(y, k & 1), c), device_id_type=MESH)
            for k in range(1, N_CHIPS) for t in range(n)]

    def start(srcs, dsts, sems):
        for cp in copies(srcs, dsts, sems):
            cp.start()

    def finish(srcs, dsts, sems):
        for cp in copies(srcs, dsts, sems):
            cp.wait_recv()
        for cp in copies(srcs, dsts, sems):
            cp.wait_send()

    return _Comm(bufs, [jax.ShapeDtypeStruct(b.shape, b.dtype) for b in bufs],
                 [pltpu.SemaphoreType.DMA((3 * n,)), pltpu.SemaphoreType.DMA((3 * n,))], start, finish)


def swap_plan(bufs):
    n = len(bufs)

    def copies(srcs, dsts, sems):
        x, y, c = _my_place()
        return [pltpu.make_async_remote_copy(src_ref=srcs[t], dst_ref=dsts[t], send_sem=sems[0].at[t],
                                             recv_sem=sems[1].at[t], device_id=(x, y, 1 - c), device_id_type=MESH)
                for t in range(n)]

    def start(srcs, dsts, sems):
        for cp in copies(srcs, dsts, sems):
            cp.start()

    def finish(srcs, dsts, sems):
        for cp in copies(srcs, dsts, sems):
            cp.wait_recv()
        for cp in copies(srcs, dsts, sems):
            cp.wait_send()

    return _Comm(bufs, [jax.ShapeDtypeStruct(b.shape, b.dtype) for b in bufs],
                 [pltpu.SemaphoreType.DMA((n,)), pltpu.SemaphoreType.DMA((n,))], start, finish)


def _vec8(*rows):
    w = rows[0].shape[-1]
    out = jnp.zeros((SUBLANES, w), F32)
    for r, v in enumerate(rows):
        out = out.at[r].set(v.reshape(w).astype(F32))
    return out


def _pack(arrays):
    parts = []
    for a in arrays:
        flat = a.reshape(-1).astype(F32)
        pad = (-flat.shape[0]) % PACK_W
        parts.append(jnp.pad(flat, (0, pad)))
    flat = jnp.concatenate(parts)
    pad = (-flat.shape[0]) % (PACK_W * SUBLANES)
    return jnp.pad(flat, (0, pad)).reshape(-1, PACK_W)


def _unpack(packed, shapes):
    flat = packed.reshape(-1)
    out, pos = [], 0
    for shp in shapes:
        size = 1
        for d in shp:
            size *= d
        out.append(flat[pos:pos + size].reshape(shp))
        pos += size + (-size) % PACK_W
    return out


def _rows2d(a):
    return a.reshape(-1, a.shape[-1])


class _Cfg:
    pass


def _adam(name, w, grads, m, v, half_rows=None, core=None):
    rows, width = w.shape
    c1 = 1.0 - ADAM_B1 ** ADAM_STEP
    c2 = 1.0 - ADAM_B2 ** ADAM_STEP

    def update(wv, mv, vv, g):
        m_new = ADAM_B1 * mv + (1.0 - ADAM_B1) * g
        v_new = ADAM_B2 * vv + (1.0 - ADAM_B2) * (g * g)
        delta = -ADAM_LR * ((m_new / c1) / (jnp.sqrt(v_new / c2) + ADAM_EPS) + ADAM_WD * wv)
        return [g, delta, m_new, v_new], []

    if half_rows is None:
        tm = _pick(rows, (128, 64, 32, 16, 8))

        def body(i, j, rows_, vecs):
            g = rows_[3]
            for extra in rows_[4:]:
                g = g + extra
            return update(rows_[0], rows_[1], rows_[2], g)

        return ew(name, body, rows // tm, 1, tm, width, [(w, 0, 0), (m, 0, 0), (v, 0, 0)] + [(g, 0, 0) for g in grads], [],
                  [(F32, rows, width)] * 4)

    tm = _pick(half_rows, (128, 64, 32, 16, 8))
    nb = half_rows // tm
    mine, theirs = grads
    n_layers = len(mine)
    core_vec = jnp.broadcast_to(core.astype(F32), (SUBLANES, width))

    def layer_tile(l):
        return lambda i: jnp.clip(i - l * 2 * nb, 0, 2 * nb - 1) % nb

    def body(i, j, rows_, vecs):
        is_mine = vecs[0][0:1, :] == ((i // nb) % 2).astype(F32)
        layer = i // (2 * nb)
        g = jnp.where(is_mine, rows_[3], rows_[3 + n_layers])
        for l in range(1, n_layers):
            g = jnp.where(layer == l, jnp.where(is_mine, rows_[3 + l], rows_[3 + n_layers + l]), g)
        return update(rows_[0], rows_[1], rows_[2], g)

    return ew(name, body, rows // tm, 1, tm, width,
              [(w, 0, 0), (m, 0, 0), (v, 0, 0)] + [(a_, layer_tile(l), 0) for l, a_ in enumerate(mine)]
              + [(a_, layer_tile(l), 0) for l, a_ in enumerate(theirs)], [(core_vec, 0)],
              [(F32, rows, width)] * 4, pass_i=True)


def _sum_rows(name, arr, nparts, rows, width, dtype=F32):
    tm = _pick(rows, (256, 128, 64, 32, 16, 8))
    nblk = rows // tm

    def body(i, j, rows_, vecs):
        acc = rows_[0].astype(F32)
        for r in rows_[1:]:
            acc = acc + r.astype(F32)
        return [acc], []

    return ew(name, body, nblk, 1, tm, width, [(arr, p * nblk, 0) for p in range(nparts)], [], [(dtype, rows, width)])[0]


def kernel(x, c, ctx, c_ctx, w_ada, b_ada, w_in, w_decay_up, b_decay_up, gla_norm_gain, w_pool_group, pool_scale, w_gla_out, w_pool_out, w_out, ln_mix_gain, ln_mix_bias, w_ffn_in, w_ffn_out, ln_ffn_gain, ln_ffn_bias, loss_target, m_c_ctx, m_w_ada, m_b_ada, m_w_in, m_w_decay_up, m_b_decay_up, m_gla_norm_gain, m_w_pool_group, m_pool_scale, m_w_gla_out, m_w_pool_out, m_w_out, m_ln_mix_gain, m_ln_mix_bias, m_w_ffn_in, m_w_ffn_out, m_ln_ffn_gain, m_ln_ffn_bias, v_c_ctx, v_w_ada, v_b_ada, v_w_in, v_w_decay_up, v_b_decay_up, v_gla_norm_gain, v_w_pool_group, v_pool_scale, v_w_gla_out, v_w_pool_out, v_w_out, v_ln_mix_gain, v_ln_mix_bias, v_w_ffn_in, v_w_ffn_out, v_ln_ffn_gain, v_ln_ffn_bias):
    weights = dict(c_ctx=c_ctx, w_ada=w_ada, b_ada=b_ada, w_in=w_in, w_decay_up=w_decay_up, b_decay_up=b_decay_up,
                   gla_norm_gain=gla_norm_gain, w_pool_group=w_pool_group, pool_scale=pool_scale, w_gla_out=w_gla_out,
                   w_pool_out=w_pool_out, w_out=w_out, ln_mix_gain=ln_mix_gain, ln_mix_bias=ln_mix_bias,
                   w_ffn_in=w_ffn_in, w_ffn_out=w_ffn_out, ln_ffn_gain=ln_ffn_gain, ln_ffn_bias=ln_ffn_bias)
    mom1 = dict(c_ctx=m_c_ctx, w_ada=m_w_ada, b_ada=m_b_ada, w_in=m_w_in, w_decay_up=m_w_decay_up, b_decay_up=m_b_decay_up,
                gla_norm_gain=m_gla_norm_gain, w_pool_group=m_w_pool_group, pool_scale=m_pool_scale, w_gla_out=m_w_gla_out,
                w_pool_out=m_w_pool_out, w_out=m_w_out, ln_mix_gain=m_ln_mix_gain, ln_mix_bias=m_ln_mix_bias,
                w_ffn_in=m_w_ffn_in, w_ffn_out=m_w_ffn_out, ln_ffn_gain=m_ln_ffn_gain, ln_ffn_bias=m_ln_ffn_bias)
    mom2 = dict(c_ctx=v_c_ctx, w_ada=v_w_ada, b_ada=v_b_ada, w_in=v_w_in, w_decay_up=v_w_decay_up, b_decay_up=v_b_decay_up,
                gla_norm_gain=v_gla_norm_gain, w_pool_group=v_w_pool_group, pool_scale=v_pool_scale, w_gla_out=v_w_gla_out,
                w_pool_out=v_w_pool_out, w_out=v_w_out, ln_mix_gain=v_ln_mix_gain, ln_mix_bias=v_ln_mix_bias,
                w_ffn_in=v_w_ffn_in, w_ffn_out=v_w_ffn_out, ln_ffn_gain=v_ln_ffn_gain, ln_ffn_bias=v_ln_ffn_bias)
    names = list(weights)

    cfg = _Cfg()
    L, D = w_ada.shape[0], x.shape[-1]
    S, CL = x.shape[1], ctx.shape[1]
    cfg.L, cfg.D, cfg.S, cfg.CL, cfg.R, cfg.TM = L, D, S, CL, S + CL, CL
    DK, DV, DP = D // 2, D, D // 2
    cfg.DK, cfg.DV, cfg.DP = DK, DV, DP
    cfg.HK, cfg.HV, cfg.PG = DK // N_HEADS, DV // N_HEADS, DP // N_POOL
    DFF = w_ffn_out.shape[1] * N_CHIPS
    NP = 2 * DK + 2 * DV + DP + 2 * D + ALR_W
    cfg.NP, cfg.DFF = NP, DFF
    R, TM, HK, HV, PG = cfg.R, cfg.TM, cfg.HK, cfg.HV, cfg.PG
    MT = R // TM
    alpha = (2.0 * L) ** 0.25
    assert S % TM == 0 and TM % CHUNK == 0 and S % GRID_W == 0 and TM % GRID_W == 0
    OFF_G, OFF_P, OFF_BGA, OFF_BGB = 2 * DK + DV, 2 * DK + 2 * DV, 2 * DK + 2 * DV + DP, 2 * DK + 2 * DV + DP + D
    ALR0 = 2 * DK + 2 * DV
    TE = 512
    TL = TM // 2
    assert D % TE == 0 and DP % TE == 0

    xi, yi, ci = _my_place()
    chip = 2 * xi + yi
    dev = 4 * xi + 2 * yi + ci

    n_ada = w_ada.shape[-1]
    c_all = all_gather8("ag_cond", jnp.pad(c.reshape(1, D), ((0, SUBLANES - 1), (0, 0))))[:, 0, :]
    cond = jnp.concatenate([c_all, c_ctx.reshape(1, D), jnp.zeros((16 - N_DEV - 1, D), F32)], axis=0)

    def silu_body(i, j, rows_, vecs):
        return [_silu(rows_[0]), _dsilu(rows_[0])], []

    act, dact = ew("cond_silu", silu_body, 1, 1, 16, D, [(cond, 0, 0)], [], [(F32, 16, D)] * 2)
    w_ada2 = w_ada.reshape(L * D, n_ada)
    b_ada_mine = lax.dynamic_slice_in_dim(b_ada, chip * n_ada, n_ada, axis=1)
    tn_ada = _pick(n_ada, (1024, 512, 256, 128))
    mods = [matmul("ada_fwd", act, w_ada2, 'nn', 16, n_ada, D, tm=16, tn=tn_ada, b_off=(l, 0),
                   bias=_vec8(b_ada_mine[l]), precise=True) for l in range(L)]
    mods_all = all_gather8("ag_mods", jnp.concatenate(mods, axis=0))
    mods_all = mods_all[0::2].reshape(N_CHIPS, L, 16, n_ada).transpose(1, 2, 0, 3).reshape(L, 16, N_MOD * D)
    modv = [_vec8(mods_all[l, N_DEV], lax.dynamic_index_in_dim(mods_all[l], dev, 0, keepdims=False)) for l in range(L)]
    MB = D // TE

    big = ['w_in', 'w_gla_out', 'w_pool_out', 'w_out', 'w_ffn_in', 'w_ffn_out', 'w_pool_group']
    hosts = [['w_in'], ['w_out', 'w_pool_out', 'w_pool_group'], ['w_ffn_out'], ['w_ffn_in'], ['w_gla_out']]
    all_hosted = sum(hosts, [])
    layer_shards = lambda nms, l: [weights[nm][l].astype(WIRE_DTYPE) for nm in nms]

    def cols_together(g):
        return jnp.moveaxis(g, 0, -2).reshape(g.shape[1:-1] + (N_CHIPS * g.shape[-1],))

    def rows_together(g):
        return g.reshape((N_CHIPS * g.shape[1], g.shape[2]))

    te_ff = _pick(DFF, (1408, 1024, 512, 256, 128))

    def assemble(gw):
        win_ref = cols_together(gw['w_in'])
        win = jnp.concatenate([win_ref[:, :ALR0], win_ref[:, ALR0 + 2 * GATE_RANK:], win_ref[:, ALR0:ALR0 + 2 * GATE_RANK],
                               jnp.zeros((D, ALR_W - 2 * GATE_RANK), WIRE_DTYPE)], axis=-1)
        wpg = jnp.moveaxis(gw['w_pool_group'], 0, 1).reshape(N_POOL, PG, PG)
        wpg_bd = jnp.zeros((N_POOL, PG, N_POOL, PG), WIRE_DTYPE)
        for g in range(N_POOL):
            wpg_bd = wpg_bd.at[g, :, g, :].set(wpg[g])
        win_bwd = jnp.pad(win_ref, ((0, 0), (0, NP - win_ref.shape[1])))
        return dict(win=win, win_bwd=win_bwd, wgo=rows_together(gw['w_gla_out']), wpo=cols_together(gw['w_pool_out']),
                    wout=rows_together(gw['w_out']), wfi=cols_together(gw['w_ffn_in']), wfo=rows_together(gw['w_ffn_out']),
                    wpg_bd=wpg_bd.reshape(DP, DP))

    first = run_comm("gather_weights", gather_plan(layer_shards(all_hosted, 0) + [w_decay_up, b_decay_up],
                                                   n_split=len(big)))
    W = [assemble(dict(zip(all_hosted, first[:len(big)])))] + [None] * (L - 1)
    next_plan = lambda l, k: gather_plan(layer_shards(hosts[k], l + 1), n_split=len(hosts[k]))
    wdu = cols_together(first[len(big)])
    bdu = cols_together(first[len(big) + 1])
    wd_pad = [[jnp.zeros((ALR_W, DK), F32).at[d * GATE_RANK:(d + 1) * GATE_RANK].set(wdu[l, d]) for d in range(2)]
              for l in range(L)]
    bd_pad = [[_vec8(bdu[l, d]) for d in range(2)] for l in range(L)]

    tn_np = _pick(NP, (1280, 1024, 768, 512, 256, 128))
    tn_d = _pick(D, (1024, 512, 256, 128))
    tn_ff2 = _pick(2 * DFF, (1024, 512, 256, 128))
    tn_ff = _pick(DFF, (2816, 1408, 1024, 512, 256, 128))
    tn_dp = _pick(DP, (1024, 512, 256, 128))
    tw = lambda n_: _pick(n_, (512, 256, 128))

    gain8 = lambda v_: _vec8(v_)

    xs = jnp.concatenate([ctx.reshape(CL, D), x.reshape(S, D)], axis=0)
    saved = []
    for l in range(L):
        sv = {}
        mv = modv[l]

        def mod_body(i, j, rows_, vecs):
            return [rows_[0] * (1.0 + _cls(vecs[1], i)) + _cls(vecs[0], i)], []

        if l == 0:
            h1 = ew("modulate", mod_body, MT, D // tn_d, TM, tn_d, [(xs, 0, 0)],
                    [(mv, 0 * (D // tn_d)), (mv, 1 * (D // tn_d))], [(MXU_DTYPE, R, D)])[0]
        Wl = W[l]
        if l + 1 < L:
            proj, got0 = matmul("proj_and_gather", h1, Wl['win'], 'nn', R, NP, D, tm=TM, tn=tn_np, comm=next_plan(l, 0))
            o_f, st_f, got1 = gla_fwd("gla_fwd_and_gather", proj, wd_pad[l][0], bd_pad[l][0], cfg, False, next_plan(l, 1))
            o_b, st_b, got2 = gla_fwd("gla_rev_and_gather", proj, wd_pad[l][1], bd_pad[l][1], cfg, True, next_plan(l, 2))
        else:
            proj = matmul("proj", h1, Wl['win'], 'nn', R, NP, D, tm=TM, tn=tn_np)
            o_f, st_f, _ = gla_fwd("gla_fwd", proj, wd_pad[l][0], bd_pad[l][0], cfg, False)
            o_b, st_b, _ = gla_fwd("gla_rev", proj, wd_pad[l][1], bd_pad[l][1], cfg, True)

        def post_body(i, j, rows_, vecs):
            o = rows_[0] + rows_[1]
            on = o * lax.rsqrt(jnp.mean(o * o, axis=-1, keepdims=True) + RMS_EPS)
            return [on * vecs[0][0:1, :] * _silu(rows_[2])], []

        a_gla = ew("gla_post", post_body, MT, N_HEADS, TM, HV, [(o_f, 0, 0), (o_b, 0, 0), (proj, 0, OFF_G // HV)],
                   [(gain8(gla_norm_gain[l]), 0)], [(MXU_DTYPE, R, DV)])[0]
        y_gla = matmul("gla_out", a_gla, Wl['wgo'], 'nn', R, D, DV, tm=TM, tn=tn_d)
        u_pool = pool_mix("pool_fwd", proj, OFF_P // PG, cfg, False)
        t_pool = matmul("pool_group", u_pool, Wl['wpg_bd'], 'nn', R, DP, DP, tm=TM, tn=tn_dp)

        def scale_body(i, j, rows_, vecs):
            return [rows_[0] * vecs[0][0:1, :]], []

        ts_pool = ew("pool_scale", scale_body, MT, DP // TE, TM, TE, [(t_pool, 0, 0)], [(gain8(pool_scale[l]), 0)],
                     [(MXU_DTYPE, R, DP)])[0]
        y_pool = matmul("pool_out", ts_pool, Wl['wpo'], 'nn', R, D, DP, tm=TM, tn=tn_d)

        def merge_body(i, j, rows_, vecs):
            return [_sigmoid(rows_[2]) * rows_[0] + _sigmoid(rows_[3]) * rows_[1]], []

        mg = ew("merge", merge_body, MT, D // TE, TM, TE,
                [(y_gla, 0, 0), (y_pool, 0, 0), (proj, 0, OFF_BGA // TE), (proj, 0, OFF_BGB // TE)], [],
                [(MXU_DTYPE, R, D)])[0]
        y_mix = matmul("mix_out", mg, Wl['wout'], 'nn', R, D, D, tm=TM, tn=tn_d)

        def ln_body(i, j, rows_, vecs):
            r = alpha * rows_[0] + _cls(vecs[0], i) * rows_[1]
            mu = jnp.mean(r, axis=-1, keepdims=True)
            rc = r - mu
            var = jnp.mean(rc * rc, axis=-1, keepdims=True)
            out = rc * lax.rsqrt(var + LN_EPS) * vecs[1][0:1, :] + vecs[2][0:1, :]
            if len(vecs) == 3:
                return [out], []
            return [out, out * (1.0 + _cls(vecs[4], i)) + _cls(vecs[3], i)], []

        x1, h2 = ew("resid_ln_mod", ln_body, R // TL, 1, TL, D, [(xs, 0, 0), (y_mix, 0, 0)],
                    [(mv, 2), (gain8(ln_mix_gain[l]), 0), (gain8(ln_mix_bias[l]), 0), (mv, 3), (mv, 4)],
                    [(F32, R, D), (MXU_DTYPE, R, D)], ctx_tiles=CL // TL)
        if l + 1 < L:
            u_ffn, got3 = matmul("ffn_in_and_gather", h2, Wl['wfi'], 'nn', R, 2 * DFF, D, tm=TM, tn=tn_ff2,
                                 comm=next_plan(l, 3))
        else:
            u_ffn = matmul("ffn_in", h2, Wl['wfi'], 'nn', R, 2 * DFF, D, tm=TM, tn=tn_ff2)
        s_ffn = swiglu("swiglu", u_ffn, MT, DFF // te_ff, TM, te_ff)
        if l + 1 < L:
            f_ffn, got4 = matmul("ffn_out_and_gather", s_ffn, Wl['wfo'], 'nn', R, D, DFF, tm=TM, tn=tn_d, comm=next_plan(l, 4))
            W[l + 1] = assemble(dict(zip(all_hosted, got0 + got1 + got2 + got3 + got4)))
        else:
            f_ffn = matmul("ffn_out", s_ffn, Wl['wfo'], 'nn', R, D, DFF, tm=TM, tn=tn_d)
        if l + 1 < L:
            x2, h_next = ew("resid_ln_mod", ln_body, R // TL, 1, TL, D, [(x1, 0, 0), (f_ffn, 0, 0)],
                            [(mv, 5), (gain8(ln_ffn_gain[l]), 0), (gain8(ln_ffn_bias[l]), 0), (modv[l + 1], 0), (modv[l + 1], 1)],
                            [(F32, R, D), (MXU_DTYPE, R, D)], ctx_tiles=CL // TL)
        else:
            x2 = ew("resid_ln", ln_body, R // TL, 1, TL, D, [(x1, 0, 0), (f_ffn, 0, 0)],
                    [(mv, 5), (gain8(ln_ffn_gain[l]), 0), (gain8(ln_ffn_bias[l]), 0)], [(F32, R, D)], ctx_tiles=CL // TL)[0]
        sv.update(xs=xs, h1=h1, proj=proj, o_f=o_f, o_b=o_b, st_f=st_f, st_b=st_b, a_gla=a_gla, y_gla=y_gla, u_pool=u_pool,
                  t_pool=t_pool, ts_pool=ts_pool, y_pool=y_pool, mg=mg, y_mix=y_mix, x1=x1, h2=h2, u_ffn=u_ffn, s_ffn=s_ffn,
                  f_ffn=f_ffn)
        saved.append(sv)
        xs = x2
        if l + 1 < L:
            h1 = h_next

    tgt = loss_target.reshape(S, D)

    def loss_body(i, j, rows_, vecs):
        d = jnp.where(i, 0.0, rows_[0] - rows_[1])
        return [d * (1.0 / D)], [_colsum(d * d)]

    d_x, sq = ew("loss", loss_body, MT, 1, TM, D, [(xs, 0, 0), (tgt, lambda i: jnp.maximum(i - 1, 0), 0)], [],
                 [(F32, R, D)], [(D, False)])

    def total_body(i, j, rows_, vecs):
        return [jnp.broadcast_to(jnp.sum(rows_[0], axis=-1, keepdims=True), (SUBLANES, D)) * (0.5 / D)], []

    loss_local = ew("loss_total", total_body, 1, 1, SUBLANES, D, [(sq, 0, 0)], [], [(F32, SUBLANES, D)])[0][0, 0]
    loss = lax.psum(loss_local, ("x", "y", "c"))

    gsmall = {nm: [None] * L for nm in ['gla_norm_gain', 'pool_scale', 'ln_mix_gain', 'ln_mix_bias', 'ln_ffn_gain',
                                        'ln_ffn_bias', 'w_decay_up', 'b_decay_up']}
    gbig = {nm: [None] * L for nm in big}
    dmod = [None] * L

    def ln_bwd(name, x_in, br, d_out, mv, gt_blk, gain, through=None):
        def body(i, j, rows_, vecs):
            gt = _cls(vecs[0], i)
            r = alpha * rows_[0] + gt * rows_[1]
            mu = jnp.mean(r, axis=-1, keepdims=True)
            rc = r - mu
            rstd = lax.rsqrt(jnp.mean(rc * rc, axis=-1, keepdims=True) + LN_EPS)
            xhat = rc * rstd
            extra = []
            if through is None:
                d_o = rows_[2]
            else:
                d_o = rows_[2] * (1.0 + _cls(vecs[2], i)) + alpha * rows_[4]
                extra = [_colsum(rows_[2] * rows_[3]), _colsum(rows_[2])]
            dxh = d_o * vecs[1][0:1, :]
            dr = rstd * (dxh - jnp.mean(dxh, axis=-1, keepdims=True) - xhat * jnp.mean(dxh * xhat, axis=-1, keepdims=True))
            return [dr, gt * dr], [_colsum(d_o * xhat), _colsum(d_o), _colsum(dr * rows_[1])] + extra

        rows_in = [(x_in, 0, 0), (br, 0, 0)]
        vecs_in = [(mv, gt_blk), (gain8(gain), 0)]
        sums = [(D, False), (D, False), (D, True)]
        if through is None:
            rows_in.append((d_out, 0, 0))
        else:
            rows_in += [(through[0], 0, 0), (through[1], 0, 0), (through[2], 0, 0)]
            vecs_in.append((mv, through[3]))
            sums += [(D, True), (D, True)]
        return ew(name, body, R // TL, 1, TL, D, rows_in, vecs_in, [(F32, R, D), (MXU_DTYPE, R, D)], sums,
                  ctx_tiles=CL // TL)

    def mod_bwd(name, d_h, x_in, d_r, mv, sc_blk):
        def body(i, j, rows_, vecs):
            return ([rows_[0] * (1.0 + _cls(vecs[0], i)) + alpha * rows_[2]],
                    [_colsum(rows_[0] * rows_[1]), _colsum(rows_[0])])

        return ew(name, body, MT, D // tn_d, TM, tn_d, [(d_h, 0, 0), (x_in, 0, 0), (d_r, 0, 0)],
                  [(mv, sc_blk * (D // tn_d))], [(F32, R, D)], [(D, True), (D, True)])

    shard_axis = {'w_in': 1, 'w_gla_out': 0, 'w_pool_out': 1, 'w_out': 0, 'w_ffn_in': 1, 'w_ffn_out': 0, 'w_pool_group': 1}
    scatter_groups = [['w_in'], ['w_ffn_in'], ['w_gla_out', 'w_pool_out', 'w_out', 'w_ffn_out', 'w_pool_group']]

    def cut_block(nm, g, which, core, dtype):
        ax = shard_axis[nm]
        width = weights[nm].shape[ax + 1]
        starts, sizes = [0] * g.ndim, list(g.shape)
        if ax == 0:
            sizes[0] = width // 2
            starts[0] = which * width + core * sizes[0]
        else:
            sizes[0], sizes[ax] = g.shape[0] // 2, width
            starts[0], starts[ax] = core * sizes[0], which * width
        return lax.dynamic_slice(g, starts, sizes).astype(dtype)

    def cut_layer(l):
        cut = dict(keep_r={}, give_r={}, keep_o={}, give_o={})
        for nm in big:
            g = gbig[nm][l]
            if nm == 'w_ffn_in':
                assert 2 * weights[nm].shape[2] == DFF and N_CHIPS == 4
                blk = lambda which, core, dt: jnp.where(which < 2, cut_block(nm, g[0], which % 2, core, dt),
                                                        cut_block(nm, g[1], which % 2, core, dt))
            else:
                blk = lambda which, core, dt: cut_block(nm, g, which, core, dt)
            others = [jnp.bitwise_xor(chip, k) for k in range(1, N_CHIPS)]
            cut['keep_r'][nm] = [blk(o_, ci, WIRE_DTYPE) for o_ in others]
            cut['give_r'][nm] = [blk(o_, 1 - ci, WIRE_DTYPE) for o_ in others]
            cut['keep_o'][nm], cut['give_o'][nm] = blk(chip, ci, F32), blk(chip, 1 - ci, F32)
        return cut

    def swap_partials_plan(cut):
        return swap_plan([a_ for nm in big for a_ in cut['give_r'][nm]] + [cut['give_o'][nm] for nm in big])

    def add_partials(cut, got):
        cut['got_o'] = dict(zip(big, got[3 * len(big):]))
        cut['send_r'] = {}
        for t, nm in enumerate(big):
            kept = [_rows2d(a_) for a_ in cut['keep_r'][nm]]
            theirs = [_rows2d(a_) for a_ in got[3 * t:3 * t + 3]]
            rows_, width = kept[0].shape
            tm_ = _pick(rows_, (256, 128, 64, 32, 16, 8))

            def add2_body(i, j, rows__, vecs):
                return [rows__[k].astype(F32) + rows__[3 + k].astype(F32) for k in range(3)], []

            cut['send_r'][nm] = ew("add_partials", add2_body, rows_ // tm_, 1, tm_, width,
                                   [(a_, 0, 0) for a_ in kept + theirs], [], [(WIRE_DTYPE, rows_, width)] * 3)
        cut['recvd'] = {}

    def sum_grads(cut):
        done = {}
        for nm in big:
            a2, b2 = _rows2d(cut['keep_o'][nm]), _rows2d(cut['got_o'][nm])
            rows_, width = a2.shape
            tm_ = _pick(rows_, (256, 128, 64, 32, 16, 8))

            def psum_body(i, j, rows__, vecs):
                return [(rows__[0] + rows__[1]) + rows__[2].astype(F32) + rows__[3].astype(F32) + rows__[4].astype(F32)], []

            done[nm] = ew("sum_grads", psum_body, rows_ // tm_, 1, tm_, width,
                          [(a2, 0, 0), (b2, 0, 0)] + [(r_, 0, 0) for r_ in cut['recvd'][nm]], [], [(F32, rows_, width)])[0]
        return done

    def scatter_group_plan(cut, group):
        return scatter_plan([cut['send_r'][nm] for nm in group])

    def received(cut, group, got):
        for t, nm in enumerate(group):
            cut['recvd'][nm] = got[3 * t:3 * t + 3]

    finished = [None] * L
    pending = None
    for l in reversed(range(L)):
        sv = saved[l]
        mv = modv[l]
        Wl = W[l]
        proj = sv['proj']
        dr2, d_f, g_gain, g_bias, g_gt_f = ln_bwd("ln_bwd", sv['x1'], sv['f_ffn'], d_x, mv, 5, ln_ffn_gain[l])
        gsmall['ln_ffn_gain'][l], gsmall['ln_ffn_bias'][l] = g_gain[0], g_bias[0]
        if pending is not None:
            d_s, got = matmul("ffn_out_dx_and_swap", d_f, Wl['wfo'], 'nt', R, DFF, D, tm=TM, tn=tn_ff,
                              comm=swap_partials_plan(pending))
            add_partials(pending, got)
        else:
            d_s = matmul("ffn_out_dx", d_f, Wl['wfo'], 'nt', R, DFF, D, tm=TM, tn=tn_ff)
        gbig['w_ffn_out'][l] = matmul("ffn_out_dw", sv['s_ffn'], d_f, 'tn', DFF, D, R, tm=tw(DFF), tn=tn_d)
        d_gate, d_up = swiglu_bwd("swiglu_bwd", d_s, sv['u_ffn'], MT, DFF // te_ff, TM, te_ff)
        tn_x = 512
        if pending is not None:
            d_h2, got = matmul("ffn_in_dx_and_scatter", d_gate, Wl['wfi'], 'nt', R, D, DFF, tm=TM, tn=tn_x,
                               second=(d_up, (0, 1)), comm=scatter_group_plan(pending, scatter_groups[0]))
            received(pending, scatter_groups[0], got)
        else:
            d_h2 = matmul("ffn_in_dx", d_gate, Wl['wfi'], 'nt', R, D, DFF, tm=TM, tn=tn_x, second=(d_up, (0, 1)))
        gbig['w_ffn_in'][l] = (matmul("ffn_in_dw", sv['h2'], d_gate, 'tn', D, DFF, R, tm=tw(D), tn=te_ff),
                               matmul("ffn_in_dw_up", sv['h2'], d_up, 'tn', D, DFF, R, tm=tw(D), tn=te_ff))
        dr1, d_y, g_gain, g_bias, g_gt_m, g_sc_f, g_sh_f = ln_bwd("ln_bwd_through", sv['xs'], sv['y_mix'], None, mv, 2,
                                                                   ln_mix_gain[l], through=(d_h2, sv['x1'], dr2, 4))
        gsmall['ln_mix_gain'][l], gsmall['ln_mix_bias'][l] = g_gain[0], g_bias[0]
        d_mg = matmul("mix_out_dx", d_y, Wl['wout'], 'nt', R, D, D, tm=TM, tn=tn_d)
        gbig['w_out'][l] = matmul("mix_out_dw", sv['mg'], d_y, 'tn', D, D, R, tm=tw(D), tn=tn_d)

        def merge_bwd_body(i, j, rows_, vecs):
            d_m, yg, yp, ba, bb = rows_
            sa, sb = _sigmoid(ba), _sigmoid(bb)
            return [d_m * sa, d_m * sb, d_m * yg * sa * (1.0 - sa), d_m * yp * sb * (1.0 - sb)], []

        d_yg, d_yp, d_bga, d_bgb = ew(
            "merge_bwd", merge_bwd_body, MT, D // TE, TM, TE,
            [(d_mg, 0, 0), (sv['y_gla'], 0, 0), (sv['y_pool'], 0, 0), (proj, 0, OFF_BGA // TE), (proj, 0, OFF_BGB // TE)], [],
            [(MXU_DTYPE, R, D), (MXU_DTYPE, R, D), (F32, R, D), (F32, R, D)])
        d_ts = matmul("pool_out_dx", d_yp, Wl['wpo'], 'nt', R, DP, D, tm=TM, tn=tn_dp)
        gbig['w_pool_out'][l] = matmul("pool_out_dw", sv['ts_pool'], d_yp, 'tn', DP, D, R, tm=tw(DP), tn=tn_d)

        def scale_bwd_body(i, j, rows_, vecs):
            return [rows_[0] * vecs[0][0:1, :]], [_colsum(rows_[0] * rows_[1])]

        d_t, g_ps = ew("pool_scale_bwd", scale_bwd_body, MT, DP // TE, TM, TE, [(d_ts, 0, 0), (sv['t_pool'], 0, 0)],
                       [(gain8(pool_scale[l]), 0)], [(MXU_DTYPE, R, DP)], [(DP, False)])
        gsmall['pool_scale'][l] = g_ps[0]
        d_u_pool = matmul("pool_group_dx", d_t, Wl['wpg_bd'], 'nt', R, DP, DP, tm=TM, tn=tn_dp)
        g_bd = matmul("pool_group_dw", sv['u_pool'], d_t, 'tn', DP, DP, R, tm=tw(DP), tn=tn_dp)
        gbig['w_pool_group'][l] = jnp.stack([g_bd[g * PG:(g + 1) * PG, g * PG:(g + 1) * PG] for g in range(N_POOL)])
        d_p = pool_mix("pool_bwd", d_u_pool, 0, cfg, True)
        d_a = matmul("gla_out_dx", d_yg, Wl['wgo'], 'nt', R, DV, D, tm=TM, tn=tn_d)
        gbig['w_gla_out'][l] = matmul("gla_out_dw", sv['a_gla'], d_yg, 'tn', DV, D, R, tm=tw(DV), tn=tn_d)

        def post_bwd_body(i, j, rows_, vecs):
            d_a_, o_f_, o_b_, g_ = rows_
            gain = vecs[0][0:1, :]
            o = o_f_ + o_b_
            rstd = lax.rsqrt(jnp.mean(o * o, axis=-1, keepdims=True) + RMS_EPS)
            on = o * rstd
            sg = _silu(g_)
            d_on = d_a_ * gain * sg
            d_o_ = rstd * (d_on - on * jnp.mean(d_on * on, axis=-1, keepdims=True))
            return [d_o_, d_a_ * on * gain * _dsilu(g_)], [_colsum(d_a_ * on * sg)]

        d_o, d_g, g_gng = ew("gla_post_bwd", post_bwd_body, MT, N_HEADS, TM, HV,
                             [(d_a, 0, 0), (sv['o_f'], 0, 0), (sv['o_b'], 0, 0), (proj, 0, OFF_G // HV)],
                             [(gain8(gla_norm_gain[l]), 0)], [(F32, R, DV), (F32, R, DV)], [(DV, False)])
        gsmall['gla_norm_gain'][l] = g_gng[0]
        part = gla_bwd("gla_bwd", proj, wd_pad[l][0], bd_pad[l][0], sv['st_f'], d_o, cfg, False)
        full = gla_bwd("gla_rev_bwd", proj, wd_pad[l][1], bd_pad[l][1], sv['st_b'], d_o, cfg, True, addends=part[:4])
        d_q, d_k, d_v, d_alr = full[:4]
        g_wd = [res[4][d * GATE_RANK:(d + 1) * GATE_RANK] for d, res in enumerate((part, full))]
        g_bd_ = [res[5][0] for res in (part, full)]
        gsmall['w_decay_up'][l], gsmall['b_decay_up'][l] = jnp.stack(g_wd), jnp.stack(g_bd_)
        d_proj = jnp.concatenate([t_.astype(MXU_DTYPE) for t_ in (d_q, d_k, d_v, d_g, d_alr[:, :2 * GATE_RANK], d_p, d_bga, d_bgb,
                                                                   jnp.zeros((R, ALR_W - 2 * GATE_RANK), F32))], axis=-1)
        if pending is not None:
            d_h1, got = matmul("proj_dx_and_scatter", d_proj, Wl['win_bwd'], 'nt', R, D, NP, tm=TM, tn=tn_x,
                               comm=scatter_group_plan(pending, scatter_groups[2]))
            received(pending, scatter_groups[2], got)
            gbig['w_in'][l], got = matmul("proj_dw_and_scatter", sv['h1'], d_proj, 'tn', D, NP, R, tm=tw(D), tn=tn_np,
                                          comm=scatter_group_plan(pending, scatter_groups[1]))
            received(pending, scatter_groups[1], got)
            finished[l + 1] = sum_grads(pending)
        else:
            d_h1 = matmul("proj_dx", d_proj, Wl['win_bwd'], 'nt', R, D, NP, tm=TM, tn=tn_x)
            gbig['w_in'][l] = matmul("proj_dw", sv['h1'], d_proj, 'tn', D, NP, R, tm=tw(D), tn=tn_np)
        d_x, g_sc_m, g_sh_m = mod_bwd("mod_bwd", d_h1, sv['xs'], dr1, mv, 1)
        dmod[l] = jnp.concatenate([g_sh_m[:2], g_sc_m[:2], g_gt_m[:2], g_sh_f[:2], g_sc_f[:2], g_gt_f[:2]], axis=-1)
        pending = cut_layer(l)

    grad_x = d_x[CL:].reshape(x.shape)
    add_partials(pending, run_comm("swap_partials", swap_partials_plan(pending)))
    received(pending, big, run_comm("scatter_grads", scatter_group_plan(pending, big)))
    finished[0] = sum_grads(pending)

    dmod = jnp.stack(dmod)
    summed = [dmod[:, 0]] + [jnp.stack(gsmall[nm]) for nm in
                             ['ln_mix_gain', 'ln_mix_bias', 'ln_ffn_gain', 'ln_ffn_bias', 'gla_norm_gain', 'pool_scale',
                              'w_decay_up', 'b_decay_up']]
    pack = _pack([dmod[:, 1]] + summed)
    prow = pack.shape[0]
    packs = all_gather8("ag_small", pack)
    tot = _sum_rows("sum_small", packs.reshape(N_DEV * prow, PACK_W), N_DEV, prow, PACK_W)
    shapes = [(L, N_MOD * D)] + [a.shape for a in summed]
    tot = _unpack(tot, shapes)
    dmod_ctx = tot[1]
    g_rep = dict(zip(['ln_mix_gain', 'ln_mix_bias', 'ln_ffn_gain', 'ln_ffn_bias', 'gla_norm_gain', 'pool_scale'], tot[2:8]))
    g_wdu_full, g_bdu_full = tot[8], tot[9]
    dmod_lat = jnp.stack([_unpack(packs[d_], shapes[:1])[0] for d_ in range(N_DEV)], axis=1)
    dm_all = jnp.concatenate([dmod_lat, dmod_ctx[:, None, :], jnp.zeros((L, 16 - N_DEV - 1, N_MOD * D), F32)], axis=1)

    dm_mine = lax.dynamic_slice_in_dim(dm_all, chip * n_ada, n_ada, axis=2)
    g_w_ada = jnp.stack([matmul("ada_dw", act, dm_mine[l], 'tn', D, n_ada, 16, tm=tw(D), tn=tn_ada, precise=True)
                         for l in range(L)])

    def bsum_body(i, j, rows_, vecs):
        return [jnp.broadcast_to(_colsum(rows_[0]), rows_[0].shape)], []

    g_b_ada = jnp.stack([ew("ada_db", bsum_body, 1, 1, 16, N_MOD * D, [(dm_all[l], 0, 0)], [],
                            [(F32, 16, N_MOD * D)])[0][0] for l in range(L)])
    part_c = [matmul("ada_dc", dm_mine[l], w_ada2, 'nt', 16, D, n_ada, tm=16, tn=tn_d, b_off=(l * (D // tn_d), 0),
                     precise=True)
              for l in range(L)]
    parts_c = all_gather8("ag_dcond", jnp.concatenate(part_c, axis=0))
    dc_rows = parts_c[0::2].reshape(N_CHIPS * L * 16, D)

    def dc_body(i, j, rows_, vecs):
        acc = rows_[0]
        for r_ in rows_[1:-1]:
            acc = acc + r_
        return [acc * rows_[-1]], []

    g_c_ctx = ew("dcond", dc_body, 1, 1, 16, D, [(dc_rows, p_, 0) for p_ in range(N_CHIPS * L)] + [(dact, 0, 0)], [],
                 [(F32, 16, D)])[0][N_DEV]

    other = run_comm("swap_grads", swap_plan([finished[l][nm] for nm in big for l in range(L)]))

    out_g, out_d, out_m, out_v = {}, {}, {}, {}
    for t, nm in enumerate(big):
        shp = weights[nm].shape
        half_rows = (shp[1] // 2) * (shp[2] if len(shp) == 4 else 1)
        res = _adam("adam_" + nm, _rows2d(weights[nm]), ([finished[l][nm] for l in range(L)], other[t * L:(t + 1) * L]),
                    _rows2d(mom1[nm]), _rows2d(mom2[nm]), half_rows=half_rows, core=ci)
        out_g[nm], out_d[nm], out_m[nm], out_v[nm] = [r_.reshape(shp) for r_ in res]
    res = _adam("adam_w_ada", _rows2d(w_ada), [_rows2d(g_w_ada)], _rows2d(m_w_ada), _rows2d(v_w_ada))
    out_g['w_ada'], out_d['w_ada'], out_m['w_ada'], out_v['w_ada'] = [r_.reshape(w_ada.shape) for r_ in res]
    n_wd, n_bd = w_decay_up.shape[-1], b_decay_up.shape[-1]
    small_g = dict(g_rep, c_ctx=g_c_ctx, b_ada=g_b_ada,
                   w_decay_up=lax.dynamic_slice_in_dim(g_wdu_full, chip * n_wd, n_wd, axis=3),
                   b_decay_up=lax.dynamic_slice_in_dim(g_bdu_full, chip * n_bd, n_bd, axis=2))
    small = [nm for nm in names if nm not in big and nm != 'w_ada']
    res = _adam("adam_small", _pack([weights[nm] for nm in small]), [_pack([small_g[nm] for nm in small])],
                _pack([mom1[nm] for nm in small]), _pack([mom2[nm] for nm in small]))
    small_shapes = [weights[nm].shape for nm in small]
    for dst, packed in zip((out_g, out_d, out_m, out_v), res):
        for nm, val in zip(small, _unpack(packed, small_shapes)):
            dst[nm] = val

    return (loss, grad_x, *[out_g[nm] for nm in names], *[out_d[nm] for nm in names],
            *[out_m[nm] for nm in names], *[out_v[nm] for nm in names])


def swiglu(name, u, nrow, nh, tm, tn):
    def kern(gate_ref, up_ref, o_ref):
        o_ref[...] = (_silu(gate_ref[...]) * up_ref[...]).astype(o_ref.dtype)

    return pl.pallas_call(
        kern, name=name, grid=(nh, nrow),
        in_specs=[pl.BlockSpec((tm, tn), lambda j, i: (i, j)), pl.BlockSpec((tm, tn), lambda j, i: (i, nh + j))],
        out_specs=pl.BlockSpec((tm, tn), lambda j, i: (i, j)),
        out_shape=jax.ShapeDtypeStruct((u.shape[0], nh * tn), MXU_DTYPE), compiler_params=_params(),
    )(u, u)


def swiglu_bwd(name, d_s, u, nrow, nh, tm, tn):
    def kern(ds_ref, gate_ref, up_ref, dg_ref, du_ref):
        gate, d_s_ = gate_ref[...], ds_ref[...]
        dg_ref[...] = (d_s_ * up_ref[...] * _dsilu(gate)).astype(dg_ref.dtype)
        du_ref[...] = (d_s_ * _silu(gate)).astype(du_ref.dtype)

    tile = pl.BlockSpec((tm, tn), lambda j, i: (i, j))
    return pl.pallas_call(
        kern, name=name, grid=(nh, nrow), in_specs=[tile, tile, pl.BlockSpec((tm, tn), lambda j, i: (i, nh + j))],
        out_specs=[tile, tile], out_shape=[jax.ShapeDtypeStruct((u.shape[0], nh * tn), MXU_DTYPE)] * 2,
        compiler_params=_params(),
    )(d_s, u, u)
```
